```python
import math
import jax, jax.numpy as jnp
from jax import lax
import numpy as np

D_MODEL = 1024
BATCH = 8
SEQ = 8192
DEPTH = 4

POOL_WINDOWS = (2, 4, 8, 16)
POOL_GROUP = D_MODEL // 8
POOL_WIDTH = POOL_GROUP * len(POOL_WINDOWS)
SSM_D_INNER = D_MODEL
SSM_HEAD_DIM = 64
SSM_HEADS = SSM_D_INNER // SSM_HEAD_DIM
SSM_GROUPS = 2
SSM_STATE = 128
SSM_CONV = 4
SSM_CHUNK = 128
SSM_CONV_DIM = SSM_D_INNER + 2 * SSM_GROUPS * SSM_STATE
EVEN_IN = POOL_WIDTH + SSM_D_INNER + SSM_CONV_DIM + SSM_HEADS
EVEN_MIX = POOL_WIDTH + SSM_D_INNER
FOX_HEADS = 8
FOX_HEAD_DIM = 64
FOX_WIDTH = FOX_HEADS * FOX_HEAD_DIM
MLA_HEADS = 8
MLA_NOPE = 64
MLA_ROPE = 32
MLA_V = 64
MLA_Q_RANK = 512
MLA_KV_RANK = 256
ROPE_THETA = 10000.0
ODD_IN = 3 * FOX_WIDTH + FOX_HEADS + MLA_Q_RANK + MLA_KV_RANK + MLA_ROPE
ODD_MIX = FOX_WIDTH + MLA_HEADS * MLA_V
ATTN_BLOCK = 128
D_FF = -(-8 * D_MODEL // (3 * 256)) * 256
DEEPNORM_ALPHA = (2 * DEPTH) ** 0.25
DEEPNORM_BETA = (8 * DEPTH) ** -0.25
N_EVEN = (DEPTH + 1) // 2
N_ODD = DEPTH // 2
LN_EPS = 1e-5
RMS_EPS = 1e-6

kernel_name = "hybrid_pool_ssd_fox_mla_deepnorm"

F32 = jnp.float32


def layer_norm(x, g, b):
    xf = x.astype(F32)
    mu = jnp.mean(xf, axis=-1, keepdims=True)
    var = jnp.mean(jnp.square(xf - mu), axis=-1, keepdims=True)
    return ((xf - mu) * lax.rsqrt(var + LN_EPS) * g + b).astype(x.dtype)


def rms_norm(x, g):
    xf = x.astype(F32)
    return (xf * lax.rsqrt(jnp.mean(jnp.square(xf), axis=-1, keepdims=True) + RMS_EPS) * g).astype(x.dtype)


def rope(x, pos):
    half = x.shape[-1] // 2
    freqs = jnp.power(ROPE_THETA, -jnp.arange(half, dtype=F32) / half)
    ang = pos.astype(F32)[:, None] * freqs[None, :]
    cos, sin = jnp.cos(ang), jnp.sin(ang)
    x1 = x[..., :half].astype(F32)
    x2 = x[..., half:].astype(F32)
    return jnp.concatenate([x1 * cos - x2 * sin, x2 * cos + x1 * sin], axis=-1).astype(x.dtype)


def multiscale_pool(u, pool_w, pool_scale):
    b, s, _ = u.shape
    ng = len(POOL_WINDOWS)
    ug = u.reshape(b, s, ng, POOL_GROUP).astype(F32)
    cs = jnp.cumsum(ug, axis=1)
    pos = jnp.arange(s)
    means = []
    for g, w in enumerate(POOL_WINDOWS):
        c = cs[:, :, g]
        lagged = jnp.pad(c, ((0, 0), (w, 0), (0, 0)))[:, :s]
        count = jnp.minimum(pos + 1, w).astype(F32)[None, :, None]
        means.append((c - lagged) / count)
    diff = (jnp.stack(means, axis=2) - ug).astype(u.dtype)
    y = jnp.einsum('bsgc,gcd->bsgd', diff, pool_w).reshape(b, s, POOL_WIDTH)
    return y * pool_scale


def causal_dwconv(u, w, bias):
    c = u.shape[-1]
    out = lax.conv_general_dilated(
        u, w[:, None, :].astype(u.dtype), window_strides=(1,),
        padding=[(SSM_CONV - 1, 0)], dimension_numbers=('NWC', 'WIO', 'NWC'),
        feature_group_count=c)
    return out + bias


def mamba2_ssd(z, xbc, dt, conv_w, conv_b, dt_bias, a_log, d_skip, norm_w):
    b, s, _ = z.shape
    G, E, P, N, L = SSM_GROUPS, SSM_HEADS // SSM_GROUPS, SSM_HEAD_DIM, SSM_STATE, SSM_CHUNK
    nc = s // L
    xbc = jax.nn.silu(causal_dwconv(xbc, conv_w, conv_b))
    xs, bm, cm = jnp.split(xbc, [SSM_D_INNER, SSM_D_INNER + G * N], axis=-1)
    dt = jax.nn.softplus(dt.astype(F32) + dt_bias)
    a = -jnp.exp(a_log.astype(F32))
    xh = xs.reshape(b, nc, L, G, E, P).astype(F32)
    dtc = dt.reshape(b, nc, L, G, E)
    xdt = xh * dtc[..., None]
    da = dtc * a.reshape(G, E)
    bc = bm.reshape(b, nc, L, G, N).astype(F32)
    cc = cm.reshape(b, nc, L, G, N).astype(F32)
    acs = jnp.cumsum(da, axis=2).transpose(0, 1, 3, 4, 2)
    seg = acs[..., :, None] - acs[..., None, :]
    tri = jnp.tril(jnp.ones((L, L), dtype=bool))
    lmat = jnp.exp(jnp.where(tri, seg, -jnp.inf))
    cb = jnp.einsum('bclgn,bcsgn->bcgls', cc, bc)
    y_diag = jnp.einsum('bcgls,bcgels,bcsgep->bclgep', cb, lmat, xdt)
    decay_states = jnp.exp(acs[..., -1:] - acs)
    states = jnp.einsum('bclgn,bcgel,bclgep->bcgepn', bc, decay_states, xdt)
    chunk_decay = jnp.exp(acs[..., -1])

    def step(h, inp):
        st, dec = inp
        return h * dec[..., None, None] + st, h

    h0 = jnp.zeros((b, G, E, P, N), F32)
    _, prev = lax.scan(step, h0, (jnp.moveaxis(states, 1, 0), jnp.moveaxis(chunk_decay, 1, 0)))
    prev = jnp.moveaxis(prev, 0, 1)
    y_off = jnp.einsum('bclgn,bcgepn,bcgel->bclgep', cc, prev, jnp.exp(acs))
    y = y_diag + y_off + xh * d_skip.astype(F32).reshape(G, E)[:, :, None]
    y = y.reshape(b, s, SSM_D_INNER) * jax.nn.silu(z.astype(F32))
    return rms_norm(y, norm_w).astype(z.dtype)


def block_causal_attention(q, k, v, log_decay_cum):
    b, h, s, dk = q.shape
    nb = s // ATTN_BLOCK
    scale = dk ** -0.5
    qb = q.reshape(b, h, nb, ATTN_BLOCK, dk).transpose(2, 0, 1, 3, 4)
    fb = None if log_decay_cum is None else log_decay_cum.reshape(b, h, nb, ATTN_BLOCK).transpose(2, 0, 1, 3)
    kpos = jnp.arange(s)

    def one_block(args):
        qi, fi, i = args
        sc = jnp.einsum('bhqd,bhkd->bhqk', qi, k).astype(F32) * scale
        if fi is not None:
            sc = sc + fi[..., :, None] - log_decay_cum[:, :, None, :]
        qpos = i * ATTN_BLOCK + jnp.arange(ATTN_BLOCK)
        sc = jnp.where(kpos[None, :] <= qpos[:, None], sc, -jnp.inf)
        p = jax.nn.softmax(sc, axis=-1).astype(v.dtype)
        return jnp.einsum('bhqk,bhkd->bhqd', p, v)

    o = lax.map(one_block, (qb, fb, jnp.arange(nb)))
    return o.transpose(1, 0, 3, 2, 4).reshape(b, s, h * v.shape[-1])


def even_mixer(x, w_in, pool_w, pool_scale, conv_w, conv_b, dt_bias, a_log, d_skip, norm_w, w_out):
    proj = x @ w_in
    c1 = POOL_WIDTH
    c2 = c1 + SSM_D_INNER
    c3 = c2 + SSM_CONV_DIM
    u, z, xbc, dt = jnp.split(proj, [c1, c2, c3], axis=-1)
    y_pool = multiscale_pool(u, pool_w, pool_scale)
    y_ssm = mamba2_ssd(z, xbc, dt, conv_w, conv_b, dt_bias, a_log, d_skip, norm_w)
    return jnp.concatenate([y_pool, y_ssm], axis=-1) @ w_out


def odd_mixer(x, w_in, fgate_b, q_norm_w, w_uq, kv_norm_w, w_ukv, w_out):
    b, s, _ = x.shape
    proj = x @ w_in
    cuts = np.cumsum([FOX_WIDTH, FOX_WIDTH, FOX_WIDTH, FOX_HEADS, MLA_Q_RANK, MLA_KV_RANK]).tolist()
    qf, kf, vf, fl, cq, ckv, kr = jnp.split(proj, cuts, axis=-1)

    def heads(t, nh):
        return t.reshape(b, s, nh, -1).transpose(0, 2, 1, 3)

    log_f = jax.nn.log_sigmoid(fl.astype(F32) + fgate_b)
    fcum = jnp.cumsum(log_f, axis=1).transpose(0, 2, 1)
    o_fox = block_causal_attention(heads(qf, FOX_HEADS), heads(kf, FOX_HEADS), heads(vf, FOX_HEADS), fcum)

    pos = jnp.arange(s)
    q = heads(rms_norm(cq, q_norm_w) @ w_uq, MLA_HEADS)
    q = jnp.concatenate([q[..., :MLA_NOPE], rope(q[..., MLA_NOPE:], pos)], axis=-1)
    kv = heads(rms_norm(ckv, kv_norm_w) @ w_ukv, MLA_HEADS)
    k_rope = jnp.broadcast_to(rope(kr[:, None], pos), (b, MLA_HEADS, s, MLA_ROPE))
    k = jnp.concatenate([kv[..., :MLA_NOPE], k_rope], axis=-1)
    o_mla = block_causal_attention(q, k, kv[..., MLA_NOPE:], None)
    return jnp.concatenate([o_fox, o_mla], axis=-1) @ w_out


def swiglu(x, w_gate, w_up, w_down):
    return (jax.nn.silu(x @ w_gate) * (x @ w_up)) @ w_down


def _fwd_setup_inputs(seed: int = 0) -> dict:
    key = jax.random.key(seed)
    ks = iter(jax.random.split(key, 40))

    def nrm(shape, std):
        return jax.random.normal(next(ks), shape, F32) * std

    def gain(shape):
        return 1.0 + 0.1 * jax.random.normal(next(ks), shape, F32)

    dt0 = jnp.exp(jax.random.uniform(next(ks), (N_EVEN, SSM_HEADS), F32, math.log(1e-3), math.log(1e-1)))
    return {
        "x": jax.random.normal(next(ks), (BATCH, SEQ, D_MODEL), F32),
        "even_w_in": nrm((N_EVEN, D_MODEL, EVEN_IN), D_MODEL ** -0.5),
        "pool_w": nrm((N_EVEN, len(POOL_WINDOWS), POOL_GROUP, POOL_GROUP), POOL_GROUP ** -0.5),
        "pool_scale": gain((N_EVEN, POOL_WIDTH)),
        "conv_w": nrm((N_EVEN, SSM_CONV, SSM_CONV_DIM), SSM_CONV ** -0.5),
        "conv_b": nrm((N_EVEN, SSM_CONV_DIM), 0.01),
        "dt_bias": dt0 + jnp.log(-jnp.expm1(-dt0)),
        "a_log": jnp.log(jax.random.uniform(next(ks), (N_EVEN, SSM_HEADS), F32, 1.0, 16.0)),
        "d_skip": gain((N_EVEN, SSM_HEADS)),
        "ssm_norm_w": gain((N_EVEN, SSM_D_INNER)),
        "even_w_out": nrm((N_EVEN, EVEN_MIX, D_MODEL), EVEN_MIX ** -0.5 * DEEPNORM_BETA),
        "odd_w_in": nrm((N_ODD, D_MODEL, ODD_IN), D_MODEL ** -0.5),
        "fgate_b": jax.random.uniform(next(ks), (N_ODD, FOX_HEADS), F32, 1.0, 5.0),
        "q_norm_w": gain((N_ODD, MLA_Q_RANK)),
        "w_uq": nrm((N_ODD, MLA_Q_RANK, MLA_HEADS * (MLA_NOPE + MLA_ROPE)), MLA_Q_RANK ** -0.5),
        "kv_norm_w": gain((N_ODD, MLA_KV_RANK)),
        "w_ukv": nrm((N_ODD, MLA_KV_RANK, MLA_HEADS * (MLA_NOPE + MLA_V)), MLA_KV_RANK ** -0.5),
        "odd_w_out": nrm((N_ODD, ODD_MIX, D_MODEL), ODD_MIX ** -0.5 * DEEPNORM_BETA),
        "ffn_w_gate": nrm((DEPTH, D_MODEL, D_FF), D_MODEL ** -0.5),
        "ffn_w_up": nrm((DEPTH, D_MODEL, D_FF), D_MODEL ** -0.5),
        "ffn_w_down": nrm((DEPTH, D_FF, D_MODEL), D_FF ** -0.5 * DEEPNORM_BETA),
        "ln_mix_g": gain((DEPTH, D_MODEL)),
        "ln_mix_b": nrm((DEPTH, D_MODEL), 0.02),
        "ln_ffn_g": gain((DEPTH, D_MODEL)),
        "ln_ffn_b": nrm((DEPTH, D_MODEL), 0.02),
    }


def _fwd_reference(x, even_w_in, pool_w, pool_scale, conv_w, conv_b, dt_bias, a_log, d_skip, ssm_norm_w,
              even_w_out, odd_w_in, fgate_b, q_norm_w, w_uq, kv_norm_w, w_ukv, odd_w_out,
              ffn_w_gate, ffn_w_up, ffn_w_down, ln_mix_g, ln_mix_b, ln_ffn_g, ln_ffn_b):
    for l in range(DEPTH):
        i = l // 2
        if l % 2 == 0:
            h = even_mixer(x, even_w_in[i], pool_w[i], pool_scale[i], conv_w[i], conv_b[i], dt_bias[i],
                           a_log[i], d_skip[i], ssm_norm_w[i], even_w_out[i])
        else:
            h = odd_mixer(x, odd_w_in[i], fgate_b[i], q_norm_w[i], w_uq[i], kv_norm_w[i], w_ukv[i], odd_w_out[i])
        x = layer_norm(DEEPNORM_ALPHA * x + h, ln_mix_g[l], ln_mix_b[l])
        x = layer_norm(DEEPNORM_ALPHA * x + swiglu(x, ffn_w_gate[l], ffn_w_up[l], ffn_w_down[l]),
                       ln_ffn_g[l], ln_ffn_b[l])
    return x


import jax as _jax
import jax.numpy as _jnp

TWIN_FORMAT = 'train_step'
FWD_PARAMS = ['x', 'even_w_in', 'pool_w', 'pool_scale', 'conv_w', 'conv_b', 'dt_bias', 'a_log', 'd_skip', 'ssm_norm_w', 'even_w_out', 'odd_w_in', 'fgate_b', 'q_norm_w', 'w_uq', 'kv_norm_w', 'w_ukv', 'odd_w_out', 'ffn_w_gate', 'ffn_w_up', 'ffn_w_down', 'ln_mix_g', 'ln_mix_b', 'ln_ffn_g', 'ln_ffn_b']
TWIN_WEIGHTS = ['even_w_in', 'pool_w', 'pool_scale', 'conv_w', 'conv_b', 'dt_bias', 'a_log', 'd_skip', 'ssm_norm_w', 'even_w_out', 'odd_w_in', 'fgate_b', 'q_norm_w', 'w_uq', 'kv_norm_w', 'w_ukv', 'odd_w_out', 'ffn_w_gate', 'ffn_w_up', 'ffn_w_down', 'ln_mix_g', 'ln_mix_b', 'ln_ffn_g', 'ln_ffn_b']
TWIN_DIFF_INPUT = 'x'
TWIN_INPUTS = ['x', 'even_w_in', 'pool_w', 'pool_scale', 'conv_w', 'conv_b', 'dt_bias', 'a_log', 'd_skip', 'ssm_norm_w', 'even_w_out', 'odd_w_in', 'fgate_b', 'q_norm_w', 'w_uq', 'kv_norm_w', 'w_ukv', 'odd_w_out', 'ffn_w_gate', 'ffn_w_up', 'ffn_w_down', 'ln_mix_g', 'ln_mix_b', 'ln_ffn_g', 'ln_ffn_b', 'loss_target', 'm_even_w_in', 'm_pool_w', 'm_pool_scale', 'm_conv_w', 'm_conv_b', 'm_dt_bias', 'm_a_log', 'm_d_skip', 'm_ssm_norm_w', 'm_even_w_out', 'm_odd_w_in', 'm_fgate_b', 'm_q_norm_w', 'm_w_uq', 'm_kv_norm_w', 'm_w_ukv', 'm_odd_w_out', 'm_ffn_w_gate', 'm_ffn_w_up', 'm_ffn_w_down', 'm_ln_mix_g', 'm_ln_mix_b', 'm_ln_ffn_g', 'm_ln_ffn_b', 'v_even_w_in', 'v_pool_w', 'v_pool_scale', 'v_conv_w', 'v_conv_b', 'v_dt_bias', 'v_a_log', 'v_d_skip', 'v_ssm_norm_w', 'v_even_w_out', 'v_odd_w_in', 'v_fgate_b', 'v_q_norm_w', 'v_w_uq', 'v_kv_norm_w', 'v_w_ukv', 'v_odd_w_out', 'v_ffn_w_gate', 'v_ffn_w_up', 'v_ffn_w_down', 'v_ln_mix_g', 'v_ln_mix_b', 'v_ln_ffn_g', 'v_ln_ffn_b']
TWIN_OUTPUTS = ['loss', 'grad_x', 'grad_even_w_in', 'grad_pool_w', 'grad_pool_scale', 'grad_conv_w', 'grad_conv_b', 'grad_dt_bias', 'grad_a_log', 'grad_d_skip', 'grad_ssm_norm_w', 'grad_even_w_out', 'grad_odd_w_in', 'grad_fgate_b', 'grad_q_norm_w', 'grad_w_uq', 'grad_kv_norm_w', 'grad_w_ukv', 'grad_odd_w_out', 'grad_ffn_w_gate', 'grad_ffn_w_up', 'grad_ffn_w_down', 'grad_ln_mix_g', 'grad_ln_mix_b', 'grad_ln_ffn_g', 'grad_ln_ffn_b', 'delta_even_w_in', 'delta_pool_w', 'delta_pool_scale', 'delta_conv_w', 'delta_conv_b', 'delta_dt_bias', 'delta_a_log', 'delta_d_skip', 'delta_ssm_norm_w', 'delta_even_w_out', 'delta_odd_w_in', 'delta_fgate_b', 'delta_q_norm_w', 'delta_w_uq', 'delta_kv_norm_w', 'delta_w_ukv', 'delta_odd_w_out', 'delta_ffn_w_gate', 'delta_ffn_w_up', 'delta_ffn_w_down', 'delta_ln_mix_g', 'delta_ln_mix_b', 'delta_ln_ffn_g', 'delta_ln_ffn_b', 'new_m_even_w_in', 'new_m_pool_w', 'new_m_pool_scale', 'new_m_conv_w', 'new_m_conv_b', 'new_m_dt_bias', 'new_m_a_log', 'new_m_d_skip', 'new_m_ssm_norm_w', 'new_m_even_w_out', 'new_m_odd_w_in', 'new_m_fgate_b', 'new_m_q_norm_w', 'new_m_w_uq', 'new_m_kv_norm_w', 'new_m_w_ukv', 'new_m_odd_w_out', 'new_m_ffn_w_gate', 'new_m_ffn_w_up', 'new_m_ffn_w_down', 'new_m_ln_mix_g', 'new_m_ln_mix_b', 'new_m_ln_ffn_g', 'new_m_ln_ffn_b', 'new_v_even_w_in', 'new_v_pool_w', 'new_v_pool_scale', 'new_v_conv_w', 'new_v_conv_b', 'new_v_dt_bias', 'new_v_a_log', 'new_v_d_skip', 'new_v_ssm_norm_w', 'new_v_even_w_out', 'new_v_odd_w_in', 'new_v_fgate_b', 'new_v_q_norm_w', 'new_v_w_uq', 'new_v_kv_norm_w', 'new_v_w_ukv', 'new_v_odd_w_out', 'new_v_ffn_w_gate', 'new_v_ffn_w_up', 'new_v_ffn_w_down', 'new_v_ln_mix_g', 'new_v_ln_mix_b', 'new_v_ln_ffn_g', 'new_v_ln_ffn_b']
TWIN_LEAF_KINDS = {'loss': 'loss', 'grad_x': 'grad_x', 'grad_even_w_in': 'grad_w', 'grad_pool_w': 'grad_w', 'grad_pool_scale': 'grad_w', 'grad_conv_w': 'grad_w', 'grad_conv_b': 'grad_w', 'grad_dt_bias': 'grad_w', 'grad_a_log': 'grad_w', 'grad_d_skip': 'grad_w', 'grad_ssm_norm_w': 'grad_w', 'grad_even_w_out': 'grad_w', 'grad_odd_w_in': 'grad_w', 'grad_fgate_b': 'grad_w', 'grad_q_norm_w': 'grad_w', 'grad_w_uq': 'grad_w', 'grad_kv_norm_w': 'grad_w', 'grad_w_ukv': 'grad_w', 'grad_odd_w_out': 'grad_w', 'grad_ffn_w_gate': 'grad_w', 'grad_ffn_w_up': 'grad_w', 'grad_ffn_w_down': 'grad_w', 'grad_ln_mix_g': 'grad_w', 'grad_ln_mix_b': 'grad_w', 'grad_ln_ffn_g': 'grad_w', 'grad_ln_ffn_b': 'grad_w', 'delta_even_w_in': 'delta_w', 'delta_pool_w': 'delta_w', 'delta_pool_scale': 'delta_w', 'delta_conv_w': 'delta_w', 'delta_conv_b': 'delta_w', 'delta_dt_bias': 'delta_w', 'delta_a_log': 'delta_w', 'delta_d_skip': 'delta_w', 'delta_ssm_norm_w': 'delta_w', 'delta_even_w_out': 'delta_w', 'delta_odd_w_in': 'delta_w', 'delta_fgate_b': 'delta_w', 'delta_q_norm_w': 'delta_w', 'delta_w_uq': 'delta_w', 'delta_kv_norm_w': 'delta_w', 'delta_w_ukv': 'delta_w', 'delta_odd_w_out': 'delta_w', 'delta_ffn_w_gate': 'delta_w', 'delta_ffn_w_up': 'delta_w', 'delta_ffn_w_down': 'delta_w', 'delta_ln_mix_g': 'delta_w', 'delta_ln_mix_b': 'delta_w', 'delta_ln_ffn_g': 'delta_w', 'delta_ln_ffn_b': 'delta_w', 'new_m_even_w_in': 'new_m', 'new_m_pool_w': 'new_m', 'new_m_pool_scale': 'new_m', 'new_m_conv_w': 'new_m', 'new_m_conv_b': 'new_m', 'new_m_dt_bias': 'new_m', 'new_m_a_log': 'new_m', 'new_m_d_skip': 'new_m', 'new_m_ssm_norm_w': 'new_m', 'new_m_even_w_out': 'new_m', 'new_m_odd_w_in': 'new_m', 'new_m_fgate_b': 'new_m', 'new_m_q_norm_w': 'new_m', 'new_m_w_uq': 'new_m', 'new_m_kv_norm_w': 'new_m', 'new_m_w_ukv': 'new_m', 'new_m_odd_w_out': 'new_m', 'new_m_ffn_w_gate': 'new_m', 'new_m_ffn_w_up': 'new_m', 'new_m_ffn_w_down': 'new_m', 'new_m_ln_mix_g': 'new_m', 'new_m_ln_mix_b': 'new_m', 'new_m_ln_ffn_g': 'new_m', 'new_m_ln_ffn_b': 'new_m', 'new_v_even_w_in': 'new_v', 'new_v_pool_w': 'new_v', 'new_v_pool_scale': 'new_v', 'new_v_conv_w': 'new_v', 'new_v_conv_b': 'new_v', 'new_v_dt_bias': 'new_v', 'new_v_a_log': 'new_v', 'new_v_d_skip': 'new_v', 'new_v_ssm_norm_w': 'new_v', 'new_v_even_w_out': 'new_v', 'new_v_odd_w_in': 'new_v', 'new_v_fgate_b': 'new_v', 'new_v_q_norm_w': 'new_v', 'new_v_w_uq': 'new_v', 'new_v_kv_norm_w': 'new_v', 'new_v_w_ukv': 'new_v', 'new_v_odd_w_out': 'new_v', 'new_v_ffn_w_gate': 'new_v', 'new_v_ffn_w_up': 'new_v', 'new_v_ffn_w_down': 'new_v', 'new_v_ln_mix_g': 'new_v', 'new_v_ln_mix_b': 'new_v', 'new_v_ln_ffn_g': 'new_v', 'new_v_ln_ffn_b': 'new_v'}


def _forward(args):
    return _fwd_reference(*[args[k] for k in FWD_PARAMS])


def _output_shape():
    def fwd():
        inp = _fwd_setup_inputs(0)
        return _fwd_reference(*[inp[k] for k in FWD_PARAMS])
    out = _jax.eval_shape(fwd)
    return out.shape, out.dtype

N_MICROBATCH = 1
ADAM_LR = 0.001
ADAM_B1 = 0.9
ADAM_B2 = 0.999
ADAM_EPS = 1e-08
ADAM_WD = 0.01
ADAM_STEP = 10
PER_EXAMPLE_BATCH_AXIS = {'x': 0, 'loss_target': 0}
SHARED_INPUTS = []
_WEIGHT_DTYPES = {'even_w_in': _jnp.float32, 'pool_w': _jnp.float32, 'pool_scale': _jnp.float32, 'conv_w': _jnp.float32, 'conv_b': _jnp.float32, 'dt_bias': _jnp.float32, 'a_log': _jnp.float32, 'd_skip': _jnp.float32, 'ssm_norm_w': _jnp.float32, 'even_w_out': _jnp.float32, 'odd_w_in': _jnp.float32, 'fgate_b': _jnp.float32, 'q_norm_w': _jnp.float32, 'w_uq': _jnp.float32, 'kv_norm_w': _jnp.float32, 'w_ukv': _jnp.float32, 'odd_w_out': _jnp.float32, 'ffn_w_gate': _jnp.float32, 'ffn_w_up': _jnp.float32, 'ffn_w_down': _jnp.float32, 'ln_mix_g': _jnp.float32, 'ln_mix_b': _jnp.float32, 'ln_ffn_g': _jnp.float32, 'ln_ffn_b': _jnp.float32}
MOMENT_SCALE = {'even_w_in': 5.389352e-02, 'pool_w': 6.159194e-02, 'pool_scale': 6.535865e-02, 'conv_w': 5.254207e-02, 'conv_b': 1.067312e-01, 'dt_bias': 1.393352e-01, 'a_log': 6.446153e-01, 'd_skip': 3.038609e-01, 'ssm_norm_w': 6.906074e-02, 'even_w_out': 1.993651e-01, 'odd_w_in': 1.970757e-02, 'fgate_b': 8.991501e-02, 'q_norm_w': 1.234365e-02, 'w_uq': 1.006226e-02, 'kv_norm_w': 2.709134e-02, 'w_ukv': 1.344111e-02, 'odd_w_out': 4.857928e-02, 'ffn_w_gate': 2.475202e-02, 'ffn_w_up': 2.520670e-02, 'ffn_w_down': 9.974515e-02, 'ln_mix_g': 1.373922e+01, 'ln_mix_b': 1.354029e+00, 'ln_ffn_g': 3.820147e+01, 'ln_ffn_b': 2.508134e+00}


def _to_microbatches(a, axis):
    t = _jnp.moveaxis(a, axis, 0)
    t = t.reshape((N_MICROBATCH, t.shape[0] // N_MICROBATCH) + t.shape[1:])
    return _jnp.moveaxis(t, 1, axis + 1)


def setup_inputs(seed: int = 0) -> dict:
    inp = _fwd_setup_inputs(seed)
    key = _jax.random.fold_in(_jax.random.key(seed), 7919)
    shape, _ = _output_shape()
    out = dict(inp)
    out["loss_target"] = _jax.random.normal(_jax.random.fold_in(key, 0), shape, _jnp.float32)
    for i, name in enumerate(TWIN_WEIGHTS):
        w = inp[name].astype(_jnp.float32)
        if MOMENT_SCALE is None:
            s = _jnp.sqrt(_jnp.mean(_jnp.square(w)) + 1e-30)
        else:
            s = MOMENT_SCALE[name]
        km, kv = _jax.random.split(_jax.random.fold_in(key, i + 1))
        out[name] = w
        out["m_" + name] = s * _jax.random.normal(km, w.shape, _jnp.float32)
        out["v_" + name] = (s * s) * _jax.random.uniform(kv, w.shape, _jnp.float32, 0.5, 1.5)
    if N_MICROBATCH > 1:
        for name, axis in PER_EXAMPLE_BATCH_AXIS.items():
            out[name] = _to_microbatches(out[name], axis)
    return {'x': out['x'], 'even_w_in': out['even_w_in'], 'pool_w': out['pool_w'], 'pool_scale': out['pool_scale'], 'conv_w': out['conv_w'], 'conv_b': out['conv_b'], 'dt_bias': out['dt_bias'], 'a_log': out['a_log'], 'd_skip': out['d_skip'], 'ssm_norm_w': out['ssm_norm_w'], 'even_w_out': out['even_w_out'], 'odd_w_in': out['odd_w_in'], 'fgate_b': out['fgate_b'], 'q_norm_w': out['q_norm_w'], 'w_uq': out['w_uq'], 'kv_norm_w': out['kv_norm_w'], 'w_ukv': out['w_ukv'], 'odd_w_out': out['odd_w_out'], 'ffn_w_gate': out['ffn_w_gate'], 'ffn_w_up': out['ffn_w_up'], 'ffn_w_down': out['ffn_w_down'], 'ln_mix_g': out['ln_mix_g'], 'ln_mix_b': out['ln_mix_b'], 'ln_ffn_g': out['ln_ffn_g'], 'ln_ffn_b': out['ln_ffn_b'], 'loss_target': out['loss_target'], 'm_even_w_in': out['m_even_w_in'], 'm_pool_w': out['m_pool_w'], 'm_pool_scale': out['m_pool_scale'], 'm_conv_w': out['m_conv_w'], 'm_conv_b': out['m_conv_b'], 'm_dt_bias': out['m_dt_bias'], 'm_a_log': out['m_a_log'], 'm_d_skip': out['m_d_skip'], 'm_ssm_norm_w': out['m_ssm_norm_w'], 'm_even_w_out': out['m_even_w_out'], 'm_odd_w_in': out['m_odd_w_in'], 'm_fgate_b': out['m_fgate_b'], 'm_q_norm_w': out['m_q_norm_w'], 'm_w_uq': out['m_w_uq'], 'm_kv_norm_w': out['m_kv_norm_w'], 'm_w_ukv': out['m_w_ukv'], 'm_odd_w_out': out['m_odd_w_out'], 'm_ffn_w_gate': out['m_ffn_w_gate'], 'm_ffn_w_up': out['m_ffn_w_up'], 'm_ffn_w_down': out['m_ffn_w_down'], 'm_ln_mix_g': out['m_ln_mix_g'], 'm_ln_mix_b': out['m_ln_mix_b'], 'm_ln_ffn_g': out['m_ln_ffn_g'], 'm_ln_ffn_b': out['m_ln_ffn_b'], 'v_even_w_in': out['v_even_w_in'], 'v_pool_w': out['v_pool_w'], 'v_pool_scale': out['v_pool_scale'], 'v_conv_w': out['v_conv_w'], 'v_conv_b': out['v_conv_b'], 'v_dt_bias': out['v_dt_bias'], 'v_a_log': out['v_a_log'], 'v_d_skip': out['v_d_skip'], 'v_ssm_norm_w': out['v_ssm_norm_w'], 'v_even_w_out': out['v_even_w_out'], 'v_odd_w_in': out['v_odd_w_in'], 'v_fgate_b': out['v_fgate_b'], 'v_q_norm_w': out['v_q_norm_w'], 'v_w_uq': out['v_w_uq'], 'v_kv_norm_w': out['v_kv_norm_w'], 'v_w_ukv': out['v_w_ukv'], 'v_odd_w_out': out['v_odd_w_out'], 'v_ffn_w_gate': out['v_ffn_w_gate'], 'v_ffn_w_up': out['v_ffn_w_up'], 'v_ffn_w_down': out['v_ffn_w_down'], 'v_ln_mix_g': out['v_ln_mix_g'], 'v_ln_mix_b': out['v_ln_mix_b'], 'v_ln_ffn_g': out['v_ln_ffn_g'], 'v_ln_ffn_b': out['v_ln_ffn_b']}


def _loss(weights, diff, rest, loss_target):
    with _jax.named_scope("forward"):
        args = {**rest, TWIN_DIFF_INPUT: diff, **{k: w.astype(_WEIGHT_DTYPES[k]) for k, w in weights.items()}}
        y = _forward(args)
    with _jax.named_scope("loss_head"):
        err = _jnp.square(y.astype(_jnp.float32) - loss_target)
        return 0.5 * _jnp.sum(_jnp.mean(err, axis=-1)) if err.ndim else 0.5 * err


def _adamw(w, g, m, v):
    m = ADAM_B1 * m + (1.0 - ADAM_B1) * g
    v = ADAM_B2 * v + (1.0 - ADAM_B2) * _jnp.square(g)
    m_hat = m / (1.0 - ADAM_B1 ** ADAM_STEP)
    v_hat = v / (1.0 - ADAM_B2 ** ADAM_STEP)
    delta = -ADAM_LR * (m_hat / (_jnp.sqrt(v_hat) + ADAM_EPS) + ADAM_WD * w)
    return delta, m, v


def reference(x, even_w_in, pool_w, pool_scale, conv_w, conv_b, dt_bias, a_log, d_skip, ssm_norm_w, even_w_out, odd_w_in, fgate_b, q_norm_w, w_uq, kv_norm_w, w_ukv, odd_w_out, ffn_w_gate, ffn_w_up, ffn_w_down, ln_mix_g, ln_mix_b, ln_ffn_g, ln_ffn_b, loss_target, m_even_w_in, m_pool_w, m_pool_scale, m_conv_w, m_conv_b, m_dt_bias, m_a_log, m_d_skip, m_ssm_norm_w, m_even_w_out, m_odd_w_in, m_fgate_b, m_q_norm_w, m_w_uq, m_kv_norm_w, m_w_ukv, m_odd_w_out, m_ffn_w_gate, m_ffn_w_up, m_ffn_w_down, m_ln_mix_g, m_ln_mix_b, m_ln_ffn_g, m_ln_ffn_b, v_even_w_in, v_pool_w, v_pool_scale, v_conv_w, v_conv_b, v_dt_bias, v_a_log, v_d_skip, v_ssm_norm_w, v_even_w_out, v_odd_w_in, v_fgate_b, v_q_norm_w, v_w_uq, v_kv_norm_w, v_w_ukv, v_odd_w_out, v_ffn_w_gate, v_ffn_w_up, v_ffn_w_down, v_ln_mix_g, v_ln_mix_b, v_ln_ffn_g, v_ln_ffn_b):
    given = dict(x=x, even_w_in=even_w_in, pool_w=pool_w, pool_scale=pool_scale, conv_w=conv_w, conv_b=conv_b, dt_bias=dt_bias, a_log=a_log, d_skip=d_skip, ssm_norm_w=ssm_norm_w, even_w_out=even_w_out, odd_w_in=odd_w_in, fgate_b=fgate_b, q_norm_w=q_norm_w, w_uq=w_uq, kv_norm_w=kv_norm_w, w_ukv=w_ukv, odd_w_out=odd_w_out, ffn_w_gate=ffn_w_gate, ffn_w_up=ffn_w_up, ffn_w_down=ffn_w_down, ln_mix_g=ln_mix_g, ln_mix_b=ln_mix_b, ln_ffn_g=ln_ffn_g, ln_ffn_b=ln_ffn_b, loss_target=loss_target, m_even_w_in=m_even_w_in, m_pool_w=m_pool_w, m_pool_scale=m_pool_scale, m_conv_w=m_conv_w, m_conv_b=m_conv_b, m_dt_bias=m_dt_bias, m_a_log=m_a_log, m_d_skip=m_d_skip, m_ssm_norm_w=m_ssm_norm_w, m_even_w_out=m_even_w_out, m_odd_w_in=m_odd_w_in, m_fgate_b=m_fgate_b, m_q_norm_w=m_q_norm_w, m_w_uq=m_w_uq, m_kv_norm_w=m_kv_norm_w, m_w_ukv=m_w_ukv, m_odd_w_out=m_odd_w_out, m_ffn_w_gate=m_ffn_w_gate, m_ffn_w_up=m_ffn_w_up, m_ffn_w_down=m_ffn_w_down, m_ln_mix_g=m_ln_mix_g, m_ln_mix_b=m_ln_mix_b, m_ln_ffn_g=m_ln_ffn_g, m_ln_ffn_b=m_ln_ffn_b, v_even_w_in=v_even_w_in, v_pool_w=v_pool_w, v_pool_scale=v_pool_scale, v_conv_w=v_conv_w, v_conv_b=v_conv_b, v_dt_bias=v_dt_bias, v_a_log=v_a_log, v_d_skip=v_d_skip, v_ssm_norm_w=v_ssm_norm_w, v_even_w_out=v_even_w_out, v_odd_w_in=v_odd_w_in, v_fgate_b=v_fgate_b, v_q_norm_w=v_q_norm_w, v_w_uq=v_w_uq, v_kv_norm_w=v_kv_norm_w, v_w_ukv=v_w_ukv, v_odd_w_out=v_odd_w_out, v_ffn_w_gate=v_ffn_w_gate, v_ffn_w_up=v_ffn_w_up, v_ffn_w_down=v_ffn_w_down, v_ln_mix_g=v_ln_mix_g, v_ln_mix_b=v_ln_mix_b, v_ln_ffn_g=v_ln_ffn_g, v_ln_ffn_b=v_ln_ffn_b)
    weights = {n: given[n] for n in TWIN_WEIGHTS}
    shared = {n: given[n] for n in SHARED_INPUTS}
    per_example = {n: given[n] for n in ['x']}
    grad_fn = _jax.value_and_grad(_loss, argnums=(0, 1))

    def one_microbatch(ex, loss_target):
        ex = dict(ex)
        diff = ex.pop(TWIN_DIFF_INPUT)
        return grad_fn(weights, diff, {**shared, **ex}, loss_target)

    if N_MICROBATCH == 1:
        loss, (grad_w, grad_x) = one_microbatch(per_example, given["loss_target"])
    else:
        def body(carry, xs):
            loss_sum, grad_sum = carry
            l_k, (gw_k, gx_k) = one_microbatch(xs[0], xs[1])
            with _jax.named_scope("update"):
                return (loss_sum + l_k, _jax.tree.map(_jnp.add, grad_sum, gw_k)), gx_k

        init = (_jnp.zeros((), _jnp.float32), _jax.tree.map(_jnp.zeros_like, weights))
        (loss, grad_w), grad_x = _jax.lax.scan(body, init, (per_example, given["loss_target"]))
    with _jax.named_scope("update"):
        delta_w, new_m, new_v = {}, {}, {}
        for n in TWIN_WEIGHTS:
            delta_w[n], new_m[n], new_v[n] = _adamw(weights[n], grad_w[n], given["m_" + n], given["v_" + n])
    return (loss, grad_x, *[grad_w[n] for n in TWIN_WEIGHTS], *[delta_w[n] for n in TWIN_WEIGHTS],
            *[new_m[n] for n in TWIN_WEIGHTS], *[new_v[n] for n in TWIN_WEIGHTS])
```

```python
import functools
import math

import jax
import jax.numpy as jnp
import numpy as np
from jax import lax
from jax.experimental import pallas as pl
from jax.experimental.pallas import tpu as pltpu

F32 = jnp.float32
BF16 = jnp.bfloat16
HI = lax.Precision.HIGHEST

N_DEV = 8
D_MODEL = 1024
DEPTH = 4
POOL_WINDOWS = (2, 4, 8, 16)
POOL_GROUP = 128
POOL_WIDTH = 512
SSM_D_INNER = 1024
SSM_HEAD_DIM = 64
SSM_HEADS = 16
SSM_GROUPS = 2
SSM_STATE = 128
SSM_CONV = 4
SSM_CHUNK = 128
SSM_CONV_DIM = 1536
EVEN_IN = 3088
EVEN_IN_PAD = 3200
FOX_HEADS = 8
FOX_WIDTH = 512
MLA_HEADS = 8
MLA_NOPE = 64
MLA_ROPE = 32
MLA_V = 64
MLA_Q_RANK = 512
MLA_KV_RANK = 256
MLA_DK_PAD = 128
ROPE_THETA = 10000.0
ODD_IN = 2344
ODD_IN_PAD = 2560
D_FF = 2816
ALPHA = (2 * DEPTH) ** 0.25
LN_EPS = 1e-5
RMS_EPS = 1e-6
ADAM_LR = 0.001
ADAM_B1 = 0.9
ADAM_B2 = 0.999
ADAM_EPS = 1e-08
ADAM_WD = 0.01
ADAM_STEP = 10
NEG = -1e30
VMEM_LIMIT = 48 * 1024 * 1024


def _cp(sem):
    return pltpu.CompilerParams(dimension_semantics=sem, vmem_limit_bytes=VMEM_LIMIT)


def _dot(a, b, ca, cb, prec=None):
    return lax.dot_general(a, b, (((ca,), (cb,)), ((), ())), preferred_element_type=F32, precision=prec)


def _sigmoid(x):
    return 1.0 / (1.0 + jnp.exp(-x))


def _softplus(x):
    return jnp.maximum(x, 0.0) + jnp.log(1.0 + jnp.exp(-jnp.abs(x)))


MESH = pl.DeviceIdType.MESH
HBM_SPEC = pl.BlockSpec(memory_space=pltpu.HBM)


def _all_gather(name, xs):
    r, c_ = xs.shape

    def body(x_ref, out_ref, send_sems, recv_sems, local_sem):
        x, y, c = lax.axis_index("x"), lax.axis_index("y"), lax.axis_index("c")
        me, sibling = (x, y, c), (x, y, 1 - c)
        chips = [(1 - x, y), (x, 1 - y), (1 - x, 1 - y)]

        def rows(px, py, pc):
            return out_ref.at[4 * px + 2 * py + pc]

        def copy(k, block, to, src=None):
            return pltpu.make_async_remote_copy(
                src_ref=rows(*block) if src is None else src,
                dst_ref=rows(*block),
                send_sem=send_sems.at[k],
                recv_sem=recv_sems.at[k],
                device_id=to,
                device_id_type=MESH,
            )

        mine = pltpu.make_async_copy(x_ref, rows(*me), local_sem)
        mine.start()
        first = [copy(0, me, sibling, src=x_ref)]
        first += [copy(1 + j, me, (*chip, c), src=x_ref) for j, chip in enumerate(chips)]
        for cp in first:
            cp.start()
        passed = [copy(4 + j, (*chip, c), sibling) for j, chip in enumerate(chips)]
        for j, chip in enumerate(chips):
            copy(1 + j, (*chip, c), me).wait_recv()
            passed[j].start()
        copy(0, sibling, me).wait_recv()
        for j, chip in enumerate(chips):
            copy(4 + j, (*chip, 1 - c), me).wait_recv()
        for cp in first + passed:
            cp.wait_send()
        mine.wait()

    return pl.pallas_call(
        body,
        name=name,
        out_shape=jax.ShapeDtypeStruct((N_DEV, r, c_), xs.dtype),
        in_specs=[HBM_SPEC],
        out_specs=HBM_SPEC,
        scratch_shapes=[pltpu.SemaphoreType.DMA((7,)), pltpu.SemaphoreType.DMA((7,)), pltpu.SemaphoreType.DMA(())],
    )(xs)


def _all_to_all(name, send):
    _, r, c_ = send.shape

    def body(s_ref, r_ref, send_sems, recv_sems, local_sem):
        x, y, c = lax.axis_index("x"), lax.axis_index("y"), lax.axis_index("c")
        me = 4 * x + 2 * y + c
        mine = pltpu.make_async_copy(s_ref.at[me], r_ref.at[me], local_sem)
        mine.start()
        copies = []
        for k in range(1, N_DEV):
            tx = 1 - x if k & 4 else x
            ty = 1 - y if k & 2 else y
            tc = 1 - c if k & 1 else c
            peer = 4 * tx + 2 * ty + tc
            cp = pltpu.make_async_remote_copy(
                src_ref=s_ref.at[peer],
                dst_ref=r_ref.at[me],
                send_sem=send_sems.at[k - 1],
                recv_sem=recv_sems.at[k - 1],
                device_id=(tx, ty, tc),
                device_id_type=MESH,
            )
            cp.start()
            landing = pltpu.make_async_remote_copy(
                src_ref=s_ref.at[me],
                dst_ref=r_ref.at[peer],
                send_sem=send_sems.at[k - 1],
                recv_sem=recv_sems.at[k - 1],
                device_id=(tx, ty, tc),
                device_id_type=MESH,
            )
            copies.append((cp, landing))
        for cp, landing in copies:
            landing.wait_recv()
        for cp, landing in copies:
            cp.wait_send()
        mine.wait()

    return pl.pallas_call(
        body,
        name=name,
        out_shape=jax.ShapeDtypeStruct(send.shape, send.dtype),
        in_specs=[HBM_SPEC],
        out_specs=HBM_SPEC,
        scratch_shapes=[pltpu.SemaphoreType.DMA((7,)), pltpu.SemaphoreType.DMA((7,)), pltpu.SemaphoreType.DMA(())],
    )(send)


def _pick(n, cands):
    for t in cands:
        if n % t == 0:
            return t
    return n


def _mm(name, a, b, mode, out_dtype):
    if mode == "nn":
        (m, k), n = a.shape, b.shape[1]
    elif mode == "nt":
        (m, k), n = a.shape, b.shape[0]
    else:
        (k, m), n = a.shape, b.shape[1]
    tm = _pick(m, (512, 256, 128))
    tn = _pick(n, (1024, 768, 640, 512, 384, 256, 128))
    tk = _pick(k, (1024, 1408, 768, 640, 512, 256, 128))
    nk = k // tk
    if mode == "nn":
        a_spec = pl.BlockSpec((tm, tk), lambda i, j, kk: (i, kk))
        b_spec = pl.BlockSpec((tk, tn), lambda i, j, kk: (kk, j))
        ca, cb = 1, 0
    elif mode == "nt":
        a_spec = pl.BlockSpec((tm, tk), lambda i, j, kk: (i, kk))
        b_spec = pl.BlockSpec((tn, tk), lambda i, j, kk: (j, kk))
        ca, cb = 1, 1
    else:
        a_spec = pl.BlockSpec((tk, tm), lambda i, j, kk: (kk, i))
        b_spec = pl.BlockSpec((tk, tn), lambda i, j, kk: (kk, j))
        ca, cb = 0, 0

    def kern(a_ref, b_ref, o_ref, acc):
        kk = pl.program_id(2)

        @pl.when(kk == 0)
        def _():
            acc[...] = jnp.zeros_like(acc)

        acc[...] += _dot(a_ref[...].astype(BF16), b_ref[...].astype(BF16), ca, cb)

        @pl.when(kk == nk - 1)
        def _():
            o_ref[...] = acc[...].astype(out_dtype)

    return pl.pallas_call(
        kern,
        name=name,
        grid=(m // tm, n // tn, nk),
        in_specs=[a_spec, b_spec],
        out_specs=pl.BlockSpec((tm, tn), lambda i, j, kk: (i, j)),
        out_shape=jax.ShapeDtypeStruct((m, n), out_dtype),
        scratch_shapes=[pltpu.VMEM((tm, tn), F32)],
        compiler_params=_cp(("parallel", "parallel", "arbitrary")),
    )(a, b)


def _rowwise(name, body, ins, outs, tm):
    n_rows = next(a.shape[0] for a, kind in ins if kind == "row")
    n = n_rows // tm
    in_specs = []
    for a, kind in ins:
        if kind == "full":
            in_specs.append(pl.BlockSpec(a.shape, lambda i, nd=a.ndim: (0,) * nd))
        elif kind == "row":
            in_specs.append(pl.BlockSpec((tm, a.shape[1]), lambda i: (i, 0)))
        elif kind == "prev":
            in_specs.append(pl.BlockSpec((tm, a.shape[1]), lambda i: (jnp.maximum(i - 1, 0), 0)))
        else:
            in_specs.append(pl.BlockSpec((tm, a.shape[1]), lambda i: (jnp.minimum(i + 1, n - 1), 0)))
    out_specs, out_shape = [], []
    for shp, dt, kind in outs:
        out_shape.append(jax.ShapeDtypeStruct(shp, dt))
        if kind == "row":
            out_specs.append(pl.BlockSpec((tm, shp[1]), lambda i: (i, 0)))
        else:
            out_specs.append(pl.BlockSpec(shp, lambda i, nd=len(shp): (0,) * nd))
    n_in = len(ins)

    def kern(*refs):
        i = pl.program_id(0)
        res = body(i, n, *[r if kind == "full" else r[...] for r, (_, kind) in zip(refs[:n_in], ins)])
        for (shp, dt, kind), val, o in zip(outs, res, refs[n_in:]):
            if kind == "row":
                o[...] = val.astype(dt)
            else:

                @pl.when(i == 0)
                def _(o=o):
                    o[...] = jnp.zeros_like(o)

                o[...] += val.astype(dt)

    return pl.pallas_call(
        kern,
        name=name,
        grid=(n,),
        in_specs=in_specs,
        out_specs=out_specs,
        out_shape=out_shape,
        compiler_params=_cp(("arbitrary",)),
    )(*[a for a, _ in ins])


def _tm(s, t):
    return min(s, t)


def _ln_fwd(x, h, g, b):
    s = x.shape[0]

    def body(i, n, xv, hv, gv, bv):
        r = ALPHA * xv + hv.astype(F32)
        mu = jnp.mean(r, axis=-1, keepdims=True)
        d = r - mu
        var = jnp.mean(d * d, axis=-1, keepdims=True)
        y = d * lax.rsqrt(var + LN_EPS) * gv[...] + bv[...]
        return y, y, r

    shp = (s, D_MODEL)
    return _rowwise("ln_fwd", body, [(x, "row"), (h, "row"), (g, "full"), (b, "full")],
                    [(shp, F32, "row"), (shp, BF16, "row"), (shp, F32, "row")], _tm(s, 256))


def _ln_bwd(a, bterm, r, g):
    s = r.shape[0]

    def body(i, n, *vals):
        if a is None:
            dyv, rv, gv = vals
        else:
            av, dyv, rv, gv = vals
            dyv = ALPHA * av + dyv
        mu = jnp.mean(rv, axis=-1, keepdims=True)
        d = rv - mu
        var = jnp.mean(d * d, axis=-1, keepdims=True)
        rstd = lax.rsqrt(var + LN_EPS)
        xhat = d * rstd
        dxh = dyv * gv[...]
        dr = rstd * (dxh - jnp.mean(dxh, axis=-1, keepdims=True) - xhat * jnp.mean(dxh * xhat, axis=-1, keepdims=True))
        return dr, dr, jnp.sum(dyv * xhat, axis=0, keepdims=True), jnp.sum(dyv, axis=0, keepdims=True)

    ins = ([] if a is None else [(a, "row")]) + [(bterm, "row"), (r, "row"), (g, "full")]
    shp = (s, D_MODEL)
    return _rowwise("ln_bwd" if a is None else "ln_bwd_res", body, ins,
                    [(shp, F32, "row"), (shp, BF16, "row"), ((1, D_MODEL), F32, "acc"), ((1, D_MODEL), F32, "acc")],
                    _tm(s, 256))


def _axpy(a, b):
    def body(i, n, av, bv):
        return (ALPHA * av + bv,)

    return _rowwise("axpy", body, [(a, "row"), (b, "row")], [(a.shape, F32, "row")], _tm(a.shape[0], 256))[0]


def _loss_head(y, target):
    s = y.shape[0]

    def body(i, n, yv, tv):
        err = yv - tv
        part = 0.5 * jnp.sum(jnp.mean(err * err, axis=-1, keepdims=True), axis=0, keepdims=True)
        return err * (1.0 / D_MODEL), part

    return _rowwise("loss_head", body, [(y, "row"), (target, "row")],
                    [((s, D_MODEL), F32, "row"), ((1, 1), F32, "acc")], _tm(s, 256))


def _swiglu_fwd(gu):
    s = gu.shape[0]

    def body(i, n, v):
        g, u = v[:, :D_FF], v[:, D_FF:]
        return (g * _sigmoid(g) * u,)

    return _rowwise("swiglu_fwd", body, [(gu, "row")], [((s, D_FF), BF16, "row")], _tm(s, 256))[0]


def _swiglu_bwd(da, gu):
    s = gu.shape[0]

    def body(i, n, dav, v):
        g, u = v[:, :D_FF], v[:, D_FF:]
        sg = _sigmoid(g)
        dg = dav * u * (sg * (1.0 + g * (1.0 - sg)))
        du = dav * (g * sg)
        return (jnp.concatenate([dg, du], axis=1),)

    return _rowwise("swiglu_bwd", body, [(da, "row"), (gu, "row")], [((s, 2 * D_FF), BF16, "row")], _tm(s, 256))[0]


def _rms_fwd(x, w):
    s, c = x.shape

    def body(i, n, xv, wv):
        rs = lax.rsqrt(jnp.mean(xv * xv, axis=-1, keepdims=True) + RMS_EPS)
        return (xv * rs * wv[...],)

    return _rowwise("rms_fwd", body, [(x, "row"), (w, "full")], [((s, c), BF16, "row")], _tm(s, 512))[0]


def _rms_bwd(dy, x, w):
    s, c = x.shape

    def body(i, n, dyv, xv, wv):
        rs = lax.rsqrt(jnp.mean(xv * xv, axis=-1, keepdims=True) + RMS_EPS)
        nv = xv * rs
        dn = dyv * wv[...]
        dx = rs * (dn - nv * jnp.mean(dn * nv, axis=-1, keepdims=True))
        return dx, jnp.sum(dyv * nv, axis=0, keepdims=True)

    return _rowwise("rms_bwd", body, [(dy, "row"), (x, "row"), (w, "full")],
                    [((s, c), F32, "row"), ((1, c), F32, "acc")], _tm(s, 512))


def _gated_rms_fwd(y, z, w):
    s, c = y.shape

    def body(i, n, yv, zv, wv):
        y2 = yv * (zv * _sigmoid(zv))
        rs = lax.rsqrt(jnp.mean(y2 * y2, axis=-1, keepdims=True) + RMS_EPS)
        return (y2 * rs * wv[...],)

    return _rowwise("gated_rms_fwd", body, [(y, "row"), (z, "row"), (w, "full")], [((s, c), BF16, "row")], _tm(s, 256))[0]


def _gated_rms_bwd(do, y, z, w):
    s, c = y.shape

    def body(i, n, dov, yv, zv, wv):
        sz = _sigmoid(zv)
        silu = zv * sz
        y2 = yv * silu
        rs = lax.rsqrt(jnp.mean(y2 * y2, axis=-1, keepdims=True) + RMS_EPS)
        nv = y2 * rs
        dn = dov * wv[...]
        dy2 = rs * (dn - nv * jnp.mean(dn * nv, axis=-1, keepdims=True))
        return dy2 * silu, dy2 * yv * (sz * (1.0 + zv * (1.0 - sz))), jnp.sum(dov * nv, axis=0, keepdims=True)

    return _rowwise("gated_rms_bwd", body, [(do, "row"), (y, "row"), (z, "row"), (w, "full")],
                    [((s, c), F32, "row"), ((s, c), F32, "row"), ((1, c), F32, "acc")], _tm(s, 256))


def _rope(name, r1, r2, cos, sin):
    def body(i, n, a, b, cv, sv):
        return a * cv - b * sv, b * cv + a * sv

    return _rowwise(name, body, [(r1, "row"), (r2, "row"), (cos, "row"), (sin, "row")],
                    [(r1.shape, F32, "row"), (r1.shape, F32, "row")], _tm(r1.shape[0], 512))


def _headsum_rope_bwd(d1_all, d2_all, cos, sin):
    s = d1_all.shape[0]
    half = MLA_ROPE // 2

    def body(i, n, a_all, b_all, cv, sv):
        rr = lax.broadcasted_iota(jnp.int32, (MLA_HEADS * half, half), 0)
        cc = lax.broadcasted_iota(jnp.int32, (MLA_HEADS * half, half), 1)
        sel = jnp.where(rr % half == cc, 1.0, 0.0).astype(F32)
        a = _dot(a_all, sel, 1, 0, HI)
        b = _dot(b_all, sel, 1, 0, HI)
        return a * cv + b * sv, b * cv - a * sv

    return _rowwise("headsum_rope_bwd", body, [(d1_all, "row"), (d2_all, "row"), (cos, "row"), (sin, "row")],
                    [((s, half), F32, "row"), ((s, half), F32, "row")], _tm(s, 512))


def _softplus_fwd(dtr, bias):
    def body(i, n, v, bv):
        return (_softplus(v + bv[...]),)

    return _rowwise("softplus_fwd", body, [(dtr, "row"), (bias, "full")], [(dtr.shape, F32, "row")], _tm(dtr.shape[0], 1024))[0]


def _softplus_bwd(ddt, dtr, bias):
    def body(i, n, dv, v, bv):
        d = dv * _sigmoid(v + bv[...])
        return d, jnp.sum(d, axis=0, keepdims=True)

    return _rowwise("softplus_bwd", body, [(ddt, "row"), (dtr, "row"), (bias, "full")],
                    [(dtr.shape, F32, "row"), ((1, dtr.shape[1]), F32, "acc")], _tm(dtr.shape[0], 1024))


def _rowdot(a, b):
    def body(i, n, av, bv):
        return (jnp.sum(av.astype(F32) * bv.astype(F32), axis=-1, keepdims=True),)

    return _rowwise("rowdot", body, [(a, "row"), (b, "row")], [((a.shape[0], 1), F32, "row")], _tm(a.shape[0], 2048))[0]


def _sum8(name, blocks):
    _, r, c = blocks.shape
    tr = _pick(r, (512, 256, 128, 64, 32, 16, 8))

    def kern(b_ref, o_ref):
        acc = b_ref[0].astype(F32)
        for d in range(1, N_DEV):
            acc = acc + b_ref[d].astype(F32)
        o_ref[...] = acc

    return pl.pallas_call(
        kern,
        name=name,
        grid=(r // tr,),
        in_specs=[pl.BlockSpec((N_DEV, tr, c), lambda i: (0, i, 0))],
        out_specs=pl.BlockSpec((tr, c), lambda i: (i, 0)),
        out_shape=jax.ShapeDtypeStruct((r, c), F32),
        compiler_params=_cp(("parallel",)),
    )(blocks)


def _adamw(name, w, g, m, v):
    r = w.shape[0]
    tm = _pick(r, (512, 256, 128, 64, 32, 16, 8))

    def body(i, n, wv, gv, mv, vv):
        m2 = ADAM_B1 * mv + (1.0 - ADAM_B1) * gv
        v2 = ADAM_B2 * vv + (1.0 - ADAM_B2) * (gv * gv)
        m_hat = m2 / (1.0 - ADAM_B1 ** ADAM_STEP)
        v_hat = v2 / (1.0 - ADAM_B2 ** ADAM_STEP)
        delta = -ADAM_LR * (m_hat / (jnp.sqrt(v_hat) + ADAM_EPS) + ADAM_WD * wv)
        return delta, m2, v2

    return _rowwise(name, body, [(w, "row"), (g, "row"), (m, "row"), (v, "row")],
                    [(w.shape, F32, "row")] * 3, tm)


def _pool_diff(i, t_rows, u, up):
    ucat = jnp.concatenate([jnp.where(i > 0, up, 0.0), u], axis=0)
    r = lax.broadcasted_iota(jnp.int32, (t_rows, 2 * t_rows), 0)
    cc = lax.broadcasted_iota(jnp.int32, (t_rows, 2 * t_rows), 1)
    lag = r + t_rows - cc
    t_col = i * t_rows + lax.broadcasted_iota(jnp.int32, (t_rows, 1), 0)
    diffs = []
    for gi, wdw in enumerate(POOL_WINDOWS):
        win = jnp.where((lag >= 0) & (lag < wdw), 1.0, 0.0).astype(F32)
        cnt = jnp.minimum(t_col + 1, wdw).astype(F32)
        ws = _dot(win, ucat[:, gi * POOL_GROUP:(gi + 1) * POOL_GROUP], 1, 0, HI)
        diffs.append(ws / cnt - u[:, gi * POOL_GROUP:(gi + 1) * POOL_GROUP])
    return diffs


def _pool_fwd(u, pw, sc):
    s = u.shape[0]
    tm = _tm(s, 256)

    def body(i, n, uv, upv, wv, scv):
        diffs = _pool_diff(i, tm, uv, upv)
        ys = [_dot(d.astype(BF16), wv[gi].astype(BF16), 1, 0) for gi, d in enumerate(diffs)]
        return (jnp.concatenate(ys, axis=1) * scv[...],)

    return _rowwise("pool_fwd", body, [(u, "row"), (u, "prev"), (pw, "full"), (sc, "full")],
                    [((s, POOL_WIDTH), BF16, "row")], tm)[0]


def _pool_bwd1(u, dy, pw, sc):
    s = u.shape[0]
    tm = _tm(s, 256)

    def body(i, n, uv, upv, dyv, wv, scv):
        diffs = _pool_diff(i, tm, uv, upv)
        scv = scv[...]
        dsc, dws, dds = [], [], []
        for gi, d in enumerate(diffs):
            sl = slice(gi * POOL_GROUP, (gi + 1) * POOL_GROUP)
            db, wb = d.astype(BF16), wv[gi].astype(BF16)
            yg = _dot(db, wb, 1, 0)
            dsc.append(jnp.sum(yg * dyv[:, sl], axis=0, keepdims=True))
            eb = (dyv[:, sl] * scv[:, sl]).astype(BF16)
            dws.append(_dot(db, eb, 0, 0)[None])
            dds.append(_dot(eb, wb, 1, 1))
        return jnp.concatenate(dds, axis=1), jnp.concatenate(dsc, axis=1), jnp.concatenate(dws, axis=0)

    return _rowwise("pool_bwd1", body, [(u, "row"), (u, "prev"), (dy, "row"), (pw, "full"), (sc, "full")],
                    [((s, POOL_WIDTH), F32, "row"), ((1, POOL_WIDTH), F32, "acc"), (pw.shape, F32, "acc")], tm)


def _pool_bwd2(dd):
    s = dd.shape[0]
    tm = _tm(s, 256)

    def body(i, n, dv, dnv):
        dcat = jnp.concatenate([dv, jnp.where(i < n - 1, dnv, 0.0)], axis=0)
        r = lax.broadcasted_iota(jnp.int32, (tm, 2 * tm), 0)
        cc = lax.broadcasted_iota(jnp.int32, (tm, 2 * tm), 1)
        lag = cc - r
        t_col = i * tm + lax.broadcasted_iota(jnp.int32, (2 * tm, 1), 0)
        outs = []
        for gi, wdw in enumerate(POOL_WINDOWS):
            sl = slice(gi * POOL_GROUP, (gi + 1) * POOL_GROUP)
            win = jnp.where((lag >= 0) & (lag < wdw), 1.0, 0.0).astype(F32)
            cnt = jnp.minimum(t_col + 1, wdw).astype(F32)
            outs.append(_dot(win, dcat[:, sl] / cnt, 1, 0, HI) - dv[:, sl])
        return (jnp.concatenate(outs, axis=1),)

    return _rowwise("pool_bwd2", body, [(dd, "row"), (dd, "next")], [((s, POOL_WIDTH), BF16, "row")], tm)[0]


def _shift_down(cur, other, j, tm):
    if j == 0:
        return cur
    rows = lax.broadcasted_iota(jnp.int32, cur.shape, 0)
    return jnp.where(rows < j, pltpu.roll(other, j, 0), pltpu.roll(cur, j, 0))


def _shift_up(cur, other, j, tm):
    if j == 0:
        return cur
    rows = lax.broadcasted_iota(jnp.int32, cur.shape, 0)
    return jnp.where(rows >= tm - j, pltpu.roll(other, tm - j, 0), pltpu.roll(cur, tm - j, 0))


def _conv_pre(i, tm, xv, xpv, wv, bv):
    xpv = jnp.where(i > 0, xpv, 0.0)
    taps = [_shift_down(xv, xpv, SSM_CONV - 1 - k, tm) for k in range(SSM_CONV)]
    pre = bv[...]
    for k in range(SSM_CONV):
        pre = pre + wv[k:k + 1, :] * taps[k]
    return pre, taps


def _conv_fwd(xbc, w, b):
    s = xbc.shape[0]
    tm = _tm(s, 256)

    def body(i, n, xv, xpv, wv, bv):
        pre, _ = _conv_pre(i, tm, xv, xpv, wv, bv)
        return (pre * _sigmoid(pre),)

    return _rowwise("conv_fwd", body, [(xbc, "row"), (xbc, "prev"), (w, "full"), (b, "full")],
                    [(xbc.shape, F32, "row")], tm)[0]


def _conv_bwd1(dxc, xbc, w, b):
    s, c = xbc.shape
    tm = _tm(s, 256)

    def body(i, n, dv, xv, xpv, wv, bv):
        pre, taps = _conv_pre(i, tm, xv, xpv, wv, bv)
        sg = _sigmoid(pre)
        dpre = dv * (sg * (1.0 + pre * (1.0 - sg)))
        tap_row = lax.broadcasted_iota(jnp.int32, (SSM_CONV, c), 0)
        dw = jnp.zeros((SSM_CONV, c), F32)
        for k in range(SSM_CONV):
            dw = dw + jnp.where(tap_row == k, jnp.sum(dpre * taps[k], axis=0, keepdims=True), 0.0)
        return dpre, dw, jnp.sum(dpre, axis=0, keepdims=True)

    return _rowwise("conv_bwd1", body, [(dxc, "row"), (xbc, "row"), (xbc, "prev"), (w, "full"), (b, "full")],
                    [((s, c), F32, "row"), ((SSM_CONV, c), F32, "acc"), ((1, c), F32, "acc")], tm)


def _conv_bwd2(dpre, w):
    s, c = dpre.shape
    tm = _tm(s, 256)

    def body(i, n, dv, dnv, wv):
        dnv = jnp.where(i < n - 1, dnv, 0.0)
        out = jnp.zeros_like(dv)
        for k in range(SSM_CONV):
            out = out + wv[k:k + 1, :] * _shift_up(dv, dnv, SSM_CONV - 1 - k, tm)
        return (out,)

    return _rowwise("conv_bwd2", body, [(dpre, "row"), (dpre, "next"), (w, "full")], [((s, c), BF16, "row")], tm)[0]


def _ssd_common(alog, dt_c, dt_r, tril, tri):
    a = -jnp.exp(alog)
    acs_c = _dot(tril, dt_c * a, 1, 0, HI)
    acs_r = _dot(dt_r * a, tril, 1, 1, HI)
    a_last = jnp.sum(dt_c * a, axis=0, keepdims=True)
    return a, acs_c, acs_r, a_last


def _ssd_fwd(xh, xht, dtc, dtr, bm, cm, alog, dsk):
    nh, s, p = xh.shape
    L = SSM_CHUNK
    nc = s // L
    E = nh // SSM_GROUPS

    def kern(xh_ref, xht_ref, dtc_ref, dtr_ref, b_ref, c_ref, alog_ref, dsk_ref, y_ref, hp_ref, h_s):
        c = pl.program_id(1)

        @pl.when(c == 0)
        def _():
            h_s[...] = jnp.zeros_like(h_s)

        bb = b_ref[...].astype(BF16)
        cb_ = c_ref[...].astype(BF16)
        cbm = _dot(cb_, bb, 1, 1)
        ri = lax.broadcasted_iota(jnp.int32, (L, L), 0)
        ci = lax.broadcasted_iota(jnp.int32, (L, L), 1)
        tri = ri >= ci
        tril = tri.astype(F32)
        for e in range(E):
            dt_c, dt_r = dtc_ref[e], dtr_ref[e]
            a, acs_c, acs_r, a_last = _ssd_common(alog_ref[e], dt_c, dt_r, tril, tri)
            lam = jnp.exp(jnp.where(tri, acs_c - acs_r, NEG))
            x = xh_ref[e]
            xdt = (x * dt_c).astype(BF16)
            hh = h_s[e]
            y = _dot((cbm * lam).astype(BF16), xdt, 1, 0)
            y = y + _dot(cb_, hh.astype(BF16), 1, 1) * jnp.exp(acs_c) + x * dsk_ref[e]
            y_ref[e] = y
            hp_ref[e] = hh
            xw = (xht_ref[e] * (dt_r * jnp.exp(a_last - acs_r))).astype(BF16)
            h_s[e] = hh * jnp.exp(a_last) + _dot(xw, bb, 1, 0)

    return pl.pallas_call(
        kern,
        name="ssd_fwd",
        grid=(SSM_GROUPS, nc),
        in_specs=[
            pl.BlockSpec((E, L, p), lambda g, c: (g, c, 0)),
            pl.BlockSpec((E, p, L), lambda g, c: (g, 0, c)),
            pl.BlockSpec((E, L, 1), lambda g, c: (g, c, 0)),
            pl.BlockSpec((E, 1, L), lambda g, c: (g, 0, c)),
            pl.BlockSpec((L, SSM_STATE), lambda g, c: (c, g)),
            pl.BlockSpec((L, SSM_STATE), lambda g, c: (c, g)),
            pl.BlockSpec((E, 1, 1), lambda g, c: (g, 0, 0)),
            pl.BlockSpec((E, 1, 1), lambda g, c: (g, 0, 0)),
        ],
        out_specs=[
            pl.BlockSpec((E, L, p), lambda g, c: (g, c, 0)),
            pl.BlockSpec((E, None, p, SSM_STATE), lambda g, c: (g, c, 0, 0)),
        ],
        out_shape=[jax.ShapeDtypeStruct((nh, s, p), F32), jax.ShapeDtypeStruct((nh, nc, p, SSM_STATE), F32)],
        scratch_shapes=[pltpu.VMEM((E, p, SSM_STATE), F32)],
        compiler_params=_cp(("arbitrary", "arbitrary")),
    )(xh, xht, dtc, dtr, bm, cm, alog, dsk)


def _ssd_bwd(dy, dyt, xh, dtc, dtr, bm, cm, hprev, alog, dsk):
    nh, s, p = xh.shape
    L = SSM_CHUNK
    nc = s // L
    E = nh // SSM_GROUPS

    def kern(dy_ref, dyt_ref, xh_ref, dtc_ref, dtr_ref, b_ref, c_ref, hp_ref, alog_ref, dsk_ref,
             dx_ref, ddt_ref, db_ref, dc_ref, dalog_ref, dd_ref, dh_s):
        c = pl.program_id(1)

        @pl.when(c == 0)
        def _():
            dh_s[...] = jnp.zeros_like(dh_s)
            dalog_ref[...] = jnp.zeros_like(dalog_ref)
            dd_ref[...] = jnp.zeros_like(dd_ref)

        bb = b_ref[...].astype(BF16)
        cb_ = c_ref[...].astype(BF16)
        cbm = _dot(cb_, bb, 1, 1)
        cbt = _dot(bb, cb_, 1, 1)
        ri = lax.broadcasted_iota(jnp.int32, (L, L), 0)
        ci = lax.broadcasted_iota(jnp.int32, (L, L), 1)
        tri = ri >= ci
        trit = ci >= ri
        tril = tri.astype(F32)
        triu = trit.astype(F32)
        db_acc = jnp.zeros((L, SSM_STATE), F32)
        dc_acc = jnp.zeros((L, SSM_STATE), F32)
        for e in range(E):
            dt_c, dt_r = dtc_ref[e], dtr_ref[e]
            a, acs_c, acs_r, a_last = _ssd_common(alog_ref[e], dt_c, dt_r, tril, tri)
            lam = jnp.exp(jnp.where(tri, acs_c - acs_r, NEG))
            lamt = jnp.exp(jnp.where(trit, acs_r - acs_c, NEG))
            x = xh_ref[e]
            xdt = x * dt_c
            xdtb = xdt.astype(BF16)
            dyv = dy_ref[e]
            dyb = dyv.astype(BF16)
            hh = hp_ref[e]
            hb = hh.astype(BF16)
            dhn = dh_s[e]
            dhnb = dhn.astype(BF16)
            ea_c = jnp.exp(acs_c)
            decay_c = jnp.exp(a_last - acs_c)
            e_last = jnp.exp(a_last)
            gm = _dot(dyb, xdtb, 1, 1)
            gt = _dot(xdtb, dyb, 1, 1)
            bdh = _dot(bb, dhnb, 1, 1)
            dxdt = _dot((cbt * lamt).astype(BF16), dyb, 1, 0) + bdh * decay_c
            dcb = gm * lam
            dcbt = gt * lamt
            yoff = _dot(cb_, hb, 1, 1) * ea_c
            dc_acc = dc_acc + _dot(dcb.astype(BF16), bb, 1, 0) + _dot(dyb, hb, 1, 0) * ea_c
            db_acc = db_acc + _dot(dcbt.astype(BF16), cb_, 1, 0) + _dot(xdtb, dhnb, 1, 0) * decay_c
            tt = decay_c * jnp.sum(xdt * bdh, axis=1, keepdims=True)
            dacs = (jnp.sum(dcb * cbm, axis=1, keepdims=True) - jnp.sum(dcbt * cbt, axis=1, keepdims=True)
                    + jnp.sum(dyv * yoff, axis=1, keepdims=True) - tt)
            tail = jnp.sum(tt, axis=0, keepdims=True) + e_last * jnp.sum(jnp.sum(dhn * hh, axis=1, keepdims=True), axis=0, keepdims=True)
            dda = _dot(triu, dacs, 1, 0, HI) + tail
            ddt_ref[e] = dda * a + jnp.sum(dxdt * x, axis=1, keepdims=True)
            dalog_ref[e] += jnp.sum(dda * dt_c, axis=0, keepdims=True) * a
            dd_ref[e] += jnp.sum(jnp.sum(dyv * x, axis=1, keepdims=True), axis=0, keepdims=True)
            dx_ref[e] = dxdt * dt_c + dyv * dsk_ref[e]
            dyw = (dyt_ref[e] * jnp.exp(acs_r)).astype(BF16)
            dh_s[e] = dhn * e_last + _dot(dyw, cb_, 1, 0)
        db_ref[...] = db_acc
        dc_ref[...] = dc_acc

    rc = lambda c: nc - 1 - c
    return pl.pallas_call(
        kern,
        name="ssd_bwd",
        grid=(SSM_GROUPS, nc),
        in_specs=[
            pl.BlockSpec((E, L, p), lambda g, c: (g, rc(c), 0)),
            pl.BlockSpec((E, p, L), lambda g, c: (g, 0, rc(c))),
            pl.BlockSpec((E, L, p), lambda g, c: (g, rc(c), 0)),
            pl.BlockSpec((E, L, 1), lambda g, c: (g, rc(c), 0)),
            pl.BlockSpec((E, 1, L), lambda g, c: (g, 0, rc(c))),
            pl.BlockSpec((L, SSM_STATE), lambda g, c: (rc(c), g)),
            pl.BlockSpec((L, SSM_STATE), lambda g, c: (rc(c), g)),
            pl.BlockSpec((E, None, p, SSM_STATE), lambda g, c: (g, rc(c), 0, 0)),
            pl.BlockSpec((E, 1, 1), lambda g, c: (g, 0, 0)),
            pl.BlockSpec((E, 1, 1), lambda g, c: (g, 0, 0)),
        ],
        out_specs=[
            pl.BlockSpec((E, L, p), lambda g, c: (g, rc(c), 0)),
            pl.BlockSpec((E, L, 1), lambda g, c: (g, rc(c), 0)),
            pl.BlockSpec((L, SSM_STATE), lambda g, c: (rc(c), g)),
            pl.BlockSpec((L, SSM_STATE), lambda g, c: (rc(c), g)),
            pl.BlockSpec((E, 1, 1), lambda g, c: (g, 0, 0)),
            pl.BlockSpec((E, 1, 1), lambda g, c: (g, 0, 0)),
        ],
        out_shape=[
            jax.ShapeDtypeStruct((nh, s, p), F32),
            jax.ShapeDtypeStruct((nh, s, 1), F32),
            jax.ShapeDtypeStruct((s, SSM_GROUPS * SSM_STATE), F32),
            jax.ShapeDtypeStruct((s, SSM_GROUPS * SSM_STATE), F32),
            jax.ShapeDtypeStruct((nh, 1, 1), F32),
            jax.ShapeDtypeStruct((nh, 1, 1), F32),
        ],
        scratch_shapes=[pltpu.VMEM((E, p, SSM_STATE), F32)],
        compiler_params=_cp(("arbitrary", "arbitrary")),
    )(dy, dyt, xh, dtc, dtr, bm, cm, hprev, alog, dsk)


def _fgate_fwd(fl, bias):
    s, h = fl.shape
    t = _tm(s, 512)

    def kern(fl_ref, b_ref, o_ref, carry):
        i = pl.program_id(0)

        @pl.when(i == 0)
        def _():
            carry[...] = jnp.zeros_like(carry)

        z = fl_ref[...] + b_ref[...]
        lf = -_softplus(-z)
        ri = lax.broadcasted_iota(jnp.int32, (t, t), 0)
        ci = lax.broadcasted_iota(jnp.int32, (t, t), 1)
        o_ref[...] = _dot((ri >= ci).astype(F32), lf, 1, 0, HI) + carry[...]
        carry[...] += jnp.sum(lf, axis=0, keepdims=True)

    return pl.pallas_call(
        kern,
        name="fgate_fwd",
        grid=(s // t,),
        in_specs=[pl.BlockSpec((t, h), lambda i: (i, 0)), pl.BlockSpec((1, h), lambda i: (0, 0))],
        out_specs=pl.BlockSpec((t, h), lambda i: (i, 0)),
        out_shape=jax.ShapeDtypeStruct((s, h), F32),
        scratch_shapes=[pltpu.VMEM((1, h), F32)],
        compiler_params=_cp(("arbitrary",)),
    )(fl, bias)


def _fgate_bwd(dfq, dfk, fl, bias):
    s, h = fl.shape
    t = _tm(s, 512)
    n = s // t

    def kern(dq_ref, dk_ref, fl_ref, b_ref, o_ref, db_ref, carry):
        i = pl.program_id(0)

        @pl.when(i == 0)
        def _():
            carry[...] = jnp.zeros_like(carry)
            db_ref[...] = jnp.zeros_like(db_ref)

        ri = lax.broadcasted_iota(jnp.int32, (t, t), 0)
        ci = lax.broadcasted_iota(jnp.int32, (t, t), 1)
        d = dq_ref[...] + dk_ref[...]
        rev = _dot((ci >= ri).astype(F32), d, 1, 0, HI) + carry[...]
        carry[...] += jnp.sum(d, axis=0, keepdims=True)
        dz = rev * _sigmoid(-(fl_ref[...] + b_ref[...]))
        o_ref[...] = dz
        db_ref[...] += jnp.sum(dz, axis=0, keepdims=True)

    return pl.pallas_call(
        kern,
        name="fgate_bwd",
        grid=(n,),
        in_specs=[pl.BlockSpec((t, h), lambda i: (n - 1 - i, 0)), pl.BlockSpec((t, h), lambda i: (n - 1 - i, 0)),
                  pl.BlockSpec((t, h), lambda i: (n - 1 - i, 0)), pl.BlockSpec((1, h), lambda i: (0, 0))],
        out_specs=[pl.BlockSpec((t, h), lambda i: (n - 1 - i, 0)), pl.BlockSpec((1, h), lambda i: (0, 0))],
        out_shape=[jax.ShapeDtypeStruct((s, h), F32), jax.ShapeDtypeStruct((1, h), F32)],
        scratch_shapes=[pltpu.VMEM((1, h), F32)],
        compiler_params=_cp(("arbitrary",)),
    )(dfq, dfk, fl, bias)


def _attn_scores(q, k, fq, fk, scale, qi, kj, t):
    sc = _dot(q, k, 1, 1) * scale
    if fq is not None:
        sc = sc + fq - fk
    rows = qi * t + lax.broadcasted_iota(jnp.int32, (t, t), 0)
    cols = kj * t + lax.broadcasted_iota(jnp.int32, (t, t), 1)
    return jnp.where(cols <= rows, sc, NEG)


def _flash_fwd(name, q, k, v, fq, fk, scale):
    nh, s, dk = q.shape
    dv = v.shape[-1]
    t = _tm(s, 512)
    nq = s // t
    bias = fq is not None

    def kern(*refs):
        if bias:
            q_ref, k_ref, v_ref, fq_ref, fk_ref, o_ref, lse_ref, m_s, l_s, acc_s = refs
        else:
            q_ref, k_ref, v_ref, o_ref, lse_ref, m_s, l_s, acc_s = refs
        qi, kj = pl.program_id(1), pl.program_id(2)

        @pl.when(kj == 0)
        def _():
            m_s[...] = jnp.full_like(m_s, NEG)
            l_s[...] = jnp.zeros_like(l_s)
            acc_s[...] = jnp.zeros_like(acc_s)

        @pl.when(kj <= qi)
        def _():
            sc = _attn_scores(q_ref[...], k_ref[...], fq_ref[...] if bias else None, fk_ref[...] if bias else None,
                              scale, qi, kj, t)
            m_new = jnp.maximum(m_s[...], jnp.max(sc, axis=1, keepdims=True))
            corr = jnp.exp(m_s[...] - m_new)
            p = jnp.exp(sc - m_new)
            l_s[...] = corr * l_s[...] + jnp.sum(p, axis=1, keepdims=True)
            acc_s[...] = acc_s[...] * corr + _dot(p.astype(BF16), v_ref[...], 1, 0)
            m_s[...] = m_new

        @pl.when(kj == nq - 1)
        def _():
            o_ref[...] = acc_s[...] / l_s[...]
            lse_ref[...] = m_s[...] + jnp.log(l_s[...])

    qspec = lambda d: pl.BlockSpec((None, t, d), lambda h, i, j: (h, i, 0))
    kspec = lambda d: pl.BlockSpec((None, t, d), lambda h, i, j: (h, jnp.minimum(j, i), 0))
    in_specs = [qspec(dk), kspec(dk), kspec(dv)]
    args = [q, k, v]
    if bias:
        in_specs += [pl.BlockSpec((None, t, 1), lambda h, i, j: (h, i, 0)),
                     pl.BlockSpec((None, 1, t), lambda h, i, j: (h, 0, jnp.minimum(j, i)))]
        args += [fq, fk]
    return pl.pallas_call(
        kern,
        name=name,
        grid=(nh, nq, nq),
        in_specs=in_specs,
        out_specs=[qspec(dv), pl.BlockSpec((None, t, 1), lambda h, i, j: (h, i, 0))],
        out_shape=[jax.ShapeDtypeStruct((nh, s, dv), F32), jax.ShapeDtypeStruct((nh, s, 1), F32)],
        scratch_shapes=[pltpu.VMEM((t, 1), F32), pltpu.VMEM((t, 1), F32), pltpu.VMEM((t, dv), F32)],
        compiler_params=_cp(("parallel", "parallel", "arbitrary")),
    )(*args)


def _flash_bwd(name, q, k, v, do, lse, delta, fq, fk, scale):
    nh, s, dk = q.shape
    dv = v.shape[-1]
    t = _tm(s, 512)
    nq = s // t
    bias = fq is not None

    def kern(*refs):
        if bias:
            (q_ref, k_ref, v_ref, do_ref, lse_ref, dl_ref, fq_ref, fk_ref,
             dq_ref, dk_ref, dv_ref, dfk_ref, dfq_ref, dk_s, dv_s, dfk_s) = refs
        else:
            q_ref, k_ref, v_ref, do_ref, lse_ref, dl_ref, dq_ref, dk_ref, dv_ref, dk_s, dv_s = refs
        kb, qi = pl.program_id(1), pl.program_id(2)

        @pl.when((kb == 0) & (qi == 0))
        def _():
            dq_ref[...] = jnp.zeros_like(dq_ref)
            if bias:
                dfq_ref[...] = jnp.zeros_like(dfq_ref)

        @pl.when(qi == 0)
        def _():
            dk_s[...] = jnp.zeros_like(dk_s)
            dv_s[...] = jnp.zeros_like(dv_s)
            if bias:
                dfk_s[...] = jnp.zeros_like(dfk_s)

        @pl.when(qi >= kb)
        def _():
            qv, kv_, dob = q_ref[...], k_ref[...], do_ref[...].astype(BF16)
            sc = _attn_scores(qv, kv_, fq_ref[...] if bias else None, fk_ref[...] if bias else None, scale, qi, kb, t)
            p = jnp.exp(sc - lse_ref[...])
            pb = p.astype(BF16)
            dv_s[...] += _dot(pb, dob, 0, 0)
            dp = _dot(dob, v_ref[...], 1, 1)
            ds = p * (dp - dl_ref[...])
            dsb = ds.astype(BF16)
            dk_s[...] += _dot(dsb, qv, 0, 0) * scale
            rows = pl.ds(pl.multiple_of(qi * t, t), t)
            dq_ref[rows, :] += _dot(dsb, kv_, 1, 0) * scale
            if bias:
                dsr = dsb.astype(F32)
                dfk_s[...] -= jnp.sum(dsr, axis=0, keepdims=True)
                dfq_ref[rows, :] += jnp.sum(dsr, axis=1, keepdims=True)

        @pl.when(qi == nq - 1)
        def _():
            dk_ref[...] = dk_s[...]
            dv_ref[...] = dv_s[...]
            if bias:
                dfk_ref[...] = dfk_s[...]

    qspec = lambda d: pl.BlockSpec((None, t, d), lambda h, j, i: (h, jnp.maximum(i, j), 0))
    kspec = lambda d: pl.BlockSpec((None, t, d), lambda h, j, i: (h, j, 0))
    in_specs = [qspec(dk), kspec(dk), kspec(dv), qspec(dv), qspec(1), qspec(1)]
    args = [q, k, v, do, lse, delta]
    out_specs = [pl.BlockSpec((None, s, dk), lambda h, j, i: (h, 0, 0)), kspec(dk), kspec(dv)]
    out_shape = [jax.ShapeDtypeStruct((nh, s, dk), F32), jax.ShapeDtypeStruct((nh, s, dk), F32),
                 jax.ShapeDtypeStruct((nh, s, dv), F32)]
    scratch = [pltpu.VMEM((t, dk), F32), pltpu.VMEM((t, dv), F32)]
    if bias:
        in_specs += [qspec(1), pl.BlockSpec((None, 1, t), lambda h, j, i: (h, 0, j))]
        args += [fq, fk]
        out_specs.append(pl.BlockSpec((None, 1, t), lambda h, j, i: (h, 0, j)))
        out_shape.append(jax.ShapeDtypeStruct((nh, 1, s), F32))
        out_specs.append(pl.BlockSpec((None, s, 1), lambda h, j, i: (h, 0, 0)))
        out_shape.append(jax.ShapeDtypeStruct((nh, s, 1), F32))
        scratch.append(pltpu.VMEM((1, t), F32))
    return pl.pallas_call(
        kern,
        name=name,
        grid=(nh, nq, nq),
        in_specs=in_specs,
        out_specs=out_specs,
        out_shape=out_shape,
        scratch_shapes=scratch,
        compiler_params=_cp(("arbitrary", "arbitrary", "arbitrary")),
    )(*args)


def _heads(t, nh):
    s = t.shape[0]
    return t.reshape(s, nh, -1).transpose(1, 0, 2)


def _unheads(t):
    nh, s, d = t.shape
    return t.transpose(1, 0, 2).reshape(s, nh * d)


def _perm_uq(w):
    r = w.shape[0]
    w3 = w.reshape(r, MLA_HEADS, MLA_NOPE + MLA_ROPE)
    half = MLA_ROPE // 2
    return jnp.concatenate([w3[:, :, :MLA_NOPE].reshape(r, -1), w3[:, :, MLA_NOPE:MLA_NOPE + half].reshape(r, -1),
                            w3[:, :, MLA_NOPE + half:].reshape(r, -1)], axis=1)


def _unperm_uq(w):
    r = w.shape[0]
    half = MLA_ROPE // 2
    n0 = MLA_HEADS * MLA_NOPE
    n1 = n0 + MLA_HEADS * half
    return jnp.concatenate([w[:, :n0].reshape(r, MLA_HEADS, MLA_NOPE), w[:, n0:n1].reshape(r, MLA_HEADS, half),
                            w[:, n1:].reshape(r, MLA_HEADS, half)], axis=2).reshape(r, -1)


def _perm_ukv(w):
    r = w.shape[0]
    w3 = w.reshape(r, MLA_HEADS, MLA_NOPE + MLA_V)
    return jnp.concatenate([w3[:, :, :MLA_NOPE].reshape(r, -1), w3[:, :, MLA_NOPE:].reshape(r, -1)], axis=1)


def _unperm_ukv(w):
    r = w.shape[0]
    n0 = MLA_HEADS * MLA_NOPE
    return jnp.concatenate([w[:, :n0].reshape(r, MLA_HEADS, MLA_NOPE), w[:, n0:].reshape(r, MLA_HEADS, MLA_V)],
                           axis=2).reshape(r, -1)


_ODD_CUTS = np.cumsum([0, FOX_WIDTH, FOX_WIDTH, FOX_WIDTH, FOX_HEADS, MLA_Q_RANK, MLA_KV_RANK, MLA_ROPE]).tolist()
_ODD_ORDER = (0, 1, 2, 4, 5, 6, 3)


def _perm_odd_in(w):
    parts = [w[:, _ODD_CUTS[j]:_ODD_CUTS[j + 1]] for j in _ODD_ORDER]
    parts.append(jnp.zeros((w.shape[0], ODD_IN_PAD - ODD_IN), w.dtype))
    return jnp.concatenate(parts, axis=1)


def _unperm_odd_in(w):
    widths = [_ODD_CUTS[j + 1] - _ODD_CUTS[j] for j in _ODD_ORDER]
    offs = np.cumsum([0] + widths).tolist()
    pieces = {j: w[:, offs[n]:offs[n + 1]] for n, j in enumerate(_ODD_ORDER)}
    return jnp.concatenate([pieces[j] for j in range(7)], axis=1)


def _pad_cols(w, n):
    return jnp.concatenate([w, jnp.zeros((w.shape[0], n - w.shape[1]), w.dtype)], axis=1)


_BIG = (("even_w_in", 2), ("even_w_out", 1), ("odd_w_in", 2), ("w_uq", 2), ("w_ukv", 2), ("odd_w_out", 1),
        ("ffn_w_gate", 2), ("ffn_w_up", 2), ("ffn_w_down", 1))
_PACK_COLS = 1024


def _pack_rows(n):
    return -(-n // (_PACK_COLS * 16)) * 16


def _unshard(blocks, shp, ax):
    t = jnp.moveaxis(blocks.reshape((N_DEV,) + tuple(shp)), 0, ax)
    full = list(shp)
    full[ax] = shp[ax] * N_DEV
    return t.reshape(full)


def _reshard(full, ax):
    shp = list(full.shape)
    t = full.reshape(shp[:ax] + [N_DEV, shp[ax] // N_DEV] + shp[ax + 1:])
    return jnp.moveaxis(t, ax, 0).reshape(N_DEV, -1)


def kernel(x, even_w_in, pool_w, pool_scale, conv_w, conv_b, dt_bias, a_log, d_skip, ssm_norm_w, even_w_out, odd_w_in, fgate_b, q_norm_w, w_uq, kv_norm_w, w_ukv, odd_w_out, ffn_w_gate, ffn_w_up, ffn_w_down, ln_mix_g, ln_mix_b, ln_ffn_g, ln_ffn_b, loss_target, m_even_w_in, m_pool_w, m_pool_scale, m_conv_w, m_conv_b, m_dt_bias, m_a_log, m_d_skip, m_ssm_norm_w, m_even_w_out, m_odd_w_in, m_fgate_b, m_q_norm_w, m_w_uq, m_kv_norm_w, m_w_ukv, m_odd_w_out, m_ffn_w_gate, m_ffn_w_up, m_ffn_w_down, m_ln_mix_g, m_ln_mix_b, m_ln_ffn_g, m_ln_ffn_b, v_even_w_in, v_pool_w, v_pool_scale, v_conv_w, v_conv_b, v_dt_bias, v_a_log, v_d_skip, v_ssm_norm_w, v_even_w_out, v_odd_w_in, v_fgate_b, v_q_norm_w, v_w_uq, v_kv_norm_w, v_w_ukv, v_odd_w_out, v_ffn_w_gate, v_ffn_w_up, v_ffn_w_down, v_ln_mix_g, v_ln_mix_b, v_ln_ffn_g, v_ln_ffn_b):
    P = dict(even_w_in=even_w_in, pool_w=pool_w, pool_scale=pool_scale, conv_w=conv_w, conv_b=conv_b, dt_bias=dt_bias,
             a_log=a_log, d_skip=d_skip, ssm_norm_w=ssm_norm_w, even_w_out=even_w_out, odd_w_in=odd_w_in,
             fgate_b=fgate_b, q_norm_w=q_norm_w, w_uq=w_uq, kv_norm_w=kv_norm_w, w_ukv=w_ukv, odd_w_out=odd_w_out,
             ffn_w_gate=ffn_w_gate, ffn_w_up=ffn_w_up, ffn_w_down=ffn_w_down, ln_mix_g=ln_mix_g, ln_mix_b=ln_mix_b,
             ln_ffn_g=ln_ffn_g, ln_ffn_b=ln_ffn_b)
    M = dict(even_w_in=m_even_w_in, pool_w=m_pool_w, pool_scale=m_pool_scale, conv_w=m_conv_w, conv_b=m_conv_b,
             dt_bias=m_dt_bias, a_log=m_a_log, d_skip=m_d_skip, ssm_norm_w=m_ssm_norm_w, even_w_out=m_even_w_out,
             odd_w_in=m_odd_w_in, fgate_b=m_fgate_b, q_norm_w=m_q_norm_w, w_uq=m_w_uq, kv_norm_w=m_kv_norm_w,
             w_ukv=m_w_ukv, odd_w_out=m_odd_w_out, ffn_w_gate=m_ffn_w_gate, ffn_w_up=m_ffn_w_up,
             ffn_w_down=m_ffn_w_down, ln_mix_g=m_ln_mix_g, ln_mix_b=m_ln_mix_b, ln_ffn_g=m_ln_ffn_g,
             ln_ffn_b=m_ln_ffn_b)
    V = dict(even_w_in=v_even_w_in, pool_w=v_pool_w, pool_scale=v_pool_scale, conv_w=v_conv_w, conv_b=v_conv_b,
             dt_bias=v_dt_bias, a_log=v_a_log, d_skip=v_d_skip, ssm_norm_w=v_ssm_norm_w, even_w_out=v_even_w_out,
             odd_w_in=v_odd_w_in, fgate_b=v_fgate_b, q_norm_w=v_q_norm_w, w_uq=v_w_uq, kv_norm_w=v_kv_norm_w,
             w_ukv=v_w_ukv, odd_w_out=v_odd_w_out, ffn_w_gate=v_ffn_w_gate, ffn_w_up=v_ffn_w_up,
             ffn_w_down=v_ffn_w_down, ln_mix_g=v_ln_mix_g, ln_mix_b=v_ln_mix_b, ln_ffn_g=v_ln_ffn_g,
             ln_ffn_b=v_ln_ffn_b)
    names = list(P)
    s = x.shape[1]
    me = 4 * lax.axis_index("x") + 2 * lax.axis_index("y") + lax.axis_index("c")

    flat = jnp.concatenate([P[n].astype(BF16).reshape(-1) for n, _ in _BIG])
    n_big = flat.shape[0]
    rows_big = _pack_rows(n_big)
    flat = jnp.concatenate([flat, jnp.zeros((rows_big * _PACK_COLS - n_big,), BF16)])
    gathered = _all_gather("ag_weights", flat.reshape(rows_big, _PACK_COLS)).reshape(N_DEV, -1)
    W = {}
    off = 0
    for n, ax in _BIG:
        shp = P[n].shape
        cnt = math.prod(shp)
        W[n] = _unshard(gathered[:, off:off + cnt], shp, ax)
        off += cnt
    small_sh = jnp.concatenate([conv_w.reshape(-1), q_norm_w.reshape(-1), kv_norm_w.reshape(-1)])
    n_ssh = small_sh.shape[0]
    small_sh = jnp.concatenate([small_sh, jnp.zeros((16 * 128 - n_ssh,), F32)]).reshape(16, 128)
    g_small = _all_gather("ag_small_weights", small_sh).reshape(N_DEV, -1)
    cw_n, qn_n = conv_w.size, q_norm_w.size
    conv_w_full = _unshard(g_small[:, :cw_n], conv_w.shape, 2)
    q_norm_full = _unshard(g_small[:, cw_n:cw_n + qn_n], q_norm_w.shape, 1)
    kv_norm_full = _unshard(g_small[:, cw_n + qn_n:n_ssh], kv_norm_w.shape, 1)

    w_in_e = [_pad_cols(W["even_w_in"][i], EVEN_IN_PAD) for i in range(2)]
    w_in_o = [_perm_odd_in(W["odd_w_in"][i]) for i in range(2)]
    w_uq_p = [_perm_uq(W["w_uq"][i]) for i in range(2)]
    w_ukv_p = [_perm_ukv(W["w_ukv"][i]) for i in range(2)]
    w_gu = [jnp.concatenate([W["ffn_w_gate"][l], W["ffn_w_up"][l]], axis=1) for l in range(DEPTH)]

    pos = jnp.arange(s, dtype=F32)
    half = MLA_ROPE // 2
    freqs = jnp.power(ROPE_THETA, -jnp.arange(half, dtype=F32) / half)
    ang = pos[:, None] * freqs[None, :]
    cos16, sin16 = jnp.cos(ang), jnp.sin(ang)
    cos128, sin128 = jnp.tile(cos16, (1, MLA_HEADS)), jnp.tile(sin16, (1, MLA_HEADS))
    row = lambda t: t.reshape(1, -1)

    xcur = x[0]
    xb = xcur.astype(BF16)
    saved = []
    for l in range(DEPTH):
        i = l // 2
        sv = dict(x_in_b=xb)
        if l % 2 == 0:
            proj = _mm("mm_in_even", xb, w_in_e[i], "nn", F32)
            u, z = proj[:, :512], proj[:, 512:1536]
            xbc, dtraw = proj[:, 1536:3072], proj[:, 3072:3088]
            ypool = _pool_fwd(u, pool_w[i], row(pool_scale[i]))
            xc = _conv_fwd(xbc, conv_w_full[i], row(conv_b[i]))
            dt = _softplus_fwd(dtraw, row(dt_bias[i]))
            xh = _heads(xc[:, :SSM_D_INNER], SSM_HEADS)
            dtc = dt.T[:, :, None]
            dtr = dt.T[:, None, :]
            bm, cm = xc[:, SSM_D_INNER:SSM_D_INNER + 256], xc[:, SSM_D_INNER + 256:]
            alog3, dsk3 = a_log[i].reshape(-1, 1, 1), d_skip[i].reshape(-1, 1, 1)
            yh, hprev = _ssd_fwd(xh, xh.transpose(0, 2, 1), dtc, dtr, bm, cm, alog3, dsk3)
            y_ssm = _unheads(yh)
            yn = _gated_rms_fwd(y_ssm, z, row(ssm_norm_w[i]))
            mix = jnp.concatenate([ypool, yn], axis=1)
            h = _mm("mm_out_even", mix, W["even_w_out"][i], "nn", F32)
            sv.update(u=u, z=z, xbc=xbc, dtraw=dtraw, xh=xh, dtc=dtc, dtr=dtr, bm=bm, cm=cm, hprev=hprev,
                      y_ssm=y_ssm, mix=mix)
        else:
            proj = _mm("mm_in_odd", xb, w_in_o[i], "nn", F32)
            qf, kf, vf = proj[:, :512], proj[:, 512:1024], proj[:, 1024:1536]
            cq, ckv = proj[:, 1536:2048], proj[:, 2048:2304]
            kr, fl = proj[:, 2304:2336], proj[:, 2336:2344]
            fcum = _fgate_fwd(fl, row(fgate_b[i]))
            fq_ = fcum.T[:, :, None]
            fk_ = fcum.T[:, None, :]
            qh, kh, vh = (_heads(t.astype(BF16), FOX_HEADS) for t in (qf, kf, vf))
            o_fox, lse_fox = _flash_fwd("fox_fwd", qh, kh, vh, fq_, fk_, 64 ** -0.5)
            qn = _rms_fwd(cq, row(q_norm_full[i]))
            qp = _mm("mm_uq", qn, w_uq_p[i], "nn", F32)
            q1, q2 = _rope("rope_q", qp[:, 512:640], qp[:, 640:768], cos128, sin128)
            kvn = _rms_fwd(ckv, row(kv_norm_full[i]))
            kvp = _mm("mm_ukv", kvn, w_ukv_p[i], "nn", F32)
            k1, k2 = _rope("rope_k", kr[:, :half], kr[:, half:], cos16, sin16)
            zpad = jnp.zeros((MLA_HEADS, s, MLA_DK_PAD - MLA_NOPE - MLA_ROPE), BF16)
            qm = jnp.concatenate([_heads(qp[:, :512], MLA_HEADS), _heads(q1, MLA_HEADS), _heads(q2, MLA_HEADS)],
                                 axis=2).astype(BF16)
            qm = jnp.concatenate([qm, zpad], axis=2)
            krope = jnp.broadcast_to(jnp.concatenate([k1, k2], axis=1)[None], (MLA_HEADS, s, MLA_ROPE))
            km = jnp.concatenate([_heads(kvp[:, :512], MLA_HEADS), krope], axis=2).astype(BF16)
            km = jnp.concatenate([km, zpad], axis=2)
            vm = _heads(kvp[:, 512:], MLA_HEADS).astype(BF16)
            o_mla, lse_mla = _flash_fwd("mla_fwd", qm, km, vm, None, None, (MLA_NOPE + MLA_ROPE) ** -0.5)
            mix = jnp.concatenate([_unheads(o_fox), _unheads(o_mla)], axis=1).astype(BF16)
            h = _mm("mm_out_odd", mix, W["odd_w_out"][i], "nn", F32)
            sv.update(fl=fl, fq=fq_, fk=fk_, qh=qh, kh=kh, vh=vh, o_fox=o_fox, lse_fox=lse_fox, cq=cq, ckv=ckv,
                      qn=qn, kvn=kvn, qm=qm, km=km, vm=vm, o_mla=o_mla, lse_mla=lse_mla, mix=mix)
        y1, y1b, r1 = _ln_fwd(xcur, h, row(ln_mix_g[l]), row(ln_mix_b[l]))
        gu = _mm("mm_ffn_in", y1b, w_gu[l], "nn", F32)
        act = _swiglu_fwd(gu)
        h2 = _mm("mm_ffn_out", act, W["ffn_w_down"][l], "nn", F32)
        y2, y2b, r2 = _ln_fwd(y1, h2, row(ln_ffn_g[l]), row(ln_ffn_b[l]))
        sv.update(r1=r1, y1b=y1b, gu=gu, act=act, r2=r2)
        saved.append(sv)
        xcur, xb = y2, y2b

    dy, loss_part = _loss_head(xcur, loss_target[0])
    loss = lax.psum(loss_part[0, 0], ("x", "y", "c"))

    G = {n: [None] * P[n].shape[0] for n in names}
    acur, dcur = None, dy
    for l in reversed(range(DEPTH)):
        i = l // 2
        sv = saved[l]
        dr2, dr2b, dg, db = _ln_bwd(acur, dcur, sv["r2"], row(ln_ffn_g[l]))
        G["ln_ffn_g"][l], G["ln_ffn_b"][l] = dg[0], db[0]
        G["ffn_w_down"][l] = _mm("mm_dw_ffn_out", sv["act"], dr2b, "tn", F32)
        dact = _mm("mm_dx_ffn_out", dr2b, W["ffn_w_down"][l], "nt", F32)
        dgu = _swiglu_bwd(dact, sv["gu"])
        dwgu = _mm("mm_dw_ffn_in", sv["y1b"], dgu, "tn", F32)
        G["ffn_w_gate"][l], G["ffn_w_up"][l] = dwgu[:, :D_FF], dwgu[:, D_FF:]
        dy1 = _mm("mm_dx_ffn_in", dgu, w_gu[l], "nt", F32)
        dr1, dr1b, dg, db = _ln_bwd(dr2, dy1, sv["r1"], row(ln_mix_g[l]))
        G["ln_mix_g"][l], G["ln_mix_b"][l] = dg[0], db[0]
        if l % 2 == 0:
            G["even_w_out"][i] = _mm("mm_dw_out_even", sv["mix"], dr1b, "tn", F32)
            dmix = _mm("mm_dx_out_even", dr1b, W["even_w_out"][i], "nt", F32)
            dd, dsc, dpw = _pool_bwd1(sv["u"], dmix[:, :POOL_WIDTH], pool_w[i], row(pool_scale[i]))
            G["pool_scale"][i], G["pool_w"][i] = dsc[0], dpw
            du = _pool_bwd2(dd)
            dys, dz, dnw = _gated_rms_bwd(dmix[:, POOL_WIDTH:], sv["y_ssm"], sv["z"], row(ssm_norm_w[i]))
            G["ssm_norm_w"][i] = dnw[0]
            dyh = _heads(dys, SSM_HEADS)
            alog3, dsk3 = a_log[i].reshape(-1, 1, 1), d_skip[i].reshape(-1, 1, 1)
            dxh, ddt3, dbm, dcm, dalog, ddsk = _ssd_bwd(dyh, dyh.transpose(0, 2, 1), sv["xh"], sv["dtc"], sv["dtr"],
                                                        sv["bm"], sv["cm"], sv["hprev"], alog3, dsk3)
            G["a_log"][i], G["d_skip"][i] = dalog.reshape(-1), ddsk.reshape(-1)
            ddtraw, ddtb = _softplus_bwd(ddt3[:, :, 0].T, sv["dtraw"], row(dt_bias[i]))
            G["dt_bias"][i] = ddtb[0]
            dxc = jnp.concatenate([_unheads(dxh), dbm, dcm], axis=1)
            dpre, dcw, dcb = _conv_bwd1(dxc, sv["xbc"], conv_w_full[i], row(conv_b[i]))
            G["conv_w"][i], G["conv_b"][i] = dcw, dcb[0]
            dxbc = _conv_bwd2(dpre, conv_w_full[i])
            dproj = jnp.concatenate([du, dz.astype(BF16), dxbc, ddtraw.astype(BF16),
                                     jnp.zeros((s, EVEN_IN_PAD - EVEN_IN), BF16)], axis=1)
            G["even_w_in"][i] = _mm("mm_dw_in_even", sv["x_in_b"], dproj, "tn", F32)[:, :EVEN_IN]
            dxb = _mm("mm_dx_in_even", dproj, w_in_e[i], "nt", F32)
        else:
            G["odd_w_out"][i] = _mm("mm_dw_out_odd", sv["mix"], dr1b, "tn", F32)
            dmix = _mm("mm_dx_out_odd", dr1b, W["odd_w_out"][i], "nt", F32)
            do_f = _heads(dmix[:, :FOX_WIDTH].astype(BF16), FOX_HEADS)
            dl_f = _rowdot(do_f.reshape(-1, 64), sv["o_fox"].reshape(-1, 64)).reshape(FOX_HEADS, s, 1)
            dqh, dkh, dvh, dfk, dfq = _flash_bwd("fox_bwd", sv["qh"], sv["kh"], sv["vh"], do_f, sv["lse_fox"], dl_f,
                                                 sv["fq"], sv["fk"], 64 ** -0.5)
            dfl, dfb = _fgate_bwd(dfq[:, :, 0].T, dfk[:, 0, :].T, sv["fl"], row(fgate_b[i]))
            G["fgate_b"][i] = dfb[0]
            do_m = _heads(dmix[:, FOX_WIDTH:].astype(BF16), MLA_HEADS)
            dl_m = _rowdot(do_m.reshape(-1, 64), sv["o_mla"].reshape(-1, 64)).reshape(MLA_HEADS, s, 1)
            dqm, dkm, dvm = _flash_bwd("mla_bwd", sv["qm"], sv["km"], sv["vm"], do_m, sv["lse_mla"], dl_m,
                                       None, None, (MLA_NOPE + MLA_ROPE) ** -0.5)
            n0, n1 = MLA_NOPE, MLA_NOPE + half
            dq1, dq2 = _rope("rope_q_bwd", _unheads(dqm[:, :, n0:n1]), _unheads(dqm[:, :, n1:n1 + half]),
                             cos128, -sin128)
            dqp = jnp.concatenate([_unheads(dqm[:, :, :n0]), dq1, dq2], axis=1).astype(BF16)
            G["w_uq"][i] = _unperm_uq(_mm("mm_dw_uq", sv["qn"], dqp, "tn", F32))
            dqn = _mm("mm_dx_uq", dqp, w_uq_p[i], "nt", F32)
            dcq, dqw = _rms_bwd(dqn, sv["cq"], row(q_norm_full[i]))
            G["q_norm_w"][i] = dqw[0]
            dk1, dk2 = _headsum_rope_bwd(_unheads(dkm[:, :, n0:n1]), _unheads(dkm[:, :, n1:n1 + half]), cos16, sin16)
            dkvp = jnp.concatenate([_unheads(dkm[:, :, :n0]), _unheads(dvm)], axis=1).astype(BF16)
            G["w_ukv"][i] = _unperm_ukv(_mm("mm_dw_ukv", sv["kvn"], dkvp, "tn", F32))
            dkvn = _mm("mm_dx_ukv", dkvp, w_ukv_p[i], "nt", F32)
            dckv, dkvw = _rms_bwd(dkvn, sv["ckv"], row(kv_norm_full[i]))
            G["kv_norm_w"][i] = dkvw[0]
            dproj = jnp.concatenate([_unheads(dqh), _unheads(dkh), _unheads(dvh), dcq, dckv, dk1, dk2, dfl,
                                     jnp.zeros((s, ODD_IN_PAD - ODD_IN), F32)], axis=1).astype(BF16)
            G["odd_w_in"][i] = _unperm_odd_in(_mm("mm_dw_in_odd", sv["x_in_b"], dproj, "tn", F32))
            dxb = _mm("mm_dx_in_odd", dproj, w_in_o[i], "nt", F32)
        acur, dcur = dr1, dxb
    grad_x = _axpy(acur, dcur)[None]

    gfull = {n: jnp.stack(G[n]) for n in names}
    send = jnp.concatenate([_reshard(gfull[n], ax).astype(BF16) for n, ax in _BIG], axis=1)
    send = jnp.concatenate([send, jnp.zeros((N_DEV, rows_big * _PACK_COLS - n_big), BF16)], axis=1)
    recv = _all_to_all("a2a_grads", send.reshape(N_DEV, rows_big, _PACK_COLS))
    gsum = _sum8("sum_grads", recv).reshape(-1)
    grads = {}
    off = 0
    for n, ax in _BIG:
        cnt = math.prod(P[n].shape)
        grads[n] = gsum[off:off + cnt].reshape(P[n].shape)
        off += cnt
    small = [n for n in names if n not in dict(_BIG)]
    sflat = jnp.concatenate([gfull[n].reshape(-1) for n in small])
    n_small = sflat.shape[0]
    rows_small = -(-n_small // (128 * 8)) * 8
    sflat = jnp.concatenate([sflat, jnp.zeros((rows_small * 128 - n_small,), F32)])
    sg = _sum8("sum_small_grads", _all_gather("ag_small_grads", sflat.reshape(rows_small, 128))).reshape(-1)
    off = 0
    for n in small:
        cnt = gfull[n].size
        gf = sg[off:off + cnt].reshape(gfull[n].shape)
        off += cnt
        if gf.shape != P[n].shape:
            width = P[n].shape[-1]
            gf = lax.dynamic_slice_in_dim(gf, me * width, width, axis=gf.ndim - 1)
        grads[n] = gf

    delta, new_m, new_v = {}, {}, {}
    for n, _ in _BIG:
        shp = P[n].shape
        two = lambda t: t.reshape(-1, shp[-1])
        d_, m_, v_ = _adamw("adamw_" + n, two(P[n]), two(grads[n]), two(M[n]), two(V[n]))
        delta[n], new_m[n], new_v[n] = d_.reshape(shp), m_.reshape(shp), v_.reshape(shp)

    def packs(d):
        f = jnp.concatenate([d[n].reshape(-1) for n in small])
        pad = -(-f.shape[0] // (128 * 8)) * 8 * 128 - f.shape[0]
        return jnp.concatenate([f, jnp.zeros((pad,), F32)]).reshape(-1, 128)

    d_, m_, v_ = _adamw("adamw_small", packs(P), packs(grads), packs(M), packs(V))
    off = 0
    for n in small:
        cnt = P[n].size
        for dst, src in ((delta, d_), (new_m, m_), (new_v, v_)):
            dst[n] = src.reshape(-1)[off:off + cnt].reshape(P[n].shape)
        off += cnt

    return (loss, grad_x, *[grads[n] for n in names], *[delta[n] for n in names],
            *[new_m[n] for n in names], *[new_v[n] for n in names])
```

```python
import functools
import math

import jax
import jax.numpy as jnp
import numpy as np
from jax import lax
from jax.experimental import pallas as pl
from jax.experimental.pallas import tpu as pltpu

F32 = jnp.float32
BF16 = jnp.bfloat16
HI = lax.Precision.HIGHEST

N_DEV = 8
D_MODEL = 1024
DEPTH = 4
POOL_WINDOWS = (2, 4, 8, 16)
POOL_GROUP = 128
POOL_WIDTH = 512
SSM_D_INNER = 1024
SSM_HEAD_DIM = 64
SSM_HEADS = 16
SSM_GROUPS = 2
SSM_STATE = 128
SSM_CONV = 4
SSM_CHUNK = 128
SSM_CONV_DIM = 1536
EVEN_IN = 3088
EVEN_IN_PAD = 3200
FOX_HEADS = 8
FOX_WIDTH = 512
MLA_HEADS = 8
MLA_NOPE = 64
MLA_ROPE = 32
MLA_V = 64
MLA_Q_RANK = 512
MLA_KV_RANK = 256
MLA_DK_PAD = 128
ROPE_THETA = 10000.0
FOX_SCALE = 64 ** -0.5
MLA_SCALE = (MLA_NOPE + MLA_ROPE) ** -0.5
ODD_IN = 2344
ODD_IN_PAD = 2560
D_FF = 2816
ALPHA = (2 * DEPTH) ** 0.25
LN_EPS = 1e-5
RMS_EPS = 1e-6
ADAM_LR = 0.001
ADAM_B1 = 0.9
ADAM_B2 = 0.999
ADAM_EPS = 1e-08
ADAM_WD = 0.01
ADAM_STEP = 10
NEG = -1e30
VMEM_LIMIT = 48 * 1024 * 1024


def _cp(sem):
    return pltpu.CompilerParams(dimension_semantics=sem, vmem_limit_bytes=VMEM_LIMIT)


def _dot(a, b, ca, cb, prec=None):
    return lax.dot_general(a, b, (((ca,), (cb,)), ((), ())), preferred_element_type=F32, precision=prec)


def _sigmoid(x):
    return 1.0 / (1.0 + jnp.exp(-x))


def _softplus(x):
    return jnp.maximum(x, 0.0) + jnp.log(1.0 + jnp.exp(-jnp.abs(x)))


MESH = pl.DeviceIdType.MESH
HBM_SPEC = pl.BlockSpec(memory_space=pltpu.HBM)


def _all_gather(name, xs):
    r, c_ = xs.shape

    def body(x_ref, out_ref, send_sems, recv_sems, local_sem):
        x, y, c = lax.axis_index("x"), lax.axis_index("y"), lax.axis_index("c")
        me, sibling = (x, y, c), (x, y, 1 - c)
        chips = [(1 - x, y), (x, 1 - y), (1 - x, 1 - y)]

        def rows(px, py, pc):
            return out_ref.at[4 * px + 2 * py + pc]

        def copy(k, block, to, src=None):
            return pltpu.make_async_remote_copy(
                src_ref=rows(*block) if src is None else src,
                dst_ref=rows(*block),
                send_sem=send_sems.at[k],
                recv_sem=recv_sems.at[k],
                device_id=to,
                device_id_type=MESH,
            )

        mine = pltpu.make_async_copy(x_ref, rows(*me), local_sem)
        mine.start()
        first = [copy(0, me, sibling, src=x_ref)]
        first += [copy(1 + j, me, (*chip, c), src=x_ref) for j, chip in enumerate(chips)]
        for cp in first:
            cp.start()
        passed = [copy(4 + j, (*chip, c), sibling) for j, chip in enumerate(chips)]
        for j, chip in enumerate(chips):
            copy(1 + j, (*chip, c), me).wait_recv()
            passed[j].start()
        copy(0, sibling, me).wait_recv()
        for j, chip in enumerate(chips):
            copy(4 + j, (*chip, 1 - c), me).wait_recv()
        for cp in first + passed:
            cp.wait_send()
        mine.wait()

    return pl.pallas_call(
        body,
        name=name,
        out_shape=jax.ShapeDtypeStruct((N_DEV, r, c_), xs.dtype),
        in_specs=[HBM_SPEC],
        out_specs=HBM_SPEC,
        scratch_shapes=[pltpu.SemaphoreType.DMA((7,)), pltpu.SemaphoreType.DMA((7,)), pltpu.SemaphoreType.DMA(())],
    )(xs)


def _all_to_all(name, send):
    _, r, c_ = send.shape

    def body(s_ref, r_ref, send_sems, recv_sems, local_sem):
        x, y, c = lax.axis_index("x"), lax.axis_index("y"), lax.axis_index("c")
        me = 4 * x + 2 * y + c
        mine = pltpu.make_async_copy(s_ref.at[me], r_ref.at[me], local_sem)
        mine.start()
        copies = []
        for k in range(1, N_DEV):
            tx = 1 - x if k & 4 else x
            ty = 1 - y if k & 2 else y
            tc = 1 - c if k & 1 else c
            peer = 4 * tx + 2 * ty + tc
            cp = pltpu.make_async_remote_copy(
                src_ref=s_ref.at[peer],
                dst_ref=r_ref.at[me],
                send_sem=send_sems.at[k - 1],
                recv_sem=recv_sems.at[k - 1],
                device_id=(tx, ty, tc),
                device_id_type=MESH,
            )
            cp.start()
            landing = pltpu.make_async_remote_copy(
                src_ref=s_ref.at[me],
                dst_ref=r_ref.at[peer],
                send_sem=send_sems.at[k - 1],
                recv_sem=recv_sems.at[k - 1],
                device_id=(tx, ty, tc),
                device_id_type=MESH,
            )
            copies.append((cp, landing))
        for cp, landing in copies:
            landing.wait_recv()
        for cp, landing in copies:
            cp.wait_send()
        mine.wait()

    return pl.pallas_call(
        body,
        name=name,
        out_shape=jax.ShapeDtypeStruct(send.shape, send.dtype),
        in_specs=[HBM_SPEC],
        out_specs=HBM_SPEC,
        scratch_shapes=[pltpu.SemaphoreType.DMA((7,)), pltpu.SemaphoreType.DMA((7,)), pltpu.SemaphoreType.DMA(())],
    )(send)


def _pick(n, cands):
    for t in cands:
        if n % t == 0:
            return t
    return n


def _mm(name, a, b, mode, out_dtype):
    if mode == "nn":
        (m, k), n = a.shape, b.shape[1]
    elif mode == "nt":
        (m, k), n = a.shape, b.shape[0]
    else:
        (k, m), n = a.shape, b.shape[1]
    tm = _pick(m, (512, 256, 128))
    tn = _pick(n, (1408, 1280, 1024, 768, 640, 512, 384, 256, 128))
    tk = _pick(k, (1024, 1408, 768, 640, 512, 256, 128))
    nk = k // tk
    swap = nk == 1 and a.size * a.dtype.itemsize * (n // tn) + b.size * b.dtype.itemsize < (
        a.size * a.dtype.itemsize + b.size * b.dtype.itemsize * (m // tm))
    ij = (lambda g0, g1: (g1, g0)) if swap else (lambda g0, g1: (g0, g1))
    if mode == "nn":
        a_spec = pl.BlockSpec((tm, tk), lambda g0, g1, kk: (ij(g0, g1)[0], kk))
        b_spec = pl.BlockSpec((tk, tn), lambda g0, g1, kk: (kk, ij(g0, g1)[1]))
        ca, cb = 1, 0
    elif mode == "nt":
        a_spec = pl.BlockSpec((tm, tk), lambda g0, g1, kk: (ij(g0, g1)[0], kk))
        b_spec = pl.BlockSpec((tn, tk), lambda g0, g1, kk: (ij(g0, g1)[1], kk))
        ca, cb = 1, 1
    else:
        a_spec = pl.BlockSpec((tk, tm), lambda g0, g1, kk: (kk, ij(g0, g1)[0]))
        b_spec = pl.BlockSpec((tk, tn), lambda g0, g1, kk: (kk, ij(g0, g1)[1]))
        ca, cb = 0, 0

    def kern(a_ref, b_ref, o_ref, acc):
        kk = pl.program_id(2)

        @pl.when(kk == 0)
        def _():
            acc[...] = jnp.zeros_like(acc)

        acc[...] += _dot(a_ref[...].astype(BF16), b_ref[...].astype(BF16), ca, cb)

        @pl.when(kk == nk - 1)
        def _():
            o_ref[...] = acc[...].astype(out_dtype)

    return pl.pallas_call(
        kern,
        name=name,
        grid=(n // tn, m // tm, nk) if swap else (m // tm, n // tn, nk),
        in_specs=[a_spec, b_spec],
        out_specs=pl.BlockSpec((tm, tn), lambda g0, g1, kk: ij(g0, g1)),
        out_shape=jax.ShapeDtypeStruct((m, n), out_dtype),
        scratch_shapes=[pltpu.VMEM((tm, tn), F32)],
        compiler_params=_cp(("parallel", "parallel", "arbitrary")),
    )(a, b)


def _rowwise(name, body, ins, outs, tm):
    n_rows = next(a.shape[0] for a, kind in ins if kind == "row")
    n = n_rows // tm
    in_specs = []
    for a, kind in ins:
        if kind == "full":
            in_specs.append(pl.BlockSpec(a.shape, lambda i, nd=a.ndim: (0,) * nd))
        elif kind == "row":
            in_specs.append(pl.BlockSpec((tm, a.shape[1]), lambda i: (i, 0)))
        elif kind == "prev":
            in_specs.append(pl.BlockSpec((tm, a.shape[1]), lambda i: (jnp.maximum(i - 1, 0), 0)))
        else:
            in_specs.append(pl.BlockSpec((tm, a.shape[1]), lambda i: (jnp.minimum(i + 1, n - 1), 0)))
    out_specs, out_shape = [], []
    for shp, dt, kind in outs:
        out_shape.append(jax.ShapeDtypeStruct(shp, dt))
        if kind == "row":
            out_specs.append(pl.BlockSpec((tm, shp[1]), lambda i: (i, 0)))
        else:
            out_specs.append(pl.BlockSpec(shp, lambda i, nd=len(shp): (0,) * nd))
    n_in = len(ins)

    def kern(*refs):
        i = pl.program_id(0)
        res = body(i, n, *[r if kind == "full" else r[...] for r, (_, kind) in zip(refs[:n_in], ins)])
        for (shp, dt, kind), val, o in zip(outs, res, refs[n_in:]):
            if kind == "row":
                o[...] = val.astype(dt)
            else:

                @pl.when(i == 0)
                def _(o=o):
                    o[...] = jnp.zeros_like(o)

                o[...] += val.astype(dt)

    return pl.pallas_call(
        kern,
        name=name,
        grid=(n,),
        in_specs=in_specs,
        out_specs=out_specs,
        out_shape=out_shape,
        compiler_params=_cp(("arbitrary",)),
    )(*[a for a, _ in ins])


def _tm(s, t):
    return min(s, t)


def _ln_fwd(x, h, g, b):
    s = x.shape[0]

    def body(i, n, xv, hv, gv, bv):
        r = ALPHA * xv + hv.astype(F32)
        mu = jnp.mean(r, axis=-1, keepdims=True)
        d = r - mu
        var = jnp.mean(d * d, axis=-1, keepdims=True)
        y = d * lax.rsqrt(var + LN_EPS) * gv[...] + bv[...]
        return y, y, r

    shp = (s, D_MODEL)
    return _rowwise("ln_fwd", body, [(x, "row"), (h, "row"), (g, "full"), (b, "full")],
                    [(shp, F32, "row"), (shp, BF16, "row"), (shp, F32, "row")], _tm(s, 256))


def _ln_bwd(a, bterm, r, g):
    s = r.shape[0]

    def body(i, n, *vals):
        if a is None:
            dyv, rv, gv = vals
        else:
            av, dyv, rv, gv = vals
            dyv = ALPHA * av + dyv
        mu = jnp.mean(rv, axis=-1, keepdims=True)
        d = rv - mu
        var = jnp.mean(d * d, axis=-1, keepdims=True)
        rstd = lax.rsqrt(var + LN_EPS)
        xhat = d * rstd
        dxh = dyv * gv[...]
        dr = rstd * (dxh - jnp.mean(dxh, axis=-1, keepdims=True) - xhat * jnp.mean(dxh * xhat, axis=-1, keepdims=True))
        return dr, dr, jnp.sum(dyv * xhat, axis=0, keepdims=True), jnp.sum(dyv, axis=0, keepdims=True)

    ins = ([] if a is None else [(a, "row")]) + [(bterm, "row"), (r, "row"), (g, "full")]
    shp = (s, D_MODEL)
    return _rowwise("ln_bwd" if a is None else "ln_bwd_res", body, ins,
                    [(shp, F32, "row"), (shp, BF16, "row"), ((1, D_MODEL), F32, "acc"), ((1, D_MODEL), F32, "acc")],
                    _tm(s, 256))


def _axpy(a, b):
    def body(i, n, av, bv):
        return (ALPHA * av + bv,)

    return _rowwise("axpy", body, [(a, "row"), (b, "row")], [(a.shape, F32, "row")], _tm(a.shape[0], 256))[0]


def _loss_head(y, target):
    s = y.shape[0]

    def body(i, n, yv, tv):
        err = yv - tv
        part = 0.5 * jnp.sum(jnp.mean(err * err, axis=-1, keepdims=True), axis=0, keepdims=True)
        return err * (1.0 / D_MODEL), part

    return _rowwise("loss_head", body, [(y, "row"), (target, "row")],
                    [((s, D_MODEL), F32, "row"), ((1, 1), F32, "acc")], _tm(s, 256))


def _swiglu_fwd(gu):
    s = gu.shape[0]

    def body(i, n, v):
        g, u = v[:, :D_FF], v[:, D_FF:]
        return (g * _sigmoid(g) * u,)

    return _rowwise("swiglu_fwd", body, [(gu, "row")], [((s, D_FF), BF16, "row")], _tm(s, 256))[0]


def _swiglu_bwd(da, gu):
    s = gu.shape[0]

    def body(i, n, dav, v):
        g, u = v[:, :D_FF], v[:, D_FF:]
        sg = _sigmoid(g)
        dg = dav * u * (sg * (1.0 + g * (1.0 - sg)))
        du = dav * (g * sg)
        return (jnp.concatenate([dg, du], axis=1),)

    return _rowwise("swiglu_bwd", body, [(da, "row"), (gu, "row")], [((s, 2 * D_FF), BF16, "row")], _tm(s, 256))[0]


def _rms_fwd(x, w):
    s, c = x.shape

    def body(i, n, xv, wv):
        rs = lax.rsqrt(jnp.mean(xv * xv, axis=-1, keepdims=True) + RMS_EPS)
        return (xv * rs * wv[...],)

    return _rowwise("rms_fwd", body, [(x, "row"), (w, "full")], [((s, c), BF16, "row")], _tm(s, 512))[0]


def _rms_bwd(dy, x, w):
    s, c = x.shape

    def body(i, n, dyv, xv, wv):
        rs = lax.rsqrt(jnp.mean(xv * xv, axis=-1, keepdims=True) + RMS_EPS)
        nv = xv * rs
        dn = dyv * wv[...]
        dx = rs * (dn - nv * jnp.mean(dn * nv, axis=-1, keepdims=True))
        return dx, jnp.sum(dyv * nv, axis=0, keepdims=True)

    return _rowwise("rms_bwd", body, [(dy, "row"), (x, "row"), (w, "full")],
                    [((s, c), F32, "row"), ((1, c), F32, "acc")], _tm(s, 512))


def _gated_rms_fwd(y, z, w):
    s, c = y.shape

    def body(i, n, yv, zv, wv):
        y2 = yv * (zv * _sigmoid(zv))
        rs = lax.rsqrt(jnp.mean(y2 * y2, axis=-1, keepdims=True) + RMS_EPS)
        return (y2 * rs * wv[...],)

    return _rowwise("gated_rms_fwd", body, [(y, "row"), (z, "row"), (w, "full")], [((s, c), BF16, "row")], _tm(s, 256))[0]


def _gated_rms_bwd(do, y, z, w):
    s, c = y.shape

    def body(i, n, dov, yv, zv, wv):
        sz = _sigmoid(zv)
        silu = zv * sz
        y2 = yv * silu
        rs = lax.rsqrt(jnp.mean(y2 * y2, axis=-1, keepdims=True) + RMS_EPS)
        nv = y2 * rs
        dn = dov * wv[...]
        dy2 = rs * (dn - nv * jnp.mean(dn * nv, axis=-1, keepdims=True))
        return dy2 * silu, dy2 * yv * (sz * (1.0 + zv * (1.0 - sz))), jnp.sum(dov * nv, axis=0, keepdims=True)

    return _rowwise("gated_rms_bwd", body, [(do, "row"), (y, "row"), (z, "row"), (w, "full")],
                    [((s, c), F32, "row"), ((s, c), F32, "row"), ((1, c), F32, "acc")], _tm(s, 256))


def _rope(name, r1, r2, cos, sin):
    def body(i, n, a, b, cv, sv):
        return a * cv - b * sv, b * cv + a * sv

    return _rowwise(name, body, [(r1, "row"), (r2, "row"), (cos, "row"), (sin, "row")],
                    [(r1.shape, F32, "row"), (r1.shape, F32, "row")], _tm(r1.shape[0], 512))


def _headsum_rope_bwd(d1_all, d2_all, cos, sin):
    s = d1_all.shape[0]
    half = MLA_ROPE // 2

    def body(i, n, a_all, b_all, cv, sv):
        rr = lax.broadcasted_iota(jnp.int32, (MLA_HEADS * half, half), 0)
        cc = lax.broadcasted_iota(jnp.int32, (MLA_HEADS * half, half), 1)
        sel = jnp.where(rr % half == cc, 1.0, 0.0).astype(F32)
        a = _dot(a_all, sel, 1, 0, HI)
        b = _dot(b_all, sel, 1, 0, HI)
        return a * cv + b * sv, b * cv - a * sv

    return _rowwise("headsum_rope_bwd", body, [(d1_all, "row"), (d2_all, "row"), (cos, "row"), (sin, "row")],
                    [((s, half), F32, "row"), ((s, half), F32, "row")], _tm(s, 512))


def _softplus_fwd(dtr, bias):
    def body(i, n, v, bv):
        return (_softplus(v + bv[...]),)

    return _rowwise("softplus_fwd", body, [(dtr, "row"), (bias, "full")], [(dtr.shape, F32, "row")], _tm(dtr.shape[0], 1024))[0]


def _softplus_bwd(ddt, dtr, bias):
    def body(i, n, dv, v, bv):
        d = dv * _sigmoid(v + bv[...])
        return d, jnp.sum(d, axis=0, keepdims=True)

    return _rowwise("softplus_bwd", body, [(ddt, "row"), (dtr, "row"), (bias, "full")],
                    [(dtr.shape, F32, "row"), ((1, dtr.shape[1]), F32, "acc")], _tm(dtr.shape[0], 1024))


def _rowdot(a, b):
    def body(i, n, av, bv):
        return (jnp.sum(av.astype(F32) * bv.astype(F32), axis=-1, keepdims=True),)

    return _rowwise("rowdot", body, [(a, "row"), (b, "row")], [((a.shape[0], 1), F32, "row")], _tm(a.shape[0], 2048))[0]


def _sum8(name, blocks):
    _, r, c = blocks.shape
    tr = _pick(r, (512, 256, 128, 64, 32, 16, 8))

    def kern(b_ref, o_ref):
        acc = b_ref[0].astype(F32)
        for d in range(1, N_DEV):
            acc = acc + b_ref[d].astype(F32)
        o_ref[...] = acc

    return pl.pallas_call(
        kern,
        name=name,
        grid=(r // tr,),
        in_specs=[pl.BlockSpec((N_DEV, tr, c), lambda i: (0, i, 0))],
        out_specs=pl.BlockSpec((tr, c), lambda i: (i, 0)),
        out_shape=jax.ShapeDtypeStruct((r, c), F32),
        compiler_params=_cp(("parallel",)),
    )(blocks)


def _adamw(name, w, g, m, v):
    r = w.shape[0]
    tm = _pick(r, (512, 256, 128, 64, 32, 16, 8))

    def body(i, n, wv, gv, mv, vv):
        m2 = ADAM_B1 * mv + (1.0 - ADAM_B1) * gv
        v2 = ADAM_B2 * vv + (1.0 - ADAM_B2) * (gv * gv)
        m_hat = m2 / (1.0 - ADAM_B1 ** ADAM_STEP)
        v_hat = v2 / (1.0 - ADAM_B2 ** ADAM_STEP)
        delta = -ADAM_LR * (m_hat / (jnp.sqrt(v_hat) + ADAM_EPS) + ADAM_WD * wv)
        return delta, m2, v2

    return _rowwise(name, body, [(w, "row"), (g, "row"), (m, "row"), (v, "row")],
                    [(w.shape, F32, "row")] * 3, tm)


def _pool_diff(i, t_rows, u, up):
    ucat = jnp.concatenate([jnp.where(i > 0, up, 0.0), u], axis=0)
    r = lax.broadcasted_iota(jnp.int32, (t_rows, 2 * t_rows), 0)
    cc = lax.broadcasted_iota(jnp.int32, (t_rows, 2 * t_rows), 1)
    lag = r + t_rows - cc
    t_col = i * t_rows + lax.broadcasted_iota(jnp.int32, (t_rows, 1), 0)
    diffs = []
    for gi, wdw in enumerate(POOL_WINDOWS):
        win = jnp.where((lag >= 0) & (lag < wdw), 1.0, 0.0).astype(F32)
        cnt = jnp.minimum(t_col + 1, wdw).astype(F32)
        ws = _dot(win, ucat[:, gi * POOL_GROUP:(gi + 1) * POOL_GROUP], 1, 0, HI)
        diffs.append(ws / cnt - u[:, gi * POOL_GROUP:(gi + 1) * POOL_GROUP])
    return diffs


def _pool_fwd(u, pw, sc):
    s = u.shape[0]
    tm = _tm(s, 256)

    def body(i, n, uv, upv, wv, scv):
        diffs = _pool_diff(i, tm, uv, upv)
        ys = [_dot(d.astype(BF16), wv[gi].astype(BF16), 1, 0) for gi, d in enumerate(diffs)]
        return (jnp.concatenate(ys, axis=1) * scv[...],)

    return _rowwise("pool_fwd", body, [(u, "row"), (u, "prev"), (pw, "full"), (sc, "full")],
                    [((s, POOL_WIDTH), BF16, "row")], tm)[0]


def _pool_bwd1(u, dy, pw, sc):
    s = u.shape[0]
    tm = _tm(s, 256)

    def body(i, n, uv, upv, dyv, wv, scv):
        diffs = _pool_diff(i, tm, uv, upv)
        scv = scv[...]
        dsc, dws, dds = [], [], []
        for gi, d in enumerate(diffs):
            sl = slice(gi * POOL_GROUP, (gi + 1) * POOL_GROUP)
            db, wb = d.astype(BF16), wv[gi].astype(BF16)
            yg = _dot(db, wb, 1, 0)
            dsc.append(jnp.sum(yg * dyv[:, sl], axis=0, keepdims=True))
            eb = (dyv[:, sl] * scv[:, sl]).astype(BF16)
            dws.append(_dot(db, eb, 0, 0)[None])
            dds.append(_dot(eb, wb, 1, 1))
        return jnp.concatenate(dds, axis=1), jnp.concatenate(dsc, axis=1), jnp.concatenate(dws, axis=0)

    return _rowwise("pool_bwd1", body, [(u, "row"), (u, "prev"), (dy, "row"), (pw, "full"), (sc, "full")],
                    [((s, POOL_WIDTH), F32, "row"), ((1, POOL_WIDTH), F32, "acc"), (pw.shape, F32, "acc")], tm)


def _pool_bwd2(dd):
    s = dd.shape[0]
    tm = _tm(s, 256)

    def body(i, n, dv, dnv):
        dcat = jnp.concatenate([dv, jnp.where(i < n - 1, dnv, 0.0)], axis=0)
        r = lax.broadcasted_iota(jnp.int32, (tm, 2 * tm), 0)
        cc = lax.broadcasted_iota(jnp.int32, (tm, 2 * tm), 1)
        lag = cc - r
        t_col = i * tm + lax.broadcasted_iota(jnp.int32, (2 * tm, 1), 0)
        outs = []
        for gi, wdw in enumerate(POOL_WINDOWS):
            sl = slice(gi * POOL_GROUP, (gi + 1) * POOL_GROUP)
            win = jnp.where((lag >= 0) & (lag < wdw), 1.0, 0.0).astype(F32)
            cnt = jnp.minimum(t_col + 1, wdw).astype(F32)
            outs.append(_dot(win, dcat[:, sl] / cnt, 1, 0, HI) - dv[:, sl])
        return (jnp.concatenate(outs, axis=1),)

    return _rowwise("pool_bwd2", body, [(dd, "row"), (dd, "next")], [((s, POOL_WIDTH), BF16, "row")], tm)[0]


def _shift_down(cur, other, j, tm):
    if j == 0:
        return cur
    rows = lax.broadcasted_iota(jnp.int32, cur.shape, 0)
    return jnp.where(rows < j, pltpu.roll(other, j, 0), pltpu.roll(cur, j, 0))


def _shift_up(cur, other, j, tm):
    if j == 0:
        return cur
    rows = lax.broadcasted_iota(jnp.int32, cur.shape, 0)
    return jnp.where(rows >= tm - j, pltpu.roll(other, tm - j, 0), pltpu.roll(cur, tm - j, 0))


def _conv_pre(i, tm, xv, xpv, wv, bv):
    xpv = jnp.where(i > 0, xpv, 0.0)
    taps = [_shift_down(xv, xpv, SSM_CONV - 1 - k, tm) for k in range(SSM_CONV)]
    pre = bv[...]
    for k in range(SSM_CONV):
        pre = pre + wv[k:k + 1, :] * taps[k]
    return pre, taps


def _conv_fwd(xbc, w, b):
    s = xbc.shape[0]
    tm = _tm(s, 256)

    def body(i, n, xv, xpv, wv, bv):
        pre, _ = _conv_pre(i, tm, xv, xpv, wv, bv)
        return (pre * _sigmoid(pre),)

    return _rowwise("conv_fwd", body, [(xbc, "row"), (xbc, "prev"), (w, "full"), (b, "full")],
                    [(xbc.shape, F32, "row")], tm)[0]


def _conv_bwd1(dxc, xbc, w, b):
    s, c = xbc.shape
    tm = _tm(s, 256)

    def body(i, n, dv, xv, xpv, wv, bv):
        pre, taps = _conv_pre(i, tm, xv, xpv, wv, bv)
        sg = _sigmoid(pre)
        dpre = dv * (sg * (1.0 + pre * (1.0 - sg)))
        tap_row = lax.broadcasted_iota(jnp.int32, (SSM_CONV, c), 0)
        dw = jnp.zeros((SSM_CONV, c), F32)
        for k in range(SSM_CONV):
            dw = dw + jnp.where(tap_row == k, jnp.sum(dpre * taps[k], axis=0, keepdims=True), 0.0)
        return dpre, dw, jnp.sum(dpre, axis=0, keepdims=True)

    return _rowwise("conv_bwd1", body, [(dxc, "row"), (xbc, "row"), (xbc, "prev"), (w, "full"), (b, "full")],
                    [((s, c), F32, "row"), ((SSM_CONV, c), F32, "acc"), ((1, c), F32, "acc")], tm)


def _conv_bwd2(dpre, w):
    s, c = dpre.shape
    tm = _tm(s, 256)

    def body(i, n, dv, dnv, wv):
        dnv = jnp.where(i < n - 1, dnv, 0.0)
        out = jnp.zeros_like(dv)
        for k in range(SSM_CONV):
            out = out + wv[k:k + 1, :] * _shift_up(dv, dnv, SSM_CONV - 1 - k, tm)
        return (out,)

    return _rowwise("conv_bwd2", body, [(dpre, "row"), (dpre, "next"), (w, "full")], [((s, c), BF16, "row")], tm)[0]


def _ssd_common(alog, dt_c, dt_r, tril, tri):
    a = -jnp.exp(alog)
    acs_c = _dot(tril, dt_c * a, 1, 0, HI)
    acs_r = _dot(dt_r * a, tril, 1, 1, HI)
    a_last = jnp.sum(dt_c * a, axis=0, keepdims=True)
    return a, acs_c, acs_r, a_last


def _ssd_fwd(xh, xht, dtc, dtr, bm, cm, alog, dsk):
    nh, s, p = xh.shape
    L = SSM_CHUNK
    nc = s // L
    E = nh // SSM_GROUPS

    def kern(xh_ref, xht_ref, dtc_ref, dtr_ref, b_ref, c_ref, alog_ref, dsk_ref, y_ref, hp_ref, h_s):
        c = pl.program_id(1)

        @pl.when(c == 0)
        def _():
            h_s[...] = jnp.zeros_like(h_s)

        bb = b_ref[...].astype(BF16)
        cb_ = c_ref[...].astype(BF16)
        cbm = _dot(cb_, bb, 1, 1)
        ri = lax.broadcasted_iota(jnp.int32, (L, L), 0)
        ci = lax.broadcasted_iota(jnp.int32, (L, L), 1)
        tri = ri >= ci
        tril = tri.astype(F32)
        for e in range(E):
            dt_c, dt_r = dtc_ref[e], dtr_ref[e]
            a, acs_c, acs_r, a_last = _ssd_common(alog_ref[e], dt_c, dt_r, tril, tri)
            lam = jnp.exp(jnp.where(tri, acs_c - acs_r, NEG))
            x = xh_ref[e]
            xdt = (x * dt_c).astype(BF16)
            hh = h_s[e]
            y = _dot((cbm * lam).astype(BF16), xdt, 1, 0)
            y = y + _dot(cb_, hh.astype(BF16), 1, 1) * jnp.exp(acs_c) + x * dsk_ref[e]
            y_ref[e] = y
            hp_ref[e] = hh
            xw = (xht_ref[e] * (dt_r * jnp.exp(a_last - acs_r))).astype(BF16)
            h_s[e] = hh * jnp.exp(a_last) + _dot(xw, bb, 1, 0)

    return pl.pallas_call(
        kern,
        name="ssd_fwd",
        grid=(SSM_GROUPS, nc),
        in_specs=[
            pl.BlockSpec((E, L, p), lambda g, c: (g, c, 0)),
            pl.BlockSpec((E, p, L), lambda g, c: (g, 0, c)),
            pl.BlockSpec((E, L, 1), lambda g, c: (g, c, 0)),
            pl.BlockSpec((E, 1, L), lambda g, c: (g, 0, c)),
            pl.BlockSpec((L, SSM_STATE), lambda g, c: (c, g)),
            pl.BlockSpec((L, SSM_STATE), lambda g, c: (c, g)),
            pl.BlockSpec((E, 1, 1), lambda g, c: (g, 0, 0)),
            pl.BlockSpec((E, 1, 1), lambda g, c: (g, 0, 0)),
        ],
        out_specs=[
            pl.BlockSpec((E, L, p), lambda g, c: (g, c, 0)),
            pl.BlockSpec((E, None, p, SSM_STATE), lambda g, c: (g, c, 0, 0)),
        ],
        out_shape=[jax.ShapeDtypeStruct((nh, s, p), F32), jax.ShapeDtypeStruct((nh, nc, p, SSM_STATE), F32)],
        scratch_shapes=[pltpu.VMEM((E, p, SSM_STATE), F32)],
        compiler_params=_cp(("arbitrary", "arbitrary")),
    )(xh, xht, dtc, dtr, bm, cm, alog, dsk)


def _ssd_bwd(dy, dyt, xh, dtc, dtr, bm, cm, hprev, alog, dsk):
    nh, s, p = xh.shape
    L = SSM_CHUNK
    nc = s // L
    E = nh // SSM_GROUPS

    def kern(dy_ref, dyt_ref, xh_ref, dtc_ref, dtr_ref, b_ref, c_ref, hp_ref, alog_ref, dsk_ref,
             dx_ref, ddt_ref, db_ref, dc_ref, dalog_ref, dd_ref, dh_s):
        c = pl.program_id(1)

        @pl.when(c == 0)
        def _():
            dh_s[...] = jnp.zeros_like(dh_s)
            dalog_ref[...] = jnp.zeros_like(dalog_ref)
            dd_ref[...] = jnp.zeros_like(dd_ref)

        bb = b_ref[...].astype(BF16)
        cb_ = c_ref[...].astype(BF16)
        cbm = _dot(cb_, bb, 1, 1)
        cbt = _dot(bb, cb_, 1, 1)
        ri = lax.broadcasted_iota(jnp.int32, (L, L), 0)
        ci = lax.broadcasted_iota(jnp.int32, (L, L), 1)
        tri = ri >= ci
        trit = ci >= ri
        tril = tri.astype(F32)
        triu = trit.astype(F32)
        db_acc = jnp.zeros((L, SSM_STATE), F32)
        dc_acc = jnp.zeros((L, SSM_STATE), F32)
        for e in range(E):
            dt_c, dt_r = dtc_ref[e], dtr_ref[e]
            a, acs_c, acs_r, a_last = _ssd_common(alog_ref[e], dt_c, dt_r, tril, tri)
            lam = jnp.exp(jnp.where(tri, acs_c - acs_r, NEG))
            lamt = jnp.exp(jnp.where(trit, acs_r - acs_c, NEG))
            x = xh_ref[e]
            xdt = x * dt_c
            xdtb = xdt.astype(BF16)
            dyv = dy_ref[e]
            dyb = dyv.astype(BF16)
            hh = hp_ref[e]
            hb = hh.astype(BF16)
            dhn = dh_s[e]
            dhnb = dhn.astype(BF16)
            ea_c = jnp.exp(acs_c)
            decay_c = jnp.exp(a_last - acs_c)
            e_last = jnp.exp(a_last)
            gm = _dot(dyb, xdtb, 1, 1)
            gt = _dot(xdtb, dyb, 1, 1)
            bdh = _dot(bb, dhnb, 1, 1)
            dxdt = _dot((cbt * lamt).astype(BF16), dyb, 1, 0) + bdh * decay_c
            dcb = gm * lam
            dcbt = gt * lamt
            yoff = _dot(cb_, hb, 1, 1) * ea_c
            dc_acc = dc_acc + _dot(dcb.astype(BF16), bb, 1, 0) + _dot(dyb, hb, 1, 0) * ea_c
            db_acc = db_acc + _dot(dcbt.astype(BF16), cb_, 1, 0) + _dot(xdtb, dhnb, 1, 0) * decay_c
            tt = decay_c * jnp.sum(xdt * bdh, axis=1, keepdims=True)
            dacs = (jnp.sum(dcb * cbm, axis=1, keepdims=True) - jnp.sum(dcbt * cbt, axis=1, keepdims=True)
                    + jnp.sum(dyv * yoff, axis=1, keepdims=True) - tt)
            tail = jnp.sum(tt, axis=0, keepdims=True) + e_last * jnp.sum(jnp.sum(dhn * hh, axis=1, keepdims=True), axis=0, keepdims=True)
            dda = _dot(triu, dacs, 1, 0, HI) + tail
            ddt_ref[e] = dda * a + jnp.sum(dxdt * x, axis=1, keepdims=True)
            dalog_ref[e] += jnp.sum(dda * dt_c, axis=0, keepdims=True) * a
            dd_ref[e] += jnp.sum(jnp.sum(dyv * x, axis=1, keepdims=True), axis=0, keepdims=True)
            dx_ref[e] = dxdt * dt_c + dyv * dsk_ref[e]
            dyw = (dyt_ref[e] * jnp.exp(acs_r)).astype(BF16)
            dh_s[e] = dhn * e_last + _dot(dyw, cb_, 1, 0)
        db_ref[...] = db_acc
        dc_ref[...] = dc_acc

    rc = lambda c: nc - 1 - c
    return pl.pallas_call(
        kern,
        name="ssd_bwd",
        grid=(SSM_GROUPS, nc),
        in_specs=[
            pl.BlockSpec((E, L, p), lambda g, c: (g, rc(c), 0)),
            pl.BlockSpec((E, p, L), lambda g, c: (g, 0, rc(c))),
            pl.BlockSpec((E, L, p), lambda g, c: (g, rc(c), 0)),
            pl.BlockSpec((E, L, 1), lambda g, c: (g, rc(c), 0)),
            pl.BlockSpec((E, 1, L), lambda g, c: (g, 0, rc(c))),
            pl.BlockSpec((L, SSM_STATE), lambda g, c: (rc(c), g)),
            pl.BlockSpec((L, SSM_STATE), lambda g, c: (rc(c), g)),
            pl.BlockSpec((E, None, p, SSM_STATE), lambda g, c: (g, rc(c), 0, 0)),
            pl.BlockSpec((E, 1, 1), lambda g, c: (g, 0, 0)),
            pl.BlockSpec((E, 1, 1), lambda g, c: (g, 0, 0)),
        ],
        out_specs=[
            pl.BlockSpec((E, L, p), lambda g, c: (g, rc(c), 0)),
            pl.BlockSpec((E, L, 1), lambda g, c: (g, rc(c), 0)),
            pl.BlockSpec((L, SSM_STATE), lambda g, c: (rc(c), g)),
            pl.BlockSpec((L, SSM_STATE), lambda g, c: (rc(c), g)),
            pl.BlockSpec((E, 1, 1), lambda g, c: (g, 0, 0)),
            pl.BlockSpec((E, 1, 1), lambda g, c: (g, 0, 0)),
        ],
        out_shape=[
            jax.ShapeDtypeStruct((nh, s, p), F32),
            jax.ShapeDtypeStruct((nh, s, 1), F32),
            jax.ShapeDtypeStruct((s, SSM_GROUPS * SSM_STATE), F32),
            jax.ShapeDtypeStruct((s, SSM_GROUPS * SSM_STATE), F32),
            jax.ShapeDtypeStruct((nh, 1, 1), F32),
            jax.ShapeDtypeStruct((nh, 1, 1), F32),
        ],
        scratch_shapes=[pltpu.VMEM((E, p, SSM_STATE), F32)],
        compiler_params=_cp(("arbitrary", "arbitrary")),
    )(dy, dyt, xh, dtc, dtr, bm, cm, hprev, alog, dsk)


def _fgate_fwd(fl, bias):
    s, h = fl.shape
    t = _tm(s, 512)

    def kern(fl_ref, b_ref, o_ref, carry):
        i = pl.program_id(0)

        @pl.when(i == 0)
        def _():
            carry[...] = jnp.zeros_like(carry)

        z = fl_ref[...] + b_ref[...]
        lf = -_softplus(-z)
        ri = lax.broadcasted_iota(jnp.int32, (t, t), 0)
        ci = lax.broadcasted_iota(jnp.int32, (t, t), 1)
        o_ref[...] = _dot((ri >= ci).astype(F32), lf, 1, 0, HI) + carry[...]
        carry[...] += jnp.sum(lf, axis=0, keepdims=True)

    return pl.pallas_call(
        kern,
        name="fgate_fwd",
        grid=(s // t,),
        in_specs=[pl.BlockSpec((t, h), lambda i: (i, 0)), pl.BlockSpec((1, h), lambda i: (0, 0))],
        out_specs=pl.BlockSpec((t, h), lambda i: (i, 0)),
        out_shape=jax.ShapeDtypeStruct((s, h), F32),
        scratch_shapes=[pltpu.VMEM((1, h), F32)],
        compiler_params=_cp(("arbitrary",)),
    )(fl, bias)


def _fgate_bwd(dfq, dfk, fl, bias):
    s, h = fl.shape
    t = _tm(s, 512)
    n = s // t

    def kern(dq_ref, dk_ref, fl_ref, b_ref, o_ref, db_ref, carry):
        i = pl.program_id(0)

        @pl.when(i == 0)
        def _():
            carry[...] = jnp.zeros_like(carry)
            db_ref[...] = jnp.zeros_like(db_ref)

        ri = lax.broadcasted_iota(jnp.int32, (t, t), 0)
        ci = lax.broadcasted_iota(jnp.int32, (t, t), 1)
        d = dq_ref[...] - dk_ref[...]
        rev = _dot((ci >= ri).astype(F32), d, 1, 0, HI) + carry[...]
        carry[...] += jnp.sum(d, axis=0, keepdims=True)
        dz = rev * _sigmoid(-(fl_ref[...] + b_ref[...]))
        o_ref[...] = dz
        db_ref[...] += jnp.sum(dz, axis=0, keepdims=True)

    return pl.pallas_call(
        kern,
        name="fgate_bwd",
        grid=(n,),
        in_specs=[pl.BlockSpec((t, h), lambda i: (n - 1 - i, 0)), pl.BlockSpec((t, h), lambda i: (n - 1 - i, 0)),
                  pl.BlockSpec((t, h), lambda i: (n - 1 - i, 0)), pl.BlockSpec((1, h), lambda i: (0, 0))],
        out_specs=[pl.BlockSpec((t, h), lambda i: (n - 1 - i, 0)), pl.BlockSpec((1, h), lambda i: (0, 0))],
        out_shape=[jax.ShapeDtypeStruct((s, h), F32), jax.ShapeDtypeStruct((1, h), F32)],
        scratch_shapes=[pltpu.VMEM((1, h), F32)],
        compiler_params=_cp(("arbitrary",)),
    )(dfq, dfk, fl, bias)


HPS = 2


def _attn_scores(q, k, fq, fk, scale, masked, t):
    sc = _dot(q, k, 1, 1)
    if scale is not None:
        sc = sc * scale
    if fq is not None:
        sc = sc + fq - fk
    if masked:
        rows = lax.broadcasted_iota(jnp.int32, (t, t), 0)
        cols = lax.broadcasted_iota(jnp.int32, (t, t), 1)
        sc = jnp.where(cols <= rows, sc, NEG)
    return sc


def _flash_fwd(name, q, k, v, fq, fk, scale):
    nh, s, dk = q.shape
    dv = v.shape[-1]
    t = _tm(s, 512)
    nq = s // t
    bias = fq is not None

    def kern(*refs):
        if bias:
            q_ref, k_ref, v_ref, fq_ref, fk_ref, o_ref, lse_ref, m_s, l_s, acc_s = refs
        else:
            q_ref, k_ref, v_ref, o_ref, lse_ref, m_s, l_s, acc_s = refs
        qi, kj = pl.program_id(1), pl.program_id(2)

        @pl.when(kj == 0)
        def _():
            m_s[...] = jnp.full_like(m_s, NEG)
            l_s[...] = jnp.zeros_like(l_s)
            acc_s[...] = jnp.zeros_like(acc_s)

        def step(masked):
            for hh in range(HPS):
                sc = _attn_scores(q_ref[hh], k_ref[hh], fq_ref[hh] if bias else None, fk_ref[hh] if bias else None,
                                  scale, masked, t)
                m_old = m_s[hh]
                m_new = jnp.maximum(m_old, jnp.max(sc, axis=1, keepdims=True))
                corr = jnp.exp(m_old - m_new)
                p = jnp.exp(sc - m_new)
                l_s[hh] = corr * l_s[hh] + jnp.sum(p, axis=1, keepdims=True)
                acc_s[hh] = acc_s[hh] * corr + _dot(p.astype(BF16), v_ref[hh], 1, 0)
                m_s[hh] = m_new

        @pl.when(kj < qi)
        def _():
            step(False)

        @pl.when(kj == qi)
        def _():
            step(True)

        @pl.when(kj == nq - 1)
        def _():
            o_ref[...] = acc_s[...] / l_s[...]
            lse_ref[...] = m_s[...] + jnp.log(l_s[...])

    qspec = lambda d: pl.BlockSpec((HPS, t, d), lambda h, i, j: (h, i, 0))
    kspec = lambda d: pl.BlockSpec((HPS, t, d), lambda h, i, j: (h, jnp.minimum(j, i), 0))
    in_specs = [qspec(dk), kspec(dk), kspec(dv)]
    args = [q, k, v]
    if bias:
        in_specs += [qspec(1), pl.BlockSpec((HPS, 1, t), lambda h, i, j: (h, 0, jnp.minimum(j, i)))]
        args += [fq, fk]
    return pl.pallas_call(
        kern,
        name=name,
        grid=(nh // HPS, nq, nq),
        in_specs=in_specs,
        out_specs=[qspec(dv), qspec(1)],
        out_shape=[jax.ShapeDtypeStruct((nh, s, dv), F32), jax.ShapeDtypeStruct((nh, s, 1), F32)],
        scratch_shapes=[pltpu.VMEM((HPS, t, 1), F32), pltpu.VMEM((HPS, t, 1), F32), pltpu.VMEM((HPS, t, dv), F32)],
        compiler_params=_cp(("parallel", "parallel", "arbitrary")),
    )(*args)


def _flash_bwd(name, q, k, v, do, lse, delta, fq, fk, qg, kg, scale, dq_scale, dk_scale):
    nh, s, dk = q.shape
    dv = v.shape[-1]
    dg = qg.shape[-1]
    t = _tm(s, 512)
    nq = s // t
    bias = fq is not None
    ext = qg is not q

    def kern(*refs):
        refs = list(refs)
        q_ref, k_ref, v_ref, do_ref, lse_ref, dl_ref = refs[:6]
        del refs[:6]
        fq_ref, fk_ref = (refs.pop(0), refs.pop(0)) if bias else (None, None)
        qg_ref, kg_ref = (refs.pop(0), refs.pop(0)) if ext else (q_ref, k_ref)
        dq_ref, dk_ref, dv_ref, dk_s, dv_s = refs
        kb, qi = pl.program_id(1), pl.program_id(2)

        @pl.when((kb == 0) & (qi == 0))
        def _():
            dq_ref[...] = jnp.zeros_like(dq_ref)

        @pl.when(qi == 0)
        def _():
            dk_s[...] = jnp.zeros_like(dk_s)
            dv_s[...] = jnp.zeros_like(dv_s)

        def step(masked):
            rows = pl.ds(pl.multiple_of(qi * t, t), t)
            for hh in range(HPS):
                dob = do_ref[hh]
                sc = _attn_scores(q_ref[hh], k_ref[hh], fq_ref[hh] if bias else None, fk_ref[hh] if bias else None,
                                  scale, masked, t)
                p = jnp.exp(sc - lse_ref[hh])
                dv_s[hh] += _dot(p.astype(BF16), dob, 0, 0)
                dp = _dot(dob, v_ref[hh], 1, 1)
                dsb = (p * (dp - dl_ref[hh])).astype(BF16)
                dk_s[hh] += _dot(dsb, qg_ref[hh], 0, 0)
                dq_ref[hh, rows, :] += _dot(dsb, kg_ref[hh], 1, 0) * dq_scale

        @pl.when(qi > kb)
        def _():
            step(False)

        @pl.when(qi == kb)
        def _():
            step(True)

        @pl.when(qi == nq - 1)
        def _():
            dk_ref[...] = dk_s[...] if dk_scale is None else dk_s[...] * dk_scale
            dv_ref[...] = dv_s[...]

    qspec = lambda d: pl.BlockSpec((HPS, t, d), lambda h, j, i: (h, jnp.maximum(i, j), 0))
    kspec = lambda d: pl.BlockSpec((HPS, t, d), lambda h, j, i: (h, j, 0))
    in_specs = [qspec(dk), kspec(dk), kspec(dv), qspec(dv), qspec(1), qspec(1)]
    args = [q, k, v, do, lse, delta]
    if bias:
        in_specs += [qspec(1), pl.BlockSpec((HPS, 1, t), lambda h, j, i: (h, 0, j))]
        args += [fq, fk]
    if ext:
        in_specs += [qspec(dg), kspec(dg)]
        args += [qg, kg]
    return pl.pallas_call(
        kern,
        name=name,
        grid=(nh // HPS, nq, nq),
        in_specs=in_specs,
        out_specs=[pl.BlockSpec((HPS, s, dg), lambda h, j, i: (h, 0, 0)), kspec(dg), kspec(dv)],
        out_shape=[jax.ShapeDtypeStruct((nh, s, dg), F32), jax.ShapeDtypeStruct((nh, s, dg), F32),
                   jax.ShapeDtypeStruct((nh, s, dv), F32)],
        scratch_shapes=[pltpu.VMEM((HPS, t, dg), F32), pltpu.VMEM((HPS, t, dv), F32)],
        compiler_params=_cp(("arbitrary", "arbitrary", "arbitrary")),
    )(*args)


def _heads(t, nh):
    s = t.shape[0]
    return t.reshape(s, nh, -1).transpose(1, 0, 2)


def _unheads(t):
    nh, s, d = t.shape
    return t.transpose(1, 0, 2).reshape(s, nh * d)


def _perm_uq(w):
    r = w.shape[0]
    w3 = w.reshape(r, MLA_HEADS, MLA_NOPE + MLA_ROPE)
    half = MLA_ROPE // 2
    return jnp.concatenate([w3[:, :, :MLA_NOPE].reshape(r, -1), w3[:, :, MLA_NOPE:MLA_NOPE + half].reshape(r, -1),
                            w3[:, :, MLA_NOPE + half:].reshape(r, -1)], axis=1)


def _unperm_uq(w):
    r = w.shape[0]
    half = MLA_ROPE // 2
    n0 = MLA_HEADS * MLA_NOPE
    n1 = n0 + MLA_HEADS * half
    return jnp.concatenate([w[:, :n0].reshape(r, MLA_HEADS, MLA_NOPE), w[:, n0:n1].reshape(r, MLA_HEADS, half),
                            w[:, n1:].reshape(r, MLA_HEADS, half)], axis=2).reshape(r, -1)


def _perm_ukv(w):
    r = w.shape[0]
    w3 = w.reshape(r, MLA_HEADS, MLA_NOPE + MLA_V)
    return jnp.concatenate([w3[:, :, :MLA_NOPE].reshape(r, -1), w3[:, :, MLA_NOPE:].reshape(r, -1)], axis=1)


def _unperm_ukv(w):
    r = w.shape[0]
    n0 = MLA_HEADS * MLA_NOPE
    return jnp.concatenate([w[:, :n0].reshape(r, MLA_HEADS, MLA_NOPE), w[:, n0:].reshape(r, MLA_HEADS, MLA_V)],
                           axis=2).reshape(r, -1)


_ODD_CUTS = np.cumsum([0, FOX_WIDTH, FOX_WIDTH, FOX_WIDTH, FOX_HEADS, MLA_Q_RANK, MLA_KV_RANK, MLA_ROPE]).tolist()
_ODD_ORDER = (0, 1, 2, 4, 5, 6, 3)


def _perm_odd_in(w):
    parts = [w[:, _ODD_CUTS[j]:_ODD_CUTS[j + 1]] for j in _ODD_ORDER]
    parts.append(jnp.zeros((w.shape[0], ODD_IN_PAD - ODD_IN), w.dtype))
    return jnp.concatenate(parts, axis=1)


def _unperm_odd_in(w):
    widths = [_ODD_CUTS[j + 1] - _ODD_CUTS[j] for j in _ODD_ORDER]
    offs = np.cumsum([0] + widths).tolist()
    pieces = {j: w[:, offs[n]:offs[n + 1]] for n, j in enumerate(_ODD_ORDER)}
    return jnp.concatenate([pieces[j] for j in range(7)], axis=1)


def _pad_cols(w, n):
    return jnp.concatenate([w, jnp.zeros((w.shape[0], n - w.shape[1]), w.dtype)], axis=1)


_BIG = (("even_w_in", 2), ("even_w_out", 1), ("odd_w_in", 2), ("w_uq", 2), ("w_ukv", 2), ("odd_w_out", 1),
        ("ffn_w_gate", 2), ("ffn_w_up", 2), ("ffn_w_down", 1))
_PACK_COLS = 1024


def _unshard(blocks, shp, ax):
    t = jnp.moveaxis(blocks.reshape((N_DEV,) + tuple(shp)), 0, ax)
    full = list(shp)
    full[ax] = shp[ax] * N_DEV
    return t.reshape(full)


def _reshard(full, ax, cols):
    shp = list(full.shape)
    t = full.reshape(shp[:ax] + [N_DEV, shp[ax] // N_DEV] + shp[ax + 1:])
    return jnp.moveaxis(t, ax, 0).reshape(N_DEV, -1, cols)


def kernel(x, even_w_in, pool_w, pool_scale, conv_w, conv_b, dt_bias, a_log, d_skip, ssm_norm_w, even_w_out, odd_w_in, fgate_b, q_norm_w, w_uq, kv_norm_w, w_ukv, odd_w_out, ffn_w_gate, ffn_w_up, ffn_w_down, ln_mix_g, ln_mix_b, ln_ffn_g, ln_ffn_b, loss_target, m_even_w_in, m_pool_w, m_pool_scale, m_conv_w, m_conv_b, m_dt_bias, m_a_log, m_d_skip, m_ssm_norm_w, m_even_w_out, m_odd_w_in, m_fgate_b, m_q_norm_w, m_w_uq, m_kv_norm_w, m_w_ukv, m_odd_w_out, m_ffn_w_gate, m_ffn_w_up, m_ffn_w_down, m_ln_mix_g, m_ln_mix_b, m_ln_ffn_g, m_ln_ffn_b, v_even_w_in, v_pool_w, v_pool_scale, v_conv_w, v_conv_b, v_dt_bias, v_a_log, v_d_skip, v_ssm_norm_w, v_even_w_out, v_odd_w_in, v_fgate_b, v_q_norm_w, v_w_uq, v_kv_norm_w, v_w_ukv, v_odd_w_out, v_ffn_w_gate, v_ffn_w_up, v_ffn_w_down, v_ln_mix_g, v_ln_mix_b, v_ln_ffn_g, v_ln_ffn_b):
    P = dict(even_w_in=even_w_in, pool_w=pool_w, pool_scale=pool_scale, conv_w=conv_w, conv_b=conv_b, dt_bias=dt_bias,
             a_log=a_log, d_skip=d_skip, ssm_norm_w=ssm_norm_w, even_w_out=even_w_out, odd_w_in=odd_w_in,
             fgate_b=fgate_b, q_norm_w=q_norm_w, w_uq=w_uq, kv_norm_w=kv_norm_w, w_ukv=w_ukv, odd_w_out=odd_w_out,
             ffn_w_gate=ffn_w_gate, ffn_w_up=ffn_w_up, ffn_w_down=ffn_w_down, ln_mix_g=ln_mix_g, ln_mix_b=ln_mix_b,
             ln_ffn_g=ln_ffn_g, ln_ffn_b=ln_ffn_b)
    M = dict(even_w_in=m_even_w_in, pool_w=m_pool_w, pool_scale=m_pool_scale, conv_w=m_conv_w, conv_b=m_conv_b,
             dt_bias=m_dt_bias, a_log=m_a_log, d_skip=m_d_skip, ssm_norm_w=m_ssm_norm_w, even_w_out=m_even_w_out,
             odd_w_in=m_odd_w_in, fgate_b=m_fgate_b, q_norm_w=m_q_norm_w, w_uq=m_w_uq, kv_norm_w=m_kv_norm_w,
             w_ukv=m_w_ukv, odd_w_out=m_odd_w_out, ffn_w_gate=m_ffn_w_gate, ffn_w_up=m_ffn_w_up,
             ffn_w_down=m_ffn_w_down, ln_mix_g=m_ln_mix_g, ln_mix_b=m_ln_mix_b, ln_ffn_g=m_ln_ffn_g,
             ln_ffn_b=m_ln_ffn_b)
    V = dict(even_w_in=v_even_w_in, pool_w=v_pool_w, pool_scale=v_pool_scale, conv_w=v_conv_w, conv_b=v_conv_b,
             dt_bias=v_dt_bias, a_log=v_a_log, d_skip=v_d_skip, ssm_norm_w=v_ssm_norm_w, even_w_out=v_even_w_out,
             odd_w_in=v_odd_w_in, fgate_b=v_fgate_b, q_norm_w=v_q_norm_w, w_uq=v_w_uq, kv_norm_w=v_kv_norm_w,
             w_ukv=v_w_ukv, odd_w_out=v_odd_w_out, ffn_w_gate=v_ffn_w_gate, ffn_w_up=v_ffn_w_up,
             ffn_w_down=v_ffn_w_down, ln_mix_g=v_ln_mix_g, ln_mix_b=v_ln_mix_b, ln_ffn_g=v_ln_ffn_g,
             ln_ffn_b=v_ln_ffn_b)
    names = list(P)
    s = x.shape[1]
    me = 4 * lax.axis_index("x") + 2 * lax.axis_index("y") + lax.axis_index("c")

    big_rows = [math.prod(P[n].shape) // _PACK_COLS for n, _ in _BIG]
    pad_rows = [-(-nr // 16) * 16 for nr in big_rows]
    rows_big = sum(pad_rows)
    packed = jnp.concatenate([
        jnp.pad(P[n].astype(BF16).reshape(-1), (0, (pr - nr) * _PACK_COLS))
        for (n, _), nr, pr in zip(_BIG, big_rows, pad_rows)]).reshape(rows_big, _PACK_COLS)
    gathered = _all_gather("ag_weights", packed)
    W = {}
    off = 0
    for (n, ax), nr, pr in zip(_BIG, big_rows, pad_rows):
        W[n] = _unshard(gathered[:, off:off + nr, :], P[n].shape, ax)
        off += pr
    kv_pad = jnp.zeros((128 - kv_norm_w.size,), F32)
    small_sh = jnp.concatenate([conv_w.reshape(-1), q_norm_w.reshape(-1), kv_norm_w.reshape(-1), kv_pad,
                                jnp.zeros((2 * 128,), F32)]).reshape(16, 128)
    g_small = _all_gather("ag_small_weights", small_sh)
    conv_w_full = _unshard(g_small[:, :12, :], conv_w.shape, 2)
    q_norm_full = _unshard(g_small[:, 12:13, :], q_norm_w.shape, 1)
    kv_norm_full = _unshard(g_small[:, 13:14, :64], kv_norm_w.shape, 1)

    w_in_e = [_pad_cols(W["even_w_in"][i], EVEN_IN_PAD) for i in range(2)]
    w_in_o = [_perm_odd_in(W["odd_w_in"][i]) for i in range(2)]
    w_uq_p = [_perm_uq(W["w_uq"][i]) for i in range(2)]
    w_ukv_p = [_perm_ukv(W["w_ukv"][i]) for i in range(2)]
    w_gu = [jnp.concatenate([W["ffn_w_gate"][l], W["ffn_w_up"][l]], axis=1) for l in range(DEPTH)]

    pos = jnp.arange(s, dtype=F32)
    half = MLA_ROPE // 2
    freqs = jnp.power(ROPE_THETA, -jnp.arange(half, dtype=F32) / half)
    ang = pos[:, None] * freqs[None, :]
    cos16, sin16 = jnp.cos(ang), jnp.sin(ang)
    cos128, sin128 = jnp.tile(cos16, (1, MLA_HEADS)), jnp.tile(sin16, (1, MLA_HEADS))
    row = lambda t: t.reshape(1, -1)

    xcur = x[0]
    xb = xcur.astype(BF16)
    saved = []
    for l in range(DEPTH):
        i = l // 2
        sv = dict(x_in_b=xb)
        if l % 2 == 0:
            proj = _mm("mm_in_even", xb, w_in_e[i], "nn", F32)
            u, z = proj[:, :512], proj[:, 512:1536]
            xbc, dtraw = proj[:, 1536:3072], proj[:, 3072:3088]
            ypool = _pool_fwd(u, pool_w[i], row(pool_scale[i]))
            xc = _conv_fwd(xbc, conv_w_full[i], row(conv_b[i]))
            dt = _softplus_fwd(dtraw, row(dt_bias[i]))
            xh = _heads(xc[:, :SSM_D_INNER], SSM_HEADS)
            dtc = dt.T[:, :, None]
            dtr = dt.T[:, None, :]
            bm, cm = xc[:, SSM_D_INNER:SSM_D_INNER + 256], xc[:, SSM_D_INNER + 256:]
            alog3, dsk3 = a_log[i].reshape(-1, 1, 1), d_skip[i].reshape(-1, 1, 1)
            yh, hprev = _ssd_fwd(xh, xh.transpose(0, 2, 1), dtc, dtr, bm, cm, alog3, dsk3)
            y_ssm = _unheads(yh)
            yn = _gated_rms_fwd(y_ssm, z, row(ssm_norm_w[i]))
            mix = jnp.concatenate([ypool, yn], axis=1)
            h = _mm("mm_out_even", mix, W["even_w_out"][i], "nn", F32)
            sv.update(u=u, z=z, xbc=xbc, dtraw=dtraw, xh=xh, dtc=dtc, dtr=dtr, bm=bm, cm=cm, hprev=hprev,
                      y_ssm=y_ssm, mix=mix)
        else:
            proj = _mm("mm_in_odd", xb, w_in_o[i], "nn", F32)
            qf, kf, vf = proj[:, :512], proj[:, 512:1024], proj[:, 1024:1536]
            cq, ckv = proj[:, 1536:2048], proj[:, 2048:2304]
            kr, fl = proj[:, 2304:2336], proj[:, 2336:2344]
            fcum = _fgate_fwd(fl, row(fgate_b[i]))
            fq_ = fcum.T[:, :, None]
            fk_ = fcum.T[:, None, :]
            qh, kh, vh = (_heads(t.astype(BF16), FOX_HEADS) for t in (qf * FOX_SCALE, kf, vf))
            o_fox, lse_fox = _flash_fwd("fox_fwd", qh, kh, vh, fq_, fk_, None)
            qn = _rms_fwd(cq, row(q_norm_full[i]))
            qp = _mm("mm_uq", qn, w_uq_p[i], "nn", F32)
            q1, q2 = _rope("rope_q", qp[:, 512:640], qp[:, 640:768], cos128, sin128)
            kvn = _rms_fwd(ckv, row(kv_norm_full[i]))
            kvp = _mm("mm_ukv", kvn, w_ukv_p[i], "nn", F32)
            k1, k2 = _rope("rope_k", kr[:, :half], kr[:, half:], cos16, sin16)
            zpad = jnp.zeros((MLA_HEADS, s, MLA_DK_PAD - MLA_NOPE - MLA_ROPE), BF16)
            qm = jnp.concatenate([_heads(qp[:, :512], MLA_HEADS), _heads(q1, MLA_HEADS), _heads(q2, MLA_HEADS)],
                                 axis=2).astype(BF16)
            qm = jnp.concatenate([qm, zpad], axis=2)
            krope = jnp.broadcast_to(jnp.concatenate([k1, k2], axis=1)[None], (MLA_HEADS, s, MLA_ROPE))
            km = jnp.concatenate([_heads(kvp[:, :512], MLA_HEADS), krope], axis=2).astype(BF16)
            km = jnp.concatenate([km, zpad], axis=2)
            vm = _heads(kvp[:, 512:], MLA_HEADS).astype(BF16)
            o_mla, lse_mla = _flash_fwd("mla_fwd", qm, km, vm, None, None, MLA_SCALE)
            mix = jnp.concatenate([_unheads(o_fox), _unheads(o_mla)], axis=1).astype(BF16)
            h = _mm("mm_out_odd", mix, W["odd_w_out"][i], "nn", F32)
            sv.update(fl=fl, fq=fq_, fk=fk_, qh=qh, kh=kh, vh=vh, o_fox=o_fox, lse_fox=lse_fox, cq=cq, ckv=ckv,
                      qn=qn, kvn=kvn, qm=qm, km=km, vm=vm, o_mla=o_mla, lse_mla=lse_mla, mix=mix)
        y1, y1b, r1 = _ln_fwd(xcur, h, row(ln_mix_g[l]), row(ln_mix_b[l]))
        gu = _mm("mm_ffn_in", y1b, w_gu[l], "nn", F32)
        act = _swiglu_fwd(gu)
        h2 = _mm("mm_ffn_out", act, W["ffn_w_down"][l], "nn", F32)
        y2, y2b, r2 = _ln_fwd(y1, h2, row(ln_ffn_g[l]), row(ln_ffn_b[l]))
        sv.update(r1=r1, y1b=y1b, gu=gu, act=act, r2=r2)
        saved.append(sv)
        xcur, xb = y2, y2b

    dy, loss_part = _loss_head(xcur, loss_target[0])
    loss = lax.psum(loss_part[0, 0], ("x", "y", "c"))

    G = {n: [None] * P[n].shape[0] for n in names}
    acur, dcur = None, dy
    for l in reversed(range(DEPTH)):
        i = l // 2
        sv = saved[l]
        dr2, dr2b, dg, db = _ln_bwd(acur, dcur, sv["r2"], row(ln_ffn_g[l]))
        G["ln_ffn_g"][l], G["ln_ffn_b"][l] = dg[0], db[0]
        G["ffn_w_down"][l] = _mm("mm_dw_ffn_out", sv["act"], dr2b, "tn", F32)
        dact = _mm("mm_dx_ffn_out", dr2b, W["ffn_w_down"][l], "nt", F32)
        dgu = _swiglu_bwd(dact, sv["gu"])
        dwgu = _mm("mm_dw_ffn_in", sv["y1b"], dgu, "tn", F32)
        G["ffn_w_gate"][l], G["ffn_w_up"][l] = dwgu[:, :D_FF], dwgu[:, D_FF:]
        dy1 = _mm("mm_dx_ffn_in", dgu, w_gu[l], "nt", F32)
        dr1, dr1b, dg, db = _ln_bwd(dr2, dy1, sv["r1"], row(ln_mix_g[l]))
        G["ln_mix_g"][l], G["ln_mix_b"][l] = dg[0], db[0]
        if l % 2 == 0:
            G["even_w_out"][i] = _mm("mm_dw_out_even", sv["mix"], dr1b, "tn", F32)
            dmix = _mm("mm_dx_out_even", dr1b, W["even_w_out"][i], "nt", F32)
            dd, dsc, dpw = _pool_bwd1(sv["u"], dmix[:, :POOL_WIDTH], pool_w[i], row(pool_scale[i]))
            G["pool_scale"][i], G["pool_w"][i] = dsc[0], dpw
            du = _pool_bwd2(dd)
            dys, dz, dnw = _gated_rms_bwd(dmix[:, POOL_WIDTH:], sv["y_ssm"], sv["z"], row(ssm_norm_w[i]))
            G["ssm_norm_w"][i] = dnw[0]
            dyh = _heads(dys, SSM_HEADS)
            alog3, dsk3 = a_log[i].reshape(-1, 1, 1), d_skip[i].reshape(-1, 1, 1)
            dxh, ddt3, dbm, dcm, dalog, ddsk = _ssd_bwd(dyh, dyh.transpose(0, 2, 1), sv["xh"], sv["dtc"], sv["dtr"],
                                                        sv["bm"], sv["cm"], sv["hprev"], alog3, dsk3)
            G["a_log"][i], G["d_skip"][i] = dalog.reshape(-1), ddsk.reshape(-1)
            ddtraw, ddtb = _softplus_bwd(ddt3[:, :, 0].T, sv["dtraw"], row(dt_bias[i]))
            G["dt_bias"][i] = ddtb[0]
            dxc = jnp.concatenate([_unheads(dxh), dbm, dcm], axis=1)
            dpre, dcw, dcb = _conv_bwd1(dxc, sv["xbc"], conv_w_full[i], row(conv_b[i]))
            G["conv_w"][i], G["conv_b"][i] = dcw, dcb[0]
            dxbc = _conv_bwd2(dpre, conv_w_full[i])
            dproj = jnp.concatenate([du, dz.astype(BF16), dxbc, ddtraw.astype(BF16),
                                     jnp.zeros((s, EVEN_IN_PAD - EVEN_IN), BF16)], axis=1)
            G["even_w_in"][i] = _mm("mm_dw_in_even", sv["x_in_b"], dproj, "tn", F32)[:, :EVEN_IN]
            dxb = _mm("mm_dx_in_even", dproj, w_in_e[i], "nt", F32)
        else:
            G["odd_w_out"][i] = _mm("mm_dw_out_odd", sv["mix"], dr1b, "tn", F32)
            dmix = _mm("mm_dx_out_odd", dr1b, W["odd_w_out"][i], "nt", F32)
            do_f = _heads(dmix[:, :FOX_WIDTH].astype(BF16), FOX_HEADS)
            dl_f = _rowdot(do_f.reshape(-1, 64), sv["o_fox"].reshape(-1, 64)).reshape(FOX_HEADS, s, 1)
            ones = jnp.ones((FOX_HEADS, s, 64), BF16)
            qg = jnp.concatenate([sv["qh"], ones], axis=2)
            kg = jnp.concatenate([sv["kh"], ones * (1.0 / FOX_SCALE)], axis=2)
            dqe, dke, dvh = _flash_bwd("fox_bwd", sv["qh"], sv["kh"], sv["vh"], do_f, sv["lse_fox"], dl_f,
                                       sv["fq"], sv["fk"], qg, kg, None, FOX_SCALE, None)
            dqh, dkh = dqe[:, :, :64], dke[:, :, :64]
            dfl, dfb = _fgate_bwd(dqe[:, :, 64].T, dke[:, :, 64].T, sv["fl"], row(fgate_b[i]))
            G["fgate_b"][i] = dfb[0]
            do_m = _heads(dmix[:, FOX_WIDTH:].astype(BF16), MLA_HEADS)
            dl_m = _rowdot(do_m.reshape(-1, 64), sv["o_mla"].reshape(-1, 64)).reshape(MLA_HEADS, s, 1)
            dqm, dkm, dvm = _flash_bwd("mla_bwd", sv["qm"], sv["km"], sv["vm"], do_m, sv["lse_mla"], dl_m,
                                       None, None, sv["qm"], sv["km"], MLA_SCALE, MLA_SCALE, MLA_SCALE)
            n0, n1 = MLA_NOPE, MLA_NOPE + half
            dq1, dq2 = _rope("rope_q_bwd", _unheads(dqm[:, :, n0:n1]), _unheads(dqm[:, :, n1:n1 + half]),
                             cos128, -sin128)
            dqp = jnp.concatenate([_unheads(dqm[:, :, :n0]), dq1, dq2], axis=1).astype(BF16)
            G["w_uq"][i] = _unperm_uq(_mm("mm_dw_uq", sv["qn"], dqp, "tn", F32))
            dqn = _mm("mm_dx_uq", dqp, w_uq_p[i], "nt", F32)
            dcq, dqw = _rms_bwd(dqn, sv["cq"], row(q_norm_full[i]))
            G["q_norm_w"][i] = dqw[0]
            dk1, dk2 = _headsum_rope_bwd(_unheads(dkm[:, :, n0:n1]), _unheads(dkm[:, :, n1:n1 + half]), cos16, sin16)
            dkvp = jnp.concatenate([_unheads(dkm[:, :, :n0]), _unheads(dvm)], axis=1).astype(BF16)
            G["w_ukv"][i] = _unperm_ukv(_mm("mm_dw_ukv", sv["kvn"], dkvp, "tn", F32))
            dkvn = _mm("mm_dx_ukv", dkvp, w_ukv_p[i], "nt", F32)
            dckv, dkvw = _rms_bwd(dkvn, sv["ckv"], row(kv_norm_full[i]))
            G["kv_norm_w"][i] = dkvw[0]
            dproj = jnp.concatenate([_unheads(dqh), _unheads(dkh), _unheads(dvh), dcq, dckv, dk1, dk2, dfl,
                                     jnp.zeros((s, ODD_IN_PAD - ODD_IN), F32)], axis=1).astype(BF16)
            G["odd_w_in"][i] = _unperm_odd_in(_mm("mm_dw_in_odd", sv["x_in_b"], dproj, "tn", F32))
            dxb = _mm("mm_dx_in_odd", dproj, w_in_o[i], "nt", F32)
        acur, dcur = dr1, dxb
    grad_x = _axpy(acur, dcur)[None]

    gfull = {n: jnp.stack(G[n]) for n in names}
    send = jnp.concatenate([
        jnp.pad(_reshard(gfull[n], ax, _PACK_COLS).astype(BF16), ((0, 0), (0, pr - nr), (0, 0)))
        for (n, ax), nr, pr in zip(_BIG, big_rows, pad_rows)], axis=1)
    recv = _all_to_all("a2a_grads", send)
    gsum = _sum8("sum_grads", recv)
    grads = {}
    off = 0
    for (n, ax), nr, pr in zip(_BIG, big_rows, pad_rows):
        grads[n] = gsum[off:off + nr].reshape(P[n].shape)
        off += pr
    small = [n for n in names if n not in dict(_BIG)]
    sflat = jnp.concatenate([gfull[n].reshape(-1) for n in small])
    n_small = sflat.shape[0]
    rows_small = -(-n_small // (128 * 8)) * 8
    sflat = jnp.concatenate([sflat, jnp.zeros((rows_small * 128 - n_small,), F32)])
    sg = _sum8("sum_small_grads", _all_gather("ag_small_grads", sflat.reshape(rows_small, 128))).reshape(-1)
    off = 0
    for n in small:
        cnt = gfull[n].size
        gf = sg[off:off + cnt].reshape(gfull[n].shape)
        off += cnt
        if gf.shape != P[n].shape:
            width = P[n].shape[-1]
            gf = lax.dynamic_slice_in_dim(gf, me * width, width, axis=gf.ndim - 1)
        grads[n] = gf

    delta, new_m, new_v = {}, {}, {}
    for n, _ in _BIG:
        shp = P[n].shape
        two = lambda t: t.reshape(-1, shp[-1])
        d_, m_, v_ = _adamw("adamw_" + n, two(P[n]), two(grads[n]), two(M[n]), two(V[n]))
        delta[n], new_m[n], new_v[n] = d_.reshape(shp), m_.reshape(shp), v_.reshape(shp)

    def packs(d):
        f = jnp.concatenate([d[n].reshape(-1) for n in small])
        pad = -(-f.shape[0] // (128 * 8)) * 8 * 128 - f.shape[0]
        return jnp.concatenate([f, jnp.zeros((pad,), F32)]).reshape(-1, 128)

    d_, m_, v_ = _adamw("adamw_small", packs(P), packs(grads), packs(M), packs(V))
    off = 0
    for n in small:
        cnt = P[n].size
        for dst, src in ((delta, d_), (new_m, m_), (new_v, v_)):
            dst[n] = src.reshape(-1)[off:off + cnt].reshape(P[n].shape)
        off += cnt

    return (loss, grad_x, *[grads[n] for n in names], *[delta[n] for n in names],
            *[new_m[n] for n in names], *[new_v[n] for n in names])
```

```python
import functools
import math

import jax
import jax.numpy as jnp
import numpy as np
from jax import lax
from jax.experimental import pallas as pl
from jax.experimental.pallas import tpu as pltpu

F32 = jnp.float32
BF16 = jnp.bfloat16
HI = lax.Precision.HIGHEST

N_DEV = 8
D_MODEL = 1024
DEPTH = 4
POOL_WINDOWS = (2, 4, 8, 16)
POOL_GROUP = 128
POOL_WIDTH = 512
SSM_D_INNER = 1024
SSM_HEAD_DIM = 64
SSM_HEADS = 16
SSM_GROUPS = 2
SSM_STATE = 128
SSM_CONV = 4
SSM_CHUNK = 128
SSM_CONV_DIM = 1536
EVEN_IN = 3088
EVEN_IN_PAD = 3200
FOX_HEADS = 8
FOX_WIDTH = 512
MLA_HEADS = 8
MLA_NOPE = 64
MLA_ROPE = 32
MLA_V = 64
MLA_Q_RANK = 512
MLA_KV_RANK = 256
MLA_DK_PAD = 128
ROPE_THETA = 10000.0
FOX_SCALE = 64 ** -0.5
MLA_SCALE = (MLA_NOPE + MLA_ROPE) ** -0.5
ODD_IN = 2344
ODD_IN_PAD = 2560
D_FF = 2816
ALPHA = (2 * DEPTH) ** 0.25
LN_EPS = 1e-5
RMS_EPS = 1e-6
ADAM_LR = 0.001
ADAM_B1 = 0.9
ADAM_B2 = 0.999
ADAM_EPS = 1e-08
ADAM_WD = 0.01
ADAM_STEP = 10
NEG = -1e30
VMEM_LIMIT = 48 * 1024 * 1024


def _cp(sem):
    return pltpu.CompilerParams(dimension_semantics=sem, vmem_limit_bytes=VMEM_LIMIT)


def _dot(a, b, ca, cb, prec=None):
    return lax.dot_general(a, b, (((ca,), (cb,)), ((), ())), preferred_element_type=F32, precision=prec)


def _sigmoid(x):
    return 1.0 / (1.0 + jnp.exp(-x))


def _softplus(x):
    return jnp.maximum(x, 0.0) + jnp.log(1.0 + jnp.exp(-jnp.abs(x)))


MESH = pl.DeviceIdType.MESH
HBM_SPEC = pl.BlockSpec(memory_space=pltpu.HBM)


def _all_gather(name, xs):
    r, c_ = xs.shape

    def body(x_ref, out_ref, send_sems, recv_sems, local_sem):
        x, y, c = lax.axis_index("x"), lax.axis_index("y"), lax.axis_index("c")
        me, sibling = (x, y, c), (x, y, 1 - c)
        chips = [(1 - x, y), (x, 1 - y), (1 - x, 1 - y)]

        def rows(px, py, pc):
            return out_ref.at[4 * px + 2 * py + pc]

        def copy(k, block, to, src=None):
            return pltpu.make_async_remote_copy(
                src_ref=rows(*block) if src is None else src,
                dst_ref=rows(*block),
                send_sem=send_sems.at[k],
                recv_sem=recv_sems.at[k],
                device_id=to,
                device_id_type=MESH,
            )

        mine = pltpu.make_async_copy(x_ref, rows(*me), local_sem)
        mine.start()
        first = [copy(0, me, sibling, src=x_ref)]
        first += [copy(1 + j, me, (*chip, c), src=x_ref) for j, chip in enumerate(chips)]
        for cp in first:
            cp.start()
        passed = [copy(4 + j, (*chip, c), sibling) for j, chip in enumerate(chips)]
        for j, chip in enumerate(chips):
            copy(1 + j, (*chip, c), me).wait_recv()
            passed[j].start()
        copy(0, sibling, me).wait_recv()
        for j, chip in enumerate(chips):
            copy(4 + j, (*chip, 1 - c), me).wait_recv()
        for cp in first + passed:
            cp.wait_send()
        mine.wait()

    return pl.pallas_call(
        body,
        name=name,
        out_shape=jax.ShapeDtypeStruct((N_DEV, r, c_), xs.dtype),
        in_specs=[HBM_SPEC],
        out_specs=HBM_SPEC,
        scratch_shapes=[pltpu.SemaphoreType.DMA((7,)), pltpu.SemaphoreType.DMA((7,)), pltpu.SemaphoreType.DMA(())],
    )(xs)


def _all_to_all(name, send):
    _, r, c_ = send.shape

    def body(s_ref, r_ref, send_sems, recv_sems, local_sem):
        x, y, c = lax.axis_index("x"), lax.axis_index("y"), lax.axis_index("c")
        me = 4 * x + 2 * y + c
        mine = pltpu.make_async_copy(s_ref.at[me], r_ref.at[me], local_sem)
        mine.start()
        copies = []
        for k in range(1, N_DEV):
            tx = 1 - x if k & 4 else x
            ty = 1 - y if k & 2 else y
            tc = 1 - c if k & 1 else c
            peer = 4 * tx + 2 * ty + tc
            cp = pltpu.make_async_remote_copy(
                src_ref=s_ref.at[peer],
                dst_ref=r_ref.at[me],
                send_sem=send_sems.at[k - 1],
                recv_sem=recv_sems.at[k - 1],
                device_id=(tx, ty, tc),
                device_id_type=MESH,
            )
            cp.start()
            landing = pltpu.make_async_remote_copy(
                src_ref=s_ref.at[me],
                dst_ref=r_ref.at[peer],
                send_sem=send_sems.at[k - 1],
                recv_sem=recv_sems.at[k - 1],
                device_id=(tx, ty, tc),
                device_id_type=MESH,
            )
            copies.append((cp, landing))
        for cp, landing in copies:
            landing.wait_recv()
        for cp, landing in copies:
            cp.wait_send()
        mine.wait()

    return pl.pallas_call(
        body,
        name=name,
        out_shape=jax.ShapeDtypeStruct(send.shape, send.dtype),
        in_specs=[HBM_SPEC],
        out_specs=HBM_SPEC,
        scratch_shapes=[pltpu.SemaphoreType.DMA((7,)), pltpu.SemaphoreType.DMA((7,)), pltpu.SemaphoreType.DMA(())],
    )(send)


def _pick(n, cands):
    for t in cands:
        if n % t == 0:
            return t
    return n


def _mm(name, a, b, mode, out_dtype):
    if mode == "nn":
        (m, k), n = a.shape, b.shape[1]
    elif mode == "nt":
        (m, k), n = a.shape, b.shape[0]
    else:
        (k, m), n = a.shape, b.shape[1]
    tm = _pick(m, (512, 256, 128))
    tn = _pick(n, (1408, 1280, 1024, 768, 640, 512, 384, 256, 128))
    tk = _pick(k, (1024, 1408, 768, 640, 512, 256, 128))
    nk = k // tk
    swap = nk == 1 and a.size * a.dtype.itemsize * (n // tn) + b.size * b.dtype.itemsize < (
        a.size * a.dtype.itemsize + b.size * b.dtype.itemsize * (m // tm))
    ij = (lambda g0, g1: (g1, g0)) if swap else (lambda g0, g1: (g0, g1))
    if mode == "nn":
        a_spec = pl.BlockSpec((tm, tk), lambda g0, g1, kk: (ij(g0, g1)[0], kk))
        b_spec = pl.BlockSpec((tk, tn), lambda g0, g1, kk: (kk, ij(g0, g1)[1]))
        ca, cb = 1, 0
    elif mode == "nt":
        a_spec = pl.BlockSpec((tm, tk), lambda g0, g1, kk: (ij(g0, g1)[0], kk))
        b_spec = pl.BlockSpec((tn, tk), lambda g0, g1, kk: (ij(g0, g1)[1], kk))
        ca, cb = 1, 1
    else:
        a_spec = pl.BlockSpec((tk, tm), lambda g0, g1, kk: (kk, ij(g0, g1)[0]))
        b_spec = pl.BlockSpec((tk, tn), lambda g0, g1, kk: (kk, ij(g0, g1)[1]))
        ca, cb = 0, 0

    def kern(a_ref, b_ref, o_ref, acc):
        kk = pl.program_id(2)

        @pl.when(kk == 0)
        def _():
            acc[...] = jnp.zeros_like(acc)

        acc[...] += _dot(a_ref[...].astype(BF16), b_ref[...].astype(BF16), ca, cb)

        @pl.when(kk == nk - 1)
        def _():
            o_ref[...] = acc[...].astype(out_dtype)

    return pl.pallas_call(
        kern,
        name=name,
        grid=(n // tn, m // tm, nk) if swap else (m // tm, n // tn, nk),
        in_specs=[a_spec, b_spec],
        out_specs=pl.BlockSpec((tm, tn), lambda g0, g1, kk: ij(g0, g1)),
        out_shape=jax.ShapeDtypeStruct((m, n), out_dtype),
        scratch_shapes=[pltpu.VMEM((tm, tn), F32)],
        compiler_params=_cp(("parallel", "parallel", "arbitrary")),
    )(a, b)


def _rowwise(name, body, ins, outs, tm):
    n_rows = next(a.shape[0] for a, kind in ins if kind == "row")
    n = n_rows // tm
    in_specs = []
    for a, kind in ins:
        if kind == "full":
            in_specs.append(pl.BlockSpec(a.shape, lambda i, nd=a.ndim: (0,) * nd))
        elif kind == "row":
            in_specs.append(pl.BlockSpec((tm, a.shape[1]), lambda i: (i, 0)))
        elif kind == "prev":
            in_specs.append(pl.BlockSpec((tm, a.shape[1]), lambda i: (jnp.maximum(i - 1, 0), 0)))
        else:
            in_specs.append(pl.BlockSpec((tm, a.shape[1]), lambda i: (jnp.minimum(i + 1, n - 1), 0)))
    out_specs, out_shape = [], []
    for shp, dt, kind in outs:
        out_shape.append(jax.ShapeDtypeStruct(shp, dt))
        if kind == "row":
            out_specs.append(pl.BlockSpec((tm, shp[1]), lambda i: (i, 0)))
        else:
            out_specs.append(pl.BlockSpec(shp, lambda i, nd=len(shp): (0,) * nd))
    n_in = len(ins)

    def kern(*refs):
        i = pl.program_id(0)
        res = body(i, n, *[r if kind == "full" else r[...] for r, (_, kind) in zip(refs[:n_in], ins)])
        for (shp, dt, kind), val, o in zip(outs, res, refs[n_in:]):
            if kind == "row":
                o[...] = val.astype(dt)
            else:

                @pl.when(i == 0)
                def _(o=o):
                    o[...] = jnp.zeros_like(o)

                o[...] += val.astype(dt)

    return pl.pallas_call(
        kern,
        name=name,
        grid=(n,),
        in_specs=in_specs,
        out_specs=out_specs,
        out_shape=out_shape,
        compiler_params=_cp(("arbitrary",)),
    )(*[a for a, _ in ins])


def _tm(s, t):
    return min(s, t)


def _ln_fwd(x, h, g, b):
    s = x.shape[0]

    def body(i, n, xv, hv, gv, bv):
        r = ALPHA * xv + hv.astype(F32)
        mu = jnp.mean(r, axis=-1, keepdims=True)
        d = r - mu
        var = jnp.mean(d * d, axis=-1, keepdims=True)
        y = d * lax.rsqrt(var + LN_EPS) * gv[...] + bv[...]
        return y, y, r

    shp = (s, D_MODEL)
    return _rowwise("ln_fwd", body, [(x, "row"), (h, "row"), (g, "full"), (b, "full")],
                    [(shp, F32, "row"), (shp, BF16, "row"), (shp, F32, "row")], _tm(s, 256))


def _ln_bwd(a, bterm, r, g):
    s = r.shape[0]

    def body(i, n, *vals):
        if a is None:
            dyv, rv, gv = vals
        else:
            av, dyv, rv, gv = vals
            dyv = ALPHA * av + dyv
        mu = jnp.mean(rv, axis=-1, keepdims=True)
        d = rv - mu
        var = jnp.mean(d * d, axis=-1, keepdims=True)
        rstd = lax.rsqrt(var + LN_EPS)
        xhat = d * rstd
        dxh = dyv * gv[...]
        dr = rstd * (dxh - jnp.mean(dxh, axis=-1, keepdims=True) - xhat * jnp.mean(dxh * xhat, axis=-1, keepdims=True))
        return dr, dr, jnp.sum(dyv * xhat, axis=0, keepdims=True), jnp.sum(dyv, axis=0, keepdims=True)

    ins = ([] if a is None else [(a, "row")]) + [(bterm, "row"), (r, "row"), (g, "full")]
    shp = (s, D_MODEL)
    return _rowwise("ln_bwd" if a is None else "ln_bwd_res", body, ins,
                    [(shp, F32, "row"), (shp, BF16, "row"), ((1, D_MODEL), F32, "acc"), ((1, D_MODEL), F32, "acc")],
                    _tm(s, 256))


def _axpy(a, b):
    def body(i, n, av, bv):
        return (ALPHA * av + bv,)

    return _rowwise("axpy", body, [(a, "row"), (b, "row")], [(a.shape, F32, "row")], _tm(a.shape[0], 256))[0]


def _loss_head(y, target):
    s = y.shape[0]

    def body(i, n, yv, tv):
        err = yv - tv
        part = 0.5 * jnp.sum(jnp.mean(err * err, axis=-1, keepdims=True), axis=0, keepdims=True)
        return err * (1.0 / D_MODEL), part

    return _rowwise("loss_head", body, [(y, "row"), (target, "row")],
                    [((s, D_MODEL), F32, "row"), ((1, 1), F32, "acc")], _tm(s, 256))


def _swiglu_fwd(gu):
    s = gu.shape[0]

    def body(i, n, v):
        g, u = v[:, :D_FF], v[:, D_FF:]
        return (g * _sigmoid(g) * u,)

    return _rowwise("swiglu_fwd", body, [(gu, "row")], [((s, D_FF), BF16, "row")], _tm(s, 256))[0]


def _swiglu_bwd(da, gu):
    s = gu.shape[0]

    def body(i, n, dav, v):
        g, u = v[:, :D_FF], v[:, D_FF:]
        sg = _sigmoid(g)
        dg = dav * u * (sg * (1.0 + g * (1.0 - sg)))
        du = dav * (g * sg)
        return (jnp.concatenate([dg, du], axis=1),)

    return _rowwise("swiglu_bwd", body, [(da, "row"), (gu, "row")], [((s, 2 * D_FF), BF16, "row")], _tm(s, 256))[0]


def _rms_fwd(x, w):
    s, c = x.shape

    def body(i, n, xv, wv):
        rs = lax.rsqrt(jnp.mean(xv * xv, axis=-1, keepdims=True) + RMS_EPS)
        return (xv * rs * wv[...],)

    return _rowwise("rms_fwd", body, [(x, "row"), (w, "full")], [((s, c), BF16, "row")], _tm(s, 512))[0]


def _rms_bwd(dy, x, w):
    s, c = x.shape

    def body(i, n, dyv, xv, wv):
        rs = lax.rsqrt(jnp.mean(xv * xv, axis=-1, keepdims=True) + RMS_EPS)
        nv = xv * rs
        dn = dyv * wv[...]
        dx = rs * (dn - nv * jnp.mean(dn * nv, axis=-1, keepdims=True))
        return dx, jnp.sum(dyv * nv, axis=0, keepdims=True)

    return _rowwise("rms_bwd", body, [(dy, "row"), (x, "row"), (w, "full")],
                    [((s, c), F32, "row"), ((1, c), F32, "acc")], _tm(s, 512))


def _gated_rms_fwd(y, z, w):
    s, c = y.shape

    def body(i, n, yv, zv, wv):
        y2 = yv * (zv * _sigmoid(zv))
        rs = lax.rsqrt(jnp.mean(y2 * y2, axis=-1, keepdims=True) + RMS_EPS)
        return (y2 * rs * wv[...],)

    return _rowwise("gated_rms_fwd", body, [(y, "row"), (z, "row"), (w, "full")], [((s, c), BF16, "row")], _tm(s, 256))[0]


def _gated_rms_bwd(do, y, z, w):
    s, c = y.shape

    def body(i, n, dov, yv, zv, wv):
        sz = _sigmoid(zv)
        silu = zv * sz
        y2 = yv * silu
        rs = lax.rsqrt(jnp.mean(y2 * y2, axis=-1, keepdims=True) + RMS_EPS)
        nv = y2 * rs
        dn = dov * wv[...]
        dy2 = rs * (dn - nv * jnp.mean(dn * nv, axis=-1, keepdims=True))
        return dy2 * silu, dy2 * yv * (sz * (1.0 + zv * (1.0 - sz))), jnp.sum(dov * nv, axis=0, keepdims=True)

    return _rowwise("gated_rms_bwd", body, [(do, "row"), (y, "row"), (z, "row"), (w, "full")],
                    [((s, c), F32, "row"), ((s, c), F32, "row"), ((1, c), F32, "acc")], _tm(s, 256))


def _rope(name, r1, r2, cos, sin):
    def body(i, n, a, b, cv, sv):
        return a * cv - b * sv, b * cv + a * sv

    return _rowwise(name, body, [(r1, "row"), (r2, "row"), (cos, "row"), (sin, "row")],
                    [(r1.shape, F32, "row"), (r1.shape, F32, "row")], _tm(r1.shape[0], 512))


def _headsum_rope_bwd(d1_all, d2_all, cos, sin):
    s = d1_all.shape[0]
    half = MLA_ROPE // 2

    def body(i, n, a_all, b_all, cv, sv):
        rr = lax.broadcasted_iota(jnp.int32, (MLA_HEADS * half, half), 0)
        cc = lax.broadcasted_iota(jnp.int32, (MLA_HEADS * half, half), 1)
        sel = jnp.where(rr % half == cc, 1.0, 0.0).astype(F32)
        a = _dot(a_all, sel, 1, 0, HI)
        b = _dot(b_all, sel, 1, 0, HI)
        return a * cv + b * sv, b * cv - a * sv

    return _rowwise("headsum_rope_bwd", body, [(d1_all, "row"), (d2_all, "row"), (cos, "row"), (sin, "row")],
                    [((s, half), F32, "row"), ((s, half), F32, "row")], _tm(s, 512))


def _softplus_fwd(dtr, bias):
    def body(i, n, v, bv):
        return (_softplus(v + bv[...]),)

    return _rowwise("softplus_fwd", body, [(dtr, "row"), (bias, "full")], [(dtr.shape, F32, "row")], _tm(dtr.shape[0], 1024))[0]


def _softplus_bwd(ddt, dtr, bias):
    def body(i, n, dv, v, bv):
        d = dv * _sigmoid(v + bv[...])
        return d, jnp.sum(d, axis=0, keepdims=True)

    return _rowwise("softplus_bwd", body, [(ddt, "row"), (dtr, "row"), (bias, "full")],
                    [(dtr.shape, F32, "row"), ((1, dtr.shape[1]), F32, "acc")], _tm(dtr.shape[0], 1024))


def _rowdot(a, b):
    def body(i, n, av, bv):
        return (jnp.sum(av.astype(F32) * bv.astype(F32), axis=-1, keepdims=True),)

    return _rowwise("rowdot", body, [(a, "row"), (b, "row")], [((a.shape[0], 1), F32, "row")], _tm(a.shape[0], 2048))[0]


def _sum8(name, blocks):
    _, r, c = blocks.shape
    tr = _pick(r, (512, 256, 128, 64, 32, 16, 8))

    def kern(b_ref, o_ref):
        acc = b_ref[0].astype(F32)
        for d in range(1, N_DEV):
            acc = acc + b_ref[d].astype(F32)
        o_ref[...] = acc

    return pl.pallas_call(
        kern,
        name=name,
        grid=(r // tr,),
        in_specs=[pl.BlockSpec((N_DEV, tr, c), lambda i: (0, i, 0))],
        out_specs=pl.BlockSpec((tr, c), lambda i: (i, 0)),
        out_shape=jax.ShapeDtypeStruct((r, c), F32),
        compiler_params=_cp(("parallel",)),
    )(blocks)


def _adamw(name, w, g, m, v):
    r = w.shape[0]
    tm = _pick(r, (512, 256, 128, 64, 32, 16, 8))

    def body(i, n, wv, gv, mv, vv):
        m2 = ADAM_B1 * mv + (1.0 - ADAM_B1) * gv
        v2 = ADAM_B2 * vv + (1.0 - ADAM_B2) * (gv * gv)
        m_hat = m2 / (1.0 - ADAM_B1 ** ADAM_STEP)
        v_hat = v2 / (1.0 - ADAM_B2 ** ADAM_STEP)
        delta = -ADAM_LR * (m_hat / (jnp.sqrt(v_hat) + ADAM_EPS) + ADAM_WD * wv)
        return delta, m2, v2

    return _rowwise(name, body, [(w, "row"), (g, "row"), (m, "row"), (v, "row")],
                    [(w.shape, F32, "row")] * 3, tm)


def _pool_diff(i, t_rows, u, up):
    ucat = jnp.concatenate([jnp.where(i > 0, up, 0.0), u], axis=0)
    r = lax.broadcasted_iota(jnp.int32, (t_rows, 2 * t_rows), 0)
    cc = lax.broadcasted_iota(jnp.int32, (t_rows, 2 * t_rows), 1)
    lag = r + t_rows - cc
    t_col = i * t_rows + lax.broadcasted_iota(jnp.int32, (t_rows, 1), 0)
    diffs = []
    for gi, wdw in enumerate(POOL_WINDOWS):
        win = jnp.where((lag >= 0) & (lag < wdw), 1.0, 0.0).astype(F32)
        cnt = jnp.minimum(t_col + 1, wdw).astype(F32)
        ws = _dot(win, ucat[:, gi * POOL_GROUP:(gi + 1) * POOL_GROUP], 1, 0, HI)
        diffs.append(ws / cnt - u[:, gi * POOL_GROUP:(gi + 1) * POOL_GROUP])
    return diffs


def _pool_fwd(u, pw, sc):
    s = u.shape[0]
    tm = _tm(s, 256)

    def body(i, n, uv, upv, wv, scv):
        diffs = _pool_diff(i, tm, uv, upv)
        ys = [_dot(d.astype(BF16), wv[gi].astype(BF16), 1, 0) for gi, d in enumerate(diffs)]
        return (jnp.concatenate(ys, axis=1) * scv[...],)

    return _rowwise("pool_fwd", body, [(u, "row"), (u, "prev"), (pw, "full"), (sc, "full")],
                    [((s, POOL_WIDTH), BF16, "row")], tm)[0]


def _pool_bwd1(u, dy, pw, sc):
    s = u.shape[0]
    tm = _tm(s, 256)

    def body(i, n, uv, upv, dyv, wv, scv):
        diffs = _pool_diff(i, tm, uv, upv)
        scv = scv[...]
        dsc, dws, dds = [], [], []
        for gi, d in enumerate(diffs):
            sl = slice(gi * POOL_GROUP, (gi + 1) * POOL_GROUP)
            db, wb = d.astype(BF16), wv[gi].astype(BF16)
            yg = _dot(db, wb, 1, 0)
            dsc.append(jnp.sum(yg * dyv[:, sl], axis=0, keepdims=True))
            eb = (dyv[:, sl] * scv[:, sl]).astype(BF16)
            dws.append(_dot(db, eb, 0, 0)[None])
            dds.append(_dot(eb, wb, 1, 1))
        return jnp.concatenate(dds, axis=1), jnp.concatenate(dsc, axis=1), jnp.concatenate(dws, axis=0)

    return _rowwise("pool_bwd1", body, [(u, "row"), (u, "prev"), (dy, "row"), (pw, "full"), (sc, "full")],
                    [((s, POOL_WIDTH), F32, "row"), ((1, POOL_WIDTH), F32, "acc"), (pw.shape, F32, "acc")], tm)


def _pool_bwd2(dd):
    s = dd.shape[0]
    tm = _tm(s, 256)

    def body(i, n, dv, dnv):
        dcat = jnp.concatenate([dv, jnp.where(i < n - 1, dnv, 0.0)], axis=0)
        r = lax.broadcasted_iota(jnp.int32, (tm, 2 * tm), 0)
        cc = lax.broadcasted_iota(jnp.int32, (tm, 2 * tm), 1)
        lag = cc - r
        t_col = i * tm + lax.broadcasted_iota(jnp.int32, (2 * tm, 1), 0)
        outs = []
        for gi, wdw in enumerate(POOL_WINDOWS):
            sl = slice(gi * POOL_GROUP, (gi + 1) * POOL_GROUP)
            win = jnp.where((lag >= 0) & (lag < wdw), 1.0, 0.0).astype(F32)
            cnt = jnp.minimum(t_col + 1, wdw).astype(F32)
            outs.append(_dot(win, dcat[:, sl] / cnt, 1, 0, HI) - dv[:, sl])
        return (jnp.concatenate(outs, axis=1),)

    return _rowwise("pool_bwd2", body, [(dd, "row"), (dd, "next")], [((s, POOL_WIDTH), BF16, "row")], tm)[0]


def _shift_down(cur, other, j, tm):
    if j == 0:
        return cur
    rows = lax.broadcasted_iota(jnp.int32, cur.shape, 0)
    return jnp.where(rows < j, pltpu.roll(other, j, 0), pltpu.roll(cur, j, 0))


def _shift_up(cur, other, j, tm):
    if j == 0:
        return cur
    rows = lax.broadcasted_iota(jnp.int32, cur.shape, 0)
    return jnp.where(rows >= tm - j, pltpu.roll(other, tm - j, 0), pltpu.roll(cur, tm - j, 0))


def _conv_pre(i, tm, xv, xpv, wv, bv):
    xpv = jnp.where(i > 0, xpv, 0.0)
    taps = [_shift_down(xv, xpv, SSM_CONV - 1 - k, tm) for k in range(SSM_CONV)]
    pre = bv[...]
    for k in range(SSM_CONV):
        pre = pre + wv[k:k + 1, :] * taps[k]
    return pre, taps


def _conv_fwd(xbc, w, b):
    s = xbc.shape[0]
    tm = _tm(s, 256)

    def body(i, n, xv, xpv, wv, bv):
        pre, _ = _conv_pre(i, tm, xv, xpv, wv, bv)
        return (pre * _sigmoid(pre),)

    return _rowwise("conv_fwd", body, [(xbc, "row"), (xbc, "prev"), (w, "full"), (b, "full")],
                    [(xbc.shape, F32, "row")], tm)[0]


def _conv_bwd1(dxc, xbc, w, b):
    s, c = xbc.shape
    tm = _tm(s, 256)

    def body(i, n, dv, xv, xpv, wv, bv):
        pre, taps = _conv_pre(i, tm, xv, xpv, wv, bv)
        sg = _sigmoid(pre)
        dpre = dv * (sg * (1.0 + pre * (1.0 - sg)))
        tap_row = lax.broadcasted_iota(jnp.int32, (SSM_CONV, c), 0)
        dw = jnp.zeros((SSM_CONV, c), F32)
        for k in range(SSM_CONV):
            dw = dw + jnp.where(tap_row == k, jnp.sum(dpre * taps[k], axis=0, keepdims=True), 0.0)
        return dpre, dw, jnp.sum(dpre, axis=0, keepdims=True)

    return _rowwise("conv_bwd1", body, [(dxc, "row"), (xbc, "row"), (xbc, "prev"), (w, "full"), (b, "full")],
                    [((s, c), F32, "row"), ((SSM_CONV, c), F32, "acc"), ((1, c), F32, "acc")], tm)


def _conv_bwd2(dpre, w):
    s, c = dpre.shape
    tm = _tm(s, 256)

    def body(i, n, dv, dnv, wv):
        dnv = jnp.where(i < n - 1, dnv, 0.0)
        out = jnp.zeros_like(dv)
        for k in range(SSM_CONV):
            out = out + wv[k:k + 1, :] * _shift_up(dv, dnv, SSM_CONV - 1 - k, tm)
        return (out,)

    return _rowwise("conv_bwd2", body, [(dpre, "row"), (dpre, "next"), (w, "full")], [((s, c), BF16, "row")], tm)[0]


def _ssd_common(alog, dt_c, dt_r, tril, tri):
    a = -jnp.exp(alog)
    acs_c = _dot(tril, dt_c * a, 1, 0, HI)
    acs_r = _dot(dt_r * a, tril, 1, 1, HI)
    a_last = jnp.sum(dt_c * a, axis=0, keepdims=True)
    return a, acs_c, acs_r, a_last


def _ssd_fwd(xh, xht, dtc, dtr, bm, cm, alog, dsk):
    nh, s, p = xh.shape
    L = SSM_CHUNK
    nc = s // L
    E = nh // SSM_GROUPS

    def kern(xh_ref, xht_ref, dtc_ref, dtr_ref, b_ref, c_ref, alog_ref, dsk_ref, y_ref, hp_ref, h_s):
        c = pl.program_id(1)

        @pl.when(c == 0)
        def _():
            h_s[...] = jnp.zeros_like(h_s)

        bb = b_ref[...].astype(BF16)
        cb_ = c_ref[...].astype(BF16)
        cbm = _dot(cb_, bb, 1, 1)
        ri = lax.broadcasted_iota(jnp.int32, (L, L), 0)
        ci = lax.broadcasted_iota(jnp.int32, (L, L), 1)
        tri = ri >= ci
        tril = tri.astype(F32)
        for e in range(E):
            dt_c, dt_r = dtc_ref[e], dtr_ref[e]
            a, acs_c, acs_r, a_last = _ssd_common(alog_ref[e], dt_c, dt_r, tril, tri)
            lam = jnp.exp(jnp.where(tri, acs_c - acs_r, NEG))
            x = xh_ref[e]
            xdt = (x * dt_c).astype(BF16)
            hh = h_s[e]
            y = _dot((cbm * lam).astype(BF16), xdt, 1, 0)
            y = y + _dot(cb_, hh.astype(BF16), 1, 1) * jnp.exp(acs_c) + x * dsk_ref[e]
            y_ref[e] = y
            hp_ref[e] = hh
            xw = (xht_ref[e] * (dt_r * jnp.exp(a_last - acs_r))).astype(BF16)
            h_s[e] = hh * jnp.exp(a_last) + _dot(xw, bb, 1, 0)

    return pl.pallas_call(
        kern,
        name="ssd_fwd",
        grid=(SSM_GROUPS, nc),
        in_specs=[
            pl.BlockSpec((E, L, p), lambda g, c: (g, c, 0)),
            pl.BlockSpec((E, p, L), lambda g, c: (g, 0, c)),
            pl.BlockSpec((E, L, 1), lambda g, c: (g, c, 0)),
            pl.BlockSpec((E, 1, L), lambda g, c: (g, 0, c)),
            pl.BlockSpec((L, SSM_STATE), lambda g, c: (c, g)),
            pl.BlockSpec((L, SSM_STATE), lambda g, c: (c, g)),
            pl.BlockSpec((E, 1, 1), lambda g, c: (g, 0, 0)),
            pl.BlockSpec((E, 1, 1), lambda g, c: (g, 0, 0)),
        ],
        out_specs=[
            pl.BlockSpec((E, L, p), lambda g, c: (g, c, 0)),
            pl.BlockSpec((E, None, p, SSM_STATE), lambda g, c: (g, c, 0, 0)),
        ],
        out_shape=[jax.ShapeDtypeStruct((nh, s, p), F32), jax.ShapeDtypeStruct((nh, nc, p, SSM_STATE), F32)],
        scratch_shapes=[pltpu.VMEM((E, p, SSM_STATE), F32)],
        compiler_params=_cp(("arbitrary", "arbitrary")),
    )(xh, xht, dtc, dtr, bm, cm, alog, dsk)


def _ssd_bwd(dy, dyt, xh, dtc, dtr, bm, cm, hprev, alog, dsk):
    nh, s, p = xh.shape
    L = SSM_CHUNK
    nc = s // L
    E = nh // SSM_GROUPS

    def kern(dy_ref, dyt_ref, xh_ref, dtc_ref, dtr_ref, b_ref, c_ref, hp_ref, alog_ref, dsk_ref,
             dx_ref, ddt_ref, db_ref, dc_ref, dalog_ref, dd_ref, dh_s):
        c = pl.program_id(1)

        @pl.when(c == 0)
        def _():
            dh_s[...] = jnp.zeros_like(dh_s)
            dalog_ref[...] = jnp.zeros_like(dalog_ref)
            dd_ref[...] = jnp.zeros_like(dd_ref)

        bb = b_ref[...].astype(BF16)
        cb_ = c_ref[...].astype(BF16)
        cbm = _dot(cb_, bb, 1, 1)
        cbt = _dot(bb, cb_, 1, 1)
        ri = lax.broadcasted_iota(jnp.int32, (L, L), 0)
        ci = lax.broadcasted_iota(jnp.int32, (L, L), 1)
        tri = ri >= ci
        trit = ci >= ri
        tril = tri.astype(F32)
        triu = trit.astype(F32)
        db_acc = jnp.zeros((L, SSM_STATE), F32)
        dc_acc = jnp.zeros((L, SSM_STATE), F32)
        for e in range(E):
            dt_c, dt_r = dtc_ref[e], dtr_ref[e]
            a, acs_c, acs_r, a_last = _ssd_common(alog_ref[e], dt_c, dt_r, tril, tri)
            lam = jnp.exp(jnp.where(tri, acs_c - acs_r, NEG))
            lamt = jnp.exp(jnp.where(trit, acs_r - acs_c, NEG))
            x = xh_ref[e]
            xdt = x * dt_c
            xdtb = xdt.astype(BF16)
            dyv = dy_ref[e]
            dyb = dyv.astype(BF16)
            hh = hp_ref[e]
            hb = hh.astype(BF16)
            dhn = dh_s[e]
            dhnb = dhn.astype(BF16)
            ea_c = jnp.exp(acs_c)
            decay_c = jnp.exp(a_last - acs_c)
            e_last = jnp.exp(a_last)
            gm = _dot(dyb, xdtb, 1, 1)
            gt = _dot(xdtb, dyb, 1, 1)
            bdh = _dot(bb, dhnb, 1, 1)
            dxdt = _dot((cbt * lamt).astype(BF16), dyb, 1, 0) + bdh * decay_c
            dcb = gm * lam
            dcbt = gt * lamt
            yoff = _dot(cb_, hb, 1, 1) * ea_c
            dc_acc = dc_acc + _dot(dcb.astype(BF16), bb, 1, 0) + _dot(dyb, hb, 1, 0) * ea_c
            db_acc = db_acc + _dot(dcbt.astype(BF16), cb_, 1, 0) + _dot(xdtb, dhnb, 1, 0) * decay_c
            tt = decay_c * jnp.sum(xdt * bdh, axis=1, keepdims=True)
            dacs = (jnp.sum(dcb * cbm, axis=1, keepdims=True) - jnp.sum(dcbt * cbt, axis=1, keepdims=True)
                    + jnp.sum(dyv * yoff, axis=1, keepdims=True) - tt)
            tail = jnp.sum(tt, axis=0, keepdims=True) + e_last * jnp.sum(jnp.sum(dhn * hh, axis=1, keepdims=True), axis=0, keepdims=True)
            dda = _dot(triu, dacs, 1, 0, HI) + tail
            ddt_ref[e] = dda * a + jnp.sum(dxdt * x, axis=1, keepdims=True)
            dalog_ref[e] += jnp.sum(dda * dt_c, axis=0, keepdims=True) * a
            dd_ref[e] += jnp.sum(jnp.sum(dyv * x, axis=1, keepdims=True), axis=0, keepdims=True)
            dx_ref[e] = dxdt * dt_c + dyv * dsk_ref[e]
            dyw = (dyt_ref[e] * jnp.exp(acs_r)).astype(BF16)
            dh_s[e] = dhn * e_last + _dot(dyw, cb_, 1, 0)
        db_ref[...] = db_acc
        dc_ref[...] = dc_acc

    rc = lambda c: nc - 1 - c
    return pl.pallas_call(
        kern,
        name="ssd_bwd",
        grid=(SSM_GROUPS, nc),
        in_specs=[
            pl.BlockSpec((E, L, p), lambda g, c: (g, rc(c), 0)),
            pl.BlockSpec((E, p, L), lambda g, c: (g, 0, rc(c))),
            pl.BlockSpec((E, L, p), lambda g, c: (g, rc(c), 0)),
            pl.BlockSpec((E, L, 1), lambda g, c: (g, rc(c), 0)),
            pl.BlockSpec((E, 1, L), lambda g, c: (g, 0, rc(c))),
            pl.BlockSpec((L, SSM_STATE), lambda g, c: (rc(c), g)),
            pl.BlockSpec((L, SSM_STATE), lambda g, c: (rc(c), g)),
            pl.BlockSpec((E, None, p, SSM_STATE), lambda g, c: (g, rc(c), 0, 0)),
            pl.BlockSpec((E, 1, 1), lambda g, c: (g, 0, 0)),
            pl.BlockSpec((E, 1, 1), lambda g, c: (g, 0, 0)),
        ],
        out_specs=[
            pl.BlockSpec((E, L, p), lambda g, c: (g, rc(c), 0)),
            pl.BlockSpec((E, L, 1), lambda g, c: (g, rc(c), 0)),
            pl.BlockSpec((L, SSM_STATE), lambda g, c: (rc(c), g)),
            pl.BlockSpec((L, SSM_STATE), lambda g, c: (rc(c), g)),
            pl.BlockSpec((E, 1, 1), lambda g, c: (g, 0, 0)),
            pl.BlockSpec((E, 1, 1), lambda g, c: (g, 0, 0)),
        ],
        out_shape=[
            jax.ShapeDtypeStruct((nh, s, p), F32),
            jax.ShapeDtypeStruct((nh, s, 1), F32),
            jax.ShapeDtypeStruct((s, SSM_GROUPS * SSM_STATE), F32),
            jax.ShapeDtypeStruct((s, SSM_GROUPS * SSM_STATE), F32),
            jax.ShapeDtypeStruct((nh, 1, 1), F32),
            jax.ShapeDtypeStruct((nh, 1, 1), F32),
        ],
        scratch_shapes=[pltpu.VMEM((E, p, SSM_STATE), F32)],
        compiler_params=_cp(("arbitrary", "arbitrary")),
    )(dy, dyt, xh, dtc, dtr, bm, cm, hprev, alog, dsk)


def _fgate_fwd(fl, bias):
    h, s = fl.shape
    t = _tm(s, 512)

    def kern(fl_ref, b_ref, o_ref, carry):
        i = pl.program_id(0)

        @pl.when(i == 0)
        def _():
            carry[...] = jnp.zeros_like(carry)

        lf = -_softplus(-(fl_ref[...] + b_ref[...]))
        ri = lax.broadcasted_iota(jnp.int32, (t, t), 0)
        ci = lax.broadcasted_iota(jnp.int32, (t, t), 1)
        o_ref[...] = _dot(lf, (ri <= ci).astype(F32), 1, 0, HI) + carry[...]
        carry[...] += jnp.sum(lf, axis=1, keepdims=True)

    return pl.pallas_call(
        kern,
        name="fgate_fwd",
        grid=(s // t,),
        in_specs=[pl.BlockSpec((h, t), lambda i: (0, i)), pl.BlockSpec((h, 1), lambda i: (0, 0))],
        out_specs=pl.BlockSpec((h, t), lambda i: (0, i)),
        out_shape=jax.ShapeDtypeStruct((h, s), F32),
        scratch_shapes=[pltpu.VMEM((h, 1), F32)],
        compiler_params=_cp(("arbitrary",)),
    )(fl, bias)


def _fgate_bwd(dqe, dke, col, fl, bias):
    h, s = fl.shape
    w = dqe.shape[-1]
    t = _tm(s, 512)
    n = s // t

    def kern(dq_ref, dk_ref, fl_ref, b_ref, o_ref, db_ref, carry):
        i = pl.program_id(0)

        @pl.when(i == 0)
        def _():
            carry[...] = jnp.zeros_like(carry)
            db_ref[...] = jnp.zeros_like(db_ref)

        sel = (lax.broadcasted_iota(jnp.int32, (1, w), 1) == col).astype(F32)
        hid = lax.broadcasted_iota(jnp.int32, (h, t), 0)
        d = jnp.zeros((h, t), F32)
        for hh in range(h):
            r = _dot(sel, dq_ref[hh], 1, 1, HI) - _dot(sel, dk_ref[hh], 1, 1, HI)
            d = d + jnp.where(hid == hh, r, 0.0)
        ri = lax.broadcasted_iota(jnp.int32, (t, t), 0)
        ci = lax.broadcasted_iota(jnp.int32, (t, t), 1)
        rev = _dot(d, (ri >= ci).astype(F32), 1, 0, HI) + carry[...]
        carry[...] += jnp.sum(d, axis=1, keepdims=True)
        dz = rev * _sigmoid(-(fl_ref[...] + b_ref[...]))
        o_ref[...] = dz
        db_ref[...] += jnp.sum(dz, axis=1, keepdims=True)

    rev_blk = lambda i: (0, n - 1 - i)
    return pl.pallas_call(
        kern,
        name="fgate_bwd",
        grid=(n,),
        in_specs=[pl.BlockSpec((h, t, w), lambda i: (0, n - 1 - i, 0)), pl.BlockSpec((h, t, w), lambda i: (0, n - 1 - i, 0)),
                  pl.BlockSpec((h, t), rev_blk), pl.BlockSpec((h, 1), lambda i: (0, 0))],
        out_specs=[pl.BlockSpec((h, t), rev_blk), pl.BlockSpec((h, 1), lambda i: (0, 0))],
        out_shape=[jax.ShapeDtypeStruct((h, s), F32), jax.ShapeDtypeStruct((h, 1), F32)],
        scratch_shapes=[pltpu.VMEM((h, 1), F32)],
        compiler_params=_cp(("arbitrary",)),
    )(dqe, dke, fl, bias)


HPS = 2


def _attn_scores(q, k, fq, fk, scale, masked, t):
    sc = _dot(q, k, 1, 1)
    if scale is not None:
        sc = sc * scale
    if fq is not None:
        sc = sc + fq - fk
    if masked:
        rows = lax.broadcasted_iota(jnp.int32, (t, t), 0)
        cols = lax.broadcasted_iota(jnp.int32, (t, t), 1)
        sc = jnp.where(cols <= rows, sc, NEG)
    return sc


def _flash_fwd(name, q, k, v, fq, fk, scale):
    nh, s, dk = q.shape
    dv = v.shape[-1]
    t = _tm(s, 512)
    nq = s // t
    bias = fq is not None

    pairs = [(i, j) for i in range(nq) for j in range(i + 1)]
    qi_of = jnp.asarray(np.array([p[0] for p in pairs], np.int32))
    kj_of = jnp.asarray(np.array([p[1] for p in pairs], np.int32))

    def kern(qi_ref, kj_ref, *refs):
        if bias:
            q_ref, k_ref, v_ref, fq_ref, fk_ref, o_ref, lse_ref, m_s, l_s, acc_s = refs
        else:
            q_ref, k_ref, v_ref, o_ref, lse_ref, m_s, l_s, acc_s = refs
        qi, kj = qi_ref[pl.program_id(1)], kj_ref[pl.program_id(1)]

        @pl.when(kj == 0)
        def _():
            m_s[...] = jnp.full_like(m_s, NEG)
            l_s[...] = jnp.zeros_like(l_s)
            acc_s[...] = jnp.zeros_like(acc_s)

        def step(masked):
            for hh in range(HPS):
                sc = _attn_scores(q_ref[hh], k_ref[hh], fq_ref[hh] if bias else None, fk_ref[hh] if bias else None,
                                  scale, masked, t)
                m_old = m_s[hh]
                m_new = jnp.maximum(m_old, jnp.max(sc, axis=1, keepdims=True))
                corr = jnp.exp(m_old - m_new)
                p = jnp.exp(sc - m_new)
                l_s[hh] = corr * l_s[hh] + jnp.sum(p, axis=1, keepdims=True)
                acc_s[hh] = acc_s[hh] * corr + _dot(p.astype(BF16), v_ref[hh], 1, 0)
                m_s[hh] = m_new

        @pl.when(kj < qi)
        def _():
            step(False)

        @pl.when(kj == qi)
        def _():
            step(True)
            o_ref[...] = acc_s[...] / l_s[...]
            lse_ref[...] = m_s[...] + jnp.log(l_s[...])

    qspec = lambda d: pl.BlockSpec((HPS, t, d), lambda h, p, qi_r, kj_r: (h, qi_r[p], 0))
    kspec = lambda d: pl.BlockSpec((HPS, t, d), lambda h, p, qi_r, kj_r: (h, kj_r[p], 0))
    in_specs = [qspec(dk), kspec(dk), kspec(dv)]
    args = [q, k, v]
    if bias:
        in_specs += [qspec(1), pl.BlockSpec((HPS, 1, t), lambda h, p, qi_r, kj_r: (h, 0, kj_r[p]))]
        args += [fq, fk]
    return pl.pallas_call(
        kern,
        name=name,
        grid_spec=pltpu.PrefetchScalarGridSpec(
            num_scalar_prefetch=2,
            grid=(nh // HPS, len(pairs)),
            in_specs=in_specs,
            out_specs=[qspec(dv), qspec(1)],
            scratch_shapes=[pltpu.VMEM((HPS, t, 1), F32), pltpu.VMEM((HPS, t, 1), F32),
                            pltpu.VMEM((HPS, t, dv), F32)],
        ),
        out_shape=[jax.ShapeDtypeStruct((nh, s, dv), F32), jax.ShapeDtypeStruct((nh, s, 1), F32)],
        compiler_params=_cp(("parallel", "arbitrary")),
    )(qi_of, kj_of, *args)


def _flash_bwd(name, q, k, v, do, lse, delta, fq, fk, qg, kg, scale, dq_scale, dk_scale):
    nh, s, dk = q.shape
    dv = v.shape[-1]
    dg = qg.shape[-1]
    t = _tm(s, 512)
    nq = s // t
    bias = fq is not None
    ext = qg is not q

    pairs = [(j, i) for j in range(nq) for i in range(j, nq)]
    kb_of = jnp.asarray(np.array([p[0] for p in pairs], np.int32))
    qi_of = jnp.asarray(np.array([p[1] for p in pairs], np.int32))

    def kern(kb_ref, qi_ref, *refs):
        refs = list(refs)
        q_ref, k_ref, v_ref, do_ref, lse_ref, dl_ref = refs[:6]
        del refs[:6]
        fq_ref, fk_ref = (refs.pop(0), refs.pop(0)) if bias else (None, None)
        qg_ref, kg_ref = (refs.pop(0), refs.pop(0)) if ext else (q_ref, k_ref)
        dq_ref, dk_ref, dv_ref, dk_s, dv_s = refs
        kb, qi = kb_ref[pl.program_id(1)], qi_ref[pl.program_id(1)]

        @pl.when(pl.program_id(1) == 0)
        def _():
            dq_ref[...] = jnp.zeros_like(dq_ref)

        @pl.when(qi == kb)
        def _():
            dk_s[...] = jnp.zeros_like(dk_s)
            dv_s[...] = jnp.zeros_like(dv_s)

        def step(masked):
            rows = pl.ds(pl.multiple_of(qi * t, t), t)
            for hh in range(HPS):
                dob = do_ref[hh]
                sc = _attn_scores(q_ref[hh], k_ref[hh], fq_ref[hh] if bias else None, fk_ref[hh] if bias else None,
                                  scale, masked, t)
                p = jnp.exp(sc - lse_ref[hh])
                dv_s[hh] += _dot(p.astype(BF16), dob, 0, 0)
                dp = _dot(dob, v_ref[hh], 1, 1)
                dsb = (p * (dp - dl_ref[hh])).astype(BF16)
                dk_s[hh] += _dot(dsb, qg_ref[hh], 0, 0)
                dq_ref[hh, rows, :] += _dot(dsb, kg_ref[hh], 1, 0) * dq_scale

        @pl.when(qi > kb)
        def _():
            step(False)

        @pl.when(qi == kb)
        def _():
            step(True)

        @pl.when(qi == nq - 1)
        def _():
            dk_ref[...] = dk_s[...] if dk_scale is None else dk_s[...] * dk_scale
            dv_ref[...] = dv_s[...]

    qspec = lambda d: pl.BlockSpec((HPS, t, d), lambda h, p, kb_r, qi_r: (h, qi_r[p], 0))
    kspec = lambda d: pl.BlockSpec((HPS, t, d), lambda h, p, kb_r, qi_r: (h, kb_r[p], 0))
    in_specs = [qspec(dk), kspec(dk), kspec(dv), qspec(dv), qspec(1), qspec(1)]
    args = [q, k, v, do, lse, delta]
    if bias:
        in_specs += [qspec(1), pl.BlockSpec((HPS, 1, t), lambda h, p, kb_r, qi_r: (h, 0, kb_r[p]))]
        args += [fq, fk]
    if ext:
        in_specs += [qspec(dg), kspec(dg)]
        args += [qg, kg]
    return pl.pallas_call(
        kern,
        name=name,
        grid_spec=pltpu.PrefetchScalarGridSpec(
            num_scalar_prefetch=2,
            grid=(nh // HPS, len(pairs)),
            in_specs=in_specs,
            out_specs=[pl.BlockSpec((HPS, s, dg), lambda h, p, kb_r, qi_r: (h, 0, 0)), kspec(dg), kspec(dv)],
            scratch_shapes=[pltpu.VMEM((HPS, t, dg), F32), pltpu.VMEM((HPS, t, dv), F32)],
        ),
        out_shape=[jax.ShapeDtypeStruct((nh, s, dg), F32), jax.ShapeDtypeStruct((nh, s, dg), F32),
                   jax.ShapeDtypeStruct((nh, s, dv), F32)],
        compiler_params=_cp(("arbitrary", "arbitrary")),
    )(kb_of, qi_of, *args)


def _heads(t, nh):
    s = t.shape[0]
    return t.reshape(s, nh, -1).transpose(1, 0, 2)


def _unheads(t):
    nh, s, d = t.shape
    return t.transpose(1, 0, 2).reshape(s, nh * d)


def _perm_uq(w):
    r = w.shape[0]
    w3 = w.reshape(r, MLA_HEADS, MLA_NOPE + MLA_ROPE)
    half = MLA_ROPE // 2
    return jnp.concatenate([w3[:, :, :MLA_NOPE].reshape(r, -1), w3[:, :, MLA_NOPE:MLA_NOPE + half].reshape(r, -1),
                            w3[:, :, MLA_NOPE + half:].reshape(r, -1)], axis=1)


def _unperm_uq(w):
    r = w.shape[0]
    half = MLA_ROPE // 2
    n0 = MLA_HEADS * MLA_NOPE
    n1 = n0 + MLA_HEADS * half
    return jnp.concatenate([w[:, :n0].reshape(r, MLA_HEADS, MLA_NOPE), w[:, n0:n1].reshape(r, MLA_HEADS, half),
                            w[:, n1:].reshape(r, MLA_HEADS, half)], axis=2).reshape(r, -1)


def _perm_ukv(w):
    r = w.shape[0]
    w3 = w.reshape(r, MLA_HEADS, MLA_NOPE + MLA_V)
    return jnp.concatenate([w3[:, :, :MLA_NOPE].reshape(r, -1), w3[:, :, MLA_NOPE:].reshape(r, -1)], axis=1)


def _unperm_ukv(w):
    r = w.shape[0]
    n0 = MLA_HEADS * MLA_NOPE
    return jnp.concatenate([w[:, :n0].reshape(r, MLA_HEADS, MLA_NOPE), w[:, n0:].reshape(r, MLA_HEADS, MLA_V)],
                           axis=2).reshape(r, -1)


_ODD_CUTS = np.cumsum([0, FOX_WIDTH, FOX_WIDTH, FOX_WIDTH, FOX_HEADS, MLA_Q_RANK, MLA_KV_RANK, MLA_ROPE]).tolist()
_ODD_ORDER = (0, 1, 2, 4, 5, 6, 3)


def _perm_odd_in(w):
    parts = [w[:, _ODD_CUTS[j]:_ODD_CUTS[j + 1]] for j in _ODD_ORDER]
    parts.append(jnp.zeros((w.shape[0], ODD_IN_PAD - ODD_IN), w.dtype))
    return jnp.concatenate(parts, axis=1)


def _unperm_odd_in(w):
    widths = [_ODD_CUTS[j + 1] - _ODD_CUTS[j] for j in _ODD_ORDER]
    offs = np.cumsum([0] + widths).tolist()
    pieces = {j: w[:, offs[n]:offs[n + 1]] for n, j in enumerate(_ODD_ORDER)}
    return jnp.concatenate([pieces[j] for j in range(7)], axis=1)


def _pad_cols(w, n):
    return jnp.concatenate([w, jnp.zeros((w.shape[0], n - w.shape[1]), w.dtype)], axis=1)


_BIG = (("even_w_in", 2), ("even_w_out", 1), ("odd_w_in", 2), ("w_uq", 2), ("w_ukv", 2), ("odd_w_out", 1),
        ("ffn_w_gate", 2), ("ffn_w_up", 2), ("ffn_w_down", 1))
_PACK_COLS = 1024


def _unshard(blocks, shp, ax):
    t = jnp.moveaxis(blocks.reshape((N_DEV,) + tuple(shp)), 0, ax)
    full = list(shp)
    full[ax] = shp[ax] * N_DEV
    return t.reshape(full)


def _reshard(full, ax, cols):
    shp = list(full.shape)
    t = full.reshape(shp[:ax] + [N_DEV, shp[ax] // N_DEV] + shp[ax + 1:])
    return jnp.moveaxis(t, ax, 0).reshape(N_DEV, -1, cols)


def kernel(x, even_w_in, pool_w, pool_scale, conv_w, conv_b, dt_bias, a_log, d_skip, ssm_norm_w, even_w_out, odd_w_in, fgate_b, q_norm_w, w_uq, kv_norm_w, w_ukv, odd_w_out, ffn_w_gate, ffn_w_up, ffn_w_down, ln_mix_g, ln_mix_b, ln_ffn_g, ln_ffn_b, loss_target, m_even_w_in, m_pool_w, m_pool_scale, m_conv_w, m_conv_b, m_dt_bias, m_a_log, m_d_skip, m_ssm_norm_w, m_even_w_out, m_odd_w_in, m_fgate_b, m_q_norm_w, m_w_uq, m_kv_norm_w, m_w_ukv, m_odd_w_out, m_ffn_w_gate, m_ffn_w_up, m_ffn_w_down, m_ln_mix_g, m_ln_mix_b, m_ln_ffn_g, m_ln_ffn_b, v_even_w_in, v_pool_w, v_pool_scale, v_conv_w, v_conv_b, v_dt_bias, v_a_log, v_d_skip, v_ssm_norm_w, v_even_w_out, v_odd_w_in, v_fgate_b, v_q_norm_w, v_w_uq, v_kv_norm_w, v_w_ukv, v_odd_w_out, v_ffn_w_gate, v_ffn_w_up, v_ffn_w_down, v_ln_mix_g, v_ln_mix_b, v_ln_ffn_g, v_ln_ffn_b):
    P = dict(even_w_in=even_w_in, pool_w=pool_w, pool_scale=pool_scale, conv_w=conv_w, conv_b=conv_b, dt_bias=dt_bias,
             a_log=a_log, d_skip=d_skip, ssm_norm_w=ssm_norm_w, even_w_out=even_w_out, odd_w_in=odd_w_in,
             fgate_b=fgate_b, q_norm_w=q_norm_w, w_uq=w_uq, kv_norm_w=kv_norm_w, w_ukv=w_ukv, odd_w_out=odd_w_out,
             ffn_w_gate=ffn_w_gate, ffn_w_up=ffn_w_up, ffn_w_down=ffn_w_down, ln_mix_g=ln_mix_g, ln_mix_b=ln_mix_b,
             ln_ffn_g=ln_ffn_g, ln_ffn_b=ln_ffn_b)
    M = dict(even_w_in=m_even_w_in, pool_w=m_pool_w, pool_scale=m_pool_scale, conv_w=m_conv_w, conv_b=m_conv_b,
             dt_bias=m_dt_bias, a_log=m_a_log, d_skip=m_d_skip, ssm_norm_w=m_ssm_norm_w, even_w_out=m_even_w_out,
             odd_w_in=m_odd_w_in, fgate_b=m_fgate_b, q_norm_w=m_q_norm_w, w_uq=m_w_uq, kv_norm_w=m_kv_norm_w,
             w_ukv=m_w_ukv, odd_w_out=m_odd_w_out, ffn_w_gate=m_ffn_w_gate, ffn_w_up=m_ffn_w_up,
             ffn_w_down=m_ffn_w_down, ln_mix_g=m_ln_mix_g, ln_mix_b=m_ln_mix_b, ln_ffn_g=m_ln_ffn_g,
             ln_ffn_b=m_ln_ffn_b)
    V = dict(even_w_in=v_even_w_in, pool_w=v_pool_w, pool_scale=v_pool_scale, conv_w=v_conv_w, conv_b=v_conv_b,
             dt_bias=v_dt_bias, a_log=v_a_log, d_skip=v_d_skip, ssm_norm_w=v_ssm_norm_w, even_w_out=v_even_w_out,
             odd_w_in=v_odd_w_in, fgate_b=v_fgate_b, q_norm_w=v_q_norm_w, w_uq=v_w_uq, kv_norm_w=v_kv_norm_w,
             w_ukv=v_w_ukv, odd_w_out=v_odd_w_out, ffn_w_gate=v_ffn_w_gate, ffn_w_up=v_ffn_w_up,
             ffn_w_down=v_ffn_w_down, ln_mix_g=v_ln_mix_g, ln_mix_b=v_ln_mix_b, ln_ffn_g=v_ln_ffn_g,
             ln_ffn_b=v_ln_ffn_b)
    names = list(P)
    s = x.shape[1]
    me = 4 * lax.axis_index("x") + 2 * lax.axis_index("y") + lax.axis_index("c")

    big_rows = [math.prod(P[n].shape) // _PACK_COLS for n, _ in _BIG]
    pad_rows = [-(-nr // 16) * 16 for nr in big_rows]
    rows_big = sum(pad_rows)
    packed = jnp.concatenate([
        jnp.pad(P[n].astype(BF16).reshape(-1), (0, (pr - nr) * _PACK_COLS))
        for (n, _), nr, pr in zip(_BIG, big_rows, pad_rows)]).reshape(rows_big, _PACK_COLS)
    gathered = _all_gather("ag_weights", packed)
    W = {}
    off = 0
    for (n, ax), nr, pr in zip(_BIG, big_rows, pad_rows):
        W[n] = _unshard(gathered[:, off:off + nr, :], P[n].shape, ax)
        off += pr
    kv_pad = jnp.zeros((128 - kv_norm_w.size,), F32)
    small_sh = jnp.concatenate([conv_w.reshape(-1), q_norm_w.reshape(-1), kv_norm_w.reshape(-1), kv_pad,
                                jnp.zeros((2 * 128,), F32)]).reshape(16, 128)
    g_small = _all_gather("ag_small_weights", small_sh)
    conv_w_full = _unshard(g_small[:, :12, :], conv_w.shape, 2)
    q_norm_full = _unshard(g_small[:, 12:13, :], q_norm_w.shape, 1)
    kv_norm_full = _unshard(g_small[:, 13:14, :64], kv_norm_w.shape, 1)

    w_in_e = [_pad_cols(W["even_w_in"][i], EVEN_IN_PAD) for i in range(2)]
    w_in_o = [_perm_odd_in(W["odd_w_in"][i]) for i in range(2)]
    w_uq_p = [_perm_uq(W["w_uq"][i]) for i in range(2)]
    w_ukv_p = [_perm_ukv(W["w_ukv"][i]) for i in range(2)]
    w_gu = [jnp.concatenate([W["ffn_w_gate"][l], W["ffn_w_up"][l]], axis=1) for l in range(DEPTH)]

    pos = jnp.arange(s, dtype=F32)
    half = MLA_ROPE // 2
    freqs = jnp.power(ROPE_THETA, -jnp.arange(half, dtype=F32) / half)
    ang = pos[:, None] * freqs[None, :]
    cos16, sin16 = jnp.cos(ang), jnp.sin(ang)
    cos128, sin128 = jnp.tile(cos16, (1, MLA_HEADS)), jnp.tile(sin16, (1, MLA_HEADS))
    row = lambda t: t.reshape(1, -1)

    xcur = x[0]
    xb = xcur.astype(BF16)
    saved = []
    for l in range(DEPTH):
        i = l // 2
        sv = dict(x_in_b=xb)
        if l % 2 == 0:
            proj = _mm("mm_in_even", xb, w_in_e[i], "nn", F32)
            u, z = proj[:, :512], proj[:, 512:1536]
            xbc, dtraw = proj[:, 1536:3072], proj[:, 3072:3088]
            ypool = _pool_fwd(u, pool_w[i], row(pool_scale[i]))
            xc = _conv_fwd(xbc, conv_w_full[i], row(conv_b[i]))
            dt = _softplus_fwd(dtraw, row(dt_bias[i]))
            xh = _heads(xc[:, :SSM_D_INNER], SSM_HEADS)
            dtc = dt.T[:, :, None]
            dtr = dt.T[:, None, :]
            bm, cm = xc[:, SSM_D_INNER:SSM_D_INNER + 256], xc[:, SSM_D_INNER + 256:]
            alog3, dsk3 = a_log[i].reshape(-1, 1, 1), d_skip[i].reshape(-1, 1, 1)
            yh, hprev = _ssd_fwd(xh, xh.transpose(0, 2, 1), dtc, dtr, bm, cm, alog3, dsk3)
            y_ssm = _unheads(yh)
            yn = _gated_rms_fwd(y_ssm, z, row(ssm_norm_w[i]))
            mix = jnp.concatenate([ypool, yn], axis=1)
            h = _mm("mm_out_even", mix, W["even_w_out"][i], "nn", F32)
            sv.update(u=u, z=z, xbc=xbc, dtraw=dtraw, xh=xh, dtc=dtc, dtr=dtr, bm=bm, cm=cm, hprev=hprev,
                      y_ssm=y_ssm, mix=mix)
        else:
            proj = _mm("mm_in_odd", xb, w_in_o[i], "nn", F32)
            qf, kf, vf = proj[:, :512], proj[:, 512:1024], proj[:, 1024:1536]
            cq, ckv = proj[:, 1536:2048], proj[:, 2048:2304]
            kr, fl = proj[:, 2304:2336], proj[:, 2336:2344]
            fl = fl.T
            fcum = _fgate_fwd(fl, fgate_b[i][:, None])
            fq_ = fcum[:, :, None]
            fk_ = fcum[:, None, :]
            qh, kh, vh = (_heads(t.astype(BF16), FOX_HEADS) for t in (qf * FOX_SCALE, kf, vf))
            o_fox, lse_fox = _flash_fwd("fox_fwd", qh, kh, vh, fq_, fk_, None)
            qn = _rms_fwd(cq, row(q_norm_full[i]))
            qp = _mm("mm_uq", qn, w_uq_p[i], "nn", F32)
            q1, q2 = _rope("rope_q", qp[:, 512:640], qp[:, 640:768], cos128, sin128)
            kvn = _rms_fwd(ckv, row(kv_norm_full[i]))
            kvp = _mm("mm_ukv", kvn, w_ukv_p[i], "nn", F32)
            k1, k2 = _rope("rope_k", kr[:, :half], kr[:, half:], cos16, sin16)
            zpad = jnp.zeros((MLA_HEADS, s, MLA_DK_PAD - MLA_NOPE - MLA_ROPE), BF16)
            qm = jnp.concatenate([_heads(qp[:, :512], MLA_HEADS), _heads(q1, MLA_HEADS), _heads(q2, MLA_HEADS)],
                                 axis=2).astype(BF16)
            qm = jnp.concatenate([qm, zpad], axis=2)
            krope = jnp.broadcast_to(jnp.concatenate([k1, k2], axis=1)[None], (MLA_HEADS, s, MLA_ROPE))
            km = jnp.concatenate([_heads(kvp[:, :512], MLA_HEADS), krope], axis=2).astype(BF16)
            km = jnp.concatenate([km, zpad], axis=2)
            vm = _heads(kvp[:, 512:], MLA_HEADS).astype(BF16)
            o_mla, lse_mla = _flash_fwd("mla_fwd", qm, km, vm, None, None, MLA_SCALE)
            mix = jnp.concatenate([_unheads(o_fox), _unheads(o_mla)], axis=1).astype(BF16)
            h = _mm("mm_out_odd", mix, W["odd_w_out"][i], "nn", F32)
            sv.update(fl=fl, fq=fq_, fk=fk_, qh=qh, kh=kh, vh=vh, o_fox=o_fox, lse_fox=lse_fox, cq=cq, ckv=ckv,
                      qn=qn, kvn=kvn, qm=qm, km=km, vm=vm, o_mla=o_mla, lse_mla=lse_mla, mix=mix)
        y1, y1b, r1 = _ln_fwd(xcur, h, row(ln_mix_g[l]), row(ln_mix_b[l]))
        gu = _mm("mm_ffn_in", y1b, w_gu[l], "nn", F32)
        act = _swiglu_fwd(gu)
        h2 = _mm("mm_ffn_out", act, W["ffn_w_down"][l], "nn", F32)
        y2, y2b, r2 = _ln_fwd(y1, h2, row(ln_ffn_g[l]), row(ln_ffn_b[l]))
        sv.update(r1=r1, y1b=y1b, gu=gu, act=act, r2=r2)
        saved.append(sv)
        xcur, xb = y2, y2b

    dy, loss_part = _loss_head(xcur, loss_target[0])
    loss = lax.psum(loss_part[0, 0], ("x", "y", "c"))

    G = {n: [None] * P[n].shape[0] for n in names}
    acur, dcur = None, dy
    for l in reversed(range(DEPTH)):
        i = l // 2
        sv = saved[l]
        dr2, dr2b, dg, db = _ln_bwd(acur, dcur, sv["r2"], row(ln_ffn_g[l]))
        G["ln_ffn_g"][l], G["ln_ffn_b"][l] = dg[0], db[0]
        G["ffn_w_down"][l] = _mm("mm_dw_ffn_out", sv["act"], dr2b, "tn", F32)
        dact = _mm("mm_dx_ffn_out", dr2b, W["ffn_w_down"][l], "nt", F32)
        dgu = _swiglu_bwd(dact, sv["gu"])
        dwgu = _mm("mm_dw_ffn_in", sv["y1b"], dgu, "tn", F32)
        G["ffn_w_gate"][l], G["ffn_w_up"][l] = dwgu[:, :D_FF], dwgu[:, D_FF:]
        dy1 = _mm("mm_dx_ffn_in", dgu, w_gu[l], "nt", F32)
        dr1, dr1b, dg, db = _ln_bwd(dr2, dy1, sv["r1"], row(ln_mix_g[l]))
        G["ln_mix_g"][l], G["ln_mix_b"][l] = dg[0], db[0]
        if l % 2 == 0:
            G["even_w_out"][i] = _mm("mm_dw_out_even", sv["mix"], dr1b, "tn", F32)
            dmix = _mm("mm_dx_out_even", dr1b, W["even_w_out"][i], "nt", F32)
            dd, dsc, dpw = _pool_bwd1(sv["u"], dmix[:, :POOL_WIDTH], pool_w[i], row(pool_scale[i]))
            G["pool_scale"][i], G["pool_w"][i] = dsc[0], dpw
            du = _pool_bwd2(dd)
            dys, dz, dnw = _gated_rms_bwd(dmix[:, POOL_WIDTH:], sv["y_ssm"], sv["z"], row(ssm_norm_w[i]))
            G["ssm_norm_w"][i] = dnw[0]
            dyh = _heads(dys, SSM_HEADS)
            alog3, dsk3 = a_log[i].reshape(-1, 1, 1), d_skip[i].reshape(-1, 1, 1)
            dxh, ddt3, dbm, dcm, dalog, ddsk = _ssd_bwd(dyh, dyh.transpose(0, 2, 1), sv["xh"], sv["dtc"], sv["dtr"],
                                                        sv["bm"], sv["cm"], sv["hprev"], alog3, dsk3)
            G["a_log"][i], G["d_skip"][i] = dalog.reshape(-1), ddsk.reshape(-1)
            ddtraw, ddtb = _softplus_bwd(ddt3[:, :, 0].T, sv["dtraw"], row(dt_bias[i]))
            G["dt_bias"][i] = ddtb[0]
            dxc = jnp.concatenate([_unheads(dxh), dbm, dcm], axis=1)
            dpre, dcw, dcb = _conv_bwd1(dxc, sv["xbc"], conv_w_full[i], row(conv_b[i]))
            G["conv_w"][i], G["conv_b"][i] = dcw, dcb[0]
            dxbc = _conv_bwd2(dpre, conv_w_full[i])
            dproj = jnp.concatenate([du, dz.astype(BF16), dxbc, ddtraw.astype(BF16),
                                     jnp.zeros((s, EVEN_IN_PAD - EVEN_IN), BF16)], axis=1)
            G["even_w_in"][i] = _mm("mm_dw_in_even", sv["x_in_b"], dproj, "tn", F32)[:, :EVEN_IN]
            dxb = _mm("mm_dx_in_even", dproj, w_in_e[i], "nt", F32)
        else:
            G["odd_w_out"][i] = _mm("mm_dw_out_odd", sv["mix"], dr1b, "tn", F32)
            dmix = _mm("mm_dx_out_odd", dr1b, W["odd_w_out"][i], "nt", F32)
            do_f = _heads(dmix[:, :FOX_WIDTH].astype(BF16), FOX_HEADS)
            dl_f = _rowdot(do_f.reshape(-1, 64), sv["o_fox"].reshape(-1, 64)).reshape(FOX_HEADS, s, 1)
            ones = jnp.ones((FOX_HEADS, s, 64), BF16)
            qg = jnp.concatenate([sv["qh"], ones], axis=2)
            kg = jnp.concatenate([sv["kh"], ones * (1.0 / FOX_SCALE)], axis=2)
            dqe, dke, dvh = _flash_bwd("fox_bwd", sv["qh"], sv["kh"], sv["vh"], do_f, sv["lse_fox"], dl_f,
                                       sv["fq"], sv["fk"], qg, kg, None, FOX_SCALE, None)
            dqh, dkh = dqe[:, :, :64], dke[:, :, :64]
            dfl, dfb = _fgate_bwd(dqe, dke, 64, sv["fl"], fgate_b[i][:, None])
            dfl = dfl.T
            G["fgate_b"][i] = dfb[:, 0]
            do_m = _heads(dmix[:, FOX_WIDTH:].astype(BF16), MLA_HEADS)
            dl_m = _rowdot(do_m.reshape(-1, 64), sv["o_mla"].reshape(-1, 64)).reshape(MLA_HEADS, s, 1)
            dqm, dkm, dvm = _flash_bwd("mla_bwd", sv["qm"], sv["km"], sv["vm"], do_m, sv["lse_mla"], dl_m,
                                       None, None, sv["qm"], sv["km"], MLA_SCALE, MLA_SCALE, MLA_SCALE)
            n0, n1 = MLA_NOPE, MLA_NOPE + half
            dq1, dq2 = _rope("rope_q_bwd", _unheads(dqm[:, :, n0:n1]), _unheads(dqm[:, :, n1:n1 + half]),
                             cos128, -sin128)
            dqp = jnp.concatenate([_unheads(dqm[:, :, :n0]), dq1, dq2], axis=1).astype(BF16)
            G["w_uq"][i] = _unperm_uq(_mm("mm_dw_uq", sv["qn"], dqp, "tn", F32))
            dqn = _mm("mm_dx_uq", dqp, w_uq_p[i], "nt", F32)
            dcq, dqw = _rms_bwd(dqn, sv["cq"], row(q_norm_full[i]))
            G["q_norm_w"][i] = dqw[0]
            dk1, dk2 = _headsum_rope_bwd(_unheads(dkm[:, :, n0:n1]), _unheads(dkm[:, :, n1:n1 + half]), cos16, sin16)
            dkvp = jnp.concatenate([_unheads(dkm[:, :, :n0]), _unheads(dvm)], axis=1).astype(BF16)
            G["w_ukv"][i] = _unperm_ukv(_mm("mm_dw_ukv", sv["kvn"], dkvp, "tn", F32))
            dkvn = _mm("mm_dx_ukv", dkvp, w_ukv_p[i], "nt", F32)
            dckv, dkvw = _rms_bwd(dkvn, sv["ckv"], row(kv_norm_full[i]))
            G["kv_norm_w"][i] = dkvw[0]
            dproj = jnp.concatenate([_unheads(dqh), _unheads(dkh), _unheads(dvh), dcq, dckv, dk1, dk2, dfl,
                                     jnp.zeros((s, ODD_IN_PAD - ODD_IN), F32)], axis=1).astype(BF16)
            G["odd_w_in"][i] = _unperm_odd_in(_mm("mm_dw_in_odd", sv["x_in_b"], dproj, "tn", F32))
            dxb = _mm("mm_dx_in_odd", dproj, w_in_o[i], "nt", F32)
        acur, dcur = dr1, dxb
    grad_x = _axpy(acur, dcur)[None]

    gfull = {n: jnp.stack(G[n]) for n in names}
    send = jnp.concatenate([
        jnp.pad(_reshard(gfull[n], ax, _PACK_COLS).astype(BF16), ((0, 0), (0, pr - nr), (0, 0)))
        for (n, ax), nr, pr in zip(_BIG, big_rows, pad_rows)], axis=1)
    recv = _all_to_all("a2a_grads", send)
    gsum = _sum8("sum_grads", recv)
    grads = {}
    off = 0
    for (n, ax), nr, pr in zip(_BIG, big_rows, pad_rows):
        grads[n] = gsum[off:off + nr].reshape(P[n].shape)
        off += pr
    small = [n for n in names if n not in dict(_BIG)]
    sflat = jnp.concatenate([gfull[n].reshape(-1) for n in small])
    n_small = sflat.shape[0]
    rows_small = -(-n_small // (128 * 8)) * 8
    sflat = jnp.concatenate([sflat, jnp.zeros((rows_small * 128 - n_small,), F32)])
    sg = _sum8("sum_small_grads", _all_gather("ag_small_grads", sflat.reshape(rows_small, 128))).reshape(-1)
    off = 0
    for n in small:
        cnt = gfull[n].size
        gf = sg[off:off + cnt].reshape(gfull[n].shape)
        off += cnt
        if gf.shape != P[n].shape:
            width = P[n].shape[-1]
            gf = lax.dynamic_slice_in_dim(gf, me * width, width, axis=gf.ndim - 1)
        grads[n] = gf

    delta, new_m, new_v = {}, {}, {}
    for n, _ in _BIG:
        shp = P[n].shape
        two = lambda t: t.reshape(-1, shp[-1])
        d_, m_, v_ = _adamw("adamw_" + n, two(P[n]), two(grads[n]), two(M[n]), two(V[n]))
        delta[n], new_m[n], new_v[n] = d_.reshape(shp), m_.reshape(shp), v_.reshape(shp)

    def packs(d):
        f = jnp.concatenate([d[n].reshape(-1) for n in small])
        pad = -(-f.shape[0] // (128 * 8)) * 8 * 128 - f.shape[0]
        return jnp.concatenate([f, jnp.zeros((pad,), F32)]).reshape(-1, 128)

    d_, m_, v_ = _adamw("adamw_small", packs(P), packs(grads), packs(M), packs(V))
    off = 0
    for n in small:
        cnt = P[n].size
        for dst, src in ((delta, d_), (new_m, m_), (new_v, v_)):
            dst[n] = src.reshape(-1)[off:off + cnt].reshape(P[n].shape)
        off += cnt

    return (loss, grad_x, *[grads[n] for n in names], *[delta[n] for n in names],
            *[new_m[n] for n in names], *[new_v[n] for n in names])
```

```python
import functools
import math

import jax
import jax.numpy as jnp
import numpy as np
from jax import lax
from jax.experimental import pallas as pl
from jax.experimental.pallas import tpu as pltpu

F32 = jnp.float32
BF16 = jnp.bfloat16
HI = lax.Precision.HIGHEST

N_DEV = 8
D_MODEL = 1024
DEPTH = 4
POOL_WINDOWS = (2, 4, 8, 16)
POOL_GROUP = 128
POOL_WIDTH = 512
SSM_D_INNER = 1024
SSM_HEAD_DIM = 64
SSM_HEADS = 16
SSM_GROUPS = 2
SSM_STATE = 128
SSM_CONV = 4
SSM_CHUNK = 128
SSM_CONV_DIM = 1536
EVEN_IN = 3088
EVEN_IN_PAD = 3200
FOX_HEADS = 8
FOX_WIDTH = 512
MLA_HEADS = 8
MLA_NOPE = 64
MLA_ROPE = 32
MLA_V = 64
MLA_Q_RANK = 512
MLA_KV_RANK = 256
MLA_DK_PAD = 128
ROPE_THETA = 10000.0
FOX_SCALE = 64 ** -0.5
MLA_SCALE = (MLA_NOPE + MLA_ROPE) ** -0.5
ODD_IN = 2344
ODD_IN_PAD = 2560
D_FF = 2816
ALPHA = (2 * DEPTH) ** 0.25
LN_EPS = 1e-5
RMS_EPS = 1e-6
ADAM_LR = 0.001
ADAM_B1 = 0.9
ADAM_B2 = 0.999
ADAM_EPS = 1e-08
ADAM_WD = 0.01
ADAM_STEP = 10
NEG = -1e30
VMEM_LIMIT = 48 * 1024 * 1024


def _cp(sem):
    return pltpu.CompilerParams(dimension_semantics=sem, vmem_limit_bytes=VMEM_LIMIT)


def _dot(a, b, ca, cb, prec=None):
    return lax.dot_general(a, b, (((ca,), (cb,)), ((), ())), preferred_element_type=F32, precision=prec)


def _sigmoid(x):
    return 1.0 / (1.0 + jnp.exp(-x))


def _softplus(x):
    return jnp.maximum(x, 0.0) + jnp.log(1.0 + jnp.exp(-jnp.abs(x)))


MESH = pl.DeviceIdType.MESH
HBM_SPEC = pl.BlockSpec(memory_space=pltpu.HBM)


def _all_gather(name, xs):
    r, c_ = xs.shape

    def body(x_ref, out_ref, send_sems, recv_sems, local_sem):
        x, y, c = lax.axis_index("x"), lax.axis_index("y"), lax.axis_index("c")
        me, sibling = (x, y, c), (x, y, 1 - c)
        chips = [(1 - x, y), (x, 1 - y), (1 - x, 1 - y)]

        def rows(px, py, pc):
            return out_ref.at[4 * px + 2 * py + pc]

        def copy(k, block, to, src=None):
            return pltpu.make_async_remote_copy(
                src_ref=rows(*block) if src is None else src,
                dst_ref=rows(*block),
                send_sem=send_sems.at[k],
                recv_sem=recv_sems.at[k],
                device_id=to,
                device_id_type=MESH,
            )

        mine = pltpu.make_async_copy(x_ref, rows(*me), local_sem)
        mine.start()
        first = [copy(0, me, sibling, src=x_ref)]
        first += [copy(1 + j, me, (*chip, c), src=x_ref) for j, chip in enumerate(chips)]
        for cp in first:
            cp.start()
        passed = [copy(4 + j, (*chip, c), sibling) for j, chip in enumerate(chips)]
        for j, chip in enumerate(chips):
            copy(1 + j, (*chip, c), me).wait_recv()
            passed[j].start()
        copy(0, sibling, me).wait_recv()
        for j, chip in enumerate(chips):
            copy(4 + j, (*chip, 1 - c), me).wait_recv()
        for cp in first + passed:
            cp.wait_send()
        mine.wait()

    return pl.pallas_call(
        body,
        name=name,
        out_shape=jax.ShapeDtypeStruct((N_DEV, r, c_), xs.dtype),
        in_specs=[HBM_SPEC],
        out_specs=HBM_SPEC,
        scratch_shapes=[pltpu.SemaphoreType.DMA((7,)), pltpu.SemaphoreType.DMA((7,)), pltpu.SemaphoreType.DMA(())],
    )(xs)


def _all_to_all(name, send):
    _, r, c_ = send.shape

    def body(s_ref, r_ref, send_sems, recv_sems, local_sem):
        x, y, c = lax.axis_index("x"), lax.axis_index("y"), lax.axis_index("c")
        me = 4 * x + 2 * y + c
        mine = pltpu.make_async_copy(s_ref.at[me], r_ref.at[me], local_sem)
        mine.start()
        copies = []
        for k in range(1, N_DEV):
            tx = 1 - x if k & 4 else x
            ty = 1 - y if k & 2 else y
            tc = 1 - c if k & 1 else c
            peer = 4 * tx + 2 * ty + tc
            cp = pltpu.make_async_remote_copy(
                src_ref=s_ref.at[peer],
                dst_ref=r_ref.at[me],
                send_sem=send_sems.at[k - 1],
                recv_sem=recv_sems.at[k - 1],
                device_id=(tx, ty, tc),
                device_id_type=MESH,
            )
            cp.start()
            landing = pltpu.make_async_remote_copy(
                src_ref=s_ref.at[me],
                dst_ref=r_ref.at[peer],
                send_sem=send_sems.at[k - 1],
                recv_sem=recv_sems.at[k - 1],
                device_id=(tx, ty, tc),
                device_id_type=MESH,
            )
            copies.append((cp, landing))
        for cp, landing in copies:
            landing.wait_recv()
        for cp, landing in copies:
            cp.wait_send()
        mine.wait()

    return pl.pallas_call(
        body,
        name=name,
        out_shape=jax.ShapeDtypeStruct(send.shape, send.dtype),
        in_specs=[HBM_SPEC],
        out_specs=HBM_SPEC,
        scratch_shapes=[pltpu.SemaphoreType.DMA((7,)), pltpu.SemaphoreType.DMA((7,)), pltpu.SemaphoreType.DMA(())],
    )(send)


def _pick(n, cands):
    for t in cands:
        if n % t == 0:
            return t
    return n


def _mm(name, a, b, mode, out_dtype):
    if mode == "nn":
        (m, k), n = a.shape, b.shape[1]
    elif mode == "nt":
        (m, k), n = a.shape, b.shape[0]
    else:
        (k, m), n = a.shape, b.shape[1]
    tm = _pick(m, (512, 256, 128))
    tn = _pick(n, (1408, 1280, 1024, 768, 640, 512, 384, 256, 128))
    tk = _pick(k, (1024, 1408, 768, 640, 512, 256, 128))
    nk = k // tk
    swap = nk == 1 and a.size * a.dtype.itemsize * (n // tn) + b.size * b.dtype.itemsize < (
        a.size * a.dtype.itemsize + b.size * b.dtype.itemsize * (m // tm))
    ij = (lambda g0, g1: (g1, g0)) if swap else (lambda g0, g1: (g0, g1))
    if mode == "nn":
        a_spec = pl.BlockSpec((tm, tk), lambda g0, g1, kk: (ij(g0, g1)[0], kk))
        b_spec = pl.BlockSpec((tk, tn), lambda g0, g1, kk: (kk, ij(g0, g1)[1]))
        ca, cb = 1, 0
    elif mode == "nt":
        a_spec = pl.BlockSpec((tm, tk), lambda g0, g1, kk: (ij(g0, g1)[0], kk))
        b_spec = pl.BlockSpec((tn, tk), lambda g0, g1, kk: (ij(g0, g1)[1], kk))
        ca, cb = 1, 1
    else:
        a_spec = pl.BlockSpec((tk, tm), lambda g0, g1, kk: (kk, ij(g0, g1)[0]))
        b_spec = pl.BlockSpec((tk, tn), lambda g0, g1, kk: (kk, ij(g0, g1)[1]))
        ca, cb = 0, 0

    def kern(a_ref, b_ref, o_ref, acc):
        kk = pl.program_id(2)

        @pl.when(kk == 0)
        def _():
            acc[...] = jnp.zeros_like(acc)

        acc[...] += _dot(a_ref[...].astype(BF16), b_ref[...].astype(BF16), ca, cb)

        @pl.when(kk == nk - 1)
        def _():
            o_ref[...] = acc[...].astype(out_dtype)

    return pl.pallas_call(
        kern,
        name=name,
        grid=(n // tn, m // tm, nk) if swap else (m // tm, n // tn, nk),
        in_specs=[a_spec, b_spec],
        out_specs=pl.BlockSpec((tm, tn), lambda g0, g1, kk: ij(g0, g1)),
        out_shape=jax.ShapeDtypeStruct((m, n), out_dtype),
        scratch_shapes=[pltpu.VMEM((tm, tn), F32)],
        compiler_params=_cp(("parallel", "parallel", "arbitrary")),
    )(a, b)


def _rowwise(name, body, ins, outs, tm):
    n_rows = next(a.shape[0] for a, kind in ins if kind == "row")
    n = n_rows // tm
    in_specs = []
    for a, kind in ins:
        if kind == "full":
            in_specs.append(pl.BlockSpec(a.shape, lambda i, nd=a.ndim: (0,) * nd))
        elif kind == "row":
            in_specs.append(pl.BlockSpec((tm, a.shape[1]), lambda i: (i, 0)))
        elif kind == "prev":
            in_specs.append(pl.BlockSpec((tm, a.shape[1]), lambda i: (jnp.maximum(i - 1, 0), 0)))
        else:
            in_specs.append(pl.BlockSpec((tm, a.shape[1]), lambda i: (jnp.minimum(i + 1, n - 1), 0)))
    out_specs, out_shape = [], []
    for shp, dt, kind in outs:
        out_shape.append(jax.ShapeDtypeStruct(shp, dt))
        if kind == "row":
            out_specs.append(pl.BlockSpec((tm, shp[1]), lambda i: (i, 0)))
        else:
            out_specs.append(pl.BlockSpec(shp, lambda i, nd=len(shp): (0,) * nd))
    n_in = len(ins)

    def kern(*refs):
        i = pl.program_id(0)
        res = body(i, n, *[r if kind == "full" else r[...] for r, (_, kind) in zip(refs[:n_in], ins)])
        for (shp, dt, kind), val, o in zip(outs, res, refs[n_in:]):
            if kind == "row":
                o[...] = val.astype(dt)
            else:

                @pl.when(i == 0)
                def _(o=o):
                    o[...] = jnp.zeros_like(o)

                o[...] += val.astype(dt)

    return pl.pallas_call(
        kern,
        name=name,
        grid=(n,),
        in_specs=in_specs,
        out_specs=out_specs,
        out_shape=out_shape,
        compiler_params=_cp(("arbitrary",)),
    )(*[a for a, _ in ins])


def _tm(s, t):
    return min(s, t)


def _ln_fwd(x, h, g, b):
    s = x.shape[0]

    def body(i, n, xv, hv, gv, bv):
        r = ALPHA * xv + hv.astype(F32)
        mu = jnp.mean(r, axis=-1, keepdims=True)
        d = r - mu
        var = jnp.mean(d * d, axis=-1, keepdims=True)
        y = d * lax.rsqrt(var + LN_EPS) * gv[...] + bv[...]
        return y, y, r

    shp = (s, D_MODEL)
    return _rowwise("ln_fwd", body, [(x, "row"), (h, "row"), (g, "full"), (b, "full")],
                    [(shp, F32, "row"), (shp, BF16, "row"), (shp, F32, "row")], _tm(s, 256))


def _ln_bwd(a, bterm, r, g):
    s = r.shape[0]

    def body(i, n, *vals):
        if a is None:
            dyv, rv, gv = vals
        else:
            av, dyv, rv, gv = vals
            dyv = ALPHA * av + dyv
        mu = jnp.mean(rv, axis=-1, keepdims=True)
        d = rv - mu
        var = jnp.mean(d * d, axis=-1, keepdims=True)
        rstd = lax.rsqrt(var + LN_EPS)
        xhat = d * rstd
        dxh = dyv * gv[...]
        dr = rstd * (dxh - jnp.mean(dxh, axis=-1, keepdims=True) - xhat * jnp.mean(dxh * xhat, axis=-1, keepdims=True))
        return dr, dr, jnp.sum(dyv * xhat, axis=0, keepdims=True), jnp.sum(dyv, axis=0, keepdims=True)

    ins = ([] if a is None else [(a, "row")]) + [(bterm, "row"), (r, "row"), (g, "full")]
    shp = (s, D_MODEL)
    return _rowwise("ln_bwd" if a is None else "ln_bwd_res", body, ins,
                    [(shp, F32, "row"), (shp, BF16, "row"), ((1, D_MODEL), F32, "acc"), ((1, D_MODEL), F32, "acc")],
                    _tm(s, 256))


def _axpy(a, b):
    def body(i, n, av, bv):
        return (ALPHA * av + bv,)

    return _rowwise("axpy", body, [(a, "row"), (b, "row")], [(a.shape, F32, "row")], _tm(a.shape[0], 256))[0]


def _loss_head(y, target):
    s = y.shape[0]

    def body(i, n, yv, tv):
        err = yv - tv
        part = 0.5 * jnp.sum(jnp.mean(err * err, axis=-1, keepdims=True), axis=0, keepdims=True)
        return err * (1.0 / D_MODEL), part

    return _rowwise("loss_head", body, [(y, "row"), (target, "row")],
                    [((s, D_MODEL), F32, "row"), ((1, 1), F32, "acc")], _tm(s, 256))


def _swiglu_fwd(gu):
    s = gu.shape[0]

    def body(i, n, v):
        g, u = v[:, :D_FF], v[:, D_FF:]
        return (g * _sigmoid(g) * u,)

    return _rowwise("swiglu_fwd", body, [(gu, "row")], [((s, D_FF), BF16, "row")], _tm(s, 256))[0]


def _swiglu_bwd(da, gu):
    s = gu.shape[0]

    def body(i, n, dav, v):
        g, u = v[:, :D_FF], v[:, D_FF:]
        sg = _sigmoid(g)
        dg = dav * u * (sg * (1.0 + g * (1.0 - sg)))
        du = dav * (g * sg)
        return (jnp.concatenate([dg, du], axis=1),)

    return _rowwise("swiglu_bwd", body, [(da, "row"), (gu, "row")], [((s, 2 * D_FF), BF16, "row")], _tm(s, 256))[0]


def _rms_fwd(x, w):
    s, c = x.shape

    def body(i, n, xv, wv):
        rs = lax.rsqrt(jnp.mean(xv * xv, axis=-1, keepdims=True) + RMS_EPS)
        return (xv * rs * wv[...],)

    return _rowwise("rms_fwd", body, [(x, "row"), (w, "full")], [((s, c), BF16, "row")], _tm(s, 512))[0]


def _rms_bwd(dy, x, w):
    s, c = x.shape

    def body(i, n, dyv, xv, wv):
        rs = lax.rsqrt(jnp.mean(xv * xv, axis=-1, keepdims=True) + RMS_EPS)
        nv = xv * rs
        dn = dyv * wv[...]
        dx = rs * (dn - nv * jnp.mean(dn * nv, axis=-1, keepdims=True))
        return dx, jnp.sum(dyv * nv, axis=0, keepdims=True)

    return _rowwise("rms_bwd", body, [(dy, "row"), (x, "row"), (w, "full")],
                    [((s, c), F32, "row"), ((1, c), F32, "acc")], _tm(s, 512))


def _gated_rms_fwd(y, z, w):
    s, c = y.shape

    def body(i, n, yv, zv, wv):
        y2 = yv * (zv * _sigmoid(zv))
        rs = lax.rsqrt(jnp.mean(y2 * y2, axis=-1, keepdims=True) + RMS_EPS)
        return (y2 * rs * wv[...],)

    return _rowwise("gated_rms_fwd", body, [(y, "row"), (z, "row"), (w, "full")], [((s, c), BF16, "row")], _tm(s, 256))[0]


def _gated_rms_bwd(do, y, z, w):
    s, c = y.shape

    def body(i, n, dov, yv, zv, wv):
        sz = _sigmoid(zv)
        silu = zv * sz
        y2 = yv * silu
        rs = lax.rsqrt(jnp.mean(y2 * y2, axis=-1, keepdims=True) + RMS_EPS)
        nv = y2 * rs
        dn = dov * wv[...]
        dy2 = rs * (dn - nv * jnp.mean(dn * nv, axis=-1, keepdims=True))
        return dy2 * silu, dy2 * yv * (sz * (1.0 + zv * (1.0 - sz))), jnp.sum(dov * nv, axis=0, keepdims=True)

    return _rowwise("gated_rms_bwd", body, [(do, "row"), (y, "row"), (z, "row"), (w, "full")],
                    [((s, c), F32, "row"), ((s, c), F32, "row"), ((1, c), F32, "acc")], _tm(s, 256))


def _rope(name, r1, r2, cos, sin):
    def body(i, n, a, b, cv, sv):
        return a * cv - b * sv, b * cv + a * sv

    return _rowwise(name, body, [(r1, "row"), (r2, "row"), (cos, "row"), (sin, "row")],
                    [(r1.shape, F32, "row"), (r1.shape, F32, "row")], _tm(r1.shape[0], 512))


def _headsum_rope_bwd(d1_all, d2_all, cos, sin):
    s = d1_all.shape[0]
    half = MLA_ROPE // 2

    def body(i, n, a_all, b_all, cv, sv):
        rr = lax.broadcasted_iota(jnp.int32, (MLA_HEADS * half, half), 0)
        cc = lax.broadcasted_iota(jnp.int32, (MLA_HEADS * half, half), 1)
        sel = jnp.where(rr % half == cc, 1.0, 0.0).astype(F32)
        a = _dot(a_all, sel, 1, 0, HI)
        b = _dot(b_all, sel, 1, 0, HI)
        return a * cv + b * sv, b * cv - a * sv

    return _rowwise("headsum_rope_bwd", body, [(d1_all, "row"), (d2_all, "row"), (cos, "row"), (sin, "row")],
                    [((s, half), F32, "row"), ((s, half), F32, "row")], _tm(s, 512))


def _softplus_fwd(dtr, bias):
    def body(i, n, v, bv):
        return (_softplus(v + bv[...]),)

    return _rowwise("softplus_fwd", body, [(dtr, "row"), (bias, "full")], [(dtr.shape, F32, "row")], _tm(dtr.shape[0], 1024))[0]


def _softplus_bwd(ddt, dtr, bias):
    def body(i, n, dv, v, bv):
        d = dv * _sigmoid(v + bv[...])
        return d, jnp.sum(d, axis=0, keepdims=True)

    return _rowwise("softplus_bwd", body, [(ddt, "row"), (dtr, "row"), (bias, "full")],
                    [(dtr.shape, F32, "row"), ((1, dtr.shape[1]), F32, "acc")], _tm(dtr.shape[0], 1024))


def _rowdot(a, b):
    def body(i, n, av, bv):
        return (jnp.sum(av.astype(F32) * bv.astype(F32), axis=-1, keepdims=True),)

    return _rowwise("rowdot", body, [(a, "row"), (b, "row")], [((a.shape[0], 1), F32, "row")], _tm(a.shape[0], 2048))[0]


def _sum8(name, blocks):
    _, r, c = blocks.shape
    tr = _pick(r, (512, 256, 128, 64, 32, 16, 8))

    def kern(b_ref, o_ref):
        acc = b_ref[0].astype(F32)
        for d in range(1, N_DEV):
            acc = acc + b_ref[d].astype(F32)
        o_ref[...] = acc

    return pl.pallas_call(
        kern,
        name=name,
        grid=(r // tr,),
        in_specs=[pl.BlockSpec((N_DEV, tr, c), lambda i: (0, i, 0))],
        out_specs=pl.BlockSpec((tr, c), lambda i: (i, 0)),
        out_shape=jax.ShapeDtypeStruct((r, c), F32),
        compiler_params=_cp(("parallel",)),
    )(blocks)


def _adamw(name, w, g, m, v):
    r = w.shape[0]
    tm = _pick(r, (512, 256, 128, 64, 32, 16, 8))

    def body(i, n, wv, gv, mv, vv):
        m2 = ADAM_B1 * mv + (1.0 - ADAM_B1) * gv
        v2 = ADAM_B2 * vv + (1.0 - ADAM_B2) * (gv * gv)
        m_hat = m2 / (1.0 - ADAM_B1 ** ADAM_STEP)
        v_hat = v2 / (1.0 - ADAM_B2 ** ADAM_STEP)
        delta = -ADAM_LR * (m_hat / (jnp.sqrt(v_hat) + ADAM_EPS) + ADAM_WD * wv)
        return delta, m2, v2

    return _rowwise(name, body, [(w, "row"), (g, "row"), (m, "row"), (v, "row")],
                    [(w.shape, F32, "row")] * 3, tm)


def _pool_diff(i, t_rows, u, up):
    ucat = jnp.concatenate([jnp.where(i > 0, up, 0.0), u], axis=0)
    r = lax.broadcasted_iota(jnp.int32, (t_rows, 2 * t_rows), 0)
    cc = lax.broadcasted_iota(jnp.int32, (t_rows, 2 * t_rows), 1)
    lag = r + t_rows - cc
    t_col = i * t_rows + lax.broadcasted_iota(jnp.int32, (t_rows, 1), 0)
    diffs = []
    for gi, wdw in enumerate(POOL_WINDOWS):
        win = jnp.where((lag >= 0) & (lag < wdw), 1.0, 0.0).astype(F32)
        cnt = jnp.minimum(t_col + 1, wdw).astype(F32)
        ws = _dot(win, ucat[:, gi * POOL_GROUP:(gi + 1) * POOL_GROUP], 1, 0, HI)
        diffs.append(ws / cnt - u[:, gi * POOL_GROUP:(gi + 1) * POOL_GROUP])
    return diffs


def _pool_fwd(u, pw, sc):
    s = u.shape[0]
    tm = _tm(s, 256)

    def body(i, n, uv, upv, wv, scv):
        diffs = _pool_diff(i, tm, uv, upv)
        ys = [_dot(d.astype(BF16), wv[gi].astype(BF16), 1, 0) for gi, d in enumerate(diffs)]
        return (jnp.concatenate(ys, axis=1) * scv[...],)

    return _rowwise("pool_fwd", body, [(u, "row"), (u, "prev"), (pw, "full"), (sc, "full")],
                    [((s, POOL_WIDTH), BF16, "row")], tm)[0]


def _pool_bwd1(u, dy, pw, sc):
    s = u.shape[0]
    tm = _tm(s, 256)

    def body(i, n, uv, upv, dyv, wv, scv):
        diffs = _pool_diff(i, tm, uv, upv)
        scv = scv[...]
        dsc, dws, dds = [], [], []
        for gi, d in enumerate(diffs):
            sl = slice(gi * POOL_GROUP, (gi + 1) * POOL_GROUP)
            db, wb = d.astype(BF16), wv[gi].astype(BF16)
            yg = _dot(db, wb, 1, 0)
            dsc.append(jnp.sum(yg * dyv[:, sl], axis=0, keepdims=True))
            eb = (dyv[:, sl] * scv[:, sl]).astype(BF16)
            dws.append(_dot(db, eb, 0, 0)[None])
            dds.append(_dot(eb, wb, 1, 1))
        return jnp.concatenate(dds, axis=1), jnp.concatenate(dsc, axis=1), jnp.concatenate(dws, axis=0)

    return _rowwise("pool_bwd1", body, [(u, "row"), (u, "prev"), (dy, "row"), (pw, "full"), (sc, "full")],
                    [((s, POOL_WIDTH), F32, "row"), ((1, POOL_WIDTH), F32, "acc"), (pw.shape, F32, "acc")], tm)


def _pool_bwd2(dd):
    s = dd.shape[0]
    tm = _tm(s, 256)

    def body(i, n, dv, dnv):
        dcat = jnp.concatenate([dv, jnp.where(i < n - 1, dnv, 0.0)], axis=0)
        r = lax.broadcasted_iota(jnp.int32, (tm, 2 * tm), 0)
        cc = lax.broadcasted_iota(jnp.int32, (tm, 2 * tm), 1)
        lag = cc - r
        t_col = i * tm + lax.broadcasted_iota(jnp.int32, (2 * tm, 1), 0)
        outs = []
        for gi, wdw in enumerate(POOL_WINDOWS):
            sl = slice(gi * POOL_GROUP, (gi + 1) * POOL_GROUP)
            win = jnp.where((lag >= 0) & (lag < wdw), 1.0, 0.0).astype(F32)
            cnt = jnp.minimum(t_col + 1, wdw).astype(F32)
            outs.append(_dot(win, dcat[:, sl] / cnt, 1, 0, HI) - dv[:, sl])
        return (jnp.concatenate(outs, axis=1),)

    return _rowwise("pool_bwd2", body, [(dd, "row"), (dd, "next")], [((s, POOL_WIDTH), BF16, "row")], tm)[0]


def _shift_down(cur, other, j, tm):
    if j == 0:
        return cur
    rows = lax.broadcasted_iota(jnp.int32, cur.shape, 0)
    return jnp.where(rows < j, pltpu.roll(other, j, 0), pltpu.roll(cur, j, 0))


def _shift_up(cur, other, j, tm):
    if j == 0:
        return cur
    rows = lax.broadcasted_iota(jnp.int32, cur.shape, 0)
    return jnp.where(rows >= tm - j, pltpu.roll(other, tm - j, 0), pltpu.roll(cur, tm - j, 0))


def _conv_pre(i, tm, xv, xpv, wv, bv):
    xpv = jnp.where(i > 0, xpv, 0.0)
    taps = [_shift_down(xv, xpv, SSM_CONV - 1 - k, tm) for k in range(SSM_CONV)]
    pre = bv[...]
    for k in range(SSM_CONV):
        pre = pre + wv[k:k + 1, :] * taps[k]
    return pre, taps


def _conv_fwd(xbc, w, b):
    s = xbc.shape[0]
    tm = _tm(s, 256)

    def body(i, n, xv, xpv, wv, bv):
        pre, _ = _conv_pre(i, tm, xv, xpv, wv, bv)
        return (pre * _sigmoid(pre),)

    return _rowwise("conv_fwd", body, [(xbc, "row"), (xbc, "prev"), (w, "full"), (b, "full")],
                    [(xbc.shape, F32, "row")], tm)[0]


def _conv_bwd1(dxc, xbc, w, b):
    s, c = xbc.shape
    tm = _tm(s, 256)

    def body(i, n, dv, xv, xpv, wv, bv):
        pre, taps = _conv_pre(i, tm, xv, xpv, wv, bv)
        sg = _sigmoid(pre)
        dpre = dv * (sg * (1.0 + pre * (1.0 - sg)))
        tap_row = lax.broadcasted_iota(jnp.int32, (SSM_CONV, c), 0)
        dw = jnp.zeros((SSM_CONV, c), F32)
        for k in range(SSM_CONV):
            dw = dw + jnp.where(tap_row == k, jnp.sum(dpre * taps[k], axis=0, keepdims=True), 0.0)
        return dpre, dw, jnp.sum(dpre, axis=0, keepdims=True)

    return _rowwise("conv_bwd1", body, [(dxc, "row"), (xbc, "row"), (xbc, "prev"), (w, "full"), (b, "full")],
                    [((s, c), F32, "row"), ((SSM_CONV, c), F32, "acc"), ((1, c), F32, "acc")], tm)


def _conv_bwd2(dpre, w):
    s, c = dpre.shape
    tm = _tm(s, 256)

    def body(i, n, dv, dnv, wv):
        dnv = jnp.where(i < n - 1, dnv, 0.0)
        out = jnp.zeros_like(dv)
        for k in range(SSM_CONV):
            out = out + wv[k:k + 1, :] * _shift_up(dv, dnv, SSM_CONV - 1 - k, tm)
        return (out,)

    return _rowwise("conv_bwd2", body, [(dpre, "row"), (dpre, "next"), (w, "full")], [((s, c), BF16, "row")], tm)[0]


def _ssd_common(alog, dt_c, dt_r, tril, tri):
    a = -jnp.exp(alog)
    acs_c = _dot(tril, dt_c * a, 1, 0, HI)
    acs_r = _dot(dt_r * a, tril, 1, 1, HI)
    a_last = jnp.sum(dt_c * a, axis=0, keepdims=True)
    return a, acs_c, acs_r, a_last


def _ssd_fwd(xh, xht, dtc, dtr, bm, cm, alog, dsk):
    nh, s, p = xh.shape
    L = SSM_CHUNK
    nc = s // L
    E = nh // SSM_GROUPS

    def kern(xh_ref, xht_ref, dtc_ref, dtr_ref, b_ref, c_ref, alog_ref, dsk_ref, y_ref, hp_ref, h_s):
        c = pl.program_id(1)

        @pl.when(c == 0)
        def _():
            h_s[...] = jnp.zeros_like(h_s)

        bb = b_ref[...].astype(BF16)
        cb_ = c_ref[...].astype(BF16)
        cbm = _dot(cb_, bb, 1, 1)
        ri = lax.broadcasted_iota(jnp.int32, (L, L), 0)
        ci = lax.broadcasted_iota(jnp.int32, (L, L), 1)
        tri = ri >= ci
        tril = tri.astype(F32)
        for e in range(E):
            dt_c, dt_r = dtc_ref[e], dtr_ref[e]
            a, acs_c, acs_r, a_last = _ssd_common(alog_ref[e], dt_c, dt_r, tril, tri)
            lam = jnp.exp(jnp.where(tri, acs_c - acs_r, NEG))
            x = xh_ref[e]
            xdt = (x * dt_c).astype(BF16)
            hh = h_s[e]
            y = _dot((cbm * lam).astype(BF16), xdt, 1, 0)
            y = y + _dot(cb_, hh.astype(BF16), 1, 1) * jnp.exp(acs_c) + x * dsk_ref[e]
            y_ref[e] = y
            hp_ref[e] = hh
            xw = (xht_ref[e] * (dt_r * jnp.exp(a_last - acs_r))).astype(BF16)
            h_s[e] = hh * jnp.exp(a_last) + _dot(xw, bb, 1, 0)

    return pl.pallas_call(
        kern,
        name="ssd_fwd",
        grid=(SSM_GROUPS, nc),
        in_specs=[
            pl.BlockSpec((E, L, p), lambda g, c: (g, c, 0)),
            pl.BlockSpec((E, p, L), lambda g, c: (g, 0, c)),
            pl.BlockSpec((E, L, 1), lambda g, c: (g, c, 0)),
            pl.BlockSpec((E, 1, L), lambda g, c: (g, 0, c)),
            pl.BlockSpec((L, SSM_STATE), lambda g, c: (c, g)),
            pl.BlockSpec((L, SSM_STATE), lambda g, c: (c, g)),
            pl.BlockSpec((E, 1, 1), lambda g, c: (g, 0, 0)),
            pl.BlockSpec((E, 1, 1), lambda g, c: (g, 0, 0)),
        ],
        out_specs=[
            pl.BlockSpec((E, L, p), lambda g, c: (g, c, 0)),
            pl.BlockSpec((E, None, p, SSM_STATE), lambda g, c: (g, c, 0, 0)),
        ],
        out_shape=[jax.ShapeDtypeStruct((nh, s, p), F32), jax.ShapeDtypeStruct((nh, nc, p, SSM_STATE), F32)],
        scratch_shapes=[pltpu.VMEM((E, p, SSM_STATE), F32)],
        compiler_params=_cp(("arbitrary", "arbitrary")),
    )(xh, xht, dtc, dtr, bm, cm, alog, dsk)


def _ssd_bwd(dy, dyt, xh, dtc, dtr, bm, cm, hprev, alog, dsk):
    nh, s, p = xh.shape
    L = SSM_CHUNK
    nc = s // L
    E = nh // SSM_GROUPS

    def kern(dy_ref, dyt_ref, xh_ref, dtc_ref, dtr_ref, b_ref, c_ref, hp_ref, alog_ref, dsk_ref,
             dx_ref, ddt_ref, db_ref, dc_ref, dalog_ref, dd_ref, dh_s):
        c = pl.program_id(1)

        @pl.when(c == 0)
        def _():
            dh_s[...] = jnp.zeros_like(dh_s)
            dalog_ref[...] = jnp.zeros_like(dalog_ref)
            dd_ref[...] = jnp.zeros_like(dd_ref)

        bb = b_ref[...].astype(BF16)
        cb_ = c_ref[...].astype(BF16)
        cbm = _dot(cb_, bb, 1, 1)
        cbt = _dot(bb, cb_, 1, 1)
        ri = lax.broadcasted_iota(jnp.int32, (L, L), 0)
        ci = lax.broadcasted_iota(jnp.int32, (L, L), 1)
        tri = ri >= ci
        trit = ci >= ri
        tril = tri.astype(F32)
        triu = trit.astype(F32)
        db_acc = jnp.zeros((L, SSM_STATE), F32)
        dc_acc = jnp.zeros((L, SSM_STATE), F32)
        for e in range(E):
            dt_c, dt_r = dtc_ref[e], dtr_ref[e]
            a, acs_c, acs_r, a_last = _ssd_common(alog_ref[e], dt_c, dt_r, tril, tri)
            lam = jnp.exp(jnp.where(tri, acs_c - acs_r, NEG))
            lamt = jnp.exp(jnp.where(trit, acs_r - acs_c, NEG))
            x = xh_ref[e]
            xdt = x * dt_c
            xdtb = xdt.astype(BF16)
            dyv = dy_ref[e]
            dyb = dyv.astype(BF16)
            hh = hp_ref[e]
            hb = hh.astype(BF16)
            dhn = dh_s[e]
            dhnb = dhn.astype(BF16)
            ea_c = jnp.exp(acs_c)
            decay_c = jnp.exp(a_last - acs_c)
            e_last = jnp.exp(a_last)
            gm = _dot(dyb, xdtb, 1, 1)
            gt = _dot(xdtb, dyb, 1, 1)
            bdh = _dot(bb, dhnb, 1, 1)
            dxdt = _dot((cbt * lamt).astype(BF16), dyb, 1, 0) + bdh * decay_c
            dcb = gm * lam
            dcbt = gt * lamt
            yoff = _dot(cb_, hb, 1, 1) * ea_c
            dc_acc = dc_acc + _dot(dcb.astype(BF16), bb, 1, 0) + _dot(dyb, hb, 1, 0) * ea_c
            db_acc = db_acc + _dot(dcbt.astype(BF16), cb_, 1, 0) + _dot(xdtb, dhnb, 1, 0) * decay_c
            tt = decay_c * jnp.sum(xdt * bdh, axis=1, keepdims=True)
            dacs = (jnp.sum(dcb * cbm, axis=1, keepdims=True) - jnp.sum(dcbt * cbt, axis=1, keepdims=True)
                    + jnp.sum(dyv * yoff, axis=1, keepdims=True) - tt)
            tail = jnp.sum(tt, axis=0, keepdims=True) + e_last * jnp.sum(jnp.sum(dhn * hh, axis=1, keepdims=True), axis=0, keepdims=True)
            dda = _dot(triu, dacs, 1, 0, HI) + tail
            ddt_ref[e] = dda * a + jnp.sum(dxdt * x, axis=1, keepdims=True)
            dalog_ref[e] += jnp.sum(dda * dt_c, axis=0, keepdims=True) * a
            dd_ref[e] += jnp.sum(jnp.sum(dyv * x, axis=1, keepdims=True), axis=0, keepdims=True)
            dx_ref[e] = dxdt * dt_c + dyv * dsk_ref[e]
            dyw = (dyt_ref[e] * jnp.exp(acs_r)).astype(BF16)
            dh_s[e] = dhn * e_last + _dot(dyw, cb_, 1, 0)
        db_ref[...] = db_acc
        dc_ref[...] = dc_acc

    rc = lambda c: nc - 1 - c
    return pl.pallas_call(
        kern,
        name="ssd_bwd",
        grid=(SSM_GROUPS, nc),
        in_specs=[
            pl.BlockSpec((E, L, p), lambda g, c: (g, rc(c), 0)),
            pl.BlockSpec((E, p, L), lambda g, c: (g, 0, rc(c))),
            pl.BlockSpec((E, L, p), lambda g, c: (g, rc(c), 0)),
            pl.BlockSpec((E, L, 1), lambda g, c: (g, rc(c), 0)),
            pl.BlockSpec((E, 1, L), lambda g, c: (g, 0, rc(c))),
            pl.BlockSpec((L, SSM_STATE), lambda g, c: (rc(c), g)),
            pl.BlockSpec((L, SSM_STATE), lambda g, c: (rc(c), g)),
            pl.BlockSpec((E, None, p, SSM_STATE), lambda g, c: (g, rc(c), 0, 0)),
            pl.BlockSpec((E, 1, 1), lambda g, c: (g, 0, 0)),
            pl.BlockSpec((E, 1, 1), lambda g, c: (g, 0, 0)),
        ],
        out_specs=[
            pl.BlockSpec((E, L, p), lambda g, c: (g, rc(c), 0)),
            pl.BlockSpec((E, L, 1), lambda g, c: (g, rc(c), 0)),
            pl.BlockSpec((L, SSM_STATE), lambda g, c: (rc(c), g)),
            pl.BlockSpec((L, SSM_STATE), lambda g, c: (rc(c), g)),
            pl.BlockSpec((E, 1, 1), lambda g, c: (g, 0, 0)),
            pl.BlockSpec((E, 1, 1), lambda g, c: (g, 0, 0)),
        ],
        out_shape=[
            jax.ShapeDtypeStruct((nh, s, p), F32),
            jax.ShapeDtypeStruct((nh, s, 1), F32),
            jax.ShapeDtypeStruct((s, SSM_GROUPS * SSM_STATE), F32),
            jax.ShapeDtypeStruct((s, SSM_GROUPS * SSM_STATE), F32),
            jax.ShapeDtypeStruct((nh, 1, 1), F32),
            jax.ShapeDtypeStruct((nh, 1, 1), F32),
        ],
        scratch_shapes=[pltpu.VMEM((E, p, SSM_STATE), F32)],
        compiler_params=_cp(("arbitrary", "arbitrary")),
    )(dy, dyt, xh, dtc, dtr, bm, cm, hprev, alog, dsk)


def _fgate_fwd(fl, bias):
    h, s = fl.shape
    t = _tm(s, 512)

    def kern(fl_ref, b_ref, o_ref, carry):
        i = pl.program_id(0)

        @pl.when(i == 0)
        def _():
            carry[...] = jnp.zeros_like(carry)

        lf = -_softplus(-(fl_ref[...] + b_ref[...]))
        ri = lax.broadcasted_iota(jnp.int32, (t, t), 0)
        ci = lax.broadcasted_iota(jnp.int32, (t, t), 1)
        o_ref[...] = _dot(lf, (ri <= ci).astype(F32), 1, 0, HI) + carry[...]
        carry[...] += jnp.sum(lf, axis=1, keepdims=True)

    return pl.pallas_call(
        kern,
        name="fgate_fwd",
        grid=(s // t,),
        in_specs=[pl.BlockSpec((h, t), lambda i: (0, i)), pl.BlockSpec((h, 1), lambda i: (0, 0))],
        out_specs=pl.BlockSpec((h, t), lambda i: (0, i)),
        out_shape=jax.ShapeDtypeStruct((h, s), F32),
        scratch_shapes=[pltpu.VMEM((h, 1), F32)],
        compiler_params=_cp(("arbitrary",)),
    )(fl, bias)


def _fgate_bwd(dqe, dke, col, fl, bias):
    h, s = fl.shape
    w = dqe.shape[-1]
    t = _tm(s, 512)
    n = s // t

    def kern(dq_ref, dk_ref, fl_ref, b_ref, o_ref, db_ref, carry):
        i = pl.program_id(0)

        @pl.when(i == 0)
        def _():
            carry[...] = jnp.zeros_like(carry)
            db_ref[...] = jnp.zeros_like(db_ref)

        sel = (lax.broadcasted_iota(jnp.int32, (1, w), 1) == col).astype(F32)
        hid = lax.broadcasted_iota(jnp.int32, (h, t), 0)
        d = jnp.zeros((h, t), F32)
        for hh in range(h):
            r = _dot(sel, dq_ref[hh], 1, 1, HI) - _dot(sel, dk_ref[hh], 1, 1, HI)
            d = d + jnp.where(hid == hh, r, 0.0)
        ri = lax.broadcasted_iota(jnp.int32, (t, t), 0)
        ci = lax.broadcasted_iota(jnp.int32, (t, t), 1)
        rev = _dot(d, (ri >= ci).astype(F32), 1, 0, HI) + carry[...]
        carry[...] += jnp.sum(d, axis=1, keepdims=True)
        dz = rev * _sigmoid(-(fl_ref[...] + b_ref[...]))
        o_ref[...] = dz
        db_ref[...] += jnp.sum(dz, axis=1, keepdims=True)

    rev_blk = lambda i: (0, n - 1 - i)
    return pl.pallas_call(
        kern,
        name="fgate_bwd",
        grid=(n,),
        in_specs=[pl.BlockSpec((h, t, w), lambda i: (0, n - 1 - i, 0)), pl.BlockSpec((h, t, w), lambda i: (0, n - 1 - i, 0)),
                  pl.BlockSpec((h, t), rev_blk), pl.BlockSpec((h, 1), lambda i: (0, 0))],
        out_specs=[pl.BlockSpec((h, t), rev_blk), pl.BlockSpec((h, 1), lambda i: (0, 0))],
        out_shape=[jax.ShapeDtypeStruct((h, s), F32), jax.ShapeDtypeStruct((h, 1), F32)],
        scratch_shapes=[pltpu.VMEM((h, 1), F32)],
        compiler_params=_cp(("arbitrary",)),
    )(dqe, dke, fl, bias)


HPS_FWD = 4
HPS_BWD = 2


def _attn_scores(q, k, fq, fk, scale, masked, t):
    sc = _dot(q, k, 1, 1)
    if scale is not None:
        sc = sc * scale
    if fq is not None:
        sc = sc + fq - fk
    if masked:
        rows = lax.broadcasted_iota(jnp.int32, (t, t), 0)
        cols = lax.broadcasted_iota(jnp.int32, (t, t), 1)
        sc = jnp.where(cols <= rows, sc, NEG)
    return sc


def _flash_fwd(name, q, k, v, fq, fk, scale):
    nh, s, dk = q.shape
    dv = v.shape[-1]
    t = _tm(s, 512)
    nq = s // t
    bias = fq is not None
    HPS = HPS_FWD

    pairs = [(i, j) for i in range(nq) for j in range(i + 1)]
    qi_of = jnp.asarray(np.array([p[0] for p in pairs], np.int32))
    kj_of = jnp.asarray(np.array([p[1] for p in pairs], np.int32))

    def kern(qi_ref, kj_ref, *refs):
        if bias:
            q_ref, k_ref, v_ref, fq_ref, fk_ref, o_ref, lse_ref, m_s, l_s, acc_s = refs
        else:
            q_ref, k_ref, v_ref, o_ref, lse_ref, m_s, l_s, acc_s = refs
        qi, kj = qi_ref[pl.program_id(1)], kj_ref[pl.program_id(1)]

        @pl.when(kj == 0)
        def _():
            m_s[...] = jnp.full_like(m_s, NEG)
            l_s[...] = jnp.zeros_like(l_s)
            acc_s[...] = jnp.zeros_like(acc_s)

        def step(masked):
            scs = [_attn_scores(q_ref[hh], k_ref[hh], fq_ref[hh] if bias else None, fk_ref[hh] if bias else None,
                                scale, masked, t) for hh in range(HPS)]
            pbs, corrs = [], []
            for hh in range(HPS):
                m_old = m_s[hh]
                m_new = jnp.maximum(m_old, jnp.max(scs[hh], axis=1, keepdims=True))
                corr = jnp.exp(m_old - m_new)
                p = jnp.exp(scs[hh] - m_new)
                l_s[hh] = corr * l_s[hh] + jnp.sum(p, axis=1, keepdims=True)
                m_s[hh] = m_new
                pbs.append(p.astype(BF16))
                corrs.append(corr)
            for hh in range(HPS):
                acc_s[hh] = acc_s[hh] * corrs[hh] + _dot(pbs[hh], v_ref[hh], 1, 0)

        @pl.when(kj < qi)
        def _():
            step(False)

        @pl.when(kj == qi)
        def _():
            step(True)
            o_ref[...] = acc_s[...] / l_s[...]
            lse_ref[...] = m_s[...] + jnp.log(l_s[...])

    qspec = lambda d: pl.BlockSpec((HPS, t, d), lambda h, p, qi_r, kj_r: (h, qi_r[p], 0))
    kspec = lambda d: pl.BlockSpec((HPS, t, d), lambda h, p, qi_r, kj_r: (h, kj_r[p], 0))
    in_specs = [qspec(dk), kspec(dk), kspec(dv)]
    args = [q, k, v]
    if bias:
        in_specs += [qspec(1), pl.BlockSpec((HPS, 1, t), lambda h, p, qi_r, kj_r: (h, 0, kj_r[p]))]
        args += [fq, fk]
    return pl.pallas_call(
        kern,
        name=name,
        grid_spec=pltpu.PrefetchScalarGridSpec(
            num_scalar_prefetch=2,
            grid=(nh // HPS, len(pairs)),
            in_specs=in_specs,
            out_specs=[qspec(dv), qspec(1)],
            scratch_shapes=[pltpu.VMEM((HPS, t, 1), F32), pltpu.VMEM((HPS, t, 1), F32),
                            pltpu.VMEM((HPS, t, dv), F32)],
        ),
        out_shape=[jax.ShapeDtypeStruct((nh, s, dv), F32), jax.ShapeDtypeStruct((nh, s, 1), F32)],
        compiler_params=_cp(("parallel", "arbitrary")),
    )(qi_of, kj_of, *args)


def _flash_bwd(name, q, k, v, do, lse, delta, fq, fk, qg, kg, scale, dq_scale, dk_scale):
    nh, s, dk = q.shape
    dv = v.shape[-1]
    dg = qg.shape[-1]
    t = _tm(s, 512)
    nq = s // t
    bias = fq is not None
    ext = qg is not q
    HPS = HPS_BWD

    pairs = [(j, i) for j in range(nq) for i in range(j, nq)]
    kb_of = jnp.asarray(np.array([p[0] for p in pairs], np.int32))
    qi_of = jnp.asarray(np.array([p[1] for p in pairs], np.int32))

    def kern(kb_ref, qi_ref, *refs):
        refs = list(refs)
        q_ref, k_ref, v_ref, do_ref, lse_ref, dl_ref = refs[:6]
        del refs[:6]
        fq_ref, fk_ref = (refs.pop(0), refs.pop(0)) if bias else (None, None)
        qg_ref, kg_ref = (refs.pop(0), refs.pop(0)) if ext else (q_ref, k_ref)
        dq_ref, dk_ref, dv_ref, dk_s, dv_s = refs
        kb, qi = kb_ref[pl.program_id(1)], qi_ref[pl.program_id(1)]

        @pl.when(pl.program_id(1) == 0)
        def _():
            dq_ref[...] = jnp.zeros_like(dq_ref)

        @pl.when(qi == kb)
        def _():
            dk_s[...] = jnp.zeros_like(dk_s)
            dv_s[...] = jnp.zeros_like(dv_s)

        def step(masked):
            rows = pl.ds(pl.multiple_of(qi * t, t), t)
            heads = range(HPS)
            scs = [_attn_scores(q_ref[hh], k_ref[hh], fq_ref[hh] if bias else None, fk_ref[hh] if bias else None,
                                scale, masked, t) for hh in heads]
            dps = [_dot(do_ref[hh], v_ref[hh], 1, 1) for hh in heads]
            ps = [jnp.exp(scs[hh] - lse_ref[hh]) for hh in heads]
            pbs = [p.astype(BF16) for p in ps]
            for hh in heads:
                dv_s[hh] += _dot(pbs[hh], do_ref[hh], 0, 0)
            dsbs = [(ps[hh] * (dps[hh] - dl_ref[hh])).astype(BF16) for hh in heads]
            for hh in heads:
                dk_s[hh] += _dot(dsbs[hh], qg_ref[hh], 0, 0)
            for hh in heads:
                dq_ref[hh, rows, :] += _dot(dsbs[hh], kg_ref[hh], 1, 0) * dq_scale

        @pl.when(qi > kb)
        def _():
            step(False)

        @pl.when(qi == kb)
        def _():
            step(True)

        @pl.when(qi == nq - 1)
        def _():
            dk_ref[...] = dk_s[...] if dk_scale is None else dk_s[...] * dk_scale
            dv_ref[...] = dv_s[...]

    qspec = lambda d: pl.BlockSpec((HPS, t, d), lambda h, p, kb_r, qi_r: (h, qi_r[p], 0))
    kspec = lambda d: pl.BlockSpec((HPS, t, d), lambda h, p, kb_r, qi_r: (h, kb_r[p], 0))
    in_specs = [qspec(dk), kspec(dk), kspec(dv), qspec(dv), qspec(1), qspec(1)]
    args = [q, k, v, do, lse, delta]
    if bias:
        in_specs += [qspec(1), pl.BlockSpec((HPS, 1, t), lambda h, p, kb_r, qi_r: (h, 0, kb_r[p]))]
        args += [fq, fk]
    if ext:
        in_specs += [qspec(dg), kspec(dg)]
        args += [qg, kg]
    return pl.pallas_call(
        kern,
        name=name,
        grid_spec=pltpu.PrefetchScalarGridSpec(
            num_scalar_prefetch=2,
            grid=(nh // HPS, len(pairs)),
            in_specs=in_specs,
            out_specs=[pl.BlockSpec((HPS, s, dg), lambda h, p, kb_r, qi_r: (h, 0, 0)), kspec(dg), kspec(dv)],
            scratch_shapes=[pltpu.VMEM((HPS, t, dg), F32), pltpu.VMEM((HPS, t, dv), F32)],
        ),
        out_shape=[jax.ShapeDtypeStruct((nh, s, dg), F32), jax.ShapeDtypeStruct((nh, s, dg), F32),
                   jax.ShapeDtypeStruct((nh, s, dv), F32)],
        compiler_params=_cp(("arbitrary", "arbitrary")),
    )(kb_of, qi_of, *args)


def _heads(t, nh):
    s = t.shape[0]
    return t.reshape(s, nh, -1).transpose(1, 0, 2)


def _unheads(t):
    nh, s, d = t.shape
    return t.transpose(1, 0, 2).reshape(s, nh * d)


def _perm_uq(w):
    r = w.shape[0]
    w3 = w.reshape(r, MLA_HEADS, MLA_NOPE + MLA_ROPE)
    half = MLA_ROPE // 2
    return jnp.concatenate([w3[:, :, :MLA_NOPE].reshape(r, -1), w3[:, :, MLA_NOPE:MLA_NOPE + half].reshape(r, -1),
                            w3[:, :, MLA_NOPE + half:].reshape(r, -1)], axis=1)


def _unperm_uq(w):
    r = w.shape[0]
    half = MLA_ROPE // 2
    n0 = MLA_HEADS * MLA_NOPE
    n1 = n0 + MLA_HEADS * half
    return jnp.concatenate([w[:, :n0].reshape(r, MLA_HEADS, MLA_NOPE), w[:, n0:n1].reshape(r, MLA_HEADS, half),
                            w[:, n1:].reshape(r, MLA_HEADS, half)], axis=2).reshape(r, -1)


def _perm_ukv(w):
    r = w.shape[0]
    w3 = w.reshape(r, MLA_HEADS, MLA_NOPE + MLA_V)
    return jnp.concatenate([w3[:, :, :MLA_NOPE].reshape(r, -1), w3[:, :, MLA_NOPE:].reshape(r, -1)], axis=1)


def _unperm_ukv(w):
    r = w.shape[0]
    n0 = MLA_HEADS * MLA_NOPE
    return jnp.concatenate([w[:, :n0].reshape(r, MLA_HEADS, MLA_NOPE), w[:, n0:].reshape(r, MLA_HEADS, MLA_V)],
                           axis=2).reshape(r, -1)


_ODD_CUTS = np.cumsum([0, FOX_WIDTH, FOX_WIDTH, FOX_WIDTH, FOX_HEADS, MLA_Q_RANK, MLA_KV_RANK, MLA_ROPE]).tolist()
_ODD_ORDER = (0, 1, 2, 4, 5, 6, 3)


def _perm_odd_in(w):
    parts = [w[:, _ODD_CUTS[j]:_ODD_CUTS[j + 1]] for j in _ODD_ORDER]
    parts.append(jnp.zeros((w.shape[0], ODD_IN_PAD - ODD_IN), w.dtype))
    return jnp.concatenate(parts, axis=1)


def _unperm_odd_in(w):
    widths = [_ODD_CUTS[j + 1] - _ODD_CUTS[j] for j in _ODD_ORDER]
    offs = np.cumsum([0] + widths).tolist()
    pieces = {j: w[:, offs[n]:offs[n + 1]] for n, j in enumerate(_ODD_ORDER)}
    return jnp.concatenate([pieces[j] for j in range(7)], axis=1)


def _pad_cols(w, n):
    return jnp.concatenate([w, jnp.zeros((w.shape[0], n - w.shape[1]), w.dtype)], axis=1)


_BIG = (("even_w_in", 2), ("even_w_out", 1), ("odd_w_in", 2), ("w_uq", 2), ("w_ukv", 2), ("odd_w_out", 1),
        ("ffn_w_gate", 2), ("ffn_w_up", 2), ("ffn_w_down", 1))
_PACK_COLS = 1024


def _unshard(blocks, shp, ax):
    t = jnp.moveaxis(blocks.reshape((N_DEV,) + tuple(shp)), 0, ax)
    full = list(shp)
    full[ax] = shp[ax] * N_DEV
    return t.reshape(full)


def _reshard(full, ax, cols):
    shp = list(full.shape)
    t = full.reshape(shp[:ax] + [N_DEV, shp[ax] // N_DEV] + shp[ax + 1:])
    return jnp.moveaxis(t, ax, 0).reshape(N_DEV, -1, cols)


def kernel(x, even_w_in, pool_w, pool_scale, conv_w, conv_b, dt_bias, a_log, d_skip, ssm_norm_w, even_w_out, odd_w_in, fgate_b, q_norm_w, w_uq, kv_norm_w, w_ukv, odd_w_out, ffn_w_gate, ffn_w_up, ffn_w_down, ln_mix_g, ln_mix_b, ln_ffn_g, ln_ffn_b, loss_target, m_even_w_in, m_pool_w, m_pool_scale, m_conv_w, m_conv_b, m_dt_bias, m_a_log, m_d_skip, m_ssm_norm_w, m_even_w_out, m_odd_w_in, m_fgate_b, m_q_norm_w, m_w_uq, m_kv_norm_w, m_w_ukv, m_odd_w_out, m_ffn_w_gate, m_ffn_w_up, m_ffn_w_down, m_ln_mix_g, m_ln_mix_b, m_ln_ffn_g, m_ln_ffn_b, v_even_w_in, v_pool_w, v_pool_scale, v_conv_w, v_conv_b, v_dt_bias, v_a_log, v_d_skip, v_ssm_norm_w, v_even_w_out, v_odd_w_in, v_fgate_b, v_q_norm_w, v_w_uq, v_kv_norm_w, v_w_ukv, v_odd_w_out, v_ffn_w_gate, v_ffn_w_up, v_ffn_w_down, v_ln_mix_g, v_ln_mix_b, v_ln_ffn_g, v_ln_ffn_b):
    P = dict(even_w_in=even_w_in, pool_w=pool_w, pool_scale=pool_scale, conv_w=conv_w, conv_b=conv_b, dt_bias=dt_bias,
             a_log=a_log, d_skip=d_skip, ssm_norm_w=ssm_norm_w, even_w_out=even_w_out, odd_w_in=odd_w_in,
             fgate_b=fgate_b, q_norm_w=q_norm_w, w_uq=w_uq, kv_norm_w=kv_norm_w, w_ukv=w_ukv, odd_w_out=odd_w_out,
             ffn_w_gate=ffn_w_gate, ffn_w_up=ffn_w_up, ffn_w_down=ffn_w_down, ln_mix_g=ln_mix_g, ln_mix_b=ln_mix_b,
             ln_ffn_g=ln_ffn_g, ln_ffn_b=ln_ffn_b)
    M = dict(even_w_in=m_even_w_in, pool_w=m_pool_w, pool_scale=m_pool_scale, conv_w=m_conv_w, conv_b=m_conv_b,
             dt_bias=m_dt_bias, a_log=m_a_log, d_skip=m_d_skip, ssm_norm_w=m_ssm_norm_w, even_w_out=m_even_w_out,
             odd_w_in=m_odd_w_in, fgate_b=m_fgate_b, q_norm_w=m_q_norm_w, w_uq=m_w_uq, kv_norm_w=m_kv_norm_w,
             w_ukv=m_w_ukv, odd_w_out=m_odd_w_out, ffn_w_gate=m_ffn_w_gate, ffn_w_up=m_ffn_w_up,
             ffn_w_down=m_ffn_w_down, ln_mix_g=m_ln_mix_g, ln_mix_b=m_ln_mix_b, ln_ffn_g=m_ln_ffn_g,
             ln_ffn_b=m_ln_ffn_b)
    V = dict(even_w_in=v_even_w_in, pool_w=v_pool_w, pool_scale=v_pool_scale, conv_w=v_conv_w, conv_b=v_conv_b,
             dt_bias=v_dt_bias, a_log=v_a_log, d_skip=v_d_skip, ssm_norm_w=v_ssm_norm_w, even_w_out=v_even_w_out,
             odd_w_in=v_odd_w_in, fgate_b=v_fgate_b, q_norm_w=v_q_norm_w, w_uq=v_w_uq, kv_norm_w=v_kv_norm_w,
             w_ukv=v_w_ukv, odd_w_out=v_odd_w_out, ffn_w_gate=v_ffn_w_gate, ffn_w_up=v_ffn_w_up,
             ffn_w_down=v_ffn_w_down, ln_mix_g=v_ln_mix_g, ln_mix_b=v_ln_mix_b, ln_ffn_g=v_ln_ffn_g,
             ln_ffn_b=v_ln_ffn_b)
    names = list(P)
    s = x.shape[1]
    me = 4 * lax.axis_index("x") + 2 * lax.axis_index("y") + lax.axis_index("c")

    big_rows = [math.prod(P[n].shape) // _PACK_COLS for n, _ in _BIG]
    pad_rows = [-(-nr // 16) * 16 for nr in big_rows]
    rows_big = sum(pad_rows)
    packed = jnp.concatenate([
        jnp.pad(P[n].astype(BF16).reshape(-1), (0, (pr - nr) * _PACK_COLS))
        for (n, _), nr, pr in zip(_BIG, big_rows, pad_rows)]).reshape(rows_big, _PACK_COLS)
    gathered = _all_gather("ag_weights", packed)
    W = {}
    off = 0
    for (n, ax), nr, pr in zip(_BIG, big_rows, pad_rows):
        W[n] = _unshard(gathered[:, off:off + nr, :], P[n].shape, ax)
        off += pr
    kv_pad = jnp.zeros((128 - kv_norm_w.size,), F32)
    small_sh = jnp.concatenate([conv_w.reshape(-1), q_norm_w.reshape(-1), kv_norm_w.reshape(-1), kv_pad,
                                jnp.zeros((2 * 128,), F32)]).reshape(16, 128)
    g_small = _all_gather("ag_small_weights", small_sh)
    conv_w_full = _unshard(g_small[:, :12, :], conv_w.shape, 2)
    q_norm_full = _unshard(g_small[:, 12:13, :], q_norm_w.shape, 1)
    kv_norm_full = _unshard(g_small[:, 13:14, :64], kv_norm_w.shape, 1)

    w_in_e = [_pad_cols(W["even_w_in"][i], EVEN_IN_PAD) for i in range(2)]
    w_in_o = [_perm_odd_in(W["odd_w_in"][i]) for i in range(2)]
    w_uq_p = [_perm_uq(W["w_uq"][i]) for i in range(2)]
    w_ukv_p = [_perm_ukv(W["w_ukv"][i]) for i in range(2)]
    w_gu = [jnp.concatenate([W["ffn_w_gate"][l], W["ffn_w_up"][l]], axis=1) for l in range(DEPTH)]

    pos = jnp.arange(s, dtype=F32)
    half = MLA_ROPE // 2
    freqs = jnp.power(ROPE_THETA, -jnp.arange(half, dtype=F32) / half)
    ang = pos[:, None] * freqs[None, :]
    cos16, sin16 = jnp.cos(ang), jnp.sin(ang)
    cos128, sin128 = jnp.tile(cos16, (1, MLA_HEADS)), jnp.tile(sin16, (1, MLA_HEADS))
    row = lambda t: t.reshape(1, -1)

    xcur = x[0]
    xb = xcur.astype(BF16)
    saved = []
    for l in range(DEPTH):
        i = l // 2
        sv = dict(x_in_b=xb)
        if l % 2 == 0:
            proj = _mm("mm_in_even", xb, w_in_e[i], "nn", F32)
            u, z = proj[:, :512], proj[:, 512:1536]
            xbc, dtraw = proj[:, 1536:3072], proj[:, 3072:3088]
            ypool = _pool_fwd(u, pool_w[i], row(pool_scale[i]))
            xc = _conv_fwd(xbc, conv_w_full[i], row(conv_b[i]))
            dt = _softplus_fwd(dtraw, row(dt_bias[i]))
            xh = _heads(xc[:, :SSM_D_INNER], SSM_HEADS)
            dtc = dt.T[:, :, None]
            dtr = dt.T[:, None, :]
            bm, cm = xc[:, SSM_D_INNER:SSM_D_INNER + 256], xc[:, SSM_D_INNER + 256:]
            alog3, dsk3 = a_log[i].reshape(-1, 1, 1), d_skip[i].reshape(-1, 1, 1)
            yh, hprev = _ssd_fwd(xh, xh.transpose(0, 2, 1), dtc, dtr, bm, cm, alog3, dsk3)
            y_ssm = _unheads(yh)
            yn = _gated_rms_fwd(y_ssm, z, row(ssm_norm_w[i]))
            mix = jnp.concatenate([ypool, yn], axis=1)
            h = _mm("mm_out_even", mix, W["even_w_out"][i], "nn", F32)
            sv.update(u=u, z=z, xbc=xbc, dtraw=dtraw, xh=xh, dtc=dtc, dtr=dtr, bm=bm, cm=cm, hprev=hprev,
                      y_ssm=y_ssm, mix=mix)
        else:
            proj = _mm("mm_in_odd", xb, w_in_o[i], "nn", F32)
            qf, kf, vf = proj[:, :512], proj[:, 512:1024], proj[:, 1024:1536]
            cq, ckv = proj[:, 1536:2048], proj[:, 2048:2304]
            kr, fl = proj[:, 2304:2336], proj[:, 2336:2344]
            fl = fl.T
            fcum = _fgate_fwd(fl, fgate_b[i][:, None])
            fq_ = fcum[:, :, None]
            fk_ = fcum[:, None, :]
            qh, kh, vh = (_heads(t.astype(BF16), FOX_HEADS) for t in (qf * FOX_SCALE, kf, vf))
            o_fox, lse_fox = _flash_fwd("fox_fwd", qh, kh, vh, fq_, fk_, None)
            qn = _rms_fwd(cq, row(q_norm_full[i]))
            qp = _mm("mm_uq", qn, w_uq_p[i], "nn", F32)
            q1, q2 = _rope("rope_q", qp[:, 512:640], qp[:, 640:768], cos128, sin128)
            kvn = _rms_fwd(ckv, row(kv_norm_full[i]))
            kvp = _mm("mm_ukv", kvn, w_ukv_p[i], "nn", F32)
            k1, k2 = _rope("rope_k", kr[:, :half], kr[:, half:], cos16, sin16)
            zpad = jnp.zeros((MLA_HEADS, s, MLA_DK_PAD - MLA_NOPE - MLA_ROPE), BF16)
            qm = jnp.concatenate([_heads(qp[:, :512], MLA_HEADS), _heads(q1, MLA_HEADS), _heads(q2, MLA_HEADS)],
                                 axis=2).astype(BF16)
            qm = jnp.concatenate([qm, zpad], axis=2)
            krope = jnp.broadcast_to(jnp.concatenate([k1, k2], axis=1)[None], (MLA_HEADS, s, MLA_ROPE))
            km = jnp.concatenate([_heads(kvp[:, :512], MLA_HEADS), krope], axis=2).astype(BF16)
            km = jnp.concatenate([km, zpad], axis=2)
            vm = _heads(kvp[:, 512:], MLA_HEADS).astype(BF16)
            o_mla, lse_mla = _flash_fwd("mla_fwd", qm, km, vm, None, None, MLA_SCALE)
            mix = jnp.concatenate([_unheads(o_fox), _unheads(o_mla)], axis=1).astype(BF16)
            h = _mm("mm_out_odd", mix, W["odd_w_out"][i], "nn", F32)
            sv.update(fl=fl, fq=fq_, fk=fk_, qh=qh, kh=kh, vh=vh, o_fox=o_fox, lse_fox=lse_fox, cq=cq, ckv=ckv,
                      qn=qn, kvn=kvn, qm=qm, km=km, vm=vm, o_mla=o_mla, lse_mla=lse_mla, mix=mix)
        y1, y1b, r1 = _ln_fwd(xcur, h, row(ln_mix_g[l]), row(ln_mix_b[l]))
        gu = _mm("mm_ffn_in", y1b, w_gu[l], "nn", F32)
        act = _swiglu_fwd(gu)
        h2 = _mm("mm_ffn_out", act, W["ffn_w_down"][l], "nn", F32)
        y2, y2b, r2 = _ln_fwd(y1, h2, row(ln_ffn_g[l]), row(ln_ffn_b[l]))
        sv.update(r1=r1, y1b=y1b, gu=gu, act=act, r2=r2)
        saved.append(sv)
        xcur, xb = y2, y2b

    dy, loss_part = _loss_head(xcur, loss_target[0])
    loss = lax.psum(loss_part[0, 0], ("x", "y", "c"))

    G = {n: [None] * P[n].shape[0] for n in names}
    acur, dcur = None, dy
    for l in reversed(range(DEPTH)):
        i = l // 2
        sv = saved[l]
        dr2, dr2b, dg, db = _ln_bwd(acur, dcur, sv["r2"], row(ln_ffn_g[l]))
        G["ln_ffn_g"][l], G["ln_ffn_b"][l] = dg[0], db[0]
        G["ffn_w_down"][l] = _mm("mm_dw_ffn_out", sv["act"], dr2b, "tn", F32)
        dact = _mm("mm_dx_ffn_out", dr2b, W["ffn_w_down"][l], "nt", F32)
        dgu = _swiglu_bwd(dact, sv["gu"])
        dwgu = _mm("mm_dw_ffn_in", sv["y1b"], dgu, "tn", F32)
        G["ffn_w_gate"][l], G["ffn_w_up"][l] = dwgu[:, :D_FF], dwgu[:, D_FF:]
        dy1 = _mm("mm_dx_ffn_in", dgu, w_gu[l], "nt", F32)
        dr1, dr1b, dg, db = _ln_bwd(dr2, dy1, sv["r1"], row(ln_mix_g[l]))
        G["ln_mix_g"][l], G["ln_mix_b"][l] = dg[0], db[0]
        if l % 2 == 0:
            G["even_w_out"][i] = _mm("mm_dw_out_even", sv["mix"], dr1b, "tn", F32)
            dmix = _mm("mm_dx_out_even", dr1b, W["even_w_out"][i], "nt", F32)
            dd, dsc, dpw = _pool_bwd1(sv["u"], dmix[:, :POOL_WIDTH], pool_w[i], row(pool_scale[i]))
            G["pool_scale"][i], G["pool_w"][i] = dsc[0], dpw
            du = _pool_bwd2(dd)
            dys, dz, dnw = _gated_rms_bwd(dmix[:, POOL_WIDTH:], sv["y_ssm"], sv["z"], row(ssm_norm_w[i]))
            G["ssm_norm_w"][i] = dnw[0]
            dyh = _heads(dys, SSM_HEADS)
            alog3, dsk3 = a_log[i].reshape(-1, 1, 1), d_skip[i].reshape(-1, 1, 1)
            dxh, ddt3, dbm, dcm, dalog, ddsk = _ssd_bwd(dyh, dyh.transpose(0, 2, 1), sv["xh"], sv["dtc"], sv["dtr"],
                                                        sv["bm"], sv["cm"], sv["hprev"], alog3, dsk3)
            G["a_log"][i], G["d_skip"][i] = dalog.reshape(-1), ddsk.reshape(-1)
            ddtraw, ddtb = _softplus_bwd(ddt3[:, :, 0].T, sv["dtraw"], row(dt_bias[i]))
            G["dt_bias"][i] = ddtb[0]
            dxc = jnp.concatenate([_unheads(dxh), dbm, dcm], axis=1)
            dpre, dcw, dcb = _conv_bwd1(dxc, sv["xbc"], conv_w_full[i], row(conv_b[i]))
            G["conv_w"][i], G["conv_b"][i] = dcw, dcb[0]
            dxbc = _conv_bwd2(dpre, conv_w_full[i])
            dproj = jnp.concatenate([du, dz.astype(BF16), dxbc, ddtraw.astype(BF16),
                                     jnp.zeros((s, EVEN_IN_PAD - EVEN_IN), BF16)], axis=1)
            G["even_w_in"][i] = _mm("mm_dw_in_even", sv["x_in_b"], dproj, "tn", F32)[:, :EVEN_IN]
            dxb = _mm("mm_dx_in_even", dproj, w_in_e[i], "nt", F32)
        else:
            G["odd_w_out"][i] = _mm("mm_dw_out_odd", sv["mix"], dr1b, "tn", F32)
            dmix = _mm("mm_dx_out_odd", dr1b, W["odd_w_out"][i], "nt", F32)
            do_f = _heads(dmix[:, :FOX_WIDTH].astype(BF16), FOX_HEADS)
            dl_f = _rowdot(do_f.reshape(-1, 64), sv["o_fox"].reshape(-1, 64)).reshape(FOX_HEADS, s, 1)
            ones = jnp.ones((FOX_HEADS, s, 64), BF16)
            qg = jnp.concatenate([sv["qh"], ones], axis=2)
            kg = jnp.concatenate([sv["kh"], ones * (1.0 / FOX_SCALE)], axis=2)
            dqe, dke, dvh = _flash_bwd("fox_bwd", sv["qh"], sv["kh"], sv["vh"], do_f, sv["lse_fox"], dl_f,
                                       sv["fq"], sv["fk"], qg, kg, None, FOX_SCALE, None)
            dqh, dkh = dqe[:, :, :64], dke[:, :, :64]
            dfl, dfb = _fgate_bwd(dqe, dke, 64, sv["fl"], fgate_b[i][:, None])
            dfl = dfl.T
            G["fgate_b"][i] = dfb[:, 0]
            do_m = _heads(dmix[:, FOX_WIDTH:].astype(BF16), MLA_HEADS)
            dl_m = _rowdot(do_m.reshape(-1, 64), sv["o_mla"].reshape(-1, 64)).reshape(MLA_HEADS, s, 1)
            dqm, dkm, dvm = _flash_bwd("mla_bwd", sv["qm"], sv["km"], sv["vm"], do_m, sv["lse_mla"], dl_m,
                                       None, None, sv["qm"], sv["km"], MLA_SCALE, MLA_SCALE, MLA_SCALE)
            n0, n1 = MLA_NOPE, MLA_NOPE + half
            dq1, dq2 = _rope("rope_q_bwd", _unheads(dqm[:, :, n0:n1]), _unheads(dqm[:, :, n1:n1 + half]),
                             cos128, -sin128)
            dqp = jnp.concatenate([_unheads(dqm[:, :, :n0]), dq1, dq2], axis=1).astype(BF16)
            G["w_uq"][i] = _unperm_uq(_mm("mm_dw_uq", sv["qn"], dqp, "tn", F32))
            dqn = _mm("mm_dx_uq", dqp, w_uq_p[i], "nt", F32)
            dcq, dqw = _rms_bwd(dqn, sv["cq"], row(q_norm_full[i]))
            G["q_norm_w"][i] = dqw[0]
            dk1, dk2 = _headsum_rope_bwd(_unheads(dkm[:, :, n0:n1]), _unheads(dkm[:, :, n1:n1 + half]), cos16, sin16)
            dkvp = jnp.concatenate([_unheads(dkm[:, :, :n0]), _unheads(dvm)], axis=1).astype(BF16)
            G["w_ukv"][i] = _unperm_ukv(_mm("mm_dw_ukv", sv["kvn"], dkvp, "tn", F32))
            dkvn = _mm("mm_dx_ukv", dkvp, w_ukv_p[i], "nt", F32)
            dckv, dkvw = _rms_bwd(dkvn, sv["ckv"], row(kv_norm_full[i]))
            G["kv_norm_w"][i] = dkvw[0]
            dproj = jnp.concatenate([_unheads(dqh), _unheads(dkh), _unheads(dvh), dcq, dckv, dk1, dk2, dfl,
                                     jnp.zeros((s, ODD_IN_PAD - ODD_IN), F32)], axis=1).astype(BF16)
            G["odd_w_in"][i] = _unperm_odd_in(_mm("mm_dw_in_odd", sv["x_in_b"], dproj, "tn", F32))
            dxb = _mm("mm_dx_in_odd", dproj, w_in_o[i], "nt", F32)
        acur, dcur = dr1, dxb
    grad_x = _axpy(acur, dcur)[None]

    gfull = {n: jnp.stack(G[n]) for n in names}
    send = jnp.concatenate([
        jnp.pad(_reshard(gfull[n], ax, _PACK_COLS).astype(BF16), ((0, 0), (0, pr - nr), (0, 0)))
        for (n, ax), nr, pr in zip(_BIG, big_rows, pad_rows)], axis=1)
    recv = _all_to_all("a2a_grads", send)
    gsum = _sum8("sum_grads", recv)
    grads = {}
    off = 0
    for (n, ax), nr, pr in zip(_BIG, big_rows, pad_rows):
        grads[n] = gsum[off:off + nr].reshape(P[n].shape)
        off += pr
    small = [n for n in names if n not in dict(_BIG)]
    sflat = jnp.concatenate([gfull[n].reshape(-1) for n in small])
    n_small = sflat.shape[0]
    rows_small = -(-n_small // (128 * 8)) * 8
    sflat = jnp.concatenate([sflat, jnp.zeros((rows_small * 128 - n_small,), F32)])
    sg = _sum8("sum_small_grads", _all_gather("ag_small_grads", sflat.reshape(rows_small, 128))).reshape(-1)
    off = 0
    for n in small:
        cnt = gfull[n].size
        gf = sg[off:off + cnt].reshape(gfull[n].shape)
        off += cnt
        if gf.shape != P[n].shape:
            width = P[n].shape[-1]
            gf = lax.dynamic_slice_in_dim(gf, me * width, width, axis=gf.ndim - 1)
        grads[n] = gf

    delta, new_m, new_v = {}, {}, {}
    for n, _ in _BIG:
        shp = P[n].shape
        two = lambda t: t.reshape(-1, shp[-1])
        d_, m_, v_ = _adamw("adamw_" + n, two(P[n]), two(grads[n]), two(M[n]), two(V[n]))
        delta[n], new_m[n], new_v[n] = d_.reshape(shp), m_.reshape(shp), v_.reshape(shp)

    def packs(d):
        f = jnp.concatenate([d[n].reshape(-1) for n in small])
        pad = -(-f.shape[0] // (128 * 8)) * 8 * 128 - f.shape[0]
        return jnp.concatenate([f, jnp.zeros((pad,), F32)]).reshape(-1, 128)

    d_, m_, v_ = _adamw("adamw_small", packs(P), packs(grads), packs(M), packs(V))
    off = 0
    for n in small:
        cnt = P[n].size
        for dst, src in ((delta, d_), (new_m, m_), (new_v, v_)):
            dst[n] = src.reshape(-1)[off:off + cnt].reshape(P[n].shape)
        off += cnt

    return (loss, grad_x, *[grads[n] for n in names], *[delta[n] for n in names],
            *[new_m[n] for n in names], *[new_v[n] for n in names])
```

```python
import functools
import math

import jax
import jax.numpy as jnp
import numpy as np
from jax import lax
from jax.experimental import pallas as pl
from jax.experimental.pallas import tpu as pltpu

F32 = jnp.float32
BF16 = jnp.bfloat16
HI = lax.Precision.HIGHEST

N_DEV = 8
D_MODEL = 1024
DEPTH = 4
POOL_WINDOWS = (2, 4, 8, 16)
POOL_GROUP = 128
POOL_WIDTH = 512
SSM_D_INNER = 1024
SSM_HEAD_DIM = 64
SSM_HEADS = 16
SSM_GROUPS = 2
SSM_STATE = 128
SSM_CONV = 4
SSM_CHUNK = 128
SSM_CONV_DIM = 1536
EVEN_IN = 3088
EVEN_IN_PAD = 3200
FOX_HEADS = 8
FOX_WIDTH = 512
MLA_HEADS = 8
MLA_NOPE = 64
MLA_ROPE = 32
MLA_V = 64
MLA_Q_RANK = 512
MLA_KV_RANK = 256
MLA_DK_PAD = 128
ROPE_THETA = 10000.0
FOX_SCALE = 64 ** -0.5
MLA_SCALE = (MLA_NOPE + MLA_ROPE) ** -0.5
ODD_IN = 2344
ODD_IN_PAD = 2560
D_FF = 2816
ALPHA = (2 * DEPTH) ** 0.25
LN_EPS = 1e-5
RMS_EPS = 1e-6
ADAM_LR = 0.001
ADAM_B1 = 0.9
ADAM_B2 = 0.999
ADAM_EPS = 1e-08
ADAM_WD = 0.01
ADAM_STEP = 10
NEG = -1e30
VMEM_LIMIT = 48 * 1024 * 1024


def _cp(sem):
    return pltpu.CompilerParams(dimension_semantics=sem, vmem_limit_bytes=VMEM_LIMIT)


def _dot(a, b, ca, cb, prec=None):
    return lax.dot_general(a, b, (((ca,), (cb,)), ((), ())), preferred_element_type=F32, precision=prec)


def _sigmoid(x):
    return 1.0 / (1.0 + jnp.exp(-x))


def _softplus(x):
    return jnp.maximum(x, 0.0) + jnp.log(1.0 + jnp.exp(-jnp.abs(x)))


MESH = pl.DeviceIdType.MESH
HBM_SPEC = pl.BlockSpec(memory_space=pltpu.HBM)


def _all_gather(name, xs):
    r, c_ = xs.shape

    def body(x_ref, out_ref, send_sems, recv_sems, local_sem):
        x, y, c = lax.axis_index("x"), lax.axis_index("y"), lax.axis_index("c")
        me, sibling = (x, y, c), (x, y, 1 - c)
        chips = [(1 - x, y), (x, 1 - y), (1 - x, 1 - y)]

        def rows(px, py, pc):
            return out_ref.at[4 * px + 2 * py + pc]

        def copy(k, block, to, src=None):
            return pltpu.make_async_remote_copy(
                src_ref=rows(*block) if src is None else src,
                dst_ref=rows(*block),
                send_sem=send_sems.at[k],
                recv_sem=recv_sems.at[k],
                device_id=to,
                device_id_type=MESH,
            )

        mine = pltpu.make_async_copy(x_ref, rows(*me), local_sem)
        mine.start()
        first = [copy(0, me, sibling, src=x_ref)]
        first += [copy(1 + j, me, (*chip, c), src=x_ref) for j, chip in enumerate(chips)]
        for cp in first:
            cp.start()
        passed = [copy(4 + j, (*chip, c), sibling) for j, chip in enumerate(chips)]
        for j, chip in enumerate(chips):
            copy(1 + j, (*chip, c), me).wait_recv()
            passed[j].start()
        copy(0, sibling, me).wait_recv()
        for j, chip in enumerate(chips):
            copy(4 + j, (*chip, 1 - c), me).wait_recv()
        for cp in first + passed:
            cp.wait_send()
        mine.wait()

    return pl.pallas_call(
        body,
        name=name,
        out_shape=jax.ShapeDtypeStruct((N_DEV, r, c_), xs.dtype),
        in_specs=[HBM_SPEC],
        out_specs=HBM_SPEC,
        scratch_shapes=[pltpu.SemaphoreType.DMA((7,)), pltpu.SemaphoreType.DMA((7,)), pltpu.SemaphoreType.DMA(())],
    )(xs)


def _all_to_all(name, send):
    _, r, c_ = send.shape

    def body(s_ref, r_ref, send_sems, recv_sems, local_sem):
        x, y, c = lax.axis_index("x"), lax.axis_index("y"), lax.axis_index("c")
        me = 4 * x + 2 * y + c
        mine = pltpu.make_async_copy(s_ref.at[me], r_ref.at[me], local_sem)
        mine.start()
        copies = []
        for k in range(1, N_DEV):
            tx = 1 - x if k & 4 else x
            ty = 1 - y if k & 2 else y
            tc = 1 - c if k & 1 else c
            peer = 4 * tx + 2 * ty + tc
            cp = pltpu.make_async_remote_copy(
                src_ref=s_ref.at[peer],
                dst_ref=r_ref.at[me],
                send_sem=send_sems.at[k - 1],
                recv_sem=recv_sems.at[k - 1],
                device_id=(tx, ty, tc),
                device_id_type=MESH,
            )
            cp.start()
            landing = pltpu.make_async_remote_copy(
                src_ref=s_ref.at[me],
                dst_ref=r_ref.at[peer],
                send_sem=send_sems.at[k - 1],
                recv_sem=recv_sems.at[k - 1],
                device_id=(tx, ty, tc),
                device_id_type=MESH,
            )
            copies.append((cp, landing))
        for cp, landing in copies:
            landing.wait_recv()
        for cp, landing in copies:
            cp.wait_send()
        mine.wait()

    return pl.pallas_call(
        body,
        name=name,
        out_shape=jax.ShapeDtypeStruct(send.shape, send.dtype),
        in_specs=[HBM_SPEC],
        out_specs=HBM_SPEC,
        scratch_shapes=[pltpu.SemaphoreType.DMA((7,)), pltpu.SemaphoreType.DMA((7,)), pltpu.SemaphoreType.DMA(())],
    )(send)


def _pick(n, cands):
    for t in cands:
        if n % t == 0:
            return t
    return n


def _mm(name, a, b, mode, out_dtype):
    if mode == "nn":
        (m, k), n = a.shape, b.shape[1]
    elif mode == "nt":
        (m, k), n = a.shape, b.shape[0]
    else:
        (k, m), n = a.shape, b.shape[1]
    tm = _pick(m, (512, 256, 128))
    tn = _pick(n, (1408, 1280, 1024, 768, 640, 512, 384, 256, 128))
    tk = _pick(k, (1024, 1408, 768, 640, 512, 256, 128))
    nk = k // tk
    swap = nk == 1 and a.size * a.dtype.itemsize * (n // tn) + b.size * b.dtype.itemsize < (
        a.size * a.dtype.itemsize + b.size * b.dtype.itemsize * (m // tm))
    ij = (lambda g0, g1: (g1, g0)) if swap else (lambda g0, g1: (g0, g1))
    if mode == "nn":
        a_spec = pl.BlockSpec((tm, tk), lambda g0, g1, kk: (ij(g0, g1)[0], kk))
        b_spec = pl.BlockSpec((tk, tn), lambda g0, g1, kk: (kk, ij(g0, g1)[1]))
        ca, cb = 1, 0
    elif mode == "nt":
        a_spec = pl.BlockSpec((tm, tk), lambda g0, g1, kk: (ij(g0, g1)[0], kk))
        b_spec = pl.BlockSpec((tn, tk), lambda g0, g1, kk: (ij(g0, g1)[1], kk))
        ca, cb = 1, 1
    else:
        a_spec = pl.BlockSpec((tk, tm), lambda g0, g1, kk: (kk, ij(g0, g1)[0]))
        b_spec = pl.BlockSpec((tk, tn), lambda g0, g1, kk: (kk, ij(g0, g1)[1]))
        ca, cb = 0, 0

    def kern(a_ref, b_ref, o_ref, acc):
        kk = pl.program_id(2)

        @pl.when(kk == 0)
        def _():
            acc[...] = jnp.zeros_like(acc)

        acc[...] += _dot(a_ref[...].astype(BF16), b_ref[...].astype(BF16), ca, cb)

        @pl.when(kk == nk - 1)
        def _():
            o_ref[...] = acc[...].astype(out_dtype)

    return pl.pallas_call(
        kern,
        name=name,
        grid=(n // tn, m // tm, nk) if swap else (m // tm, n // tn, nk),
        in_specs=[a_spec, b_spec],
        out_specs=pl.BlockSpec((tm, tn), lambda g0, g1, kk: ij(g0, g1)),
        out_shape=jax.ShapeDtypeStruct((m, n), out_dtype),
        scratch_shapes=[pltpu.VMEM((tm, tn), F32)],
        compiler_params=_cp(("parallel", "parallel", "arbitrary")),
    )(a, b)


def _rowwise(name, body, ins, outs, tm):
    n_rows = next(a.shape[0] for a, kind in ins if kind == "row")
    n = n_rows // tm
    in_specs = []
    for a, kind in ins:
        if kind == "full":
            in_specs.append(pl.BlockSpec(a.shape, lambda i, nd=a.ndim: (0,) * nd))
        elif kind == "row":
            in_specs.append(pl.BlockSpec((tm, a.shape[1]), lambda i: (i, 0)))
        elif kind == "prev":
            in_specs.append(pl.BlockSpec((tm, a.shape[1]), lambda i: (jnp.maximum(i - 1, 0), 0)))
        else:
            in_specs.append(pl.BlockSpec((tm, a.shape[1]), lambda i: (jnp.minimum(i + 1, n - 1), 0)))
    out_specs, out_shape = [], []
    for shp, dt, kind in outs:
        out_shape.append(jax.ShapeDtypeStruct(shp, dt))
        if kind == "row":
            out_specs.append(pl.BlockSpec((tm, shp[1]), lambda i: (i, 0)))
        else:
            out_specs.append(pl.BlockSpec(shp, lambda i, nd=len(shp): (0,) * nd))
    n_in = len(ins)

    def kern(*refs):
        i = pl.program_id(0)
        res = body(i, n, *[r if kind == "full" else r[...] for r, (_, kind) in zip(refs[:n_in], ins)])
        for (shp, dt, kind), val, o in zip(outs, res, refs[n_in:]):
            if kind == "row":
                o[...] = val.astype(dt)
            else:

                @pl.when(i == 0)
                def _(o=o):
                    o[...] = jnp.zeros_like(o)

                o[...] += val.astype(dt)

    return pl.pallas_call(
        kern,
        name=name,
        grid=(n,),
        in_specs=in_specs,
        out_specs=out_specs,
        out_shape=out_shape,
        compiler_params=_cp(("arbitrary",)),
    )(*[a for a, _ in ins])


def _tm(s, t):
    return min(s, t)


def _ln_fwd(x, h, g, b):
    s = x.shape[0]

    def body(i, n, xv, hv, gv, bv):
        r = ALPHA * xv + hv.astype(F32)
        mu = jnp.mean(r, axis=-1, keepdims=True)
        d = r - mu
        var = jnp.mean(d * d, axis=-1, keepdims=True)
        y = d * lax.rsqrt(var + LN_EPS) * gv[...] + bv[...]
        return y, y, r

    shp = (s, D_MODEL)
    return _rowwise("ln_fwd", body, [(x, "row"), (h, "row"), (g, "full"), (b, "full")],
                    [(shp, F32, "row"), (shp, BF16, "row"), (shp, F32, "row")], _tm(s, 256))


def _ln_bwd(a, bterm, r, g):
    s = r.shape[0]

    def body(i, n, *vals):
        if a is None:
            dyv, rv, gv = vals
        else:
            av, dyv, rv, gv = vals
            dyv = ALPHA * av + dyv
        mu = jnp.mean(rv, axis=-1, keepdims=True)
        d = rv - mu
        var = jnp.mean(d * d, axis=-1, keepdims=True)
        rstd = lax.rsqrt(var + LN_EPS)
        xhat = d * rstd
        dxh = dyv * gv[...]
        dr = rstd * (dxh - jnp.mean(dxh, axis=-1, keepdims=True) - xhat * jnp.mean(dxh * xhat, axis=-1, keepdims=True))
        return dr, dr, jnp.sum(dyv * xhat, axis=0, keepdims=True), jnp.sum(dyv, axis=0, keepdims=True)

    ins = ([] if a is None else [(a, "row")]) + [(bterm, "row"), (r, "row"), (g, "full")]
    shp = (s, D_MODEL)
    return _rowwise("ln_bwd" if a is None else "ln_bwd_res", body, ins,
                    [(shp, F32, "row"), (shp, BF16, "row"), ((1, D_MODEL), F32, "acc"), ((1, D_MODEL), F32, "acc")],
                    _tm(s, 256))


def _axpy(a, b):
    def body(i, n, av, bv):
        return (ALPHA * av + bv,)

    return _rowwise("axpy", body, [(a, "row"), (b, "row")], [(a.shape, F32, "row")], _tm(a.shape[0], 256))[0]


def _loss_head(y, target):
    s = y.shape[0]

    def body(i, n, yv, tv):
        err = yv - tv
        part = 0.5 * jnp.sum(jnp.mean(err * err, axis=-1, keepdims=True), axis=0, keepdims=True)
        return err * (1.0 / D_MODEL), part

    return _rowwise("loss_head", body, [(y, "row"), (target, "row")],
                    [((s, D_MODEL), F32, "row"), ((1, 1), F32, "acc")], _tm(s, 256))


def _swiglu_fwd(gu):
    s = gu.shape[0]

    def body(i, n, v):
        g, u = v[:, :D_FF], v[:, D_FF:]
        return (g * _sigmoid(g) * u,)

    return _rowwise("swiglu_fwd", body, [(gu, "row")], [((s, D_FF), BF16, "row")], _tm(s, 256))[0]


def _swiglu_bwd(da, gu):
    s = gu.shape[0]

    def body(i, n, dav, v):
        g, u = v[:, :D_FF], v[:, D_FF:]
        sg = _sigmoid(g)
        dg = dav * u * (sg * (1.0 + g * (1.0 - sg)))
        du = dav * (g * sg)
        return (jnp.concatenate([dg, du], axis=1),)

    return _rowwise("swiglu_bwd", body, [(da, "row"), (gu, "row")], [((s, 2 * D_FF), BF16, "row")], _tm(s, 256))[0]


def _rms_fwd(x, w):
    s, c = x.shape

    def body(i, n, xv, wv):
        rs = lax.rsqrt(jnp.mean(xv * xv, axis=-1, keepdims=True) + RMS_EPS)
        return (xv * rs * wv[...],)

    return _rowwise("rms_fwd", body, [(x, "row"), (w, "full")], [((s, c), BF16, "row")], _tm(s, 512))[0]


def _rms_bwd(dy, x, w):
    s, c = x.shape

    def body(i, n, dyv, xv, wv):
        rs = lax.rsqrt(jnp.mean(xv * xv, axis=-1, keepdims=True) + RMS_EPS)
        nv = xv * rs
        dn = dyv * wv[...]
        dx = rs * (dn - nv * jnp.mean(dn * nv, axis=-1, keepdims=True))
        return dx, jnp.sum(dyv * nv, axis=0, keepdims=True)

    return _rowwise("rms_bwd", body, [(dy, "row"), (x, "row"), (w, "full")],
                    [((s, c), F32, "row"), ((1, c), F32, "acc")], _tm(s, 512))


def _gated_rms_fwd(y, z, w):
    s, c = y.shape

    def body(i, n, yv, zv, wv):
        y2 = yv * (zv * _sigmoid(zv))
        rs = lax.rsqrt(jnp.mean(y2 * y2, axis=-1, keepdims=True) + RMS_EPS)
        return (y2 * rs * wv[...],)

    return _rowwise("gated_rms_fwd", body, [(y, "row"), (z, "row"), (w, "full")], [((s, c), BF16, "row")], _tm(s, 256))[0]


def _gated_rms_bwd(do, y, z, w):
    s, c = y.shape

    def body(i, n, dov, yv, zv, wv):
        sz = _sigmoid(zv)
        silu = zv * sz
        y2 = yv * silu
        rs = lax.rsqrt(jnp.mean(y2 * y2, axis=-1, keepdims=True) + RMS_EPS)
        nv = y2 * rs
        dn = dov * wv[...]
        dy2 = rs * (dn - nv * jnp.mean(dn * nv, axis=-1, keepdims=True))
        return dy2 * silu, dy2 * yv * (sz * (1.0 + zv * (1.0 - sz))), jnp.sum(dov * nv, axis=0, keepdims=True)

    return _rowwise("gated_rms_bwd", body, [(do, "row"), (y, "row"), (z, "row"), (w, "full")],
                    [((s, c), F32, "row"), ((s, c), F32, "row"), ((1, c), F32, "acc")], _tm(s, 256))


def _rope(name, r1, r2, cos, sin):
    def body(i, n, a, b, cv, sv):
        return a * cv - b * sv, b * cv + a * sv

    return _rowwise(name, body, [(r1, "row"), (r2, "row"), (cos, "row"), (sin, "row")],
                    [(r1.shape, F32, "row"), (r1.shape, F32, "row")], _tm(r1.shape[0], 512))


def _headsum_rope_bwd(d1_all, d2_all, cos, sin):
    s = d1_all.shape[0]
    half = MLA_ROPE // 2

    def body(i, n, a_all, b_all, cv, sv):
        rr = lax.broadcasted_iota(jnp.int32, (MLA_HEADS * half, half), 0)
        cc = lax.broadcasted_iota(jnp.int32, (MLA_HEADS * half, half), 1)
        sel = jnp.where(rr % half == cc, 1.0, 0.0).astype(F32)
        a = _dot(a_all, sel, 1, 0, HI)
        b = _dot(b_all, sel, 1, 0, HI)
        return a * cv + b * sv, b * cv - a * sv

    return _rowwise("headsum_rope_bwd", body, [(d1_all, "row"), (d2_all, "row"), (cos, "row"), (sin, "row")],
                    [((s, half), F32, "row"), ((s, half), F32, "row")], _tm(s, 512))


def _softplus_fwd(dtr, bias):
    def body(i, n, v, bv):
        return (_softplus(v + bv[...]),)

    return _rowwise("softplus_fwd", body, [(dtr, "row"), (bias, "full")], [(dtr.shape, F32, "row")], _tm(dtr.shape[0], 1024))[0]


def _softplus_bwd(ddt, dtr, bias):
    def body(i, n, dv, v, bv):
        d = dv * _sigmoid(v + bv[...])
        return d, jnp.sum(d, axis=0, keepdims=True)

    return _rowwise("softplus_bwd", body, [(ddt, "row"), (dtr, "row"), (bias, "full")],
                    [(dtr.shape, F32, "row"), ((1, dtr.shape[1]), F32, "acc")], _tm(dtr.shape[0], 1024))


def _rowdot(a, b):
    def body(i, n, av, bv):
        return (jnp.sum(av.astype(F32) * bv.astype(F32), axis=-1, keepdims=True),)

    return _rowwise("rowdot", body, [(a, "row"), (b, "row")], [((a.shape[0], 1), F32, "row")], _tm(a.shape[0], 2048))[0]


def _sum8(name, blocks):
    _, r, c = blocks.shape
    tr = _pick(r, (512, 256, 128, 64, 32, 16, 8))

    def kern(b_ref, o_ref):
        acc = b_ref[0].astype(F32)
        for d in range(1, N_DEV):
            acc = acc + b_ref[d].astype(F32)
        o_ref[...] = acc

    return pl.pallas_call(
        kern,
        name=name,
        grid=(r // tr,),
        in_specs=[pl.BlockSpec((N_DEV, tr, c), lambda i: (0, i, 0))],
        out_specs=pl.BlockSpec((tr, c), lambda i: (i, 0)),
        out_shape=jax.ShapeDtypeStruct((r, c), F32),
        compiler_params=_cp(("parallel",)),
    )(blocks)


def _adamw(name, w, g, m, v):
    r = w.shape[0]
    tm = _pick(r, (512, 256, 128, 64, 32, 16, 8))

    def body(i, n, wv, gv, mv, vv):
        m2 = ADAM_B1 * mv + (1.0 - ADAM_B1) * gv
        v2 = ADAM_B2 * vv + (1.0 - ADAM_B2) * (gv * gv)
        m_hat = m2 / (1.0 - ADAM_B1 ** ADAM_STEP)
        v_hat = v2 / (1.0 - ADAM_B2 ** ADAM_STEP)
        delta = -ADAM_LR * (m_hat / (jnp.sqrt(v_hat) + ADAM_EPS) + ADAM_WD * wv)
        return delta, m2, v2

    return _rowwise(name, body, [(w, "row"), (g, "row"), (m, "row"), (v, "row")],
                    [(w.shape, F32, "row")] * 3, tm)


def _pool_diff(i, t_rows, u, up):
    ucat = jnp.concatenate([jnp.where(i > 0, up, 0.0), u], axis=0)
    r = lax.broadcasted_iota(jnp.int32, (t_rows, 2 * t_rows), 0)
    cc = lax.broadcasted_iota(jnp.int32, (t_rows, 2 * t_rows), 1)
    lag = r + t_rows - cc
    t_col = i * t_rows + lax.broadcasted_iota(jnp.int32, (t_rows, 1), 0)
    diffs = []
    for gi, wdw in enumerate(POOL_WINDOWS):
        win = jnp.where((lag >= 0) & (lag < wdw), 1.0, 0.0).astype(F32)
        cnt = jnp.minimum(t_col + 1, wdw).astype(F32)
        ws = _dot(win, ucat[:, gi * POOL_GROUP:(gi + 1) * POOL_GROUP], 1, 0, HI)
        diffs.append(ws / cnt - u[:, gi * POOL_GROUP:(gi + 1) * POOL_GROUP])
    return diffs


def _pool_fwd(u, pw, sc):
    s = u.shape[0]
    tm = _tm(s, 256)

    def body(i, n, uv, upv, wv, scv):
        diffs = _pool_diff(i, tm, uv, upv)
        ys = [_dot(d.astype(BF16), wv[gi].astype(BF16), 1, 0) for gi, d in enumerate(diffs)]
        return (jnp.concatenate(ys, axis=1) * scv[...],)

    return _rowwise("pool_fwd", body, [(u, "row"), (u, "prev"), (pw, "full"), (sc, "full")],
                    [((s, POOL_WIDTH), BF16, "row")], tm)[0]


def _pool_bwd1(u, dy, pw, sc):
    s = u.shape[0]
    tm = _tm(s, 256)

    def body(i, n, uv, upv, dyv, wv, scv):
        diffs = _pool_diff(i, tm, uv, upv)
        scv = scv[...]
        dsc, dws, dds = [], [], []
        for gi, d in enumerate(diffs):
            sl = slice(gi * POOL_GROUP, (gi + 1) * POOL_GROUP)
            db, wb = d.astype(BF16), wv[gi].astype(BF16)
            yg = _dot(db, wb, 1, 0)
            dsc.append(jnp.sum(yg * dyv[:, sl], axis=0, keepdims=True))
            eb = (dyv[:, sl] * scv[:, sl]).astype(BF16)
            dws.append(_dot(db, eb, 0, 0)[None])
            dds.append(_dot(eb, wb, 1, 1))
        return jnp.concatenate(dds, axis=1), jnp.concatenate(dsc, axis=1), jnp.concatenate(dws, axis=0)

    return _rowwise("pool_bwd1", body, [(u, "row"), (u, "prev"), (dy, "row"), (pw, "full"), (sc, "full")],
                    [((s, POOL_WIDTH), F32, "row"), ((1, POOL_WIDTH), F32, "acc"), (pw.shape, F32, "acc")], tm)


def _pool_bwd2(dd):
    s = dd.shape[0]
    tm = _tm(s, 256)

    def body(i, n, dv, dnv):
        dcat = jnp.concatenate([dv, jnp.where(i < n - 1, dnv, 0.0)], axis=0)
        r = lax.broadcasted_iota(jnp.int32, (tm, 2 * tm), 0)
        cc = lax.broadcasted_iota(jnp.int32, (tm, 2 * tm), 1)
        lag = cc - r
        t_col = i * tm + lax.broadcasted_iota(jnp.int32, (2 * tm, 1), 0)
        outs = []
        for gi, wdw in enumerate(POOL_WINDOWS):
            sl = slice(gi * POOL_GROUP, (gi + 1) * POOL_GROUP)
            win = jnp.where((lag >= 0) & (lag < wdw), 1.0, 0.0).astype(F32)
            cnt = jnp.minimum(t_col + 1, wdw).astype(F32)
            outs.append(_dot(win, dcat[:, sl] / cnt, 1, 0, HI) - dv[:, sl])
        return (jnp.concatenate(outs, axis=1),)

    return _rowwise("pool_bwd2", body, [(dd, "row"), (dd, "next")], [((s, POOL_WIDTH), BF16, "row")], tm)[0]


def _shift_down(cur, other, j, tm):
    if j == 0:
        return cur
    rows = lax.broadcasted_iota(jnp.int32, cur.shape, 0)
    return jnp.where(rows < j, pltpu.roll(other, j, 0), pltpu.roll(cur, j, 0))


def _shift_up(cur, other, j, tm):
    if j == 0:
        return cur
    rows = lax.broadcasted_iota(jnp.int32, cur.shape, 0)
    return jnp.where(rows >= tm - j, pltpu.roll(other, tm - j, 0), pltpu.roll(cur, tm - j, 0))


def _conv_pre(i, tm, xv, xpv, wv, bv):
    xpv = jnp.where(i > 0, xpv, 0.0)
    taps = [_shift_down(xv, xpv, SSM_CONV - 1 - k, tm) for k in range(SSM_CONV)]
    pre = bv[...]
    for k in range(SSM_CONV):
        pre = pre + wv[k:k + 1, :] * taps[k]
    return pre, taps


def _conv_fwd(xbc, w, b):
    s = xbc.shape[0]
    tm = _tm(s, 256)

    def body(i, n, xv, xpv, wv, bv):
        pre, _ = _conv_pre(i, tm, xv, xpv, wv, bv)
        return (pre * _sigmoid(pre),)

    return _rowwise("conv_fwd", body, [(xbc, "row"), (xbc, "prev"), (w, "full"), (b, "full")],
                    [(xbc.shape, F32, "row")], tm)[0]


def _conv_bwd1(dxc, xbc, w, b):
    s, c = xbc.shape
    tm = _tm(s, 256)

    def body(i, n, dv, xv, xpv, wv, bv):
        pre, taps = _conv_pre(i, tm, xv, xpv, wv, bv)
        sg = _sigmoid(pre)
        dpre = dv * (sg * (1.0 + pre * (1.0 - sg)))
        tap_row = lax.broadcasted_iota(jnp.int32, (SSM_CONV, c), 0)
        dw = jnp.zeros((SSM_CONV, c), F32)
        for k in range(SSM_CONV):
            dw = dw + jnp.where(tap_row == k, jnp.sum(dpre * taps[k], axis=0, keepdims=True), 0.0)
        return dpre, dw, jnp.sum(dpre, axis=0, keepdims=True)

    return _rowwise("conv_bwd1", body, [(dxc, "row"), (xbc, "row"), (xbc, "prev"), (w, "full"), (b, "full")],
                    [((s, c), F32, "row"), ((SSM_CONV, c), F32, "acc"), ((1, c), F32, "acc")], tm)


def _conv_bwd2(dpre, w):
    s, c = dpre.shape
    tm = _tm(s, 256)

    def body(i, n, dv, dnv, wv):
        dnv = jnp.where(i < n - 1, dnv, 0.0)
        out = jnp.zeros_like(dv)
        for k in range(SSM_CONV):
            out = out + wv[k:k + 1, :] * _shift_up(dv, dnv, SSM_CONV - 1 - k, tm)
        return (out,)

    return _rowwise("conv_bwd2", body, [(dpre, "row"), (dpre, "next"), (w, "full")], [((s, c), BF16, "row")], tm)[0]


def _ssd_common(alog, dt_c, dt_r, tril, tri):
    a = -jnp.exp(alog)
    acs_c = _dot(tril, dt_c * a, 1, 0, HI)
    acs_r = _dot(dt_r * a, tril, 1, 1, HI)
    a_last = jnp.sum(dt_c * a, axis=0, keepdims=True)
    return a, acs_c, acs_r, a_last


def _ssd_fwd(xh, xht, dtc, dtr, bm, cm, alog, dsk):
    nh, s, p = xh.shape
    L = SSM_CHUNK
    nc = s // L
    E = nh // SSM_GROUPS

    def kern(xh_ref, xht_ref, dtc_ref, dtr_ref, b_ref, c_ref, alog_ref, dsk_ref, y_ref, hp_ref, h_s):
        c = pl.program_id(1)

        @pl.when(c == 0)
        def _():
            h_s[...] = jnp.zeros_like(h_s)

        bb = b_ref[...].astype(BF16)
        cb_ = c_ref[...].astype(BF16)
        cbm = _dot(cb_, bb, 1, 1)
        ri = lax.broadcasted_iota(jnp.int32, (L, L), 0)
        ci = lax.broadcasted_iota(jnp.int32, (L, L), 1)
        tri = ri >= ci
        tril = tri.astype(F32)
        hs = range(E)
        com = [_ssd_common(alog_ref[e], dtc_ref[e], dtr_ref[e], tril, tri) for e in hs]
        lam = [jnp.exp(jnp.where(tri, com[e][1] - com[e][2], NEG)) for e in hs]
        xs = [xh_ref[e] for e in hs]
        xdt = [(xs[e] * dtc_ref[e]).astype(BF16) for e in hs]
        hh = [h_s[e] for e in hs]
        y_diag = [_dot((cbm * lam[e]).astype(BF16), xdt[e], 1, 0) for e in hs]
        y_off = [_dot(cb_, hh[e].astype(BF16), 1, 1) for e in hs]
        xw = [(xht_ref[e] * (dtr_ref[e] * jnp.exp(com[e][3] - com[e][2]))).astype(BF16) for e in hs]
        st = [_dot(xw[e], bb, 1, 0) for e in hs]
        for e in hs:
            y_ref[e] = y_diag[e] + y_off[e] * jnp.exp(com[e][1]) + xs[e] * dsk_ref[e]
            hp_ref[e] = hh[e]
            h_s[e] = hh[e] * jnp.exp(com[e][3]) + st[e]

    return pl.pallas_call(
        kern,
        name="ssd_fwd",
        grid=(SSM_GROUPS, nc),
        in_specs=[
            pl.BlockSpec((E, L, p), lambda g, c: (g, c, 0)),
            pl.BlockSpec((E, p, L), lambda g, c: (g, 0, c)),
            pl.BlockSpec((E, L, 1), lambda g, c: (g, c, 0)),
            pl.BlockSpec((E, 1, L), lambda g, c: (g, 0, c)),
            pl.BlockSpec((L, SSM_STATE), lambda g, c: (c, g)),
            pl.BlockSpec((L, SSM_STATE), lambda g, c: (c, g)),
            pl.BlockSpec((E, 1, 1), lambda g, c: (g, 0, 0)),
            pl.BlockSpec((E, 1, 1), lambda g, c: (g, 0, 0)),
        ],
        out_specs=[
            pl.BlockSpec((E, L, p), lambda g, c: (g, c, 0)),
            pl.BlockSpec((E, None, p, SSM_STATE), lambda g, c: (g, c, 0, 0)),
        ],
        out_shape=[jax.ShapeDtypeStruct((nh, s, p), F32), jax.ShapeDtypeStruct((nh, nc, p, SSM_STATE), F32)],
        scratch_shapes=[pltpu.VMEM((E, p, SSM_STATE), F32)],
        compiler_params=_cp(("arbitrary", "arbitrary")),
    )(xh, xht, dtc, dtr, bm, cm, alog, dsk)


def _ssd_bwd(dy, dyt, xh, dtc, dtr, bm, cm, hprev, alog, dsk):
    nh, s, p = xh.shape
    L = SSM_CHUNK
    nc = s // L
    E = nh // SSM_GROUPS

    def kern(dy_ref, dyt_ref, xh_ref, dtc_ref, dtr_ref, b_ref, c_ref, hp_ref, alog_ref, dsk_ref,
             dx_ref, ddt_ref, db_ref, dc_ref, dalog_ref, dd_ref, dh_s):
        c = pl.program_id(1)

        @pl.when(c == 0)
        def _():
            dh_s[...] = jnp.zeros_like(dh_s)
            dalog_ref[...] = jnp.zeros_like(dalog_ref)
            dd_ref[...] = jnp.zeros_like(dd_ref)

        bb = b_ref[...].astype(BF16)
        cb_ = c_ref[...].astype(BF16)
        cbm = _dot(cb_, bb, 1, 1)
        cbt = _dot(bb, cb_, 1, 1)
        ri = lax.broadcasted_iota(jnp.int32, (L, L), 0)
        ci = lax.broadcasted_iota(jnp.int32, (L, L), 1)
        tri = ri >= ci
        trit = ci >= ri
        tril = tri.astype(F32)
        triu = trit.astype(F32)
        hs = range(E)
        com = [_ssd_common(alog_ref[e], dtc_ref[e], dtr_ref[e], tril, tri) for e in hs]
        lam = [jnp.exp(jnp.where(tri, com[e][1] - com[e][2], NEG)) for e in hs]
        lamt = [jnp.exp(jnp.where(trit, com[e][2] - com[e][1], NEG)) for e in hs]
        xs = [xh_ref[e] for e in hs]
        xdt = [xs[e] * dtc_ref[e] for e in hs]
        xdtb = [t.astype(BF16) for t in xdt]
        dyv = [dy_ref[e] for e in hs]
        dyb = [t.astype(BF16) for t in dyv]
        hh = [hp_ref[e] for e in hs]
        hb = [t.astype(BF16) for t in hh]
        dhn = [dh_s[e] for e in hs]
        dhnb = [t.astype(BF16) for t in dhn]
        ea_c = [jnp.exp(com[e][1]) for e in hs]
        decay_c = [jnp.exp(com[e][3] - com[e][1]) for e in hs]
        e_last = [jnp.exp(com[e][3]) for e in hs]
        gm = [_dot(dyb[e], xdtb[e], 1, 1) for e in hs]
        gt = [_dot(xdtb[e], dyb[e], 1, 1) for e in hs]
        bdh = [_dot(bb, dhnb[e], 1, 1) for e in hs]
        dxdt = [_dot((cbt * lamt[e]).astype(BF16), dyb[e], 1, 0) + bdh[e] * decay_c[e] for e in hs]
        dcb = [gm[e] * lam[e] for e in hs]
        dcbt = [gt[e] * lamt[e] for e in hs]
        yoff = [_dot(cb_, hb[e], 1, 1) * ea_c[e] for e in hs]
        dcs = [_dot(dcb[e].astype(BF16), bb, 1, 0) + _dot(dyb[e], hb[e], 1, 0) * ea_c[e] for e in hs]
        dbs = [_dot(dcbt[e].astype(BF16), cb_, 1, 0) + _dot(xdtb[e], dhnb[e], 1, 0) * decay_c[e] for e in hs]
        dyw = [(dyt_ref[e] * jnp.exp(com[e][2])).astype(BF16) for e in hs]
        dh_new = [dhn[e] * e_last[e] + _dot(dyw[e], cb_, 1, 0) for e in hs]
        db_acc, dc_acc = dbs[0], dcs[0]
        for e in range(1, E):
            db_acc = db_acc + dbs[e]
            dc_acc = dc_acc + dcs[e]
        for e in hs:
            a = com[e][0]
            tt = decay_c[e] * jnp.sum(xdt[e] * bdh[e], axis=1, keepdims=True)
            dacs = (jnp.sum(dcb[e] * cbm, axis=1, keepdims=True) - jnp.sum(dcbt[e] * cbt, axis=1, keepdims=True)
                    + jnp.sum(dyv[e] * yoff[e], axis=1, keepdims=True) - tt)
            tail = jnp.sum(tt, axis=0, keepdims=True) + e_last[e] * jnp.sum(
                jnp.sum(dhn[e] * hh[e], axis=1, keepdims=True), axis=0, keepdims=True)
            dda = _dot(triu, dacs, 1, 0, HI) + tail
            ddt_ref[e] = dda * a + jnp.sum(dxdt[e] * xs[e], axis=1, keepdims=True)
            dalog_ref[e] += jnp.sum(dda * dtc_ref[e], axis=0, keepdims=True) * a
            dd_ref[e] += jnp.sum(jnp.sum(dyv[e] * xs[e], axis=1, keepdims=True), axis=0, keepdims=True)
            dx_ref[e] = dxdt[e] * dtc_ref[e] + dyv[e] * dsk_ref[e]
            dh_s[e] = dh_new[e]
        db_ref[...] = db_acc
        dc_ref[...] = dc_acc

    rc = lambda c: nc - 1 - c
    return pl.pallas_call(
        kern,
        name="ssd_bwd",
        grid=(SSM_GROUPS, nc),
        in_specs=[
            pl.BlockSpec((E, L, p), lambda g, c: (g, rc(c), 0)),
            pl.BlockSpec((E, p, L), lambda g, c: (g, 0, rc(c))),
            pl.BlockSpec((E, L, p), lambda g, c: (g, rc(c), 0)),
            pl.BlockSpec((E, L, 1), lambda g, c: (g, rc(c), 0)),
            pl.BlockSpec((E, 1, L), lambda g, c: (g, 0, rc(c))),
            pl.BlockSpec((L, SSM_STATE), lambda g, c: (rc(c), g)),
            pl.BlockSpec((L, SSM_STATE), lambda g, c: (rc(c), g)),
            pl.BlockSpec((E, None, p, SSM_STATE), lambda g, c: (g, rc(c), 0, 0)),
            pl.BlockSpec((E, 1, 1), lambda g, c: (g, 0, 0)),
            pl.BlockSpec((E, 1, 1), lambda g, c: (g, 0, 0)),
        ],
        out_specs=[
            pl.BlockSpec((E, L, p), lambda g, c: (g, rc(c), 0)),
            pl.BlockSpec((E, L, 1), lambda g, c: (g, rc(c), 0)),
            pl.BlockSpec((L, SSM_STATE), lambda g, c: (rc(c), g)),
            pl.BlockSpec((L, SSM_STATE), lambda g, c: (rc(c), g)),
            pl.BlockSpec((E, 1, 1), lambda g, c: (g, 0, 0)),
            pl.BlockSpec((E, 1, 1), lambda g, c: (g, 0, 0)),
        ],
        out_shape=[
            jax.ShapeDtypeStruct((nh, s, p), F32),
            jax.ShapeDtypeStruct((nh, s, 1), F32),
            jax.ShapeDtypeStruct((s, SSM_GROUPS * SSM_STATE), F32),
            jax.ShapeDtypeStruct((s, SSM_GROUPS * SSM_STATE), F32),
            jax.ShapeDtypeStruct((nh, 1, 1), F32),
            jax.ShapeDtypeStruct((nh, 1, 1), F32),
        ],
        scratch_shapes=[pltpu.VMEM((E, p, SSM_STATE), F32)],
        compiler_params=_cp(("arbitrary", "arbitrary")),
    )(dy, dyt, xh, dtc, dtr, bm, cm, hprev, alog, dsk)


def _fgate_fwd(fl, bias):
    h, s = fl.shape
    t = _tm(s, 512)

    def kern(fl_ref, b_ref, o_ref, carry):
        i = pl.program_id(0)

        @pl.when(i == 0)
        def _():
            carry[...] = jnp.zeros_like(carry)

        lf = -_softplus(-(fl_ref[...] + b_ref[...]))
        ri = lax.broadcasted_iota(jnp.int32, (t, t), 0)
        ci = lax.broadcasted_iota(jnp.int32, (t, t), 1)
        o_ref[...] = _dot(lf, (ri <= ci).astype(F32), 1, 0, HI) + carry[...]
        carry[...] += jnp.sum(lf, axis=1, keepdims=True)

    return pl.pallas_call(
        kern,
        name="fgate_fwd",
        grid=(s // t,),
        in_specs=[pl.BlockSpec((h, t), lambda i: (0, i)), pl.BlockSpec((h, 1), lambda i: (0, 0))],
        out_specs=pl.BlockSpec((h, t), lambda i: (0, i)),
        out_shape=jax.ShapeDtypeStruct((h, s), F32),
        scratch_shapes=[pltpu.VMEM((h, 1), F32)],
        compiler_params=_cp(("arbitrary",)),
    )(fl, bias)


def _fgate_bwd(dqe, dke, col, fl, bias):
    h, s = fl.shape
    w = dqe.shape[-1]
    t = _tm(s, 512)
    n = s // t

    def kern(dq_ref, dk_ref, fl_ref, b_ref, o_ref, db_ref, carry):
        i = pl.program_id(0)

        @pl.when(i == 0)
        def _():
            carry[...] = jnp.zeros_like(carry)
            db_ref[...] = jnp.zeros_like(db_ref)

        sel = (lax.broadcasted_iota(jnp.int32, (1, w), 1) == col).astype(F32)
        hid = lax.broadcasted_iota(jnp.int32, (h, t), 0)
        d = jnp.zeros((h, t), F32)
        for hh in range(h):
            r = _dot(sel, dq_ref[hh], 1, 1, HI) - _dot(sel, dk_ref[hh], 1, 1, HI)
            d = d + jnp.where(hid == hh, r, 0.0)
        ri = lax.broadcasted_iota(jnp.int32, (t, t), 0)
        ci = lax.broadcasted_iota(jnp.int32, (t, t), 1)
        rev = _dot(d, (ri >= ci).astype(F32), 1, 0, HI) + carry[...]
        carry[...] += jnp.sum(d, axis=1, keepdims=True)
        dz = rev * _sigmoid(-(fl_ref[...] + b_ref[...]))
        o_ref[...] = dz
        db_ref[...] += jnp.sum(dz, axis=1, keepdims=True)

    rev_blk = lambda i: (0, n - 1 - i)
    return pl.pallas_call(
        kern,
        name="fgate_bwd",
        grid=(n,),
        in_specs=[pl.BlockSpec((h, t, w), lambda i: (0, n - 1 - i, 0)), pl.BlockSpec((h, t, w), lambda i: (0, n - 1 - i, 0)),
                  pl.BlockSpec((h, t), rev_blk), pl.BlockSpec((h, 1), lambda i: (0, 0))],
        out_specs=[pl.BlockSpec((h, t), rev_blk), pl.BlockSpec((h, 1), lambda i: (0, 0))],
        out_shape=[jax.ShapeDtypeStruct((h, s), F32), jax.ShapeDtypeStruct((h, 1), F32)],
        scratch_shapes=[pltpu.VMEM((h, 1), F32)],
        compiler_params=_cp(("arbitrary",)),
    )(dqe, dke, fl, bias)


HPS_FWD = 4
HPS_BWD = 4


def _attn_scores(q, k, fq, fk, scale, masked, t):
    sc = _dot(q, k, 1, 1)
    if scale is not None:
        sc = sc * scale
    if fq is not None:
        sc = sc + fq - fk
    if masked:
        rows = lax.broadcasted_iota(jnp.int32, (t, t), 0)
        cols = lax.broadcasted_iota(jnp.int32, (t, t), 1)
        sc = jnp.where(cols <= rows, sc, NEG)
    return sc


def _flash_fwd(name, q, k, v, fq, fk, scale):
    nh, s, dk = q.shape
    dv = v.shape[-1]
    t = _tm(s, 512)
    nq = s // t
    bias = fq is not None
    HPS = HPS_FWD

    pairs = [(i, j) for i in range(nq) for j in range(i + 1)]
    qi_of = jnp.asarray(np.array([p[0] for p in pairs], np.int32))
    kj_of = jnp.asarray(np.array([p[1] for p in pairs], np.int32))

    def kern(qi_ref, kj_ref, *refs):
        if bias:
            q_ref, k_ref, v_ref, fq_ref, fk_ref, o_ref, lse_ref, m_s, l_s, acc_s = refs
        else:
            q_ref, k_ref, v_ref, o_ref, lse_ref, m_s, l_s, acc_s = refs
        qi, kj = qi_ref[pl.program_id(1)], kj_ref[pl.program_id(1)]

        @pl.when(kj == 0)
        def _():
            m_s[...] = jnp.full_like(m_s, NEG)
            l_s[...] = jnp.zeros_like(l_s)
            acc_s[...] = jnp.zeros_like(acc_s)

        def step(masked):
            scs = [_attn_scores(q_ref[hh], k_ref[hh], fq_ref[hh] if bias else None, fk_ref[hh] if bias else None,
                                scale, masked, t) for hh in range(HPS)]
            pbs, corrs = [], []
            for hh in range(HPS):
                m_old = m_s[hh]
                m_new = jnp.maximum(m_old, jnp.max(scs[hh], axis=1, keepdims=True))
                corr = jnp.exp(m_old - m_new)
                p = jnp.exp(scs[hh] - m_new)
                l_s[hh] = corr * l_s[hh] + jnp.sum(p, axis=1, keepdims=True)
                m_s[hh] = m_new
                pbs.append(p.astype(BF16))
                corrs.append(corr)
            for hh in range(HPS):
                acc_s[hh] = acc_s[hh] * corrs[hh] + _dot(pbs[hh], v_ref[hh], 1, 0)

        @pl.when(kj < qi)
        def _():
            step(False)

        @pl.when(kj == qi)
        def _():
            step(True)
            o_ref[...] = acc_s[...] / l_s[...]
            lse_ref[...] = m_s[...] + jnp.log(l_s[...])

    qspec = lambda d: pl.BlockSpec((HPS, t, d), lambda h, p, qi_r, kj_r: (h, qi_r[p], 0))
    kspec = lambda d: pl.BlockSpec((HPS, t, d), lambda h, p, qi_r, kj_r: (h, kj_r[p], 0))
    in_specs = [qspec(dk), kspec(dk), kspec(dv)]
    args = [q, k, v]
    if bias:
        in_specs += [qspec(1), pl.BlockSpec((HPS, 1, t), lambda h, p, qi_r, kj_r: (h, 0, kj_r[p]))]
        args += [fq, fk]
    return pl.pallas_call(
        kern,
        name=name,
        grid_spec=pltpu.PrefetchScalarGridSpec(
            num_scalar_prefetch=2,
            grid=(nh // HPS, len(pairs)),
            in_specs=in_specs,
            out_specs=[qspec(dv), qspec(1)],
            scratch_shapes=[pltpu.VMEM((HPS, t, 1), F32), pltpu.VMEM((HPS, t, 1), F32),
                            pltpu.VMEM((HPS, t, dv), F32)],
        ),
        out_shape=[jax.ShapeDtypeStruct((nh, s, dv), F32), jax.ShapeDtypeStruct((nh, s, 1), F32)],
        compiler_params=_cp(("parallel", "arbitrary")),
    )(qi_of, kj_of, *args)


def _flash_bwd(name, q, k, v, do, lse, delta, fq, fk, qg, kg, scale, dq_scale, dk_scale):
    nh, s, dk = q.shape
    dv = v.shape[-1]
    dg = qg.shape[-1]
    t = _tm(s, 512)
    nq = s // t
    bias = fq is not None
    ext = qg is not q
    HPS = HPS_BWD

    pairs = [(j, i) for j in range(nq) for i in range(j, nq)]
    kb_of = jnp.asarray(np.array([p[0] for p in pairs], np.int32))
    qi_of = jnp.asarray(np.array([p[1] for p in pairs], np.int32))

    def kern(kb_ref, qi_ref, *refs):
        refs = list(refs)
        q_ref, k_ref, v_ref, do_ref, lse_ref, dl_ref = refs[:6]
        del refs[:6]
        fq_ref, fk_ref = (refs.pop(0), refs.pop(0)) if bias else (None, None)
        qg_ref, kg_ref = (refs.pop(0), refs.pop(0)) if ext else (q_ref, k_ref)
        dq_ref, dk_ref, dv_ref, dk_s, dv_s = refs
        kb, qi = kb_ref[pl.program_id(1)], qi_ref[pl.program_id(1)]

        @pl.when(pl.program_id(1) == 0)
        def _():
            dq_ref[...] = jnp.zeros_like(dq_ref)

        @pl.when(qi == kb)
        def _():
            dk_s[...] = jnp.zeros_like(dk_s)
            dv_s[...] = jnp.zeros_like(dv_s)

        def step(masked):
            rows = pl.ds(pl.multiple_of(qi * t, t), t)
            heads = range(HPS)
            scs = [_attn_scores(q_ref[hh], k_ref[hh], fq_ref[hh] if bias else None, fk_ref[hh] if bias else None,
                                scale, masked, t) for hh in heads]
            dps = [_dot(do_ref[hh], v_ref[hh], 1, 1) for hh in heads]
            ps = [jnp.exp(scs[hh] - lse_ref[hh]) for hh in heads]
            pbs = [p.astype(BF16) for p in ps]
            for hh in heads:
                dv_s[hh] += _dot(pbs[hh], do_ref[hh], 0, 0)
            dsbs = [(ps[hh] * (dps[hh] - dl_ref[hh])).astype(BF16) for hh in heads]
            for hh in heads:
                dk_s[hh] += _dot(dsbs[hh], qg_ref[hh], 0, 0)
            for hh in heads:
                dq_ref[hh, rows, :] += _dot(dsbs[hh], kg_ref[hh], 1, 0) * dq_scale

        @pl.when(qi > kb)
        def _():
            step(False)

        @pl.when(qi == kb)
        def _():
            step(True)

        @pl.when(qi == nq - 1)
        def _():
            dk_ref[...] = dk_s[...] if dk_scale is None else dk_s[...] * dk_scale
            dv_ref[...] = dv_s[...]

    qspec = lambda d: pl.BlockSpec((HPS, t, d), lambda h, p, kb_r, qi_r: (h, qi_r[p], 0))
    kspec = lambda d: pl.BlockSpec((HPS, t, d), lambda h, p, kb_r, qi_r: (h, kb_r[p], 0))
    in_specs = [qspec(dk), kspec(dk), kspec(dv), qspec(dv), qspec(1), qspec(1)]
    args = [q, k, v, do, lse, delta]
    if bias:
        in_specs += [qspec(1), pl.BlockSpec((HPS, 1, t), lambda h, p, kb_r, qi_r: (h, 0, kb_r[p]))]
        args += [fq, fk]
    if ext:
        in_specs += [qspec(dg), kspec(dg)]
        args += [qg, kg]
    return pl.pallas_call(
        kern,
        name=name,
        grid_spec=pltpu.PrefetchScalarGridSpec(
            num_scalar_prefetch=2,
            grid=(nh // HPS, len(pairs)),
            in_specs=in_specs,
            out_specs=[pl.BlockSpec((HPS, s, dg), lambda h, p, kb_r, qi_r: (h, 0, 0), pipeline_mode=pl.Buffered(1)),
                       kspec(dg), kspec(dv)],
            scratch_shapes=[pltpu.VMEM((HPS, t, dg), F32), pltpu.VMEM((HPS, t, dv), F32)],
        ),
        out_shape=[jax.ShapeDtypeStruct((nh, s, dg), F32), jax.ShapeDtypeStruct((nh, s, dg), F32),
                   jax.ShapeDtypeStruct((nh, s, dv), F32)],
        compiler_params=_cp(("arbitrary", "arbitrary")),
    )(kb_of, qi_of, *args)


def _heads(t, nh):
    s = t.shape[0]
    return t.reshape(s, nh, -1).transpose(1, 0, 2)


def _unheads(t):
    nh, s, d = t.shape
    return t.transpose(1, 0, 2).reshape(s, nh * d)


def _perm_uq(w):
    r = w.shape[0]
    w3 = w.reshape(r, MLA_HEADS, MLA_NOPE + MLA_ROPE)
    half = MLA_ROPE // 2
    return jnp.concatenate([w3[:, :, :MLA_NOPE].reshape(r, -1), w3[:, :, MLA_NOPE:MLA_NOPE + half].reshape(r, -1),
                            w3[:, :, MLA_NOPE + half:].reshape(r, -1)], axis=1)


def _unperm_uq(w):
    r = w.shape[0]
    half = MLA_ROPE // 2
    n0 = MLA_HEADS * MLA_NOPE
    n1 = n0 + MLA_HEADS * half
    return jnp.concatenate([w[:, :n0].reshape(r, MLA_HEADS, MLA_NOPE), w[:, n0:n1].reshape(r, MLA_HEADS, half),
                            w[:, n1:].reshape(r, MLA_HEADS, half)], axis=2).reshape(r, -1)


def _perm_ukv(w):
    r = w.shape[0]
    w3 = w.reshape(r, MLA_HEADS, MLA_NOPE + MLA_V)
    return jnp.concatenate([w3[:, :, :MLA_NOPE].reshape(r, -1), w3[:, :, MLA_NOPE:].reshape(r, -1)], axis=1)


def _unperm_ukv(w):
    r = w.shape[0]
    n0 = MLA_HEADS * MLA_NOPE
    return jnp.concatenate([w[:, :n0].reshape(r, MLA_HEADS, MLA_NOPE), w[:, n0:].reshape(r, MLA_HEADS, MLA_V)],
                           axis=2).reshape(r, -1)


_ODD_CUTS = np.cumsum([0, FOX_WIDTH, FOX_WIDTH, FOX_WIDTH, FOX_HEADS, MLA_Q_RANK, MLA_KV_RANK, MLA_ROPE]).tolist()
_ODD_ORDER = (0, 1, 2, 4, 5, 6, 3)


def _perm_odd_in(w):
    parts = [w[:, _ODD_CUTS[j]:_ODD_CUTS[j + 1]] for j in _ODD_ORDER]
    parts.append(jnp.zeros((w.shape[0], ODD_IN_PAD - ODD_IN), w.dtype))
    return jnp.concatenate(parts, axis=1)


def _unperm_odd_in(w):
    widths = [_ODD_CUTS[j + 1] - _ODD_CUTS[j] for j in _ODD_ORDER]
    offs = np.cumsum([0] + widths).tolist()
    pieces = {j: w[:, offs[n]:offs[n + 1]] for n, j in enumerate(_ODD_ORDER)}
    return jnp.concatenate([pieces[j] for j in range(7)], axis=1)


def _pad_cols(w, n):
    return jnp.concatenate([w, jnp.zeros((w.shape[0], n - w.shape[1]), w.dtype)], axis=1)


_BIG = (("even_w_in", 2), ("even_w_out", 1), ("odd_w_in", 2), ("w_uq", 2), ("w_ukv", 2), ("odd_w_out", 1),
        ("ffn_w_gate", 2), ("ffn_w_up", 2), ("ffn_w_down", 1))
_PACK_COLS = 1024


def _unshard(blocks, shp, ax):
    t = jnp.moveaxis(blocks.reshape((N_DEV,) + tuple(shp)), 0, ax)
    full = list(shp)
    full[ax] = shp[ax] * N_DEV
    return t.reshape(full)


def _reshard(full, ax, cols):
    shp = list(full.shape)
    t = full.reshape(shp[:ax] + [N_DEV, shp[ax] // N_DEV] + shp[ax + 1:])
    return jnp.moveaxis(t, ax, 0).reshape(N_DEV, -1, cols)


def kernel(x, even_w_in, pool_w, pool_scale, conv_w, conv_b, dt_bias, a_log, d_skip, ssm_norm_w, even_w_out, odd_w_in, fgate_b, q_norm_w, w_uq, kv_norm_w, w_ukv, odd_w_out, ffn_w_gate, ffn_w_up, ffn_w_down, ln_mix_g, ln_mix_b, ln_ffn_g, ln_ffn_b, loss_target, m_even_w_in, m_pool_w, m_pool_scale, m_conv_w, m_conv_b, m_dt_bias, m_a_log, m_d_skip, m_ssm_norm_w, m_even_w_out, m_odd_w_in, m_fgate_b, m_q_norm_w, m_w_uq, m_kv_norm_w, m_w_ukv, m_odd_w_out, m_ffn_w_gate, m_ffn_w_up, m_ffn_w_down, m_ln_mix_g, m_ln_mix_b, m_ln_ffn_g, m_ln_ffn_b, v_even_w_in, v_pool_w, v_pool_scale, v_conv_w, v_conv_b, v_dt_bias, v_a_log, v_d_skip, v_ssm_norm_w, v_even_w_out, v_odd_w_in, v_fgate_b, v_q_norm_w, v_w_uq, v_kv_norm_w, v_w_ukv, v_odd_w_out, v_ffn_w_gate, v_ffn_w_up, v_ffn_w_down, v_ln_mix_g, v_ln_mix_b, v_ln_ffn_g, v_ln_ffn_b):
    P = dict(even_w_in=even_w_in, pool_w=pool_w, pool_scale=pool_scale, conv_w=conv_w, conv_b=conv_b, dt_bias=dt_bias,
             a_log=a_log, d_skip=d_skip, ssm_norm_w=ssm_norm_w, even_w_out=even_w_out, odd_w_in=odd_w_in,
             fgate_b=fgate_b, q_norm_w=q_norm_w, w_uq=w_uq, kv_norm_w=kv_norm_w, w_ukv=w_ukv, odd_w_out=odd_w_out,
             ffn_w_gate=ffn_w_gate, ffn_w_up=ffn_w_up, ffn_w_down=ffn_w_down, ln_mix_g=ln_mix_g, ln_mix_b=ln_mix_b,
             ln_ffn_g=ln_ffn_g, ln_ffn_b=ln_ffn_b)
    M = dict(even_w_in=m_even_w_in, pool_w=m_pool_w, pool_scale=m_pool_scale, conv_w=m_conv_w, conv_b=m_conv_b,
             dt_bias=m_dt_bias, a_log=m_a_log, d_skip=m_d_skip, ssm_norm_w=m_ssm_norm_w, even_w_out=m_even_w_out,
             odd_w_in=m_odd_w_in, fgate_b=m_fgate_b, q_norm_w=m_q_norm_w, w_uq=m_w_uq, kv_norm_w=m_kv_norm_w,
             w_ukv=m_w_ukv, odd_w_out=m_odd_w_out, ffn_w_gate=m_ffn_w_gate, ffn_w_up=m_ffn_w_up,
             ffn_w_down=m_ffn_w_down, ln_mix_g=m_ln_mix_g, ln_mix_b=m_ln_mix_b, ln_ffn_g=m_ln_ffn_g,
             ln_ffn_b=m_ln_ffn_b)
    V = dict(even_w_in=v_even_w_in, pool_w=v_pool_w, pool_scale=v_pool_scale, conv_w=v_conv_w, conv_b=v_conv_b,
             dt_bias=v_dt_bias, a_log=v_a_log, d_skip=v_d_skip, ssm_norm_w=v_ssm_norm_w, even_w_out=v_even_w_out,
             odd_w_in=v_odd_w_in, fgate_b=v_fgate_b, q_norm_w=v_q_norm_w, w_uq=v_w_uq, kv_norm_w=v_kv_norm_w,
             w_ukv=v_w_ukv, odd_w_out=v_odd_w_out, ffn_w_gate=v_ffn_w_gate, ffn_w_up=v_ffn_w_up,
             ffn_w_down=v_ffn_w_down, ln_mix_g=v_ln_mix_g, ln_mix_b=v_ln_mix_b, ln_ffn_g=v_ln_ffn_g,
             ln_ffn_b=v_ln_ffn_b)
    names = list(P)
    s = x.shape[1]
    me = 4 * lax.axis_index("x") + 2 * lax.axis_index("y") + lax.axis_index("c")

    big_rows = [math.prod(P[n].shape) // _PACK_COLS for n, _ in _BIG]
    pad_rows = [-(-nr // 16) * 16 for nr in big_rows]
    rows_big = sum(pad_rows)
    packed = jnp.concatenate([
        jnp.pad(P[n].astype(BF16).reshape(-1), (0, (pr - nr) * _PACK_COLS))
        for (n, _), nr, pr in zip(_BIG, big_rows, pad_rows)]).reshape(rows_big, _PACK_COLS)
    gathered = _all_gather("ag_weights", packed)
    W = {}
    off = 0
    for (n, ax), nr, pr in zip(_BIG, big_rows, pad_rows):
        W[n] = _unshard(gathered[:, off:off + nr, :], P[n].shape, ax)
        off += pr
    kv_pad = jnp.zeros((128 - kv_norm_w.size,), F32)
    small_sh = jnp.concatenate([conv_w.reshape(-1), q_norm_w.reshape(-1), kv_norm_w.reshape(-1), kv_pad,
                                jnp.zeros((2 * 128,), F32)]).reshape(16, 128)
    g_small = _all_gather("ag_small_weights", small_sh)
    conv_w_full = _unshard(g_small[:, :12, :], conv_w.shape, 2)
    q_norm_full = _unshard(g_small[:, 12:13, :], q_norm_w.shape, 1)
    kv_norm_full = _unshard(g_small[:, 13:14, :64], kv_norm_w.shape, 1)

    w_in_e = [_pad_cols(W["even_w_in"][i], EVEN_IN_PAD) for i in range(2)]
    w_in_o = [_perm_odd_in(W["odd_w_in"][i]) for i in range(2)]
    w_uq_p = [_perm_uq(W["w_uq"][i]) for i in range(2)]
    w_ukv_p = [_perm_ukv(W["w_ukv"][i]) for i in range(2)]
    w_gu = [jnp.concatenate([W["ffn_w_gate"][l], W["ffn_w_up"][l]], axis=1) for l in range(DEPTH)]

    pos = jnp.arange(s, dtype=F32)
    half = MLA_ROPE // 2
    freqs = jnp.power(ROPE_THETA, -jnp.arange(half, dtype=F32) / half)
    ang = pos[:, None] * freqs[None, :]
    cos16, sin16 = jnp.cos(ang), jnp.sin(ang)
    cos128, sin128 = jnp.tile(cos16, (1, MLA_HEADS)), jnp.tile(sin16, (1, MLA_HEADS))
    row = lambda t: t.reshape(1, -1)

    xcur = x[0]
    xb = xcur.astype(BF16)
    saved = []
    for l in range(DEPTH):
        i = l // 2
        sv = dict(x_in_b=xb)
        if l % 2 == 0:
            proj = _mm("mm_in_even", xb, w_in_e[i], "nn", F32)
            u, z = proj[:, :512], proj[:, 512:1536]
            xbc, dtraw = proj[:, 1536:3072], proj[:, 3072:3088]
            ypool = _pool_fwd(u, pool_w[i], row(pool_scale[i]))
            xc = _conv_fwd(xbc, conv_w_full[i], row(conv_b[i]))
            dt = _softplus_fwd(dtraw, row(dt_bias[i]))
            xh = _heads(xc[:, :SSM_D_INNER], SSM_HEADS)
            dtc = dt.T[:, :, None]
            dtr = dt.T[:, None, :]
            bm, cm = xc[:, SSM_D_INNER:SSM_D_INNER + 256], xc[:, SSM_D_INNER + 256:]
            alog3, dsk3 = a_log[i].reshape(-1, 1, 1), d_skip[i].reshape(-1, 1, 1)
            yh, hprev = _ssd_fwd(xh, xh.transpose(0, 2, 1), dtc, dtr, bm, cm, alog3, dsk3)
            y_ssm = _unheads(yh)
            yn = _gated_rms_fwd(y_ssm, z, row(ssm_norm_w[i]))
            mix = jnp.concatenate([ypool, yn], axis=1)
            h = _mm("mm_out_even", mix, W["even_w_out"][i], "nn", F32)
            sv.update(u=u, z=z, xbc=xbc, dtraw=dtraw, xh=xh, dtc=dtc, dtr=dtr, bm=bm, cm=cm, hprev=hprev,
                      y_ssm=y_ssm, mix=mix)
        else:
            proj = _mm("mm_in_odd", xb, w_in_o[i], "nn", F32)
            qf, kf, vf = proj[:, :512], proj[:, 512:1024], proj[:, 1024:1536]
            cq, ckv = proj[:, 1536:2048], proj[:, 2048:2304]
            kr, fl = proj[:, 2304:2336], proj[:, 2336:2344]
            fl = fl.T
            fcum = _fgate_fwd(fl, fgate_b[i][:, None])
            fq_ = fcum[:, :, None]
            fk_ = fcum[:, None, :]
            qh, kh, vh = (_heads(t.astype(BF16), FOX_HEADS) for t in (qf * FOX_SCALE, kf, vf))
            o_fox, lse_fox = _flash_fwd("fox_fwd", qh, kh, vh, fq_, fk_, None)
            qn = _rms_fwd(cq, row(q_norm_full[i]))
            qp = _mm("mm_uq", qn, w_uq_p[i], "nn", F32)
            q1, q2 = _rope("rope_q", qp[:, 512:640], qp[:, 640:768], cos128, sin128)
            kvn = _rms_fwd(ckv, row(kv_norm_full[i]))
            kvp = _mm("mm_ukv", kvn, w_ukv_p[i], "nn", F32)
            k1, k2 = _rope("rope_k", kr[:, :half], kr[:, half:], cos16, sin16)
            zpad = jnp.zeros((MLA_HEADS, s, MLA_DK_PAD - MLA_NOPE - MLA_ROPE), BF16)
            qm = jnp.concatenate([_heads(qp[:, :512], MLA_HEADS), _heads(q1, MLA_HEADS), _heads(q2, MLA_HEADS)],
                                 axis=2).astype(BF16)
            qm = jnp.concatenate([qm, zpad], axis=2)
            krope = jnp.broadcast_to(jnp.concatenate([k1, k2], axis=1)[None], (MLA_HEADS, s, MLA_ROPE))
            km = jnp.concatenate([_heads(kvp[:, :512], MLA_HEADS), krope], axis=2).astype(BF16)
            km = jnp.concatenate([km, zpad], axis=2)
            vm = _heads(kvp[:, 512:], MLA_HEADS).astype(BF16)
            o_mla, lse_mla = _flash_fwd("mla_fwd", qm, km, vm, None, None, MLA_SCALE)
            mix = jnp.concatenate([_unheads(o_fox), _unheads(o_mla)], axis=1).astype(BF16)
            h = _mm("mm_out_odd", mix, W["odd_w_out"][i], "nn", F32)
            sv.update(fl=fl, fq=fq_, fk=fk_, qh=qh, kh=kh, vh=vh, o_fox=o_fox, lse_fox=lse_fox, cq=cq, ckv=ckv,
                      qn=qn, kvn=kvn, qm=qm, km=km, vm=vm, o_mla=o_mla, lse_mla=lse_mla, mix=mix)
        y1, y1b, r1 = _ln_fwd(xcur, h, row(ln_mix_g[l]), row(ln_mix_b[l]))
        gu = _mm("mm_ffn_in", y1b, w_gu[l], "nn", F32)
        act = _swiglu_fwd(gu)
        h2 = _mm("mm_ffn_out", act, W["ffn_w_down"][l], "nn", F32)
        y2, y2b, r2 = _ln_fwd(y1, h2, row(ln_ffn_g[l]), row(ln_ffn_b[l]))
        sv.update(r1=r1, y1b=y1b, gu=gu, act=act, r2=r2)
        saved.append(sv)
        xcur, xb = y2, y2b

    dy, loss_part = _loss_head(xcur, loss_target[0])
    loss = lax.psum(loss_part[0, 0], ("x", "y", "c"))

    G = {n: [None] * P[n].shape[0] for n in names}
    acur, dcur = None, dy
    for l in reversed(range(DEPTH)):
        i = l // 2
        sv = saved[l]
        dr2, dr2b, dg, db = _ln_bwd(acur, dcur, sv["r2"], row(ln_ffn_g[l]))
        G["ln_ffn_g"][l], G["ln_ffn_b"][l] = dg[0], db[0]
        G["ffn_w_down"][l] = _mm("mm_dw_ffn_out", sv["act"], dr2b, "tn", F32)
        dact = _mm("mm_dx_ffn_out", dr2b, W["ffn_w_down"][l], "nt", F32)
        dgu = _swiglu_bwd(dact, sv["gu"])
        dwgu = _mm("mm_dw_ffn_in", sv["y1b"], dgu, "tn", F32)
        G["ffn_w_gate"][l], G["ffn_w_up"][l] = dwgu[:, :D_FF], dwgu[:, D_FF:]
        dy1 = _mm("mm_dx_ffn_in", dgu, w_gu[l], "nt", F32)
        dr1, dr1b, dg, db = _ln_bwd(dr2, dy1, sv["r1"], row(ln_mix_g[l]))
        G["ln_mix_g"][l], G["ln_mix_b"][l] = dg[0], db[0]
        if l % 2 == 0:
            G["even_w_out"][i] = _mm("mm_dw_out_even", sv["mix"], dr1b, "tn", F32)
            dmix = _mm("mm_dx_out_even", dr1b, W["even_w_out"][i], "nt", F32)
            dd, dsc, dpw = _pool_bwd1(sv["u"], dmix[:, :POOL_WIDTH], pool_w[i], row(pool_scale[i]))
            G["pool_scale"][i], G["pool_w"][i] = dsc[0], dpw
            du = _pool_bwd2(dd)
            dys, dz, dnw = _gated_rms_bwd(dmix[:, POOL_WIDTH:], sv["y_ssm"], sv["z"], row(ssm_norm_w[i]))
            G["ssm_norm_w"][i] = dnw[0]
            dyh = _heads(dys, SSM_HEADS)
            alog3, dsk3 = a_log[i].reshape(-1, 1, 1), d_skip[i].reshape(-1, 1, 1)
            dxh, ddt3, dbm, dcm, dalog, ddsk = _ssd_bwd(dyh, dyh.transpose(0, 2, 1), sv["xh"], sv["dtc"], sv["dtr"],
                                                        sv["bm"], sv["cm"], sv["hprev"], alog3, dsk3)
            G["a_log"][i], G["d_skip"][i] = dalog.reshape(-1), ddsk.reshape(-1)
            ddtraw, ddtb = _softplus_bwd(ddt3[:, :, 0].T, sv["dtraw"], row(dt_bias[i]))
            G["dt_bias"][i] = ddtb[0]
            dxc = jnp.concatenate([_unheads(dxh), dbm, dcm], axis=1)
            dpre, dcw, dcb = _conv_bwd1(dxc, sv["xbc"], conv_w_full[i], row(conv_b[i]))
            G["conv_w"][i], G["conv_b"][i] = dcw, dcb[0]
            dxbc = _conv_bwd2(dpre, conv_w_full[i])
            dproj = jnp.concatenate([du, dz.astype(BF16), dxbc, ddtraw.astype(BF16),
                                     jnp.zeros((s, EVEN_IN_PAD - EVEN_IN), BF16)], axis=1)
            G["even_w_in"][i] = _mm("mm_dw_in_even", sv["x_in_b"], dproj, "tn", F32)[:, :EVEN_IN]
            dxb = _mm("mm_dx_in_even", dproj, w_in_e[i], "nt", F32)
        else:
            G["odd_w_out"][i] = _mm("mm_dw_out_odd", sv["mix"], dr1b, "tn", F32)
            dmix = _mm("mm_dx_out_odd", dr1b, W["odd_w_out"][i], "nt", F32)
            do_f = _heads(dmix[:, :FOX_WIDTH].astype(BF16), FOX_HEADS)
            dl_f = _rowdot(do_f.reshape(-1, 64), sv["o_fox"].reshape(-1, 64)).reshape(FOX_HEADS, s, 1)
            ones = jnp.ones((FOX_HEADS, s, 64), BF16)
            qg = jnp.concatenate([sv["qh"], ones], axis=2)
            kg = jnp.concatenate([sv["kh"], ones * (1.0 / FOX_SCALE)], axis=2)
            dqe, dke, dvh = _flash_bwd("fox_bwd", sv["qh"], sv["kh"], sv["vh"], do_f, sv["lse_fox"], dl_f,
                                       sv["fq"], sv["fk"], qg, kg, None, FOX_SCALE, None)
            dqh, dkh = dqe[:, :, :64], dke[:, :, :64]
            dfl, dfb = _fgate_bwd(dqe, dke, 64, sv["fl"], fgate_b[i][:, None])
            dfl = dfl.T
            G["fgate_b"][i] = dfb[:, 0]
            do_m = _heads(dmix[:, FOX_WIDTH:].astype(BF16), MLA_HEADS)
            dl_m = _rowdot(do_m.reshape(-1, 64), sv["o_mla"].reshape(-1, 64)).reshape(MLA_HEADS, s, 1)
            dqm, dkm, dvm = _flash_bwd("mla_bwd", sv["qm"], sv["km"], sv["vm"], do_m, sv["lse_mla"], dl_m,
                                       None, None, sv["qm"], sv["km"], MLA_SCALE, MLA_SCALE, MLA_SCALE)
            n0, n1 = MLA_NOPE, MLA_NOPE + half
            dq1, dq2 = _rope("rope_q_bwd", _unheads(dqm[:, :, n0:n1]), _unheads(dqm[:, :, n1:n1 + half]),
                             cos128, -sin128)
            dqp = jnp.concatenate([_unheads(dqm[:, :, :n0]), dq1, dq2], axis=1).astype(BF16)
            G["w_uq"][i] = _unperm_uq(_mm("mm_dw_uq", sv["qn"], dqp, "tn", F32))
            dqn = _mm("mm_dx_uq", dqp, w_uq_p[i], "nt", F32)
            dcq, dqw = _rms_bwd(dqn, sv["cq"], row(q_norm_full[i]))
            G["q_norm_w"][i] = dqw[0]
            dk1, dk2 = _headsum_rope_bwd(_unheads(dkm[:, :, n0:n1]), _unheads(dkm[:, :, n1:n1 + half]), cos16, sin16)
            dkvp = jnp.concatenate([_unheads(dkm[:, :, :n0]), _unheads(dvm)], axis=1).astype(BF16)
            G["w_ukv"][i] = _unperm_ukv(_mm("mm_dw_ukv", sv["kvn"], dkvp, "tn", F32))
            dkvn = _mm("mm_dx_ukv", dkvp, w_ukv_p[i], "nt", F32)
            dckv, dkvw = _rms_bwd(dkvn, sv["ckv"], row(kv_norm_full[i]))
            G["kv_norm_w"][i] = dkvw[0]
            dproj = jnp.concatenate([_unheads(dqh), _unheads(dkh), _unheads(dvh), dcq, dckv, dk1, dk2, dfl,
                                     jnp.zeros((s, ODD_IN_PAD - ODD_IN), F32)], axis=1).astype(BF16)
            G["odd_w_in"][i] = _unperm_odd_in(_mm("mm_dw_in_odd", sv["x_in_b"], dproj, "tn", F32))
            dxb = _mm("mm_dx_in_odd", dproj, w_in_o[i], "nt", F32)
        acur, dcur = dr1, dxb
    grad_x = _axpy(acur, dcur)[None]

    gfull = {n: jnp.stack(G[n]) for n in names}
    send = jnp.concatenate([
        jnp.pad(_reshard(gfull[n], ax, _PACK_COLS).astype(BF16), ((0, 0), (0, pr - nr), (0, 0)))
        for (n, ax), nr, pr in zip(_BIG, big_rows, pad_rows)], axis=1)
    recv = _all_to_all("a2a_grads", send)
    gsum = _sum8("sum_grads", recv)
    grads = {}
    off = 0
    for (n, ax), nr, pr in zip(_BIG, big_rows, pad_rows):
        grads[n] = gsum[off:off + nr].reshape(P[n].shape)
        off += pr
    small = [n for n in names if n not in dict(_BIG)]
    sflat = jnp.concatenate([gfull[n].reshape(-1) for n in small])
    n_small = sflat.shape[0]
    rows_small = -(-n_small // (128 * 8)) * 8
    sflat = jnp.concatenate([sflat, jnp.zeros((rows_small * 128 - n_small,), F32)])
    sg = _sum8("sum_small_grads", _all_gather("ag_small_grads", sflat.reshape(rows_small, 128))).reshape(-1)
    off = 0
    for n in small:
        cnt = gfull[n].size
        gf = sg[off:off + cnt].reshape(gfull[n].shape)
        off += cnt
        if gf.shape != P[n].shape:
            width = P[n].shape[-1]
            gf = lax.dynamic_slice_in_dim(gf, me * width, width, axis=gf.ndim - 1)
        grads[n] = gf

    delta, new_m, new_v = {}, {}, {}
    for n, _ in _BIG:
        shp = P[n].shape
        two = lambda t: t.reshape(-1, shp[-1])
        d_, m_, v_ = _adamw("adamw_" + n, two(P[n]), two(grads[n]), two(M[n]), two(V[n]))
        delta[n], new_m[n], new_v[n] = d_.reshape(shp), m_.reshape(shp), v_.reshape(shp)

    def packs(d):
        f = jnp.concatenate([d[n].reshape(-1) for n in small])
        pad = -(-f.shape[0] // (128 * 8)) * 8 * 128 - f.shape[0]
        return jnp.concatenate([f, jnp.zeros((pad,), F32)]).reshape(-1, 128)

    d_, m_, v_ = _adamw("adamw_small", packs(P), packs(grads), packs(M), packs(V))
    off = 0
    for n in small:
        cnt = P[n].size
        for dst, src in ((delta, d_), (new_m, m_), (new_v, v_)):
            dst[n] = src.reshape(-1)[off:off + cnt].reshape(P[n].shape)
        off += cnt

    return (loss, grad_x, *[grads[n] for n in names], *[delta[n] for n in names],
            *[new_m[n] for n in names], *[new_v[n] for n in names])
```

```python
import functools
import math

import jax
import jax.numpy as jnp
import numpy as np
from jax import lax
from jax.experimental import pallas as pl
from jax.experimental.pallas import tpu as pltpu

F32 = jnp.float32
BF16 = jnp.bfloat16
HI = lax.Precision.HIGHEST

N_DEV = 8
D_MODEL = 1024
DEPTH = 4
POOL_WINDOWS = (2, 4, 8, 16)
POOL_GROUP = 128
POOL_WIDTH = 512
SSM_D_INNER = 1024
SSM_HEAD_DIM = 64
SSM_HEADS = 16
SSM_GROUPS = 2
SSM_STATE = 128
SSM_CONV = 4
SSM_CHUNK = 128
SSM_CONV_DIM = 1536
EVEN_IN = 3088
EVEN_IN_PAD = 3200
FOX_HEADS = 8
FOX_WIDTH = 512
MLA_HEADS = 8
MLA_NOPE = 64
MLA_ROPE = 32
MLA_V = 64
MLA_Q_RANK = 512
MLA_KV_RANK = 256
MLA_DK_PAD = 128
ROPE_THETA = 10000.0
FOX_SCALE = 64 ** -0.5
MLA_SCALE = (MLA_NOPE + MLA_ROPE) ** -0.5
ODD_IN = 2344
ODD_IN_PAD = 2560
D_FF = 2816
ALPHA = (2 * DEPTH) ** 0.25
LN_EPS = 1e-5
RMS_EPS = 1e-6
ADAM_LR = 0.001
ADAM_B1 = 0.9
ADAM_B2 = 0.999
ADAM_EPS = 1e-08
ADAM_WD = 0.01
ADAM_STEP = 10
NEG = -1e30
VMEM_LIMIT = 48 * 1024 * 1024


def _cp(sem):
    return pltpu.CompilerParams(dimension_semantics=sem, vmem_limit_bytes=VMEM_LIMIT)


def _dot(a, b, ca, cb, prec=None):
    return lax.dot_general(a, b, (((ca,), (cb,)), ((), ())), preferred_element_type=F32, precision=prec)


def _sigmoid(x):
    return 1.0 / (1.0 + jnp.exp(-x))


def _softplus(x):
    return jnp.maximum(x, 0.0) + jnp.log(1.0 + jnp.exp(-jnp.abs(x)))


MESH = pl.DeviceIdType.MESH
HBM_SPEC = pl.BlockSpec(memory_space=pltpu.HBM)


def _all_gather(name, xs):
    r, c_ = xs.shape

    def body(x_ref, out_ref, send_sems, recv_sems, local_sem):
        x, y, c = lax.axis_index("x"), lax.axis_index("y"), lax.axis_index("c")
        me, sibling = (x, y, c), (x, y, 1 - c)
        chips = [(1 - x, y), (x, 1 - y), (1 - x, 1 - y)]

        def rows(px, py, pc):
            return out_ref.at[4 * px + 2 * py + pc]

        def copy(k, block, to, src=None):
            return pltpu.make_async_remote_copy(
                src_ref=rows(*block) if src is None else src,
                dst_ref=rows(*block),
                send_sem=send_sems.at[k],
                recv_sem=recv_sems.at[k],
                device_id=to,
                device_id_type=MESH,
            )

        mine = pltpu.make_async_copy(x_ref, rows(*me), local_sem)
        mine.start()
        first = [copy(0, me, sibling, src=x_ref)]
        first += [copy(1 + j, me, (*chip, c), src=x_ref) for j, chip in enumerate(chips)]
        for cp in first:
            cp.start()
        passed = [copy(4 + j, (*chip, c), sibling) for j, chip in enumerate(chips)]
        for j, chip in enumerate(chips):
            copy(1 + j, (*chip, c), me).wait_recv()
            passed[j].start()
        copy(0, sibling, me).wait_recv()
        for j, chip in enumerate(chips):
            copy(4 + j, (*chip, 1 - c), me).wait_recv()
        for cp in first + passed:
            cp.wait_send()
        mine.wait()

    return pl.pallas_call(
        body,
        name=name,
        out_shape=jax.ShapeDtypeStruct((N_DEV, r, c_), xs.dtype),
        in_specs=[HBM_SPEC],
        out_specs=HBM_SPEC,
        scratch_shapes=[pltpu.SemaphoreType.DMA((7,)), pltpu.SemaphoreType.DMA((7,)), pltpu.SemaphoreType.DMA(())],
    )(xs)


def _all_to_all(name, send):
    _, r, c_ = send.shape

    def body(s_ref, r_ref, send_sems, recv_sems, local_sem):
        x, y, c = lax.axis_index("x"), lax.axis_index("y"), lax.axis_index("c")
        me = 4 * x + 2 * y + c
        mine = pltpu.make_async_copy(s_ref.at[me], r_ref.at[me], local_sem)
        mine.start()
        copies = []
        for k in range(1, N_DEV):
            tx = 1 - x if k & 4 else x
            ty = 1 - y if k & 2 else y
            tc = 1 - c if k & 1 else c
            peer = 4 * tx + 2 * ty + tc
            cp = pltpu.make_async_remote_copy(
                src_ref=s_ref.at[peer],
                dst_ref=r_ref.at[me],
                send_sem=send_sems.at[k - 1],
                recv_sem=recv_sems.at[k - 1],
                device_id=(tx, ty, tc),
                device_id_type=MESH,
            )
            cp.start()
            landing = pltpu.make_async_remote_copy(
                src_ref=s_ref.at[me],
                dst_ref=r_ref.at[peer],
                send_sem=send_sems.at[k - 1],
                recv_sem=recv_sems.at[k - 1],
                device_id=(tx, ty, tc),
                device_id_type=MESH,
            )
            copies.append((cp, landing))
        for cp, landing in copies:
            landing.wait_recv()
        for cp, landing in copies:
            cp.wait_send()
        mine.wait()

    return pl.pallas_call(
        body,
        name=name,
        out_shape=jax.ShapeDtypeStruct(send.shape, send.dtype),
        in_specs=[HBM_SPEC],
        out_specs=HBM_SPEC,
        scratch_shapes=[pltpu.SemaphoreType.DMA((7,)), pltpu.SemaphoreType.DMA((7,)), pltpu.SemaphoreType.DMA(())],
    )(send)


def _pick(n, cands):
    for t in cands:
        if n % t == 0:
            return t
    return n


def _mm(name, a, b, mode, out_dtype):
    if mode == "nn":
        (m, k), n = a.shape, b.shape[1]
    elif mode == "nt":
        (m, k), n = a.shape, b.shape[0]
    else:
        (k, m), n = a.shape, b.shape[1]
    tm = _pick(m, (512, 256, 128))
    tn = _pick(n, (1408, 1280, 1024, 768, 640, 512, 384, 256, 128))
    tk = _pick(k, (1024, 1408, 768, 640, 512, 256, 128))
    nk = k // tk
    swap = nk == 1 and a.size * a.dtype.itemsize * (n // tn) + b.size * b.dtype.itemsize < (
        a.size * a.dtype.itemsize + b.size * b.dtype.itemsize * (m // tm))
    ij = (lambda g0, g1: (g1, g0)) if swap else (lambda g0, g1: (g0, g1))
    if mode == "nn":
        a_spec = pl.BlockSpec((tm, tk), lambda g0, g1, kk: (ij(g0, g1)[0], kk))
        b_spec = pl.BlockSpec((tk, tn), lambda g0, g1, kk: (kk, ij(g0, g1)[1]))
        ca, cb = 1, 0
    elif mode == "nt":
        a_spec = pl.BlockSpec((tm, tk), lambda g0, g1, kk: (ij(g0, g1)[0], kk))
        b_spec = pl.BlockSpec((tn, tk), lambda g0, g1, kk: (ij(g0, g1)[1], kk))
        ca, cb = 1, 1
    else:
        a_spec = pl.BlockSpec((tk, tm), lambda g0, g1, kk: (kk, ij(g0, g1)[0]))
        b_spec = pl.BlockSpec((tk, tn), lambda g0, g1, kk: (kk, ij(g0, g1)[1]))
        ca, cb = 0, 0

    def kern(a_ref, b_ref, o_ref, acc):
        kk = pl.program_id(2)

        @pl.when(kk == 0)
        def _():
            acc[...] = jnp.zeros_like(acc)

        acc[...] += _dot(a_ref[...].astype(BF16), b_ref[...].astype(BF16), ca, cb)

        @pl.when(kk == nk - 1)
        def _():
            o_ref[...] = acc[...].astype(out_dtype)

    return pl.pallas_call(
        kern,
        name=name,
        grid=(n // tn, m // tm, nk) if swap else (m // tm, n // tn, nk),
        in_specs=[a_spec, b_spec],
        out_specs=pl.BlockSpec((tm, tn), lambda g0, g1, kk: ij(g0, g1)),
        out_shape=jax.ShapeDtypeStruct((m, n), out_dtype),
        scratch_shapes=[pltpu.VMEM((tm, tn), F32)],
        compiler_params=_cp(("parallel", "parallel", "arbitrary")),
    )(a, b)


def _rowwise(name, body, ins, outs, tm):
    n_rows = next(a.shape[0] for a, kind in ins if kind == "row")
    n = n_rows // tm
    in_specs = []
    for a, kind in ins:
        if kind == "full":
            in_specs.append(pl.BlockSpec(a.shape, lambda i, nd=a.ndim: (0,) * nd))
        elif kind == "row":
            in_specs.append(pl.BlockSpec((tm, a.shape[1]), lambda i: (i, 0)))
        elif kind == "prev":
            in_specs.append(pl.BlockSpec((tm, a.shape[1]), lambda i: (jnp.maximum(i - 1, 0), 0)))
        else:
            in_specs.append(pl.BlockSpec((tm, a.shape[1]), lambda i: (jnp.minimum(i + 1, n - 1), 0)))
    out_specs, out_shape = [], []
    for shp, dt, kind in outs:
        out_shape.append(jax.ShapeDtypeStruct(shp, dt))
        if kind == "row":
            out_specs.append(pl.BlockSpec((tm, shp[1]), lambda i: (i, 0)))
        else:
            out_specs.append(pl.BlockSpec(shp, lambda i, nd=len(shp): (0,) * nd))
    n_in = len(ins)

    def kern(*refs):
        i = pl.program_id(0)
        res = body(i, n, *[r if kind == "full" else r[...] for r, (_, kind) in zip(refs[:n_in], ins)])
        for (shp, dt, kind), val, o in zip(outs, res, refs[n_in:]):
            if kind == "row":
                o[...] = val.astype(dt)
            else:

                @pl.when(i == 0)
                def _(o=o):
                    o[...] = jnp.zeros_like(o)

                o[...] += val.astype(dt)

    return pl.pallas_call(
        kern,
        name=name,
        grid=(n,),
        in_specs=in_specs,
        out_specs=out_specs,
        out_shape=out_shape,
        compiler_params=_cp(("arbitrary",)),
    )(*[a for a, _ in ins])


def _tm(s, t):
    return min(s, t)


def _ln_fwd(x, h, g, b):
    s = x.shape[0]

    def body(i, n, xv, hv, gv, bv):
        r = ALPHA * xv + hv.astype(F32)
        mu = jnp.mean(r, axis=-1, keepdims=True)
        d = r - mu
        var = jnp.mean(d * d, axis=-1, keepdims=True)
        y = d * lax.rsqrt(var + LN_EPS) * gv[...] + bv[...]
        return y, y, r

    shp = (s, D_MODEL)
    return _rowwise("ln_fwd", body, [(x, "row"), (h, "row"), (g, "full"), (b, "full")],
                    [(shp, F32, "row"), (shp, BF16, "row"), (shp, F32, "row")], _tm(s, 256))


def _ln_bwd(a, bterm, r, g):
    s = r.shape[0]

    def body(i, n, *vals):
        if a is None:
            dyv, rv, gv = vals
        else:
            av, dyv, rv, gv = vals
            dyv = ALPHA * av + dyv
        mu = jnp.mean(rv, axis=-1, keepdims=True)
        d = rv - mu
        var = jnp.mean(d * d, axis=-1, keepdims=True)
        rstd = lax.rsqrt(var + LN_EPS)
        xhat = d * rstd
        dxh = dyv * gv[...]
        dr = rstd * (dxh - jnp.mean(dxh, axis=-1, keepdims=True) - xhat * jnp.mean(dxh * xhat, axis=-1, keepdims=True))
        return dr, dr, jnp.sum(dyv * xhat, axis=0, keepdims=True), jnp.sum(dyv, axis=0, keepdims=True)

    ins = ([] if a is None else [(a, "row")]) + [(bterm, "row"), (r, "row"), (g, "full")]
    shp = (s, D_MODEL)
    return _rowwise("ln_bwd" if a is None else "ln_bwd_res", body, ins,
                    [(shp, F32, "row"), (shp, BF16, "row"), ((1, D_MODEL), F32, "acc"), ((1, D_MODEL), F32, "acc")],
                    _tm(s, 256))


def _axpy(a, b):
    def body(i, n, av, bv):
        return (ALPHA * av + bv,)

    return _rowwise("axpy", body, [(a, "row"), (b, "row")], [(a.shape, F32, "row")], _tm(a.shape[0], 256))[0]


def _loss_head(y, target):
    s = y.shape[0]

    def body(i, n, yv, tv):
        err = yv - tv
        part = 0.5 * jnp.sum(jnp.mean(err * err, axis=-1, keepdims=True), axis=0, keepdims=True)
        return err * (1.0 / D_MODEL), part

    return _rowwise("loss_head", body, [(y, "row"), (target, "row")],
                    [((s, D_MODEL), F32, "row"), ((1, 1), F32, "acc")], _tm(s, 256))


def _swiglu_fwd(gu):
    s = gu.shape[0]

    def body(i, n, v):
        g, u = v[:, :D_FF].astype(F32), v[:, D_FF:].astype(F32)
        return (g * _sigmoid(g) * u,)

    return _rowwise("swiglu_fwd", body, [(gu, "row")], [((s, D_FF), BF16, "row")], _tm(s, 256))[0]


def _swiglu_bwd(da, gu):
    s = gu.shape[0]

    def body(i, n, dav, v):
        dav = dav.astype(F32)
        g, u = v[:, :D_FF].astype(F32), v[:, D_FF:].astype(F32)
        sg = _sigmoid(g)
        dg = dav * u * (sg * (1.0 + g * (1.0 - sg)))
        du = dav * (g * sg)
        return (jnp.concatenate([dg, du], axis=1),)

    return _rowwise("swiglu_bwd", body, [(da, "row"), (gu, "row")], [((s, 2 * D_FF), BF16, "row")], _tm(s, 256))[0]


def _rms_fwd(x, w):
    s, c = x.shape

    def body(i, n, xv, wv):
        rs = lax.rsqrt(jnp.mean(xv * xv, axis=-1, keepdims=True) + RMS_EPS)
        return (xv * rs * wv[...],)

    return _rowwise("rms_fwd", body, [(x, "row"), (w, "full")], [((s, c), BF16, "row")], _tm(s, 512))[0]


def _rms_bwd(dy, x, w):
    s, c = x.shape

    def body(i, n, dyv, xv, wv):
        rs = lax.rsqrt(jnp.mean(xv * xv, axis=-1, keepdims=True) + RMS_EPS)
        nv = xv * rs
        dn = dyv * wv[...]
        dx = rs * (dn - nv * jnp.mean(dn * nv, axis=-1, keepdims=True))
        return dx, jnp.sum(dyv * nv, axis=0, keepdims=True)

    return _rowwise("rms_bwd", body, [(dy, "row"), (x, "row"), (w, "full")],
                    [((s, c), F32, "row"), ((1, c), F32, "acc")], _tm(s, 512))


def _gated_rms_fwd(y, z, w):
    s, c = y.shape

    def body(i, n, yv, zv, wv):
        y2 = yv * (zv * _sigmoid(zv))
        rs = lax.rsqrt(jnp.mean(y2 * y2, axis=-1, keepdims=True) + RMS_EPS)
        return (y2 * rs * wv[...],)

    return _rowwise("gated_rms_fwd", body, [(y, "row"), (z, "row"), (w, "full")], [((s, c), BF16, "row")], _tm(s, 256))[0]


def _gated_rms_bwd(do, y, z, w):
    s, c = y.shape

    def body(i, n, dov, yv, zv, wv):
        sz = _sigmoid(zv)
        silu = zv * sz
        y2 = yv * silu
        rs = lax.rsqrt(jnp.mean(y2 * y2, axis=-1, keepdims=True) + RMS_EPS)
        nv = y2 * rs
        dn = dov * wv[...]
        dy2 = rs * (dn - nv * jnp.mean(dn * nv, axis=-1, keepdims=True))
        return dy2 * silu, dy2 * yv * (sz * (1.0 + zv * (1.0 - sz))), jnp.sum(dov * nv, axis=0, keepdims=True)

    return _rowwise("gated_rms_bwd", body, [(do, "row"), (y, "row"), (z, "row"), (w, "full")],
                    [((s, c), F32, "row"), ((s, c), F32, "row"), ((1, c), F32, "acc")], _tm(s, 256))


def _rope(name, r1, r2, cos, sin):
    def body(i, n, a, b, cv, sv):
        return a * cv - b * sv, b * cv + a * sv

    return _rowwise(name, body, [(r1, "row"), (r2, "row"), (cos, "row"), (sin, "row")],
                    [(r1.shape, F32, "row"), (r1.shape, F32, "row")], _tm(r1.shape[0], 512))


def _headsum_rope_bwd(d1_all, d2_all, cos, sin):
    s = d1_all.shape[0]
    half = MLA_ROPE // 2

    def body(i, n, a_all, b_all, cv, sv):
        rr = lax.broadcasted_iota(jnp.int32, (MLA_HEADS * half, half), 0)
        cc = lax.broadcasted_iota(jnp.int32, (MLA_HEADS * half, half), 1)
        sel = jnp.where(rr % half == cc, 1.0, 0.0).astype(F32)
        a = _dot(a_all, sel, 1, 0, HI)
        b = _dot(b_all, sel, 1, 0, HI)
        return a * cv + b * sv, b * cv - a * sv

    return _rowwise("headsum_rope_bwd", body, [(d1_all, "row"), (d2_all, "row"), (cos, "row"), (sin, "row")],
                    [((s, half), F32, "row"), ((s, half), F32, "row")], _tm(s, 512))


def _softplus_fwd(dtr, bias):
    def body(i, n, v, bv):
        return (_softplus(v + bv[...]),)

    return _rowwise("softplus_fwd", body, [(dtr, "row"), (bias, "full")], [(dtr.shape, F32, "row")], _tm(dtr.shape[0], 1024))[0]


def _softplus_bwd(ddt, dtr, bias):
    def body(i, n, dv, v, bv):
        d = dv * _sigmoid(v + bv[...])
        return d, jnp.sum(d, axis=0, keepdims=True)

    return _rowwise("softplus_bwd", body, [(ddt, "row"), (dtr, "row"), (bias, "full")],
                    [(dtr.shape, F32, "row"), ((1, dtr.shape[1]), F32, "acc")], _tm(dtr.shape[0], 1024))


def _rowdot(a, b):
    def body(i, n, av, bv):
        return (jnp.sum(av.astype(F32) * bv.astype(F32), axis=-1, keepdims=True),)

    return _rowwise("rowdot", body, [(a, "row"), (b, "row")], [((a.shape[0], 1), F32, "row")], _tm(a.shape[0], 2048))[0]


def _sum8(name, blocks):
    _, r, c = blocks.shape
    tr = _pick(r, (512, 256, 128, 64, 32, 16, 8))

    def kern(b_ref, o_ref):
        acc = b_ref[0].astype(F32)
        for d in range(1, N_DEV):
            acc = acc + b_ref[d].astype(F32)
        o_ref[...] = acc

    return pl.pallas_call(
        kern,
        name=name,
        grid=(r // tr,),
        in_specs=[pl.BlockSpec((N_DEV, tr, c), lambda i: (0, i, 0))],
        out_specs=pl.BlockSpec((tr, c), lambda i: (i, 0)),
        out_shape=jax.ShapeDtypeStruct((r, c), F32),
        compiler_params=_cp(("parallel",)),
    )(blocks)


def _adamw(name, w, g, m, v):
    r = w.shape[0]
    tm = _pick(r, (512, 256, 128, 64, 32, 16, 8))

    def body(i, n, wv, gv, mv, vv):
        m2 = ADAM_B1 * mv + (1.0 - ADAM_B1) * gv
        v2 = ADAM_B2 * vv + (1.0 - ADAM_B2) * (gv * gv)
        m_hat = m2 / (1.0 - ADAM_B1 ** ADAM_STEP)
        v_hat = v2 / (1.0 - ADAM_B2 ** ADAM_STEP)
        delta = -ADAM_LR * (m_hat / (jnp.sqrt(v_hat) + ADAM_EPS) + ADAM_WD * wv)
        return delta, m2, v2

    return _rowwise(name, body, [(w, "row"), (g, "row"), (m, "row"), (v, "row")],
                    [(w.shape, F32, "row")] * 3, tm)


def _pool_diff(i, t_rows, u, up):
    ucat = jnp.concatenate([jnp.where(i > 0, up, 0.0), u], axis=0)
    r = lax.broadcasted_iota(jnp.int32, (t_rows, 2 * t_rows), 0)
    cc = lax.broadcasted_iota(jnp.int32, (t_rows, 2 * t_rows), 1)
    lag = r + t_rows - cc
    t_col = i * t_rows + lax.broadcasted_iota(jnp.int32, (t_rows, 1), 0)
    diffs = []
    for gi, wdw in enumerate(POOL_WINDOWS):
        win = jnp.where((lag >= 0) & (lag < wdw), 1.0, 0.0).astype(F32)
        cnt = jnp.minimum(t_col + 1, wdw).astype(F32)
        ws = _dot(win, ucat[:, gi * POOL_GROUP:(gi + 1) * POOL_GROUP], 1, 0, HI)
        diffs.append(ws / cnt - u[:, gi * POOL_GROUP:(gi + 1) * POOL_GROUP])
    return diffs


def _pool_fwd(u, pw, sc):
    s = u.shape[0]
    tm = _tm(s, 256)

    def body(i, n, uv, upv, wv, scv):
        diffs = _pool_diff(i, tm, uv, upv)
        ys = [_dot(d.astype(BF16), wv[gi].astype(BF16), 1, 0) for gi, d in enumerate(diffs)]
        return (jnp.concatenate(ys, axis=1) * scv[...],)

    return _rowwise("pool_fwd", body, [(u, "row"), (u, "prev"), (pw, "full"), (sc, "full")],
                    [((s, POOL_WIDTH), BF16, "row")], tm)[0]


def _pool_bwd1(u, dy, pw, sc):
    s = u.shape[0]
    tm = _tm(s, 256)

    def body(i, n, uv, upv, dyv, wv, scv):
        diffs = _pool_diff(i, tm, uv, upv)
        scv = scv[...]
        dsc, dws, dds = [], [], []
        for gi, d in enumerate(diffs):
            sl = slice(gi * POOL_GROUP, (gi + 1) * POOL_GROUP)
            db, wb = d.astype(BF16), wv[gi].astype(BF16)
            yg = _dot(db, wb, 1, 0)
            dsc.append(jnp.sum(yg * dyv[:, sl], axis=0, keepdims=True))
            eb = (dyv[:, sl] * scv[:, sl]).astype(BF16)
            dws.append(_dot(db, eb, 0, 0)[None])
            dds.append(_dot(eb, wb, 1, 1))
        return jnp.concatenate(dds, axis=1), jnp.concatenate(dsc, axis=1), jnp.concatenate(dws, axis=0)

    return _rowwise("pool_bwd1", body, [(u, "row"), (u, "prev"), (dy, "row"), (pw, "full"), (sc, "full")],
                    [((s, POOL_WIDTH), F32, "row"), ((1, POOL_WIDTH), F32, "acc"), (pw.shape, F32, "acc")], tm)


def _pool_bwd2(dd):
    s = dd.shape[0]
    tm = _tm(s, 256)

    def body(i, n, dv, dnv):
        dcat = jnp.concatenate([dv, jnp.where(i < n - 1, dnv, 0.0)], axis=0)
        r = lax.broadcasted_iota(jnp.int32, (tm, 2 * tm), 0)
        cc = lax.broadcasted_iota(jnp.int32, (tm, 2 * tm), 1)
        lag = cc - r
        t_col = i * tm + lax.broadcasted_iota(jnp.int32, (2 * tm, 1), 0)
        outs = []
        for gi, wdw in enumerate(POOL_WINDOWS):
            sl = slice(gi * POOL_GROUP, (gi + 1) * POOL_GROUP)
            win = jnp.where((lag >= 0) & (lag < wdw), 1.0, 0.0).astype(F32)
            cnt = jnp.minimum(t_col + 1, wdw).astype(F32)
            outs.append(_dot(win, dcat[:, sl] / cnt, 1, 0, HI) - dv[:, sl])
        return (jnp.concatenate(outs, axis=1),)

    return _rowwise("pool_bwd2", body, [(dd, "row"), (dd, "next")], [((s, POOL_WIDTH), BF16, "row")], tm)[0]


def _shift_down(cur, other, j, tm):
    if j == 0:
        return cur
    rows = lax.broadcasted_iota(jnp.int32, cur.shape, 0)
    return jnp.where(rows < j, pltpu.roll(other, j, 0), pltpu.roll(cur, j, 0))


def _shift_up(cur, other, j, tm):
    if j == 0:
        return cur
    rows = lax.broadcasted_iota(jnp.int32, cur.shape, 0)
    return jnp.where(rows >= tm - j, pltpu.roll(other, tm - j, 0), pltpu.roll(cur, tm - j, 0))


def _conv_pre(i, tm, xv, xpv, wv, bv):
    xpv = jnp.where(i > 0, xpv, 0.0)
    taps = [_shift_down(xv, xpv, SSM_CONV - 1 - k, tm) for k in range(SSM_CONV)]
    pre = bv[...]
    for k in range(SSM_CONV):
        pre = pre + wv[k:k + 1, :] * taps[k]
    return pre, taps


def _conv_fwd(xbc, w, b):
    s = xbc.shape[0]
    tm = _tm(s, 256)

    def body(i, n, xv, xpv, wv, bv):
        pre, _ = _conv_pre(i, tm, xv, xpv, wv, bv)
        return (pre * _sigmoid(pre),)

    return _rowwise("conv_fwd", body, [(xbc, "row"), (xbc, "prev"), (w, "full"), (b, "full")],
                    [(xbc.shape, F32, "row")], tm)[0]


def _conv_bwd1(dxc, xbc, w, b):
    s, c = xbc.shape
    tm = _tm(s, 256)

    def body(i, n, dv, xv, xpv, wv, bv):
        pre, taps = _conv_pre(i, tm, xv, xpv, wv, bv)
        sg = _sigmoid(pre)
        dpre = dv * (sg * (1.0 + pre * (1.0 - sg)))
        tap_row = lax.broadcasted_iota(jnp.int32, (SSM_CONV, c), 0)
        dw = jnp.zeros((SSM_CONV, c), F32)
        for k in range(SSM_CONV):
            dw = dw + jnp.where(tap_row == k, jnp.sum(dpre * taps[k], axis=0, keepdims=True), 0.0)
        return dpre, dw, jnp.sum(dpre, axis=0, keepdims=True)

    return _rowwise("conv_bwd1", body, [(dxc, "row"), (xbc, "row"), (xbc, "prev"), (w, "full"), (b, "full")],
                    [((s, c), F32, "row"), ((SSM_CONV, c), F32, "acc"), ((1, c), F32, "acc")], tm)


def _conv_bwd2(dpre, w):
    s, c = dpre.shape
    tm = _tm(s, 256)

    def body(i, n, dv, dnv, wv):
        dnv = jnp.where(i < n - 1, dnv, 0.0)
        out = jnp.zeros_like(dv)
        for k in range(SSM_CONV):
            out = out + wv[k:k + 1, :] * _shift_up(dv, dnv, SSM_CONV - 1 - k, tm)
        return (out,)

    return _rowwise("conv_bwd2", body, [(dpre, "row"), (dpre, "next"), (w, "full")], [((s, c), BF16, "row")], tm)[0]


def _ssd_common(alog, dt_c, dt_r, tril, tri):
    a = -jnp.exp(alog)
    acs_c = _dot(tril, dt_c * a, 1, 0, HI)
    acs_r = _dot(dt_r * a, tril, 1, 1, HI)
    a_last = jnp.sum(dt_c * a, axis=0, keepdims=True)
    return a, acs_c, acs_r, a_last


def _ssd_fwd(xh, xht, dtc, dtr, bm, cm, alog, dsk):
    nh, s, p = xh.shape
    L = SSM_CHUNK
    nc = s // L
    E = nh // SSM_GROUPS

    def kern(xh_ref, xht_ref, dtc_ref, dtr_ref, b_ref, c_ref, alog_ref, dsk_ref, y_ref, hp_ref, h_s):
        c = pl.program_id(1)

        @pl.when(c == 0)
        def _():
            h_s[...] = jnp.zeros_like(h_s)

        bb = b_ref[...].astype(BF16)
        cb_ = c_ref[...].astype(BF16)
        cbm = _dot(cb_, bb, 1, 1)
        ri = lax.broadcasted_iota(jnp.int32, (L, L), 0)
        ci = lax.broadcasted_iota(jnp.int32, (L, L), 1)
        tri = ri >= ci
        tril = tri.astype(F32)
        hs = range(E)
        com = [_ssd_common(alog_ref[e], dtc_ref[e], dtr_ref[e], tril, tri) for e in hs]
        lam = [jnp.exp(jnp.where(tri, com[e][1] - com[e][2], NEG)) for e in hs]
        xs = [xh_ref[e] for e in hs]
        xdt = [(xs[e] * dtc_ref[e]).astype(BF16) for e in hs]
        hh = [h_s[e] for e in hs]
        y_diag = [_dot((cbm * lam[e]).astype(BF16), xdt[e], 1, 0) for e in hs]
        y_off = [_dot(cb_, hh[e].astype(BF16), 1, 1) for e in hs]
        xw = [(xht_ref[e] * (dtr_ref[e] * jnp.exp(com[e][3] - com[e][2]))).astype(BF16) for e in hs]
        st = [_dot(xw[e], bb, 1, 0) for e in hs]
        for e in hs:
            y_ref[e] = y_diag[e] + y_off[e] * jnp.exp(com[e][1]) + xs[e] * dsk_ref[e]
            hp_ref[e] = hh[e]
            h_s[e] = hh[e] * jnp.exp(com[e][3]) + st[e]

    return pl.pallas_call(
        kern,
        name="ssd_fwd",
        grid=(SSM_GROUPS, nc),
        in_specs=[
            pl.BlockSpec((E, L, p), lambda g, c: (g, c, 0)),
            pl.BlockSpec((E, p, L), lambda g, c: (g, 0, c)),
            pl.BlockSpec((E, L, 1), lambda g, c: (g, c, 0)),
            pl.BlockSpec((E, 1, L), lambda g, c: (g, 0, c)),
            pl.BlockSpec((L, SSM_STATE), lambda g, c: (c, g)),
            pl.BlockSpec((L, SSM_STATE), lambda g, c: (c, g)),
            pl.BlockSpec((E, 1, 1), lambda g, c: (g, 0, 0)),
            pl.BlockSpec((E, 1, 1), lambda g, c: (g, 0, 0)),
        ],
        out_specs=[
            pl.BlockSpec((E, L, p), lambda g, c: (g, c, 0)),
            pl.BlockSpec((E, None, p, SSM_STATE), lambda g, c: (g, c, 0, 0)),
        ],
        out_shape=[jax.ShapeDtypeStruct((nh, s, p), F32), jax.ShapeDtypeStruct((nh, nc, p, SSM_STATE), F32)],
        scratch_shapes=[pltpu.VMEM((E, p, SSM_STATE), F32)],
        compiler_params=_cp(("arbitrary", "arbitrary")),
    )(xh, xht, dtc, dtr, bm, cm, alog, dsk)


def _ssd_bwd(dy, dyt, xh, dtc, dtr, bm, cm, hprev, alog, dsk):
    nh, s, p = xh.shape
    L = SSM_CHUNK
    nc = s // L
    E = nh // SSM_GROUPS

    def kern(dy_ref, dyt_ref, xh_ref, dtc_ref, dtr_ref, b_ref, c_ref, hp_ref, alog_ref, dsk_ref,
             dx_ref, ddt_ref, db_ref, dc_ref, dalog_ref, dd_ref, dh_s):
        c = pl.program_id(1)

        @pl.when(c == 0)
        def _():
            dh_s[...] = jnp.zeros_like(dh_s)
            dalog_ref[...] = jnp.zeros_like(dalog_ref)
            dd_ref[...] = jnp.zeros_like(dd_ref)

        bb = b_ref[...].astype(BF16)
        cb_ = c_ref[...].astype(BF16)
        cbm = _dot(cb_, bb, 1, 1)
        cbt = _dot(bb, cb_, 1, 1)
        ri = lax.broadcasted_iota(jnp.int32, (L, L), 0)
        ci = lax.broadcasted_iota(jnp.int32, (L, L), 1)
        tri = ri >= ci
        trit = ci >= ri
        tril = tri.astype(F32)
        triu = trit.astype(F32)
        hs = range(E)
        com = [_ssd_common(alog_ref[e], dtc_ref[e], dtr_ref[e], tril, tri) for e in hs]
        lam = [jnp.exp(jnp.where(tri, com[e][1] - com[e][2], NEG)) for e in hs]
        lamt = [jnp.exp(jnp.where(trit, com[e][2] - com[e][1], NEG)) for e in hs]
        xs = [xh_ref[e] for e in hs]
        xdt = [xs[e] * dtc_ref[e] for e in hs]
        xdtb = [t.astype(BF16) for t in xdt]
        dyv = [dy_ref[e] for e in hs]
        dyb = [t.astype(BF16) for t in dyv]
        hh = [hp_ref[e] for e in hs]
        hb = [t.astype(BF16) for t in hh]
        dhn = [dh_s[e] for e in hs]
        dhnb = [t.astype(BF16) for t in dhn]
        ea_c = [jnp.exp(com[e][1]) for e in hs]
        decay_c = [jnp.exp(com[e][3] - com[e][1]) for e in hs]
        e_last = [jnp.exp(com[e][3]) for e in hs]
        gm = [_dot(dyb[e], xdtb[e], 1, 1) for e in hs]
        gt = [_dot(xdtb[e], dyb[e], 1, 1) for e in hs]
        bdh = [_dot(bb, dhnb[e], 1, 1) for e in hs]
        dxdt = [_dot((cbt * lamt[e]).astype(BF16), dyb[e], 1, 0) + bdh[e] * decay_c[e] for e in hs]
        dcb = [gm[e] * lam[e] for e in hs]
        dcbt = [gt[e] * lamt[e] for e in hs]
        yoff = [_dot(cb_, hb[e], 1, 1) * ea_c[e] for e in hs]
        dcs = [_dot(dcb[e].astype(BF16), bb, 1, 0) + _dot(dyb[e], hb[e], 1, 0) * ea_c[e] for e in hs]
        dbs = [_dot(dcbt[e].astype(BF16), cb_, 1, 0) + _dot(xdtb[e], dhnb[e], 1, 0) * decay_c[e] for e in hs]
        dyw = [(dyt_ref[e] * jnp.exp(com[e][2])).astype(BF16) for e in hs]
        dh_new = [dhn[e] * e_last[e] + _dot(dyw[e], cb_, 1, 0) for e in hs]
        db_acc, dc_acc = dbs[0], dcs[0]
        for e in range(1, E):
            db_acc = db_acc + dbs[e]
            dc_acc = dc_acc + dcs[e]
        for e in hs:
            a = com[e][0]
            tt = decay_c[e] * jnp.sum(xdt[e] * bdh[e], axis=1, keepdims=True)
            dacs = (jnp.sum(dcb[e] * cbm, axis=1, keepdims=True) - jnp.sum(dcbt[e] * cbt, axis=1, keepdims=True)
                    + jnp.sum(dyv[e] * yoff[e], axis=1, keepdims=True) - tt)
            tail = jnp.sum(tt, axis=0, keepdims=True) + e_last[e] * jnp.sum(
                jnp.sum(dhn[e] * hh[e], axis=1, keepdims=True), axis=0, keepdims=True)
            dda = _dot(triu, dacs, 1, 0, HI) + tail
            ddt_ref[e] = dda * a + jnp.sum(dxdt[e] * xs[e], axis=1, keepdims=True)
            dalog_ref[e] += jnp.sum(dda * dtc_ref[e], axis=0, keepdims=True) * a
            dd_ref[e] += jnp.sum(jnp.sum(dyv[e] * xs[e], axis=1, keepdims=True), axis=0, keepdims=True)
            dx_ref[e] = dxdt[e] * dtc_ref[e] + dyv[e] * dsk_ref[e]
            dh_s[e] = dh_new[e]
        db_ref[...] = db_acc
        dc_ref[...] = dc_acc

    rc = lambda c: nc - 1 - c
    return pl.pallas_call(
        kern,
        name="ssd_bwd",
        grid=(SSM_GROUPS, nc),
        in_specs=[
            pl.BlockSpec((E, L, p), lambda g, c: (g, rc(c), 0)),
            pl.BlockSpec((E, p, L), lambda g, c: (g, 0, rc(c))),
            pl.BlockSpec((E, L, p), lambda g, c: (g, rc(c), 0)),
            pl.BlockSpec((E, L, 1), lambda g, c: (g, rc(c), 0)),
            pl.BlockSpec((E, 1, L), lambda g, c: (g, 0, rc(c))),
            pl.BlockSpec((L, SSM_STATE), lambda g, c: (rc(c), g)),
            pl.BlockSpec((L, SSM_STATE), lambda g, c: (rc(c), g)),
            pl.BlockSpec((E, None, p, SSM_STATE), lambda g, c: (g, rc(c), 0, 0)),
            pl.BlockSpec((E, 1, 1), lambda g, c: (g, 0, 0)),
            pl.BlockSpec((E, 1, 1), lambda g, c: (g, 0, 0)),
        ],
        out_specs=[
            pl.BlockSpec((E, L, p), lambda g, c: (g, rc(c), 0)),
            pl.BlockSpec((E, L, 1), lambda g, c: (g, rc(c), 0)),
            pl.BlockSpec((L, SSM_STATE), lambda g, c: (rc(c), g)),
            pl.BlockSpec((L, SSM_STATE), lambda g, c: (rc(c), g)),
            pl.BlockSpec((E, 1, 1), lambda g, c: (g, 0, 0)),
            pl.BlockSpec((E, 1, 1), lambda g, c: (g, 0, 0)),
        ],
        out_shape=[
            jax.ShapeDtypeStruct((nh, s, p), F32),
            jax.ShapeDtypeStruct((nh, s, 1), F32),
            jax.ShapeDtypeStruct((s, SSM_GROUPS * SSM_STATE), F32),
            jax.ShapeDtypeStruct((s, SSM_GROUPS * SSM_STATE), F32),
            jax.ShapeDtypeStruct((nh, 1, 1), F32),
            jax.ShapeDtypeStruct((nh, 1, 1), F32),
        ],
        scratch_shapes=[pltpu.VMEM((E, p, SSM_STATE), F32)],
        compiler_params=_cp(("arbitrary", "arbitrary")),
    )(dy, dyt, xh, dtc, dtr, bm, cm, hprev, alog, dsk)


def _fgate_fwd(fl, bias):
    h, s = fl.shape
    t = _tm(s, 512)

    def kern(fl_ref, b_ref, o_ref, carry):
        i = pl.program_id(0)

        @pl.when(i == 0)
        def _():
            carry[...] = jnp.zeros_like(carry)

        lf = -_softplus(-(fl_ref[...] + b_ref[...]))
        ri = lax.broadcasted_iota(jnp.int32, (t, t), 0)
        ci = lax.broadcasted_iota(jnp.int32, (t, t), 1)
        o_ref[...] = _dot(lf, (ri <= ci).astype(F32), 1, 0, HI) + carry[...]
        carry[...] += jnp.sum(lf, axis=1, keepdims=True)

    return pl.pallas_call(
        kern,
        name="fgate_fwd",
        grid=(s // t,),
        in_specs=[pl.BlockSpec((h, t), lambda i: (0, i)), pl.BlockSpec((h, 1), lambda i: (0, 0))],
        out_specs=pl.BlockSpec((h, t), lambda i: (0, i)),
        out_shape=jax.ShapeDtypeStruct((h, s), F32),
        scratch_shapes=[pltpu.VMEM((h, 1), F32)],
        compiler_params=_cp(("arbitrary",)),
    )(fl, bias)


def _fgate_bwd(dqe, dke, col, fl, bias):
    h, s = fl.shape
    w = dqe.shape[-1]
    t = _tm(s, 512)
    n = s // t

    def kern(dq_ref, dk_ref, fl_ref, b_ref, o_ref, db_ref, carry):
        i = pl.program_id(0)

        @pl.when(i == 0)
        def _():
            carry[...] = jnp.zeros_like(carry)
            db_ref[...] = jnp.zeros_like(db_ref)

        sel = (lax.broadcasted_iota(jnp.int32, (1, w), 1) == col).astype(F32)
        hid = lax.broadcasted_iota(jnp.int32, (h, t), 0)
        d = jnp.zeros((h, t), F32)
        for hh in range(h):
            r = _dot(sel, dq_ref[hh], 1, 1, HI) - _dot(sel, dk_ref[hh], 1, 1, HI)
            d = d + jnp.where(hid == hh, r, 0.0)
        ri = lax.broadcasted_iota(jnp.int32, (t, t), 0)
        ci = lax.broadcasted_iota(jnp.int32, (t, t), 1)
        rev = _dot(d, (ri >= ci).astype(F32), 1, 0, HI) + carry[...]
        carry[...] += jnp.sum(d, axis=1, keepdims=True)
        dz = rev * _sigmoid(-(fl_ref[...] + b_ref[...]))
        o_ref[...] = dz
        db_ref[...] += jnp.sum(dz, axis=1, keepdims=True)

    rev_blk = lambda i: (0, n - 1 - i)
    return pl.pallas_call(
        kern,
        name="fgate_bwd",
        grid=(n,),
        in_specs=[pl.BlockSpec((h, t, w), lambda i: (0, n - 1 - i, 0)), pl.BlockSpec((h, t, w), lambda i: (0, n - 1 - i, 0)),
                  pl.BlockSpec((h, t), rev_blk), pl.BlockSpec((h, 1), lambda i: (0, 0))],
        out_specs=[pl.BlockSpec((h, t), rev_blk), pl.BlockSpec((h, 1), lambda i: (0, 0))],
        out_shape=[jax.ShapeDtypeStruct((h, s), F32), jax.ShapeDtypeStruct((h, 1), F32)],
        scratch_shapes=[pltpu.VMEM((h, 1), F32)],
        compiler_params=_cp(("arbitrary",)),
    )(dqe, dke, fl, bias)


HPS_FWD = 4
HPS_BWD = 4


def _attn_scores(q, k, fq, fk, scale, masked, t):
    sc = _dot(q, k, 1, 1)
    if scale is not None:
        sc = sc * scale
    if fq is not None:
        sc = sc + fq - fk
    if masked:
        rows = lax.broadcasted_iota(jnp.int32, (t, t), 0)
        cols = lax.broadcasted_iota(jnp.int32, (t, t), 1)
        sc = jnp.where(cols <= rows, sc, NEG)
    return sc


def _flash_fwd(name, q, k, v, fq, fk, scale):
    nh, s, dk = q.shape
    dv = v.shape[-1]
    t = _tm(s, 512)
    nq = s // t
    bias = fq is not None
    HPS = HPS_FWD

    pairs = [(i, j) for i in range(nq) for j in range(i + 1)]
    qi_of = jnp.asarray(np.array([p[0] for p in pairs], np.int32))
    kj_of = jnp.asarray(np.array([p[1] for p in pairs], np.int32))

    def kern(qi_ref, kj_ref, *refs):
        if bias:
            q_ref, k_ref, v_ref, fq_ref, fk_ref, o_ref, lse_ref, m_s, l_s, acc_s = refs
        else:
            q_ref, k_ref, v_ref, o_ref, lse_ref, m_s, l_s, acc_s = refs
        qi, kj = qi_ref[pl.program_id(1)], kj_ref[pl.program_id(1)]

        @pl.when(kj == 0)
        def _():
            m_s[...] = jnp.full_like(m_s, NEG)
            l_s[...] = jnp.zeros_like(l_s)
            acc_s[...] = jnp.zeros_like(acc_s)

        def step(masked):
            scs = [_attn_scores(q_ref[hh], k_ref[hh], fq_ref[hh] if bias else None, fk_ref[hh] if bias else None,
                                scale, masked, t) for hh in range(HPS)]
            pbs, corrs = [], []
            for hh in range(HPS):
                m_old = m_s[hh]
                m_new = jnp.maximum(m_old, jnp.max(scs[hh], axis=1, keepdims=True))
                corr = jnp.exp(m_old - m_new)
                p = jnp.exp(scs[hh] - m_new)
                l_s[hh] = corr * l_s[hh] + jnp.sum(p, axis=1, keepdims=True)
                m_s[hh] = m_new
                pbs.append(p.astype(BF16))
                corrs.append(corr)
            for hh in range(HPS):
                acc_s[hh] = acc_s[hh] * corrs[hh] + _dot(pbs[hh], v_ref[hh], 1, 0)

        @pl.when(kj < qi)
        def _():
            step(False)

        @pl.when(kj == qi)
        def _():
            step(True)
            o_ref[...] = acc_s[...] / l_s[...]
            lse_ref[...] = m_s[...] + jnp.log(l_s[...])

    qspec = lambda d: pl.BlockSpec((HPS, t, d), lambda h, p, qi_r, kj_r: (h, qi_r[p], 0))
    kspec = lambda d: pl.BlockSpec((HPS, t, d), lambda h, p, qi_r, kj_r: (h, kj_r[p], 0))
    in_specs = [qspec(dk), kspec(dk), kspec(dv)]
    args = [q, k, v]
    if bias:
        in_specs += [qspec(1), pl.BlockSpec((HPS, 1, t), lambda h, p, qi_r, kj_r: (h, 0, kj_r[p]))]
        args += [fq, fk]
    return pl.pallas_call(
        kern,
        name=name,
        grid_spec=pltpu.PrefetchScalarGridSpec(
            num_scalar_prefetch=2,
            grid=(nh // HPS, len(pairs)),
            in_specs=in_specs,
            out_specs=[qspec(dv), qspec(1)],
            scratch_shapes=[pltpu.VMEM((HPS, t, 1), F32), pltpu.VMEM((HPS, t, 1), F32),
                            pltpu.VMEM((HPS, t, dv), F32)],
        ),
        out_shape=[jax.ShapeDtypeStruct((nh, s, dv), F32), jax.ShapeDtypeStruct((nh, s, 1), F32)],
        compiler_params=_cp(("parallel", "arbitrary")),
    )(qi_of, kj_of, *args)


def _flash_bwd(name, q, k, v, do, lse, delta, fq, fk, qg, kg, scale, dq_scale, dk_scale):
    nh, s, dk = q.shape
    dv = v.shape[-1]
    dg = qg.shape[-1]
    t = _tm(s, 512)
    nq = s // t
    bias = fq is not None
    ext = qg is not q
    HPS = HPS_BWD

    pairs = [(j, i) for j in range(nq) for i in range(j, nq)]
    kb_of = jnp.asarray(np.array([p[0] for p in pairs], np.int32))
    qi_of = jnp.asarray(np.array([p[1] for p in pairs], np.int32))

    def kern(kb_ref, qi_ref, *refs):
        refs = list(refs)
        q_ref, k_ref, v_ref, do_ref, lse_ref, dl_ref = refs[:6]
        del refs[:6]
        fq_ref, fk_ref = (refs.pop(0), refs.pop(0)) if bias else (None, None)
        qg_ref, kg_ref = (refs.pop(0), refs.pop(0)) if ext else (q_ref, k_ref)
        dq_ref, dk_ref, dv_ref, dk_s, dv_s = refs
        kb, qi = kb_ref[pl.program_id(1)], qi_ref[pl.program_id(1)]

        @pl.when(pl.program_id(1) == 0)
        def _():
            dq_ref[...] = jnp.zeros_like(dq_ref)

        @pl.when(qi == kb)
        def _():
            dk_s[...] = jnp.zeros_like(dk_s)
            dv_s[...] = jnp.zeros_like(dv_s)

        def step(masked):
            rows = pl.ds(pl.multiple_of(qi * t, t), t)
            heads = range(HPS)
            scs = [_attn_scores(q_ref[hh], k_ref[hh], fq_ref[hh] if bias else None, fk_ref[hh] if bias else None,
                                scale, masked, t) for hh in heads]
            dps = [_dot(do_ref[hh], v_ref[hh], 1, 1) for hh in heads]
            ps = [jnp.exp(scs[hh] - lse_ref[hh]) for hh in heads]
            pbs = [p.astype(BF16) for p in ps]
            for hh in heads:
                dv_s[hh] += _dot(pbs[hh], do_ref[hh], 0, 0)
            dsbs = [(ps[hh] * (dps[hh] - dl_ref[hh])).astype(BF16) for hh in heads]
            for hh in heads:
                dk_s[hh] += _dot(dsbs[hh], qg_ref[hh], 0, 0)
            for hh in heads:
                dq_ref[hh, rows, :] += _dot(dsbs[hh], kg_ref[hh], 1, 0) * dq_scale

        @pl.when(qi > kb)
        def _():
            step(False)

        @pl.when(qi == kb)
        def _():
            step(True)

        @pl.when(qi == nq - 1)
        def _():
            dk_ref[...] = dk_s[...] if dk_scale is None else dk_s[...] * dk_scale
            dv_ref[...] = dv_s[...]

    qspec = lambda d: pl.BlockSpec((HPS, t, d), lambda h, p, kb_r, qi_r: (h, qi_r[p], 0))
    kspec = lambda d: pl.BlockSpec((HPS, t, d), lambda h, p, kb_r, qi_r: (h, kb_r[p], 0))
    in_specs = [qspec(dk), kspec(dk), kspec(dv), qspec(dv), qspec(1), qspec(1)]
    args = [q, k, v, do, lse, delta]
    if bias:
        in_specs += [qspec(1), pl.BlockSpec((HPS, 1, t), lambda h, p, kb_r, qi_r: (h, 0, kb_r[p]))]
        args += [fq, fk]
    if ext:
        in_specs += [qspec(dg), kspec(dg)]
        args += [qg, kg]
    return pl.pallas_call(
        kern,
        name=name,
        grid_spec=pltpu.PrefetchScalarGridSpec(
            num_scalar_prefetch=2,
            grid=(nh // HPS, len(pairs)),
            in_specs=in_specs,
            out_specs=[pl.BlockSpec((HPS, s, dg), lambda h, p, kb_r, qi_r: (h, 0, 0), pipeline_mode=pl.Buffered(1)),
                       kspec(dg), kspec(dv)],
            scratch_shapes=[pltpu.VMEM((HPS, t, dg), F32), pltpu.VMEM((HPS, t, dv), F32)],
        ),
        out_shape=[jax.ShapeDtypeStruct((nh, s, dg), F32), jax.ShapeDtypeStruct((nh, s, dg), F32),
                   jax.ShapeDtypeStruct((nh, s, dv), F32)],
        compiler_params=_cp(("arbitrary", "arbitrary")),
    )(kb_of, qi_of, *args)


def _heads(t, nh):
    s = t.shape[0]
    return t.reshape(s, nh, -1).transpose(1, 0, 2)


def _unheads(t):
    nh, s, d = t.shape
    return t.transpose(1, 0, 2).reshape(s, nh * d)


def _perm_uq(w):
    r = w.shape[0]
    w3 = w.reshape(r, MLA_HEADS, MLA_NOPE + MLA_ROPE)
    half = MLA_ROPE // 2
    return jnp.concatenate([w3[:, :, :MLA_NOPE].reshape(r, -1), w3[:, :, MLA_NOPE:MLA_NOPE + half].reshape(r, -1),
                            w3[:, :, MLA_NOPE + half:].reshape(r, -1)], axis=1)


def _unperm_uq(w):
    r = w.shape[0]
    half = MLA_ROPE // 2
    n0 = MLA_HEADS * MLA_NOPE
    n1 = n0 + MLA_HEADS * half
    return jnp.concatenate([w[:, :n0].reshape(r, MLA_HEADS, MLA_NOPE), w[:, n0:n1].reshape(r, MLA_HEADS, half),
                            w[:, n1:].reshape(r, MLA_HEADS, half)], axis=2).reshape(r, -1)


def _perm_ukv(w):
    r = w.shape[0]
    w3 = w.reshape(r, MLA_HEADS, MLA_NOPE + MLA_V)
    return jnp.concatenate([w3[:, :, :MLA_NOPE].reshape(r, -1), w3[:, :, MLA_NOPE:].reshape(r, -1)], axis=1)


def _unperm_ukv(w):
    r = w.shape[0]
    n0 = MLA_HEADS * MLA_NOPE
    return jnp.concatenate([w[:, :n0].reshape(r, MLA_HEADS, MLA_NOPE), w[:, n0:].reshape(r, MLA_HEADS, MLA_V)],
                           axis=2).reshape(r, -1)


_ODD_CUTS = np.cumsum([0, FOX_WIDTH, FOX_WIDTH, FOX_WIDTH, FOX_HEADS, MLA_Q_RANK, MLA_KV_RANK, MLA_ROPE]).tolist()
_ODD_ORDER = (0, 1, 2, 4, 5, 6, 3)


def _perm_odd_in(w):
    parts = [w[:, _ODD_CUTS[j]:_ODD_CUTS[j + 1]] for j in _ODD_ORDER]
    parts.append(jnp.zeros((w.shape[0], ODD_IN_PAD - ODD_IN), w.dtype))
    return jnp.concatenate(parts, axis=1)


def _unperm_odd_in(w):
    widths = [_ODD_CUTS[j + 1] - _ODD_CUTS[j] for j in _ODD_ORDER]
    offs = np.cumsum([0] + widths).tolist()
    pieces = {j: w[:, offs[n]:offs[n + 1]] for n, j in enumerate(_ODD_ORDER)}
    return jnp.concatenate([pieces[j] for j in range(7)], axis=1)


def _pad_cols(w, n):
    return jnp.concatenate([w, jnp.zeros((w.shape[0], n - w.shape[1]), w.dtype)], axis=1)


_BIG = (("even_w_in", 2), ("even_w_out", 1), ("odd_w_in", 2), ("w_uq", 2), ("w_ukv", 2), ("odd_w_out", 1),
        ("ffn_w_gate", 2), ("ffn_w_up", 2), ("ffn_w_down", 1))
_PACK_COLS = 1024


def _unshard(blocks, shp, ax):
    t = jnp.moveaxis(blocks.reshape((N_DEV,) + tuple(shp)), 0, ax)
    full = list(shp)
    full[ax] = shp[ax] * N_DEV
    return t.reshape(full)


def _reshard(full, ax, cols):
    shp = list(full.shape)
    t = full.reshape(shp[:ax] + [N_DEV, shp[ax] // N_DEV] + shp[ax + 1:])
    return jnp.moveaxis(t, ax, 0).reshape(N_DEV, -1, cols)


def kernel(x, even_w_in, pool_w, pool_scale, conv_w, conv_b, dt_bias, a_log, d_skip, ssm_norm_w, even_w_out, odd_w_in, fgate_b, q_norm_w, w_uq, kv_norm_w, w_ukv, odd_w_out, ffn_w_gate, ffn_w_up, ffn_w_down, ln_mix_g, ln_mix_b, ln_ffn_g, ln_ffn_b, loss_target, m_even_w_in, m_pool_w, m_pool_scale, m_conv_w, m_conv_b, m_dt_bias, m_a_log, m_d_skip, m_ssm_norm_w, m_even_w_out, m_odd_w_in, m_fgate_b, m_q_norm_w, m_w_uq, m_kv_norm_w, m_w_ukv, m_odd_w_out, m_ffn_w_gate, m_ffn_w_up, m_ffn_w_down, m_ln_mix_g, m_ln_mix_b, m_ln_ffn_g, m_ln_ffn_b, v_even_w_in, v_pool_w, v_pool_scale, v_conv_w, v_conv_b, v_dt_bias, v_a_log, v_d_skip, v_ssm_norm_w, v_even_w_out, v_odd_w_in, v_fgate_b, v_q_norm_w, v_w_uq, v_kv_norm_w, v_w_ukv, v_odd_w_out, v_ffn_w_gate, v_ffn_w_up, v_ffn_w_down, v_ln_mix_g, v_ln_mix_b, v_ln_ffn_g, v_ln_ffn_b):
    P = dict(even_w_in=even_w_in, pool_w=pool_w, pool_scale=pool_scale, conv_w=conv_w, conv_b=conv_b, dt_bias=dt_bias,
             a_log=a_log, d_skip=d_skip, ssm_norm_w=ssm_norm_w, even_w_out=even_w_out, odd_w_in=odd_w_in,
             fgate_b=fgate_b, q_norm_w=q_norm_w, w_uq=w_uq, kv_norm_w=kv_norm_w, w_ukv=w_ukv, odd_w_out=odd_w_out,
             ffn_w_gate=ffn_w_gate, ffn_w_up=ffn_w_up, ffn_w_down=ffn_w_down, ln_mix_g=ln_mix_g, ln_mix_b=ln_mix_b,
             ln_ffn_g=ln_ffn_g, ln_ffn_b=ln_ffn_b)
    M = dict(even_w_in=m_even_w_in, pool_w=m_pool_w, pool_scale=m_pool_scale, conv_w=m_conv_w, conv_b=m_conv_b,
             dt_bias=m_dt_bias, a_log=m_a_log, d_skip=m_d_skip, ssm_norm_w=m_ssm_norm_w, even_w_out=m_even_w_out,
             odd_w_in=m_odd_w_in, fgate_b=m_fgate_b, q_norm_w=m_q_norm_w, w_uq=m_w_uq, kv_norm_w=m_kv_norm_w,
             w_ukv=m_w_ukv, odd_w_out=m_odd_w_out, ffn_w_gate=m_ffn_w_gate, ffn_w_up=m_ffn_w_up,
             ffn_w_down=m_ffn_w_down, ln_mix_g=m_ln_mix_g, ln_mix_b=m_ln_mix_b, ln_ffn_g=m_ln_ffn_g,
             ln_ffn_b=m_ln_ffn_b)
    V = dict(even_w_in=v_even_w_in, pool_w=v_pool_w, pool_scale=v_pool_scale, conv_w=v_conv_w, conv_b=v_conv_b,
             dt_bias=v_dt_bias, a_log=v_a_log, d_skip=v_d_skip, ssm_norm_w=v_ssm_norm_w, even_w_out=v_even_w_out,
             odd_w_in=v_odd_w_in, fgate_b=v_fgate_b, q_norm_w=v_q_norm_w, w_uq=v_w_uq, kv_norm_w=v_kv_norm_w,
             w_ukv=v_w_ukv, odd_w_out=v_odd_w_out, ffn_w_gate=v_ffn_w_gate, ffn_w_up=v_ffn_w_up,
             ffn_w_down=v_ffn_w_down, ln_mix_g=v_ln_mix_g, ln_mix_b=v_ln_mix_b, ln_ffn_g=v_ln_ffn_g,
             ln_ffn_b=v_ln_ffn_b)
    names = list(P)
    s = x.shape[1]
    me = 4 * lax.axis_index("x") + 2 * lax.axis_index("y") + lax.axis_index("c")

    big_rows = [math.prod(P[n].shape) // _PACK_COLS for n, _ in _BIG]
    pad_rows = [-(-nr // 16) * 16 for nr in big_rows]
    rows_big = sum(pad_rows)
    packed = jnp.concatenate([
        jnp.pad(P[n].astype(BF16).reshape(-1), (0, (pr - nr) * _PACK_COLS))
        for (n, _), nr, pr in zip(_BIG, big_rows, pad_rows)]).reshape(rows_big, _PACK_COLS)
    gathered = _all_gather("ag_weights", packed)
    W = {}
    off = 0
    for (n, ax), nr, pr in zip(_BIG, big_rows, pad_rows):
        W[n] = _unshard(gathered[:, off:off + nr, :], P[n].shape, ax)
        off += pr
    kv_pad = jnp.zeros((128 - kv_norm_w.size,), F32)
    small_sh = jnp.concatenate([conv_w.reshape(-1), q_norm_w.reshape(-1), kv_norm_w.reshape(-1), kv_pad,
                                jnp.zeros((2 * 128,), F32)]).reshape(16, 128)
    g_small = _all_gather("ag_small_weights", small_sh)
    conv_w_full = _unshard(g_small[:, :12, :], conv_w.shape, 2)
    q_norm_full = _unshard(g_small[:, 12:13, :], q_norm_w.shape, 1)
    kv_norm_full = _unshard(g_small[:, 13:14, :64], kv_norm_w.shape, 1)

    w_in_e = [_pad_cols(W["even_w_in"][i], EVEN_IN_PAD) for i in range(2)]
    w_in_o = [_perm_odd_in(W["odd_w_in"][i]) for i in range(2)]
    w_uq_p = [_perm_uq(W["w_uq"][i]) for i in range(2)]
    w_ukv_p = [_perm_ukv(W["w_ukv"][i]) for i in range(2)]
    w_gu = [jnp.concatenate([W["ffn_w_gate"][l], W["ffn_w_up"][l]], axis=1) for l in range(DEPTH)]

    pos = jnp.arange(s, dtype=F32)
    half = MLA_ROPE // 2
    freqs = jnp.power(ROPE_THETA, -jnp.arange(half, dtype=F32) / half)
    ang = pos[:, None] * freqs[None, :]
    cos16, sin16 = jnp.cos(ang), jnp.sin(ang)
    cos128, sin128 = jnp.tile(cos16, (1, MLA_HEADS)), jnp.tile(sin16, (1, MLA_HEADS))
    row = lambda t: t.reshape(1, -1)

    xcur = x[0]
    xb = xcur.astype(BF16)
    saved = []
    for l in range(DEPTH):
        i = l // 2
        sv = dict(x_in_b=xb)
        if l % 2 == 0:
            proj = _mm("mm_in_even", xb, w_in_e[i], "nn", F32)
            u, z = proj[:, :512], proj[:, 512:1536]
            xbc, dtraw = proj[:, 1536:3072], proj[:, 3072:3088]
            ypool = _pool_fwd(u, pool_w[i], row(pool_scale[i]))
            xc = _conv_fwd(xbc, conv_w_full[i], row(conv_b[i]))
            dt = _softplus_fwd(dtraw, row(dt_bias[i]))
            xh = _heads(xc[:, :SSM_D_INNER], SSM_HEADS)
            dtc = dt.T[:, :, None]
            dtr = dt.T[:, None, :]
            bm, cm = xc[:, SSM_D_INNER:SSM_D_INNER + 256], xc[:, SSM_D_INNER + 256:]
            alog3, dsk3 = a_log[i].reshape(-1, 1, 1), d_skip[i].reshape(-1, 1, 1)
            yh, hprev = _ssd_fwd(xh, xh.transpose(0, 2, 1), dtc, dtr, bm, cm, alog3, dsk3)
            y_ssm = _unheads(yh)
            yn = _gated_rms_fwd(y_ssm, z, row(ssm_norm_w[i]))
            mix = jnp.concatenate([ypool, yn], axis=1)
            h = _mm("mm_out_even", mix, W["even_w_out"][i], "nn", F32)
            sv.update(u=u, z=z, xbc=xbc, dtraw=dtraw, xh=xh, dtc=dtc, dtr=dtr, bm=bm, cm=cm, hprev=hprev,
                      y_ssm=y_ssm, mix=mix)
        else:
            proj = _mm("mm_in_odd", xb, w_in_o[i], "nn", F32)
            qf, kf, vf = proj[:, :512], proj[:, 512:1024], proj[:, 1024:1536]
            cq, ckv = proj[:, 1536:2048], proj[:, 2048:2304]
            kr, fl = proj[:, 2304:2336], proj[:, 2336:2344]
            fl = fl.T
            fcum = _fgate_fwd(fl, fgate_b[i][:, None])
            fq_ = fcum[:, :, None]
            fk_ = fcum[:, None, :]
            qh, kh, vh = (_heads(t.astype(BF16), FOX_HEADS) for t in (qf * FOX_SCALE, kf, vf))
            o_fox, lse_fox = _flash_fwd("fox_fwd", qh, kh, vh, fq_, fk_, None)
            qn = _rms_fwd(cq, row(q_norm_full[i]))
            qp = _mm("mm_uq", qn, w_uq_p[i], "nn", F32)
            q1, q2 = _rope("rope_q", qp[:, 512:640], qp[:, 640:768], cos128, sin128)
            kvn = _rms_fwd(ckv, row(kv_norm_full[i]))
            kvp = _mm("mm_ukv", kvn, w_ukv_p[i], "nn", F32)
            k1, k2 = _rope("rope_k", kr[:, :half], kr[:, half:], cos16, sin16)
            zpad = jnp.zeros((MLA_HEADS, s, MLA_DK_PAD - MLA_NOPE - MLA_ROPE), BF16)
            qm = jnp.concatenate([_heads(qp[:, :512], MLA_HEADS), _heads(q1, MLA_HEADS), _heads(q2, MLA_HEADS)],
                                 axis=2).astype(BF16)
            qm = jnp.concatenate([qm, zpad], axis=2)
            krope = jnp.broadcast_to(jnp.concatenate([k1, k2], axis=1)[None], (MLA_HEADS, s, MLA_ROPE))
            km = jnp.concatenate([_heads(kvp[:, :512], MLA_HEADS), krope], axis=2).astype(BF16)
            km = jnp.concatenate([km, zpad], axis=2)
            vm = _heads(kvp[:, 512:], MLA_HEADS).astype(BF16)
            zero_q, zero_k = jnp.zeros((MLA_HEADS, s, 1), F32), jnp.zeros((MLA_HEADS, 1, s), F32)
            o_mla, lse_mla = _flash_fwd("mla_fwd", qm, km, vm, zero_q, zero_k, MLA_SCALE)
            mix = jnp.concatenate([_unheads(o_fox), _unheads(o_mla)], axis=1).astype(BF16)
            h = _mm("mm_out_odd", mix, W["odd_w_out"][i], "nn", F32)
            sv.update(fl=fl, fq=fq_, fk=fk_, qh=qh, kh=kh, vh=vh, o_fox=o_fox, lse_fox=lse_fox, cq=cq, ckv=ckv,
                      qn=qn, kvn=kvn, qm=qm, km=km, vm=vm, o_mla=o_mla, lse_mla=lse_mla, mix=mix)
        y1, y1b, r1 = _ln_fwd(xcur, h, row(ln_mix_g[l]), row(ln_mix_b[l]))
        gu = _mm("mm_ffn_in", y1b, w_gu[l], "nn", BF16)
        act = _swiglu_fwd(gu)
        h2 = _mm("mm_ffn_out", act, W["ffn_w_down"][l], "nn", F32)
        y2, y2b, r2 = _ln_fwd(y1, h2, row(ln_ffn_g[l]), row(ln_ffn_b[l]))
        sv.update(r1=r1, y1b=y1b, gu=gu, act=act, r2=r2)
        saved.append(sv)
        xcur, xb = y2, y2b

    dy, loss_part = _loss_head(xcur, loss_target[0])
    loss = lax.psum(loss_part[0, 0], ("x", "y", "c"))

    G = {n: [None] * P[n].shape[0] for n in names}
    acur, dcur = None, dy
    for l in reversed(range(DEPTH)):
        i = l // 2
        sv = saved[l]
        dr2, dr2b, dg, db = _ln_bwd(acur, dcur, sv["r2"], row(ln_ffn_g[l]))
        G["ln_ffn_g"][l], G["ln_ffn_b"][l] = dg[0], db[0]
        G["ffn_w_down"][l] = _mm("mm_dw_ffn_out", sv["act"], dr2b, "tn", F32)
        dact = _mm("mm_dx_ffn_out", dr2b, W["ffn_w_down"][l], "nt", BF16)
        dgu = _swiglu_bwd(dact, sv["gu"])
        dwgu = _mm("mm_dw_ffn_in", sv["y1b"], dgu, "tn", F32)
        G["ffn_w_gate"][l], G["ffn_w_up"][l] = dwgu[:, :D_FF], dwgu[:, D_FF:]
        dy1 = _mm("mm_dx_ffn_in", dgu, w_gu[l], "nt", F32)
        dr1, dr1b, dg, db = _ln_bwd(dr2, dy1, sv["r1"], row(ln_mix_g[l]))
        G["ln_mix_g"][l], G["ln_mix_b"][l] = dg[0], db[0]
        if l % 2 == 0:
            G["even_w_out"][i] = _mm("mm_dw_out_even", sv["mix"], dr1b, "tn", F32)
            dmix = _mm("mm_dx_out_even", dr1b, W["even_w_out"][i], "nt", F32)
            dd, dsc, dpw = _pool_bwd1(sv["u"], dmix[:, :POOL_WIDTH], pool_w[i], row(pool_scale[i]))
            G["pool_scale"][i], G["pool_w"][i] = dsc[0], dpw
            du = _pool_bwd2(dd)
            dys, dz, dnw = _gated_rms_bwd(dmix[:, POOL_WIDTH:], sv["y_ssm"], sv["z"], row(ssm_norm_w[i]))
            G["ssm_norm_w"][i] = dnw[0]
            dyh = _heads(dys, SSM_HEADS)
            alog3, dsk3 = a_log[i].reshape(-1, 1, 1), d_skip[i].reshape(-1, 1, 1)
            dxh, ddt3, dbm, dcm, dalog, ddsk = _ssd_bwd(dyh, dyh.transpose(0, 2, 1), sv["xh"], sv["dtc"], sv["dtr"],
                                                        sv["bm"], sv["cm"], sv["hprev"], alog3, dsk3)
            G["a_log"][i], G["d_skip"][i] = dalog.reshape(-1), ddsk.reshape(-1)
            ddtraw, ddtb = _softplus_bwd(ddt3[:, :, 0].T, sv["dtraw"], row(dt_bias[i]))
            G["dt_bias"][i] = ddtb[0]
            dxc = jnp.concatenate([_unheads(dxh), dbm, dcm], axis=1)
            dpre, dcw, dcb = _conv_bwd1(dxc, sv["xbc"], conv_w_full[i], row(conv_b[i]))
            G["conv_w"][i], G["conv_b"][i] = dcw, dcb[0]
            dxbc = _conv_bwd2(dpre, conv_w_full[i])
            dproj = jnp.concatenate([du, dz.astype(BF16), dxbc, ddtraw.astype(BF16),
                                     jnp.zeros((s, EVEN_IN_PAD - EVEN_IN), BF16)], axis=1)
            G["even_w_in"][i] = _mm("mm_dw_in_even", sv["x_in_b"], dproj, "tn", F32)[:, :EVEN_IN]
            dxb = _mm("mm_dx_in_even", dproj, w_in_e[i], "nt", F32)
        else:
            G["odd_w_out"][i] = _mm("mm_dw_out_odd", sv["mix"], dr1b, "tn", F32)
            dmix = _mm("mm_dx_out_odd", dr1b, W["odd_w_out"][i], "nt", F32)
            do_f = _heads(dmix[:, :FOX_WIDTH].astype(BF16), FOX_HEADS)
            dl_f = _rowdot(do_f.reshape(-1, 64), sv["o_fox"].reshape(-1, 64)).reshape(FOX_HEADS, s, 1)
            ones = jnp.ones((FOX_HEADS, s, 64), BF16)
            qg = jnp.concatenate([sv["qh"], ones], axis=2)
            kg = jnp.concatenate([sv["kh"], ones * (1.0 / FOX_SCALE)], axis=2)
            dqe, dke, dvh = _flash_bwd("fox_bwd", sv["qh"], sv["kh"], sv["vh"], do_f, sv["lse_fox"], dl_f,
                                       sv["fq"], sv["fk"], qg, kg, None, FOX_SCALE, None)
            dqh, dkh = dqe[:, :, :64], dke[:, :, :64]
            dfl, dfb = _fgate_bwd(dqe, dke, 64, sv["fl"], fgate_b[i][:, None])
            dfl = dfl.T
            G["fgate_b"][i] = dfb[:, 0]
            do_m = _heads(dmix[:, FOX_WIDTH:].astype(BF16), MLA_HEADS)
            dl_m = _rowdot(do_m.reshape(-1, 64), sv["o_mla"].reshape(-1, 64)).reshape(MLA_HEADS, s, 1)
            dqm, dkm, dvm = _flash_bwd("mla_bwd", sv["qm"], sv["km"], sv["vm"], do_m, sv["lse_mla"], dl_m,
                                       None, None, sv["qm"], sv["km"], MLA_SCALE, MLA_SCALE, MLA_SCALE)
            n0, n1 = MLA_NOPE, MLA_NOPE + half
            dq1, dq2 = _rope("rope_q_bwd", _unheads(dqm[:, :, n0:n1]), _unheads(dqm[:, :, n1:n1 + half]),
                             cos128, -sin128)
            dqp = jnp.concatenate([_unheads(dqm[:, :, :n0]), dq1, dq2], axis=1).astype(BF16)
            G["w_uq"][i] = _unperm_uq(_mm("mm_dw_uq", sv["qn"], dqp, "tn", F32))
            dqn = _mm("mm_dx_uq", dqp, w_uq_p[i], "nt", F32)
            dcq, dqw = _rms_bwd(dqn, sv["cq"], row(q_norm_full[i]))
            G["q_norm_w"][i] = dqw[0]
            dk1, dk2 = _headsum_rope_bwd(_unheads(dkm[:, :, n0:n1]), _unheads(dkm[:, :, n1:n1 + half]), cos16, sin16)
            dkvp = jnp.concatenate([_unheads(dkm[:, :, :n0]), _unheads(dvm)], axis=1).astype(BF16)
            G["w_ukv"][i] = _unperm_ukv(_mm("mm_dw_ukv", sv["kvn"], dkvp, "tn", F32))
            dkvn = _mm("mm_dx_ukv", dkvp, w_ukv_p[i], "nt", F32)
            dckv, dkvw = _rms_bwd(dkvn, sv["ckv"], row(kv_norm_full[i]))
            G["kv_norm_w"][i] = dkvw[0]
            dproj = jnp.concatenate([_unheads(dqh), _unheads(dkh), _unheads(dvh), dcq, dckv, dk1, dk2, dfl,
                                     jnp.zeros((s, ODD_IN_PAD - ODD_IN), F32)], axis=1).astype(BF16)
            G["odd_w_in"][i] = _unperm_odd_in(_mm("mm_dw_in_odd", sv["x_in_b"], dproj, "tn", F32))
            dxb = _mm("mm_dx_in_odd", dproj, w_in_o[i], "nt", F32)
        acur, dcur = dr1, dxb
    grad_x = _axpy(acur, dcur)[None]

    gfull = {n: jnp.stack(G[n]) for n in names}
    send = jnp.concatenate([
        jnp.pad(_reshard(gfull[n], ax, _PACK_COLS).astype(BF16), ((0, 0), (0, pr - nr), (0, 0)))
        for (n, ax), nr, pr in zip(_BIG, big_rows, pad_rows)], axis=1)
    recv = _all_to_all("a2a_grads", send)
    gsum = _sum8("sum_grads", recv)
    grads = {}
    off = 0
    for (n, ax), nr, pr in zip(_BIG, big_rows, pad_rows):
        grads[n] = gsum[off:off + nr].reshape(P[n].shape)
        off += pr
    small = [n for n in names if n not in dict(_BIG)]
    sflat = jnp.concatenate([gfull[n].reshape(-1) for n in small])
    n_small = sflat.shape[0]
    rows_small = -(-n_small // (128 * 8)) * 8
    sflat = jnp.concatenate([sflat, jnp.zeros((rows_small * 128 - n_small,), F32)])
    sg = _sum8("sum_small_grads", _all_gather("ag_small_grads", sflat.reshape(rows_small, 128))).reshape(-1)
    off = 0
    for n in small:
        cnt = gfull[n].size
        gf = sg[off:off + cnt].reshape(gfull[n].shape)
        off += cnt
        if gf.shape != P[n].shape:
            width = P[n].shape[-1]
            gf = lax.dynamic_slice_in_dim(gf, me * width, width, axis=gf.ndim - 1)
        grads[n] = gf

    delta, new_m, new_v = {}, {}, {}
    for n, _ in _BIG:
        shp = P[n].shape
        two = lambda t: t.reshape(-1, shp[-1])
        d_, m_, v_ = _adamw("adamw_" + n, two(P[n]), two(grads[n]), two(M[n]), two(V[n]))
        delta[n], new_m[n], new_v[n] = d_.reshape(shp), m_.reshape(shp), v_.reshape(shp)

    def packs(d):
        f = jnp.concatenate([d[n].reshape(-1) for n in small])
        pad = -(-f.shape[0] // (128 * 8)) * 8 * 128 - f.shape[0]
        return jnp.concatenate([f, jnp.zeros((pad,), F32)]).reshape(-1, 128)

    d_, m_, v_ = _adamw("adamw_small", packs(P), packs(grads), packs(M), packs(V))
    off = 0
    for n in small:
        cnt = P[n].size
        for dst, src in ((delta, d_), (new_m, m_), (new_v, v_)):
            dst[n] = src.reshape(-1)[off:off + cnt].reshape(P[n].shape)
        off += cnt

    return (loss, grad_x, *[grads[n] for n in names], *[delta[n] for n in names],
            *[new_m[n] for n in names], *[new_v[n] for n in names])
```

```python
import math

import jax
import jax.numpy as jnp
import numpy as np
from jax import lax
from jax.experimental import pallas as pl
from jax.experimental.pallas import tpu as pltpu

F32 = jnp.float32
BF16 = jnp.bfloat16
HI = lax.Precision.HIGHEST

N_DEV = 8
D_MODEL = 1024
DEPTH = 4
POOL_WINDOWS = (2, 4, 8, 16)
POOL_GROUP = 128
POOL_WIDTH = 512
SSM_D_INNER = 1024
SSM_HEAD_DIM = 64
SSM_HEADS = 16
SSM_GROUPS = 2
SSM_STATE = 128
SSM_CONV = 4
SSM_CHUNK = 128
SSM_CONV_DIM = 1536
EVEN_IN = 3088
EVEN_IN_PAD = 3200
FOX_HEADS = 8
FOX_WIDTH = 512
MLA_HEADS = 8
MLA_NOPE = 64
MLA_ROPE = 32
MLA_V = 64
MLA_Q_RANK = 512
MLA_KV_RANK = 256
MLA_DK_PAD = 128
ROPE_THETA = 10000.0
FOX_SCALE = 64 ** -0.5
MLA_SCALE = (MLA_NOPE + MLA_ROPE) ** -0.5
ODD_IN = 2344
ODD_IN_PAD = 2560
D_FF = 2816
ALPHA = (2 * DEPTH) ** 0.25
LN_EPS = 1e-5
RMS_EPS = 1e-6
ADAM_LR = 0.001
ADAM_B1 = 0.9
ADAM_B2 = 0.999
ADAM_EPS = 1e-08
ADAM_WD = 0.01
ADAM_STEP = 10
NEG = -1e30
VMEM_LIMIT = 48 * 1024 * 1024


def _cp(sem):
    return pltpu.CompilerParams(dimension_semantics=sem, vmem_limit_bytes=VMEM_LIMIT)


def _dot(a, b, ca, cb, prec=None):
    return lax.dot_general(a, b, (((ca,), (cb,)), ((), ())), preferred_element_type=F32, precision=prec)


def _sigmoid(x):
    return 1.0 / (1.0 + jnp.exp(-x))


def _softplus(x):
    return jnp.maximum(x, 0.0) + jnp.log(1.0 + jnp.exp(-jnp.abs(x)))


MESH = pl.DeviceIdType.MESH
HBM_SPEC = pl.BlockSpec(memory_space=pltpu.HBM)


def _all_gather(name, xs):
    r, c_ = xs.shape

    def body(x_ref, out_ref, send_sems, recv_sems, local_sem):
        x, y, c = lax.axis_index("x"), lax.axis_index("y"), lax.axis_index("c")
        me, sibling = (x, y, c), (x, y, 1 - c)
        chips = [(1 - x, y), (x, 1 - y), (1 - x, 1 - y)]

        def rows(px, py, pc):
            return out_ref.at[4 * px + 2 * py + pc]

        def copy(k, block, to, src=None):
            return pltpu.make_async_remote_copy(
                src_ref=rows(*block) if src is None else src,
                dst_ref=rows(*block),
                send_sem=send_sems.at[k],
                recv_sem=recv_sems.at[k],
                device_id=to,
                device_id_type=MESH,
            )

        mine = pltpu.make_async_copy(x_ref, rows(*me), local_sem)
        mine.start()
        first = [copy(0, me, sibling, src=x_ref)]
        first += [copy(1 + j, me, (*chip, c), src=x_ref) for j, chip in enumerate(chips)]
        for cp in first:
            cp.start()
        passed = [copy(4 + j, (*chip, c), sibling) for j, chip in enumerate(chips)]
        for j, chip in enumerate(chips):
            copy(1 + j, (*chip, c), me).wait_recv()
            passed[j].start()
        copy(0, sibling, me).wait_recv()
        for j, chip in enumerate(chips):
            copy(4 + j, (*chip, 1 - c), me).wait_recv()
        for cp in first + passed:
            cp.wait_send()
        mine.wait()

    return pl.pallas_call(
        body,
        name=name,
        out_shape=jax.ShapeDtypeStruct((N_DEV, r, c_), xs.dtype),
        in_specs=[HBM_SPEC],
        out_specs=HBM_SPEC,
        scratch_shapes=[pltpu.SemaphoreType.DMA((7,)), pltpu.SemaphoreType.DMA((7,)), pltpu.SemaphoreType.DMA(())],
    )(xs)


def _all_to_all(name, send):
    _, r, c_ = send.shape

    def body(s_ref, r_ref, send_sems, recv_sems, local_sem):
        x, y, c = lax.axis_index("x"), lax.axis_index("y"), lax.axis_index("c")
        me = 4 * x + 2 * y + c
        mine = pltpu.make_async_copy(s_ref.at[me], r_ref.at[me], local_sem)
        mine.start()
        copies = []
        for k in range(1, N_DEV):
            tx = 1 - x if k & 4 else x
            ty = 1 - y if k & 2 else y
            tc = 1 - c if k & 1 else c
            peer = 4 * tx + 2 * ty + tc
            cp = pltpu.make_async_remote_copy(
                src_ref=s_ref.at[peer],
                dst_ref=r_ref.at[me],
                send_sem=send_sems.at[k - 1],
                recv_sem=recv_sems.at[k - 1],
                device_id=(tx, ty, tc),
                device_id_type=MESH,
            )
            cp.start()
            landing = pltpu.make_async_remote_copy(
                src_ref=s_ref.at[me],
                dst_ref=r_ref.at[peer],
                send_sem=send_sems.at[k - 1],
                recv_sem=recv_sems.at[k - 1],
                device_id=(tx, ty, tc),
                device_id_type=MESH,
            )
            copies.append((cp, landing))
        for cp, landing in copies:
            landing.wait_recv()
        for cp, landing in copies:
            cp.wait_send()
        mine.wait()

    return pl.pallas_call(
        body,
        name=name,
        out_shape=jax.ShapeDtypeStruct(send.shape, send.dtype),
        in_specs=[HBM_SPEC],
        out_specs=HBM_SPEC,
        scratch_shapes=[pltpu.SemaphoreType.DMA((7,)), pltpu.SemaphoreType.DMA((7,)), pltpu.SemaphoreType.DMA(())],
    )(send)


def _pick(n, cands):
    for t in cands:
        if n % t == 0:
            return t
    return n


def _mm(name, a, b, mode, out_dtype):
    if mode == "nn":
        (m, k), n = a.shape, b.shape[1]
    elif mode == "nt":
        (m, k), n = a.shape, b.shape[0]
    else:
        (k, m), n = a.shape, b.shape[1]
    tm = _pick(m, (512, 256, 128))
    tn = _pick(n, (1408, 1280, 1024, 768, 640, 512, 384, 256, 128))
    tk = _pick(k, (1024, 1408, 768, 640, 512, 256, 128))
    nk = k // tk
    swap = nk == 1 and a.size * a.dtype.itemsize * (n // tn) + b.size * b.dtype.itemsize < (
        a.size * a.dtype.itemsize + b.size * b.dtype.itemsize * (m // tm))
    ij = (lambda g0, g1: (g1, g0)) if swap else (lambda g0, g1: (g0, g1))
    if mode == "nn":
        a_spec = pl.BlockSpec((tm, tk), lambda g0, g1, kk: (ij(g0, g1)[0], kk))
        b_spec = pl.BlockSpec((tk, tn), lambda g0, g1, kk: (kk, ij(g0, g1)[1]))
        ca, cb = 1, 0
    elif mode == "nt":
        a_spec = pl.BlockSpec((tm, tk), lambda g0, g1, kk: (ij(g0, g1)[0], kk))
        b_spec = pl.BlockSpec((tn, tk), lambda g0, g1, kk: (ij(g0, g1)[1], kk))
        ca, cb = 1, 1
    else:
        a_spec = pl.BlockSpec((tk, tm), lambda g0, g1, kk: (kk, ij(g0, g1)[0]))
        b_spec = pl.BlockSpec((tk, tn), lambda g0, g1, kk: (kk, ij(g0, g1)[1]))
        ca, cb = 0, 0

    def kern(a_ref, b_ref, o_ref, acc):
        kk = pl.program_id(2)

        @pl.when(kk == 0)
        def _():
            acc[...] = jnp.zeros_like(acc)

        acc[...] += _dot(a_ref[...].astype(BF16), b_ref[...].astype(BF16), ca, cb)

        @pl.when(kk == nk - 1)
        def _():
            o_ref[...] = acc[...].astype(out_dtype)

    return pl.pallas_call(
        kern,
        name=name,
        grid=(n // tn, m // tm, nk) if swap else (m // tm, n // tn, nk),
        in_specs=[a_spec, b_spec],
        out_specs=pl.BlockSpec((tm, tn), lambda g0, g1, kk: ij(g0, g1)),
        out_shape=jax.ShapeDtypeStruct((m, n), out_dtype),
        scratch_shapes=[pltpu.VMEM((tm, tn), F32)],
        compiler_params=_cp(("parallel", "parallel", "arbitrary")),
    )(a, b)


def _rowwise(name, body, ins, outs, tm):
    n_rows = next(a.shape[0] for a, kind in ins if kind == "row")
    n = n_rows // tm
    in_specs = []
    for a, kind in ins:
        if kind == "full":
            in_specs.append(pl.BlockSpec(a.shape, lambda i, nd=a.ndim: (0,) * nd))
        elif kind == "row":
            in_specs.append(pl.BlockSpec((tm, a.shape[1]), lambda i: (i, 0)))
        elif kind == "prev":
            in_specs.append(pl.BlockSpec((tm, a.shape[1]), lambda i: (jnp.maximum(i - 1, 0), 0)))
        else:
            in_specs.append(pl.BlockSpec((tm, a.shape[1]), lambda i: (jnp.minimum(i + 1, n - 1), 0)))
    out_specs, out_shape = [], []
    for shp, dt, kind in outs:
        out_shape.append(jax.ShapeDtypeStruct(shp, dt))
        if kind == "row":
            out_specs.append(pl.BlockSpec((tm, shp[1]), lambda i: (i, 0)))
        else:
            out_specs.append(pl.BlockSpec(shp, lambda i, nd=len(shp): (0,) * nd))
    n_in = len(ins)

    def kern(*refs):
        i = pl.program_id(0)
        res = body(i, n, *[r if kind == "full" else r[...] for r, (_, kind) in zip(refs[:n_in], ins)])
        for (shp, dt, kind), val, o in zip(outs, res, refs[n_in:]):
            if kind == "row":
                o[...] = val.astype(dt)
            else:

                @pl.when(i == 0)
                def _(o=o):
                    o[...] = jnp.zeros_like(o)

                o[...] += val.astype(dt)

    return pl.pallas_call(
        kern,
        name=name,
        grid=(n,),
        in_specs=in_specs,
        out_specs=out_specs,
        out_shape=out_shape,
        compiler_params=_cp(("arbitrary",)),
    )(*[a for a, _ in ins])


def _tm(s, t):
    return min(s, t)


def _ln_fwd(x, h, g, b):
    s = x.shape[0]

    def body(i, n, xv, hv, gv, bv):
        r = ALPHA * xv + hv.astype(F32)
        mu = jnp.mean(r, axis=-1, keepdims=True)
        d = r - mu
        var = jnp.mean(d * d, axis=-1, keepdims=True)
        y = d * lax.rsqrt(var + LN_EPS) * gv[...] + bv[...]
        return y, y, r

    shp = (s, D_MODEL)
    return _rowwise("ln_fwd", body, [(x, "row"), (h, "row"), (g, "full"), (b, "full")],
                    [(shp, F32, "row"), (shp, BF16, "row"), (shp, F32, "row")], _tm(s, 256))


def _ln_bwd(a, bterm, r, g):
    s = r.shape[0]

    def body(i, n, *vals):
        if a is None:
            dyv, rv, gv = vals
        else:
            av, dyv, rv, gv = vals
            dyv = ALPHA * av + dyv
        mu = jnp.mean(rv, axis=-1, keepdims=True)
        d = rv - mu
        var = jnp.mean(d * d, axis=-1, keepdims=True)
        rstd = lax.rsqrt(var + LN_EPS)
        xhat = d * rstd
        dxh = dyv * gv[...]
        dr = rstd * (dxh - jnp.mean(dxh, axis=-1, keepdims=True) - xhat * jnp.mean(dxh * xhat, axis=-1, keepdims=True))
        return dr, dr, jnp.sum(dyv * xhat, axis=0, keepdims=True), jnp.sum(dyv, axis=0, keepdims=True)

    ins = ([] if a is None else [(a, "row")]) + [(bterm, "row"), (r, "row"), (g, "full")]
    shp = (s, D_MODEL)
    return _rowwise("ln_bwd" if a is None else "ln_bwd_res", body, ins,
                    [(shp, F32, "row"), (shp, BF16, "row"), ((1, D_MODEL), F32, "acc"), ((1, D_MODEL), F32, "acc")],
                    _tm(s, 256))


def _axpy(a, b):
    def body(i, n, av, bv):
        return (ALPHA * av + bv,)

    return _rowwise("axpy", body, [(a, "row"), (b, "row")], [(a.shape, F32, "row")], _tm(a.shape[0], 256))[0]


def _loss_head(y, target):
    s = y.shape[0]

    def body(i, n, yv, tv):
        err = yv - tv
        part = 0.5 * jnp.sum(jnp.mean(err * err, axis=-1, keepdims=True), axis=0, keepdims=True)
        return err * (1.0 / D_MODEL), part

    return _rowwise("loss_head", body, [(y, "row"), (target, "row")],
                    [((s, D_MODEL), F32, "row"), ((1, 1), F32, "acc")], _tm(s, 256))


def _swiglu_fwd(gu):
    s = gu.shape[0]

    def body(i, n, v):
        g, u = v[:, :D_FF].astype(F32), v[:, D_FF:].astype(F32)
        return (g * _sigmoid(g) * u,)

    return _rowwise("swiglu_fwd", body, [(gu, "row")], [((s, D_FF), BF16, "row")], _tm(s, 256))[0]


def _swiglu_bwd(da, gu):
    s = gu.shape[0]

    def body(i, n, dav, v):
        dav = dav.astype(F32)
        g, u = v[:, :D_FF].astype(F32), v[:, D_FF:].astype(F32)
        sg = _sigmoid(g)
        dg = dav * u * (sg * (1.0 + g * (1.0 - sg)))
        du = dav * (g * sg)
        return (jnp.concatenate([dg, du], axis=1),)

    return _rowwise("swiglu_bwd", body, [(da, "row"), (gu, "row")], [((s, 2 * D_FF), BF16, "row")], _tm(s, 256))[0]


def _rms_fwd(x, w):
    s, c = x.shape

    def body(i, n, xv, wv):
        rs = lax.rsqrt(jnp.mean(xv * xv, axis=-1, keepdims=True) + RMS_EPS)
        return (xv * rs * wv[...],)

    return _rowwise("rms_fwd", body, [(x, "row"), (w, "full")], [((s, c), BF16, "row")], _tm(s, 512))[0]


def _rms_bwd(dy, x, w):
    s, c = x.shape

    def body(i, n, dyv, xv, wv):
        rs = lax.rsqrt(jnp.mean(xv * xv, axis=-1, keepdims=True) + RMS_EPS)
        nv = xv * rs
        dn = dyv * wv[...]
        dx = rs * (dn - nv * jnp.mean(dn * nv, axis=-1, keepdims=True))
        return dx, jnp.sum(dyv * nv, axis=0, keepdims=True)

    return _rowwise("rms_bwd", body, [(dy, "row"), (x, "row"), (w, "full")],
                    [((s, c), F32, "row"), ((1, c), F32, "acc")], _tm(s, 512))


def _gated_rms_fwd(y, z, w):
    s, c = y.shape

    def body(i, n, yv, zv, wv):
        y2 = yv * (zv * _sigmoid(zv))
        rs = lax.rsqrt(jnp.mean(y2 * y2, axis=-1, keepdims=True) + RMS_EPS)
        return (y2 * rs * wv[...],)

    return _rowwise("gated_rms_fwd", body, [(y, "row"), (z, "row"), (w, "full")], [((s, c), BF16, "row")], _tm(s, 256))[0]


def _gated_rms_bwd(do, y, z, w):
    s, c = y.shape

    def body(i, n, dov, yv, zv, wv):
        sz = _sigmoid(zv)
        silu = zv * sz
        y2 = yv * silu
        rs = lax.rsqrt(jnp.mean(y2 * y2, axis=-1, keepdims=True) + RMS_EPS)
        nv = y2 * rs
        dn = dov * wv[...]
        dy2 = rs * (dn - nv * jnp.mean(dn * nv, axis=-1, keepdims=True))
        return dy2 * silu, dy2 * yv * (sz * (1.0 + zv * (1.0 - sz))), jnp.sum(dov * nv, axis=0, keepdims=True)

    return _rowwise("gated_rms_bwd", body, [(do, "row"), (y, "row"), (z, "row"), (w, "full")],
                    [((s, c), F32, "row"), ((s, c), F32, "row"), ((1, c), F32, "acc")], _tm(s, 256))


def _rope(name, r1, r2, cos, sin):
    def body(i, n, a, b, cv, sv):
        return a * cv - b * sv, b * cv + a * sv

    return _rowwise(name, body, [(r1, "row"), (r2, "row"), (cos, "row"), (sin, "row")],
                    [(r1.shape, F32, "row"), (r1.shape, F32, "row")], _tm(r1.shape[0], 512))


def _headsum_rope_bwd(d1_all, d2_all, cos, sin):
    s = d1_all.shape[0]
    half = MLA_ROPE // 2

    def body(i, n, a_all, b_all, cv, sv):
        rr = lax.broadcasted_iota(jnp.int32, (MLA_HEADS * half, half), 0)
        cc = lax.broadcasted_iota(jnp.int32, (MLA_HEADS * half, half), 1)
        sel = jnp.where(rr % half == cc, 1.0, 0.0).astype(F32)
        a = _dot(a_all, sel, 1, 0, HI)
        b = _dot(b_all, sel, 1, 0, HI)
        return a * cv + b * sv, b * cv - a * sv

    return _rowwise("headsum_rope_bwd", body, [(d1_all, "row"), (d2_all, "row"), (cos, "row"), (sin, "row")],
                    [((s, half), F32, "row"), ((s, half), F32, "row")], _tm(s, 512))


def _softplus_fwd(dtr, bias):
    def body(i, n, v, bv):
        return (_softplus(v + bv[...]),)

    return _rowwise("softplus_fwd", body, [(dtr, "row"), (bias, "full")], [(dtr.shape, F32, "row")], _tm(dtr.shape[0], 1024))[0]


def _softplus_bwd(ddt, dtr, bias):
    def body(i, n, dv, v, bv):
        d = dv * _sigmoid(v + bv[...])
        return d, jnp.sum(d, axis=0, keepdims=True)

    return _rowwise("softplus_bwd", body, [(ddt, "row"), (dtr, "row"), (bias, "full")],
                    [(dtr.shape, F32, "row"), ((1, dtr.shape[1]), F32, "acc")], _tm(dtr.shape[0], 1024))


def _rowdot(a, b):
    def body(i, n, av, bv):
        return (jnp.sum(av.astype(F32) * bv.astype(F32), axis=-1, keepdims=True),)

    return _rowwise("rowdot", body, [(a, "row"), (b, "row")], [((a.shape[0], 1), F32, "row")], _tm(a.shape[0], 2048))[0]


def _sum8(name, blocks):
    _, r, c = blocks.shape
    tr = _pick(r, (512, 256, 128, 64, 32, 16, 8))

    def kern(b_ref, o_ref):
        acc = b_ref[0].astype(F32)
        for d in range(1, N_DEV):
            acc = acc + b_ref[d].astype(F32)
        o_ref[...] = acc

    return pl.pallas_call(
        kern,
        name=name,
        grid=(r // tr,),
        in_specs=[pl.BlockSpec((N_DEV, tr, c), lambda i: (0, i, 0))],
        out_specs=pl.BlockSpec((tr, c), lambda i: (i, 0)),
        out_shape=jax.ShapeDtypeStruct((r, c), F32),
        compiler_params=_cp(("parallel",)),
    )(blocks)


def _adamw(name, w, g, m, v):
    r = w.shape[0]
    tm = _pick(r, (512, 256, 128, 64, 32, 16, 8))

    def body(i, n, wv, gv, mv, vv):
        m2 = ADAM_B1 * mv + (1.0 - ADAM_B1) * gv
        v2 = ADAM_B2 * vv + (1.0 - ADAM_B2) * (gv * gv)
        m_hat = m2 / (1.0 - ADAM_B1 ** ADAM_STEP)
        v_hat = v2 / (1.0 - ADAM_B2 ** ADAM_STEP)
        delta = -ADAM_LR * (m_hat / (jnp.sqrt(v_hat) + ADAM_EPS) + ADAM_WD * wv)
        return delta, m2, v2

    return _rowwise(name, body, [(w, "row"), (g, "row"), (m, "row"), (v, "row")],
                    [(w.shape, F32, "row")] * 3, tm)


def _pool_diff(i, t_rows, u, up):
    ucat = jnp.concatenate([jnp.where(i > 0, up, 0.0), u], axis=0)
    r = lax.broadcasted_iota(jnp.int32, (t_rows, 2 * t_rows), 0)
    cc = lax.broadcasted_iota(jnp.int32, (t_rows, 2 * t_rows), 1)
    lag = r + t_rows - cc
    t_col = i * t_rows + lax.broadcasted_iota(jnp.int32, (t_rows, 1), 0)
    diffs = []
    for gi, wdw in enumerate(POOL_WINDOWS):
        win = jnp.where((lag >= 0) & (lag < wdw), 1.0, 0.0).astype(F32)
        cnt = jnp.minimum(t_col + 1, wdw).astype(F32)
        ws = _dot(win, ucat[:, gi * POOL_GROUP:(gi + 1) * POOL_GROUP], 1, 0, HI)
        diffs.append(ws / cnt - u[:, gi * POOL_GROUP:(gi + 1) * POOL_GROUP])
    return diffs


def _pool_fwd(u, pw, sc):
    s = u.shape[0]
    tm = _tm(s, 256)

    def body(i, n, uv, upv, wv, scv):
        diffs = _pool_diff(i, tm, uv, upv)
        ys = [_dot(d.astype(BF16), wv[gi].astype(BF16), 1, 0) for gi, d in enumerate(diffs)]
        return (jnp.concatenate(ys, axis=1) * scv[...],)

    return _rowwise("pool_fwd", body, [(u, "row"), (u, "prev"), (pw, "full"), (sc, "full")],
                    [((s, POOL_WIDTH), BF16, "row")], tm)[0]


def _pool_bwd1(u, dy, pw, sc):
    s = u.shape[0]
    tm = _tm(s, 256)

    def body(i, n, uv, upv, dyv, wv, scv):
        diffs = _pool_diff(i, tm, uv, upv)
        scv = scv[...]
        dsc, dws, dds = [], [], []
        for gi, d in enumerate(diffs):
            sl = slice(gi * POOL_GROUP, (gi + 1) * POOL_GROUP)
            db, wb = d.astype(BF16), wv[gi].astype(BF16)
            yg = _dot(db, wb, 1, 0)
            dsc.append(jnp.sum(yg * dyv[:, sl], axis=0, keepdims=True))
            eb = (dyv[:, sl] * scv[:, sl]).astype(BF16)
            dws.append(_dot(db, eb, 0, 0)[None])
            dds.append(_dot(eb, wb, 1, 1))
        return jnp.concatenate(dds, axis=1), jnp.concatenate(dsc, axis=1), jnp.concatenate(dws, axis=0)

    return _rowwise("pool_bwd1", body, [(u, "row"), (u, "prev"), (dy, "row"), (pw, "full"), (sc, "full")],
                    [((s, POOL_WIDTH), F32, "row"), ((1, POOL_WIDTH), F32, "acc"), (pw.shape, F32, "acc")], tm)


def _pool_bwd2(dd):
    s = dd.shape[0]
    tm = _tm(s, 256)

    def body(i, n, dv, dnv):
        dcat = jnp.concatenate([dv, jnp.where(i < n - 1, dnv, 0.0)], axis=0)
        r = lax.broadcasted_iota(jnp.int32, (tm, 2 * tm), 0)
        cc = lax.broadcasted_iota(jnp.int32, (tm, 2 * tm), 1)
        lag = cc - r
        t_col = i * tm + lax.broadcasted_iota(jnp.int32, (2 * tm, 1), 0)
        outs = []
        for gi, wdw in enumerate(POOL_WINDOWS):
            sl = slice(gi * POOL_GROUP, (gi + 1) * POOL_GROUP)
            win = jnp.where((lag >= 0) & (lag < wdw), 1.0, 0.0).astype(F32)
            cnt = jnp.minimum(t_col + 1, wdw).astype(F32)
            outs.append(_dot(win, dcat[:, sl] / cnt, 1, 0, HI) - dv[:, sl])
        return (jnp.concatenate(outs, axis=1),)

    return _rowwise("pool_bwd2", body, [(dd, "row"), (dd, "next")], [((s, POOL_WIDTH), BF16, "row")], tm)[0]


def _shift_down(cur, other, j, tm):
    if j == 0:
        return cur
    rows = lax.broadcasted_iota(jnp.int32, cur.shape, 0)
    return jnp.where(rows < j, pltpu.roll(other, j, 0), pltpu.roll(cur, j, 0))


def _shift_up(cur, other, j, tm):
    if j == 0:
        return cur
    rows = lax.broadcasted_iota(jnp.int32, cur.shape, 0)
    return jnp.where(rows >= tm - j, pltpu.roll(other, tm - j, 0), pltpu.roll(cur, tm - j, 0))


def _conv_pre(i, tm, xv, xpv, wv, bv):
    xpv = jnp.where(i > 0, xpv, 0.0)
    taps = [_shift_down(xv, xpv, SSM_CONV - 1 - k, tm) for k in range(SSM_CONV)]
    pre = bv[...]
    for k in range(SSM_CONV):
        pre = pre + wv[k:k + 1, :] * taps[k]
    return pre, taps


def _conv_fwd(xbc, w, b):
    s = xbc.shape[0]
    tm = _tm(s, 256)

    def body(i, n, xv, xpv, wv, bv):
        pre, _ = _conv_pre(i, tm, xv, xpv, wv, bv)
        return (pre * _sigmoid(pre),)

    return _rowwise("conv_fwd", body, [(xbc, "row"), (xbc, "prev"), (w, "full"), (b, "full")],
                    [(xbc.shape, F32, "row")], tm)[0]


def _conv_bwd1(dxc, xbc, w, b):
    s, c = xbc.shape
    tm = _tm(s, 256)

    def body(i, n, dv, xv, xpv, wv, bv):
        pre, taps = _conv_pre(i, tm, xv, xpv, wv, bv)
        sg = _sigmoid(pre)
        dpre = dv * (sg * (1.0 + pre * (1.0 - sg)))
        tap_row = lax.broadcasted_iota(jnp.int32, (SSM_CONV, c), 0)
        dw = jnp.zeros((SSM_CONV, c), F32)
        for k in range(SSM_CONV):
            dw = dw + jnp.where(tap_row == k, jnp.sum(dpre * taps[k], axis=0, keepdims=True), 0.0)
        return dpre, dw, jnp.sum(dpre, axis=0, keepdims=True)

    return _rowwise("conv_bwd1", body, [(dxc, "row"), (xbc, "row"), (xbc, "prev"), (w, "full"), (b, "full")],
                    [((s, c), F32, "row"), ((SSM_CONV, c), F32, "acc"), ((1, c), F32, "acc")], tm)


def _conv_bwd2(dpre, w):
    s, c = dpre.shape
    tm = _tm(s, 256)

    def body(i, n, dv, dnv, wv):
        dnv = jnp.where(i < n - 1, dnv, 0.0)
        out = jnp.zeros_like(dv)
        for k in range(SSM_CONV):
            out = out + wv[k:k + 1, :] * _shift_up(dv, dnv, SSM_CONV - 1 - k, tm)
        return (out,)

    return _rowwise("conv_bwd2", body, [(dpre, "row"), (dpre, "next"), (w, "full")], [((s, c), BF16, "row")], tm)[0]


def _ssd_common(alog, dt_c, dt_r, tril, tri):
    a = -jnp.exp(alog)
    acs_c = _dot(tril, dt_c * a, 1, 0, HI)
    acs_r = _dot(dt_r * a, tril, 1, 1, HI)
    a_last = jnp.sum(dt_c * a, axis=0, keepdims=True)
    return a, acs_c, acs_r, a_last


def _ssd_fwd(xh, xht, dtc, dtr, bm, cm, alog, dsk):
    nh, s, p = xh.shape
    L = SSM_CHUNK
    nc = s // L
    E = nh // SSM_GROUPS

    def kern(xh_ref, xht_ref, dtc_ref, dtr_ref, b_ref, c_ref, alog_ref, dsk_ref, y_ref, hp_ref, h_s):
        c = pl.program_id(1)

        @pl.when(c == 0)
        def _():
            h_s[...] = jnp.zeros_like(h_s)

        bb = b_ref[...].astype(BF16)
        cb_ = c_ref[...].astype(BF16)
        cbm = _dot(cb_, bb, 1, 1)
        ri = lax.broadcasted_iota(jnp.int32, (L, L), 0)
        ci = lax.broadcasted_iota(jnp.int32, (L, L), 1)
        tri = ri >= ci
        tril = tri.astype(F32)
        hs = range(E)
        com = [_ssd_common(alog_ref[e], dtc_ref[e], dtr_ref[e], tril, tri) for e in hs]
        lam = [jnp.exp(jnp.where(tri, com[e][1] - com[e][2], NEG)) for e in hs]
        xs = [xh_ref[e] for e in hs]
        xdt = [(xs[e] * dtc_ref[e]).astype(BF16) for e in hs]
        hh = [h_s[e] for e in hs]
        y_diag = [_dot((cbm * lam[e]).astype(BF16), xdt[e], 1, 0) for e in hs]
        y_off = [_dot(cb_, hh[e].astype(BF16), 1, 1) for e in hs]
        xw = [(xht_ref[e] * (dtr_ref[e] * jnp.exp(com[e][3] - com[e][2]))).astype(BF16) for e in hs]
        st = [_dot(xw[e], bb, 1, 0) for e in hs]
        for e in hs:
            y_ref[e] = y_diag[e] + y_off[e] * jnp.exp(com[e][1]) + xs[e] * dsk_ref[e]
            hp_ref[e] = hh[e]
            h_s[e] = hh[e] * jnp.exp(com[e][3]) + st[e]

    return pl.pallas_call(
        kern,
        name="ssd_fwd",
        grid=(SSM_GROUPS, nc),
        in_specs=[
            pl.BlockSpec((E, L, p), lambda g, c: (g, c, 0)),
            pl.BlockSpec((E, p, L), lambda g, c: (g, 0, c)),
            pl.BlockSpec((E, L, 1), lambda g, c: (g, c, 0)),
            pl.BlockSpec((E, 1, L), lambda g, c: (g, 0, c)),
            pl.BlockSpec((L, SSM_STATE), lambda g, c: (c, g)),
            pl.BlockSpec((L, SSM_STATE), lambda g, c: (c, g)),
            pl.BlockSpec((E, 1, 1), lambda g, c: (g, 0, 0)),
            pl.BlockSpec((E, 1, 1), lambda g, c: (g, 0, 0)),
        ],
        out_specs=[
            pl.BlockSpec((E, L, p), lambda g, c: (g, c, 0)),
            pl.BlockSpec((E, None, p, SSM_STATE), lambda g, c: (g, c, 0, 0)),
        ],
        out_shape=[jax.ShapeDtypeStruct((nh, s, p), F32), jax.ShapeDtypeStruct((nh, nc, p, SSM_STATE), F32)],
        scratch_shapes=[pltpu.VMEM((E, p, SSM_STATE), F32)],
        compiler_params=_cp(("arbitrary", "arbitrary")),
    )(xh, xht, dtc, dtr, bm, cm, alog, dsk)


def _ssd_bwd(dy, dyt, xh, dtc, dtr, bm, cm, hprev, alog, dsk):
    nh, s, p = xh.shape
    L = SSM_CHUNK
    nc = s // L
    E = nh // SSM_GROUPS

    def kern(dy_ref, dyt_ref, xh_ref, dtc_ref, dtr_ref, b_ref, c_ref, hp_ref, alog_ref, dsk_ref,
             dx_ref, ddt_ref, db_ref, dc_ref, dalog_ref, dd_ref, dh_s):
        c = pl.program_id(1)

        @pl.when(c == 0)
        def _():
            dh_s[...] = jnp.zeros_like(dh_s)
            dalog_ref[...] = jnp.zeros_like(dalog_ref)
            dd_ref[...] = jnp.zeros_like(dd_ref)

        bb = b_ref[...].astype(BF16)
        cb_ = c_ref[...].astype(BF16)
        cbm = _dot(cb_, bb, 1, 1)
        cbt = _dot(bb, cb_, 1, 1)
        ri = lax.broadcasted_iota(jnp.int32, (L, L), 0)
        ci = lax.broadcasted_iota(jnp.int32, (L, L), 1)
        tri = ri >= ci
        trit = ci >= ri
        tril = tri.astype(F32)
        triu = trit.astype(F32)
        hs = range(E)
        com = [_ssd_common(alog_ref[e], dtc_ref[e], dtr_ref[e], tril, tri) for e in hs]
        lam = [jnp.exp(jnp.where(tri, com[e][1] - com[e][2], NEG)) for e in hs]
        lamt = [jnp.exp(jnp.where(trit, com[e][2] - com[e][1], NEG)) for e in hs]
        xs = [xh_ref[e] for e in hs]
        xdt = [xs[e] * dtc_ref[e] for e in hs]
        xdtb = [t.astype(BF16) for t in xdt]
        dyv = [dy_ref[e] for e in hs]
        dyb = [t.astype(BF16) for t in dyv]
        hh = [hp_ref[e] for e in hs]
        hb = [t.astype(BF16) for t in hh]
        dhn = [dh_s[e] for e in hs]
        dhnb = [t.astype(BF16) for t in dhn]
        ea_c = [jnp.exp(com[e][1]) for e in hs]
        decay_c = [jnp.exp(com[e][3] - com[e][1]) for e in hs]
        e_last = [jnp.exp(com[e][3]) for e in hs]
        gm = [_dot(dyb[e], xdtb[e], 1, 1) for e in hs]
        gt = [_dot(xdtb[e], dyb[e], 1, 1) for e in hs]
        bdh = [_dot(bb, dhnb[e], 1, 1) for e in hs]
        dxdt = [_dot((cbt * lamt[e]).astype(BF16), dyb[e], 1, 0) + bdh[e] * decay_c[e] for e in hs]
        dcb = [gm[e] * lam[e] for e in hs]
        dcbt = [gt[e] * lamt[e] for e in hs]
        yoff = [_dot(cb_, hb[e], 1, 1) * ea_c[e] for e in hs]
        dcs = [_dot(dcb[e].astype(BF16), bb, 1, 0) + _dot(dyb[e], hb[e], 1, 0) * ea_c[e] for e in hs]
        dbs = [_dot(dcbt[e].astype(BF16), cb_, 1, 0) + _dot(xdtb[e], dhnb[e], 1, 0) * decay_c[e] for e in hs]
        dyw = [(dyt_ref[e] * jnp.exp(com[e][2])).astype(BF16) for e in hs]
        dh_new = [dhn[e] * e_last[e] + _dot(dyw[e], cb_, 1, 0) for e in hs]
        db_acc, dc_acc = dbs[0], dcs[0]
        for e in range(1, E):
            db_acc = db_acc + dbs[e]
            dc_acc = dc_acc + dcs[e]
        for e in hs:
            a = com[e][0]
            tt = decay_c[e] * jnp.sum(xdt[e] * bdh[e], axis=1, keepdims=True)
            dacs = (jnp.sum(dcb[e] * cbm, axis=1, keepdims=True) - jnp.sum(dcbt[e] * cbt, axis=1, keepdims=True)
                    + jnp.sum(dyv[e] * yoff[e], axis=1, keepdims=True) - tt)
            tail = jnp.sum(tt, axis=0, keepdims=True) + e_last[e] * jnp.sum(
                jnp.sum(dhn[e] * hh[e], axis=1, keepdims=True), axis=0, keepdims=True)
            dda = _dot(triu, dacs, 1, 0, HI) + tail
            ddt_ref[e] = dda * a + jnp.sum(dxdt[e] * xs[e], axis=1, keepdims=True)
            dalog_ref[e] += jnp.sum(dda * dtc_ref[e], axis=0, keepdims=True) * a
            dd_ref[e] += jnp.sum(jnp.sum(dyv[e] * xs[e], axis=1, keepdims=True), axis=0, keepdims=True)
            dx_ref[e] = dxdt[e] * dtc_ref[e] + dyv[e] * dsk_ref[e]
            dh_s[e] = dh_new[e]
        db_ref[...] = db_acc
        dc_ref[...] = dc_acc

    rc = lambda c: nc - 1 - c
    return pl.pallas_call(
        kern,
        name="ssd_bwd",
        grid=(SSM_GROUPS, nc),
        in_specs=[
            pl.BlockSpec((E, L, p), lambda g, c: (g, rc(c), 0)),
            pl.BlockSpec((E, p, L), lambda g, c: (g, 0, rc(c))),
            pl.BlockSpec((E, L, p), lambda g, c: (g, rc(c), 0)),
            pl.BlockSpec((E, L, 1), lambda g, c: (g, rc(c), 0)),
            pl.BlockSpec((E, 1, L), lambda g, c: (g, 0, rc(c))),
            pl.BlockSpec((L, SSM_STATE), lambda g, c: (rc(c), g)),
            pl.BlockSpec((L, SSM_STATE), lambda g, c: (rc(c), g)),
            pl.BlockSpec((E, None, p, SSM_STATE), lambda g, c: (g, rc(c), 0, 0)),
            pl.BlockSpec((E, 1, 1), lambda g, c: (g, 0, 0)),
            pl.BlockSpec((E, 1, 1), lambda g, c: (g, 0, 0)),
        ],
        out_specs=[
            pl.BlockSpec((E, L, p), lambda g, c: (g, rc(c), 0)),
            pl.BlockSpec((E, L, 1), lambda g, c: (g, rc(c), 0)),
            pl.BlockSpec((L, SSM_STATE), lambda g, c: (rc(c), g)),
            pl.BlockSpec((L, SSM_STATE), lambda g, c: (rc(c), g)),
            pl.BlockSpec((E, 1, 1), lambda g, c: (g, 0, 0)),
            pl.BlockSpec((E, 1, 1), lambda g, c: (g, 0, 0)),
        ],
        out_shape=[
            jax.ShapeDtypeStruct((nh, s, p), F32),
            jax.ShapeDtypeStruct((nh, s, 1), F32),
            jax.ShapeDtypeStruct((s, SSM_GROUPS * SSM_STATE), F32),
            jax.ShapeDtypeStruct((s, SSM_GROUPS * SSM_STATE), F32),
            jax.ShapeDtypeStruct((nh, 1, 1), F32),
            jax.ShapeDtypeStruct((nh, 1, 1), F32),
        ],
        scratch_shapes=[pltpu.VMEM((E, p, SSM_STATE), F32)],
        compiler_params=_cp(("arbitrary", "arbitrary")),
    )(dy, dyt, xh, dtc, dtr, bm, cm, hprev, alog, dsk)


def _fgate_fwd(fl, bias):
    h, s = fl.shape
    t = _tm(s, 512)

    def kern(fl_ref, b_ref, o_ref, carry):
        i = pl.program_id(0)

        @pl.when(i == 0)
        def _():
            carry[...] = jnp.zeros_like(carry)

        lf = -_softplus(-(fl_ref[...] + b_ref[...]))
        ri = lax.broadcasted_iota(jnp.int32, (t, t), 0)
        ci = lax.broadcasted_iota(jnp.int32, (t, t), 1)
        o_ref[...] = _dot(lf, (ri <= ci).astype(F32), 1, 0, HI) + carry[...]
        carry[...] += jnp.sum(lf, axis=1, keepdims=True)

    return pl.pallas_call(
        kern,
        name="fgate_fwd",
        grid=(s // t,),
        in_specs=[pl.BlockSpec((h, t), lambda i: (0, i)), pl.BlockSpec((h, 1), lambda i: (0, 0))],
        out_specs=pl.BlockSpec((h, t), lambda i: (0, i)),
        out_shape=jax.ShapeDtypeStruct((h, s), F32),
        scratch_shapes=[pltpu.VMEM((h, 1), F32)],
        compiler_params=_cp(("arbitrary",)),
    )(fl, bias)


def _fgate_bwd(dqe, dke, col, fl, bias):
    h, s = fl.shape
    w = dqe.shape[-1]
    t = _tm(s, 512)
    n = s // t

    def kern(dq_ref, dk_ref, fl_ref, b_ref, o_ref, db_ref, carry):
        i = pl.program_id(0)

        @pl.when(i == 0)
        def _():
            carry[...] = jnp.zeros_like(carry)
            db_ref[...] = jnp.zeros_like(db_ref)

        sel = (lax.broadcasted_iota(jnp.int32, (1, w), 1) == col).astype(F32)
        hid = lax.broadcasted_iota(jnp.int32, (h, t), 0)
        d = jnp.zeros((h, t), F32)
        for hh in range(h):
            r = _dot(sel, dq_ref[hh], 1, 1, HI) - _dot(sel, dk_ref[hh], 1, 1, HI)
            d = d + jnp.where(hid == hh, r, 0.0)
        ri = lax.broadcasted_iota(jnp.int32, (t, t), 0)
        ci = lax.broadcasted_iota(jnp.int32, (t, t), 1)
        rev = _dot(d, (ri >= ci).astype(F32), 1, 0, HI) + carry[...]
        carry[...] += jnp.sum(d, axis=1, keepdims=True)
        dz = rev * _sigmoid(-(fl_ref[...] + b_ref[...]))
        o_ref[...] = dz
        db_ref[...] += jnp.sum(dz, axis=1, keepdims=True)

    rev_blk = lambda i: (0, n - 1 - i)
    return pl.pallas_call(
        kern,
        name="fgate_bwd",
        grid=(n,),
        in_specs=[pl.BlockSpec((h, t, w), lambda i: (0, n - 1 - i, 0)), pl.BlockSpec((h, t, w), lambda i: (0, n - 1 - i, 0)),
                  pl.BlockSpec((h, t), rev_blk), pl.BlockSpec((h, 1), lambda i: (0, 0))],
        out_specs=[pl.BlockSpec((h, t), rev_blk), pl.BlockSpec((h, 1), lambda i: (0, 0))],
        out_shape=[jax.ShapeDtypeStruct((h, s), F32), jax.ShapeDtypeStruct((h, 1), F32)],
        scratch_shapes=[pltpu.VMEM((h, 1), F32)],
        compiler_params=_cp(("arbitrary",)),
    )(dqe, dke, fl, bias)


HPS_FWD = 8
HPS_BWD = 4


def _attn_scores(q, k, fq, fk, scale, masked, t):
    sc = _dot(q, k, 1, 1)
    if scale is not None:
        sc = sc * scale
    if fq is not None:
        sc = sc + fq - fk
    if masked:
        rows = lax.broadcasted_iota(jnp.int32, (t, t), 0)
        cols = lax.broadcasted_iota(jnp.int32, (t, t), 1)
        sc = jnp.where(cols <= rows, sc, NEG)
    return sc


def _flash_fwd(name, q, k, v, fq, fk, scale):
    nh, s, dk = q.shape
    dv = v.shape[-1]
    t = _tm(s, 512)
    nq = s // t
    bias = fq is not None
    HPS = HPS_FWD

    pairs = [(i, j) for i in range(nq) for j in range(i + 1)]
    qi_of = jnp.asarray(np.array([p[0] for p in pairs], np.int32))
    kj_of = jnp.asarray(np.array([p[1] for p in pairs], np.int32))

    def kern(qi_ref, kj_ref, *refs):
        if bias:
            q_ref, k_ref, v_ref, fq_ref, fk_ref, o_ref, lse_ref, m_s, l_s, acc_s = refs
        else:
            q_ref, k_ref, v_ref, o_ref, lse_ref, m_s, l_s, acc_s = refs
        qi, kj = qi_ref[pl.program_id(1)], kj_ref[pl.program_id(1)]

        @pl.when(kj == 0)
        def _():
            m_s[...] = jnp.full_like(m_s, NEG)
            l_s[...] = jnp.zeros_like(l_s)
            acc_s[...] = jnp.zeros_like(acc_s)

        def step(masked):
            scs = [_attn_scores(q_ref[hh], k_ref[hh], fq_ref[hh] if bias else None, fk_ref[hh] if bias else None,
                                scale, masked, t) for hh in range(HPS)]
            pbs, corrs = [], []
            for hh in range(HPS):
                m_old = m_s[hh]
                m_new = jnp.maximum(m_old, jnp.max(scs[hh], axis=1, keepdims=True))
                corr = jnp.exp(m_old - m_new)
                p = jnp.exp(scs[hh] - m_new)
                l_s[hh] = corr * l_s[hh] + jnp.sum(p, axis=1, keepdims=True)
                m_s[hh] = m_new
                pbs.append(p.astype(BF16))
                corrs.append(corr)
            for hh in range(HPS):
                acc_s[hh] = acc_s[hh] * corrs[hh] + _dot(pbs[hh], v_ref[hh], 1, 0)

        @pl.when(kj < qi)
        def _():
            step(False)

        @pl.when(kj == qi)
        def _():
            step(True)
            o_ref[...] = acc_s[...] / l_s[...]
            lse_ref[...] = m_s[...] + jnp.log(l_s[...])

    qspec = lambda d: pl.BlockSpec((HPS, t, d), lambda h, p, qi_r, kj_r: (h, qi_r[p], 0))
    kspec = lambda d: pl.BlockSpec((HPS, t, d), lambda h, p, qi_r, kj_r: (h, kj_r[p], 0))
    in_specs = [qspec(dk), kspec(dk), kspec(dv)]
    args = [q, k, v]
    if bias:
        in_specs += [qspec(1), pl.BlockSpec((HPS, 1, t), lambda h, p, qi_r, kj_r: (h, 0, kj_r[p]))]
        args += [fq, fk]
    return pl.pallas_call(
        kern,
        name=name,
        grid_spec=pltpu.PrefetchScalarGridSpec(
            num_scalar_prefetch=2,
            grid=(nh // HPS, len(pairs)),
            in_specs=in_specs,
            out_specs=[qspec(dv), qspec(1)],
            scratch_shapes=[pltpu.VMEM((HPS, t, 1), F32), pltpu.VMEM((HPS, t, 1), F32),
                            pltpu.VMEM((HPS, t, dv), F32)],
        ),
        out_shape=[jax.ShapeDtypeStruct((nh, s, dv), F32), jax.ShapeDtypeStruct((nh, s, 1), F32)],
        compiler_params=_cp(("parallel", "arbitrary")),
    )(qi_of, kj_of, *args)


def _flash_bwd(name, q, k, v, do, lse, delta, fq, fk, qg, kg, scale, dq_scale, dk_scale):
    nh, s, dk = q.shape
    dv = v.shape[-1]
    dg = qg.shape[-1]
    t = _tm(s, 512)
    nq = s // t
    bias = fq is not None
    ext = qg is not q
    HPS = HPS_BWD

    pairs = [(j, i) for j in range(nq) for i in range(j, nq)]
    kb_of = jnp.asarray(np.array([p[0] for p in pairs], np.int32))
    qi_of = jnp.asarray(np.array([p[1] for p in pairs], np.int32))

    def kern(kb_ref, qi_ref, *refs):
        refs = list(refs)
        q_ref, k_ref, v_ref, do_ref, lse_ref, dl_ref = refs[:6]
        del refs[:6]
        fq_ref, fk_ref = (refs.pop(0), refs.pop(0)) if bias else (None, None)
        qg_ref, kg_ref = (refs.pop(0), refs.pop(0)) if ext else (q_ref, k_ref)
        dq_ref, dk_ref, dv_ref, dk_s, dv_s = refs
        kb, qi = kb_ref[pl.program_id(1)], qi_ref[pl.program_id(1)]

        @pl.when(pl.program_id(1) == 0)
        def _():
            dq_ref[...] = jnp.zeros_like(dq_ref)

        @pl.when(qi == kb)
        def _():
            dk_s[...] = jnp.zeros_like(dk_s)
            dv_s[...] = jnp.zeros_like(dv_s)

        def step(masked):
            rows = pl.ds(pl.multiple_of(qi * t, t), t)
            heads = range(HPS)
            scs = [_attn_scores(q_ref[hh], k_ref[hh], fq_ref[hh] if bias else None, fk_ref[hh] if bias else None,
                                scale, masked, t) for hh in heads]
            dps = [_dot(do_ref[hh], v_ref[hh], 1, 1) for hh in heads]
            ps = [jnp.exp(scs[hh] - lse_ref[hh]) for hh in heads]
            pbs = [p.astype(BF16) for p in ps]
            for hh in heads:
                dv_s[hh] += _dot(pbs[hh], do_ref[hh], 0, 0)
            dsbs = [(ps[hh] * (dps[hh] - dl_ref[hh])).astype(BF16) for hh in heads]
            for hh in heads:
                dk_s[hh] += _dot(dsbs[hh], qg_ref[hh], 0, 0)
            for hh in heads:
                dq_ref[hh, rows, :] += _dot(dsbs[hh], kg_ref[hh], 1, 0) * dq_scale

        @pl.when(qi > kb)
        def _():
            step(False)

        @pl.when(qi == kb)
        def _():
            step(True)

        @pl.when(qi == nq - 1)
        def _():
            dk_ref[...] = dk_s[...] if dk_scale is None else dk_s[...] * dk_scale
            dv_ref[...] = dv_s[...]

    qspec = lambda d: pl.BlockSpec((HPS, t, d), lambda h, p, kb_r, qi_r: (h, qi_r[p], 0))
    kspec = lambda d: pl.BlockSpec((HPS, t, d), lambda h, p, kb_r, qi_r: (h, kb_r[p], 0))
    in_specs = [qspec(dk), kspec(dk), kspec(dv), qspec(dv), qspec(1), qspec(1)]
    args = [q, k, v, do, lse, delta]
    if bias:
        in_specs += [qspec(1), pl.BlockSpec((HPS, 1, t), lambda h, p, kb_r, qi_r: (h, 0, kb_r[p]))]
        args += [fq, fk]
    if ext:
        in_specs += [qspec(dg), kspec(dg)]
        args += [qg, kg]
    return pl.pallas_call(
        kern,
        name=name,
        grid_spec=pltpu.PrefetchScalarGridSpec(
            num_scalar_prefetch=2,
            grid=(nh // HPS, len(pairs)),
            in_specs=in_specs,
            out_specs=[pl.BlockSpec((HPS, s, dg), lambda h, p, kb_r, qi_r: (h, 0, 0), pipeline_mode=pl.Buffered(1)),
                       kspec(dg), kspec(dv)],
            scratch_shapes=[pltpu.VMEM((HPS, t, dg), F32), pltpu.VMEM((HPS, t, dv), F32)],
        ),
        out_shape=[jax.ShapeDtypeStruct((nh, s, dg), F32), jax.ShapeDtypeStruct((nh, s, dg), F32),
                   jax.ShapeDtypeStruct((nh, s, dv), F32)],
        compiler_params=_cp(("arbitrary", "arbitrary")),
    )(kb_of, qi_of, *args)


def _heads(t, nh):
    s = t.shape[0]
    return t.reshape(s, nh, -1).transpose(1, 0, 2)


def _unheads(t):
    nh, s, d = t.shape
    return t.transpose(1, 0, 2).reshape(s, nh * d)


def _perm_uq(w):
    r = w.shape[0]
    w3 = w.reshape(r, MLA_HEADS, MLA_NOPE + MLA_ROPE)
    half = MLA_ROPE // 2
    return jnp.concatenate([w3[:, :, :MLA_NOPE].reshape(r, -1), w3[:, :, MLA_NOPE:MLA_NOPE + half].reshape(r, -1),
                            w3[:, :, MLA_NOPE + half:].reshape(r, -1)], axis=1)


def _unperm_uq(w):
    r = w.shape[0]
    half = MLA_ROPE // 2
    n0 = MLA_HEADS * MLA_NOPE
    n1 = n0 + MLA_HEADS * half
    return jnp.concatenate([w[:, :n0].reshape(r, MLA_HEADS, MLA_NOPE), w[:, n0:n1].reshape(r, MLA_HEADS, half),
                            w[:, n1:].reshape(r, MLA_HEADS, half)], axis=2).reshape(r, -1)


def _perm_ukv(w):
    r = w.shape[0]
    w3 = w.reshape(r, MLA_HEADS, MLA_NOPE + MLA_V)
    return jnp.concatenate([w3[:, :, :MLA_NOPE].reshape(r, -1), w3[:, :, MLA_NOPE:].reshape(r, -1)], axis=1)


def _unperm_ukv(w):
    r = w.shape[0]
    n0 = MLA_HEADS * MLA_NOPE
    return jnp.concatenate([w[:, :n0].reshape(r, MLA_HEADS, MLA_NOPE), w[:, n0:].reshape(r, MLA_HEADS, MLA_V)],
                           axis=2).reshape(r, -1)


_ODD_CUTS = np.cumsum([0, FOX_WIDTH, FOX_WIDTH, FOX_WIDTH, FOX_HEADS, MLA_Q_RANK, MLA_KV_RANK, MLA_ROPE]).tolist()
_ODD_ORDER = (0, 1, 2, 4, 5, 6, 3)


def _perm_odd_in(w):
    parts = [w[:, _ODD_CUTS[j]:_ODD_CUTS[j + 1]] for j in _ODD_ORDER]
    parts.append(jnp.zeros((w.shape[0], ODD_IN_PAD - ODD_IN), w.dtype))
    return jnp.concatenate(parts, axis=1)


def _unperm_odd_in(w):
    widths = [_ODD_CUTS[j + 1] - _ODD_CUTS[j] for j in _ODD_ORDER]
    offs = np.cumsum([0] + widths).tolist()
    pieces = {j: w[:, offs[n]:offs[n + 1]] for n, j in enumerate(_ODD_ORDER)}
    return jnp.concatenate([pieces[j] for j in range(7)], axis=1)


def _pad_cols(w, n):
    return jnp.concatenate([w, jnp.zeros((w.shape[0], n - w.shape[1]), w.dtype)], axis=1)


_BIG = (("even_w_in", 2), ("even_w_out", 1), ("odd_w_in", 2), ("w_uq", 2), ("w_ukv", 2), ("odd_w_out", 1),
        ("ffn_w_gate", 2), ("ffn_w_up", 2), ("ffn_w_down", 1))
_PACK_COLS = 1024
SMALL_LANES = 128


def _unshard(blocks, shp, ax):
    t = jnp.moveaxis(blocks.reshape((N_DEV,) + tuple(shp)), 0, ax)
    full = list(shp)
    full[ax] = shp[ax] * N_DEV
    return t.reshape(full)


def _reshard(full, ax, cols):
    shp = list(full.shape)
    t = full.reshape(shp[:ax] + [N_DEV, shp[ax] // N_DEV] + shp[ax + 1:])
    return jnp.moveaxis(t, ax, 0).reshape(N_DEV, -1, cols)


def kernel(x, even_w_in, pool_w, pool_scale, conv_w, conv_b, dt_bias, a_log, d_skip, ssm_norm_w, even_w_out, odd_w_in, fgate_b, q_norm_w, w_uq, kv_norm_w, w_ukv, odd_w_out, ffn_w_gate, ffn_w_up, ffn_w_down, ln_mix_g, ln_mix_b, ln_ffn_g, ln_ffn_b, loss_target, m_even_w_in, m_pool_w, m_pool_scale, m_conv_w, m_conv_b, m_dt_bias, m_a_log, m_d_skip, m_ssm_norm_w, m_even_w_out, m_odd_w_in, m_fgate_b, m_q_norm_w, m_w_uq, m_kv_norm_w, m_w_ukv, m_odd_w_out, m_ffn_w_gate, m_ffn_w_up, m_ffn_w_down, m_ln_mix_g, m_ln_mix_b, m_ln_ffn_g, m_ln_ffn_b, v_even_w_in, v_pool_w, v_pool_scale, v_conv_w, v_conv_b, v_dt_bias, v_a_log, v_d_skip, v_ssm_norm_w, v_even_w_out, v_odd_w_in, v_fgate_b, v_q_norm_w, v_w_uq, v_kv_norm_w, v_w_ukv, v_odd_w_out, v_ffn_w_gate, v_ffn_w_up, v_ffn_w_down, v_ln_mix_g, v_ln_mix_b, v_ln_ffn_g, v_ln_ffn_b):
    P = dict(even_w_in=even_w_in, pool_w=pool_w, pool_scale=pool_scale, conv_w=conv_w, conv_b=conv_b, dt_bias=dt_bias,
             a_log=a_log, d_skip=d_skip, ssm_norm_w=ssm_norm_w, even_w_out=even_w_out, odd_w_in=odd_w_in,
             fgate_b=fgate_b, q_norm_w=q_norm_w, w_uq=w_uq, kv_norm_w=kv_norm_w, w_ukv=w_ukv, odd_w_out=odd_w_out,
             ffn_w_gate=ffn_w_gate, ffn_w_up=ffn_w_up, ffn_w_down=ffn_w_down, ln_mix_g=ln_mix_g, ln_mix_b=ln_mix_b,
             ln_ffn_g=ln_ffn_g, ln_ffn_b=ln_ffn_b)
    M = dict(even_w_in=m_even_w_in, pool_w=m_pool_w, pool_scale=m_pool_scale, conv_w=m_conv_w, conv_b=m_conv_b,
             dt_bias=m_dt_bias, a_log=m_a_log, d_skip=m_d_skip, ssm_norm_w=m_ssm_norm_w, even_w_out=m_even_w_out,
             odd_w_in=m_odd_w_in, fgate_b=m_fgate_b, q_norm_w=m_q_norm_w, w_uq=m_w_uq, kv_norm_w=m_kv_norm_w,
             w_ukv=m_w_ukv, odd_w_out=m_odd_w_out, ffn_w_gate=m_ffn_w_gate, ffn_w_up=m_ffn_w_up,
             ffn_w_down=m_ffn_w_down, ln_mix_g=m_ln_mix_g, ln_mix_b=m_ln_mix_b, ln_ffn_g=m_ln_ffn_g,
             ln_ffn_b=m_ln_ffn_b)
    V = dict(even_w_in=v_even_w_in, pool_w=v_pool_w, pool_scale=v_pool_scale, conv_w=v_conv_w, conv_b=v_conv_b,
             dt_bias=v_dt_bias, a_log=v_a_log, d_skip=v_d_skip, ssm_norm_w=v_ssm_norm_w, even_w_out=v_even_w_out,
             odd_w_in=v_odd_w_in, fgate_b=v_fgate_b, q_norm_w=v_q_norm_w, w_uq=v_w_uq, kv_norm_w=v_kv_norm_w,
             w_ukv=v_w_ukv, odd_w_out=v_odd_w_out, ffn_w_gate=v_ffn_w_gate, ffn_w_up=v_ffn_w_up,
             ffn_w_down=v_ffn_w_down, ln_mix_g=v_ln_mix_g, ln_mix_b=v_ln_mix_b, ln_ffn_g=v_ln_ffn_g,
             ln_ffn_b=v_ln_ffn_b)
    names = list(P)
    s = x.shape[1]
    me = 4 * lax.axis_index("x") + 2 * lax.axis_index("y") + lax.axis_index("c")

    big_rows = [math.prod(P[n].shape) // _PACK_COLS for n, _ in _BIG]
    pad_rows = [-(-nr // 16) * 16 for nr in big_rows]
    rows_big = sum(pad_rows)
    packed = jnp.concatenate([
        jnp.pad(P[n].astype(BF16).reshape(-1), (0, (pr - nr) * _PACK_COLS))
        for (n, _), nr, pr in zip(_BIG, big_rows, pad_rows)]).reshape(rows_big, _PACK_COLS)
    gathered = _all_gather("ag_weights", packed)
    W = {}
    off = 0
    for (n, ax), nr, pr in zip(_BIG, big_rows, pad_rows):
        W[n] = _unshard(gathered[:, off:off + nr, :], P[n].shape, ax)
        off += pr
    cw_rows, qn_rows, kv_n = conv_w.size // SMALL_LANES, q_norm_w.size // SMALL_LANES, kv_norm_w.size
    small_rows = -(-(cw_rows + qn_rows + 1) // 8) * 8
    small_sh = jnp.concatenate([conv_w.reshape(-1), q_norm_w.reshape(-1), kv_norm_w.reshape(-1),
                                jnp.zeros((SMALL_LANES - kv_n,), F32),
                                jnp.zeros(((small_rows - cw_rows - qn_rows - 1) * SMALL_LANES,), F32)])
    g_small = _all_gather("ag_small_weights", small_sh.reshape(small_rows, SMALL_LANES))
    conv_w_full = _unshard(g_small[:, :cw_rows, :], conv_w.shape, 2)
    q_norm_full = _unshard(g_small[:, cw_rows:cw_rows + qn_rows, :], q_norm_w.shape, 1)
    kv_norm_full = _unshard(g_small[:, cw_rows + qn_rows:cw_rows + qn_rows + 1, :kv_n], kv_norm_w.shape, 1)

    w_in_e = [_pad_cols(W["even_w_in"][i], EVEN_IN_PAD) for i in range(2)]
    w_in_o = [_perm_odd_in(W["odd_w_in"][i]) for i in range(2)]
    w_uq_p = [_perm_uq(W["w_uq"][i]) for i in range(2)]
    w_ukv_p = [_perm_ukv(W["w_ukv"][i]) for i in range(2)]
    w_gu = [jnp.concatenate([W["ffn_w_gate"][l], W["ffn_w_up"][l]], axis=1) for l in range(DEPTH)]

    pos = jnp.arange(s, dtype=F32)
    half = MLA_ROPE // 2
    freqs = jnp.power(ROPE_THETA, -jnp.arange(half, dtype=F32) / half)
    ang = pos[:, None] * freqs[None, :]
    cos16, sin16 = jnp.cos(ang), jnp.sin(ang)
    cos128, sin128 = jnp.tile(cos16, (1, MLA_HEADS)), jnp.tile(sin16, (1, MLA_HEADS))
    row = lambda t: t.reshape(1, -1)

    xcur = x[0]
    xb = xcur.astype(BF16)
    saved = []
    for l in range(DEPTH):
        i = l // 2
        sv = dict(x_in_b=xb)
        if l % 2 == 0:
            proj = _mm("mm_in_even", xb, w_in_e[i], "nn", F32)
            u, z = proj[:, :512], proj[:, 512:1536]
            xbc, dtraw = proj[:, 1536:3072], proj[:, 3072:3088]
            ypool = _pool_fwd(u, pool_w[i], row(pool_scale[i]))
            xc = _conv_fwd(xbc, conv_w_full[i], row(conv_b[i]))
            dt = _softplus_fwd(dtraw, row(dt_bias[i]))
            xh = _heads(xc[:, :SSM_D_INNER], SSM_HEADS)
            dtc = dt.T[:, :, None]
            dtr = dt.T[:, None, :]
            bm, cm = xc[:, SSM_D_INNER:SSM_D_INNER + 256], xc[:, SSM_D_INNER + 256:]
            alog3, dsk3 = a_log[i].reshape(-1, 1, 1), d_skip[i].reshape(-1, 1, 1)
            yh, hprev = _ssd_fwd(xh, xh.transpose(0, 2, 1), dtc, dtr, bm, cm, alog3, dsk3)
            y_ssm = _unheads(yh)
            yn = _gated_rms_fwd(y_ssm, z, row(ssm_norm_w[i]))
            mix = jnp.concatenate([ypool, yn], axis=1)
            h = _mm("mm_out_even", mix, W["even_w_out"][i], "nn", F32)
            sv.update(u=u, z=z, xbc=xbc, dtraw=dtraw, xh=xh, dtc=dtc, dtr=dtr, bm=bm, cm=cm, hprev=hprev,
                      y_ssm=y_ssm, mix=mix)
        else:
            proj = _mm("mm_in_odd", xb, w_in_o[i], "nn", F32)
            qf, kf, vf = proj[:, :512], proj[:, 512:1024], proj[:, 1024:1536]
            cq, ckv = proj[:, 1536:2048], proj[:, 2048:2304]
            kr, fl = proj[:, 2304:2336], proj[:, 2336:2344]
            fl = fl.T
            fcum = _fgate_fwd(fl, fgate_b[i][:, None])
            fq_ = fcum[:, :, None]
            fk_ = fcum[:, None, :]
            qh, kh, vh = (_heads(t.astype(BF16), FOX_HEADS) for t in (qf * FOX_SCALE, kf, vf))
            o_fox, lse_fox = _flash_fwd("fox_fwd", qh, kh, vh, fq_, fk_, None)
            qn = _rms_fwd(cq, row(q_norm_full[i]))
            qp = _mm("mm_uq", qn, w_uq_p[i], "nn", F32)
            q1, q2 = _rope("rope_q", qp[:, 512:640], qp[:, 640:768], cos128, sin128)
            kvn = _rms_fwd(ckv, row(kv_norm_full[i]))
            kvp = _mm("mm_ukv", kvn, w_ukv_p[i], "nn", F32)
            k1, k2 = _rope("rope_k", kr[:, :half], kr[:, half:], cos16, sin16)
            zpad = jnp.zeros((MLA_HEADS, s, MLA_DK_PAD - MLA_NOPE - MLA_ROPE), BF16)
            qm = jnp.concatenate([_heads(qp[:, :512], MLA_HEADS), _heads(q1, MLA_HEADS), _heads(q2, MLA_HEADS)],
                                 axis=2).astype(BF16)
            qm = jnp.concatenate([qm, zpad], axis=2)
            krope = jnp.broadcast_to(jnp.concatenate([k1, k2], axis=1)[None], (MLA_HEADS, s, MLA_ROPE))
            km = jnp.concatenate([_heads(kvp[:, :512], MLA_HEADS), krope], axis=2).astype(BF16)
            km = jnp.concatenate([km, zpad], axis=2)
            vm = _heads(kvp[:, 512:], MLA_HEADS).astype(BF16)
            zero_q, zero_k = jnp.zeros((MLA_HEADS, s, 1), F32), jnp.zeros((MLA_HEADS, 1, s), F32)
            o_mla, lse_mla = _flash_fwd("mla_fwd", qm, km, vm, zero_q, zero_k, MLA_SCALE)
            mix = jnp.concatenate([_unheads(o_fox), _unheads(o_mla)], axis=1).astype(BF16)
            h = _mm("mm_out_odd", mix, W["odd_w_out"][i], "nn", F32)
            sv.update(fl=fl, fq=fq_, fk=fk_, qh=qh, kh=kh, vh=vh, o_fox=o_fox, lse_fox=lse_fox, cq=cq, ckv=ckv,
                      qn=qn, kvn=kvn, qm=qm, km=km, vm=vm, o_mla=o_mla, lse_mla=lse_mla, mix=mix)
        y1, y1b, r1 = _ln_fwd(xcur, h, row(ln_mix_g[l]), row(ln_mix_b[l]))
        gu = _mm("mm_ffn_in", y1b, w_gu[l], "nn", BF16)
        act = _swiglu_fwd(gu)
        h2 = _mm("mm_ffn_out", act, W["ffn_w_down"][l], "nn", F32)
        y2, y2b, r2 = _ln_fwd(y1, h2, row(ln_ffn_g[l]), row(ln_ffn_b[l]))
        sv.update(r1=r1, y1b=y1b, gu=gu, act=act, r2=r2)
        saved.append(sv)
        xcur, xb = y2, y2b

    dy, loss_part = _loss_head(xcur, loss_target[0])
    loss = lax.psum(loss_part[0, 0], ("x", "y", "c"))

    G = {n: [None] * P[n].shape[0] for n in names}
    acur, dcur = None, dy
    for l in reversed(range(DEPTH)):
        i = l // 2
        sv = saved[l]
        dr2, dr2b, dg, db = _ln_bwd(acur, dcur, sv["r2"], row(ln_ffn_g[l]))
        G["ln_ffn_g"][l], G["ln_ffn_b"][l] = dg[0], db[0]
        G["ffn_w_down"][l] = _mm("mm_dw_ffn_out", sv["act"], dr2b, "tn", F32)
        dact = _mm("mm_dx_ffn_out", dr2b, W["ffn_w_down"][l], "nt", BF16)
        dgu = _swiglu_bwd(dact, sv["gu"])
        dwgu = _mm("mm_dw_ffn_in", sv["y1b"], dgu, "tn", F32)
        G["ffn_w_gate"][l], G["ffn_w_up"][l] = dwgu[:, :D_FF], dwgu[:, D_FF:]
        dy1 = _mm("mm_dx_ffn_in", dgu, w_gu[l], "nt", F32)
        dr1, dr1b, dg, db = _ln_bwd(dr2, dy1, sv["r1"], row(ln_mix_g[l]))
        G["ln_mix_g"][l], G["ln_mix_b"][l] = dg[0], db[0]
        if l % 2 == 0:
            G["even_w_out"][i] = _mm("mm_dw_out_even", sv["mix"], dr1b, "tn", F32)
            dmix = _mm("mm_dx_out_even", dr1b, W["even_w_out"][i], "nt", F32)
            dd, dsc, dpw = _pool_bwd1(sv["u"], dmix[:, :POOL_WIDTH], pool_w[i], row(pool_scale[i]))
            G["pool_scale"][i], G["pool_w"][i] = dsc[0], dpw
            du = _pool_bwd2(dd)
            dys, dz, dnw = _gated_rms_bwd(dmix[:, POOL_WIDTH:], sv["y_ssm"], sv["z"], row(ssm_norm_w[i]))
            G["ssm_norm_w"][i] = dnw[0]
            dyh = _heads(dys, SSM_HEADS)
            alog3, dsk3 = a_log[i].reshape(-1, 1, 1), d_skip[i].reshape(-1, 1, 1)
            dxh, ddt3, dbm, dcm, dalog, ddsk = _ssd_bwd(dyh, dyh.transpose(0, 2, 1), sv["xh"], sv["dtc"], sv["dtr"],
                                                        sv["bm"], sv["cm"], sv["hprev"], alog3, dsk3)
            G["a_log"][i], G["d_skip"][i] = dalog.reshape(-1), ddsk.reshape(-1)
            ddtraw, ddtb = _softplus_bwd(ddt3[:, :, 0].T, sv["dtraw"], row(dt_bias[i]))
            G["dt_bias"][i] = ddtb[0]
            dxc = jnp.concatenate([_unheads(dxh), dbm, dcm], axis=1)
            dpre, dcw, dcb = _conv_bwd1(dxc, sv["xbc"], conv_w_full[i], row(conv_b[i]))
            G["conv_w"][i], G["conv_b"][i] = dcw, dcb[0]
            dxbc = _conv_bwd2(dpre, conv_w_full[i])
            dproj = jnp.concatenate([du, dz.astype(BF16), dxbc, ddtraw.astype(BF16),
                                     jnp.zeros((s, EVEN_IN_PAD - EVEN_IN), BF16)], axis=1)
            G["even_w_in"][i] = _mm("mm_dw_in_even", sv["x_in_b"], dproj, "tn", F32)[:, :EVEN_IN]
            dxb = _mm("mm_dx_in_even", dproj, w_in_e[i], "nt", F32)
        else:
            G["odd_w_out"][i] = _mm("mm_dw_out_odd", sv["mix"], dr1b, "tn", F32)
            dmix = _mm("mm_dx_out_odd", dr1b, W["odd_w_out"][i], "nt", BF16)
            do_f = _heads(dmix[:, :FOX_WIDTH], FOX_HEADS)
            dl_f = _rowdot(do_f.reshape(-1, 64), sv["o_fox"].reshape(-1, 64)).reshape(FOX_HEADS, s, 1)
            ones = jnp.ones((FOX_HEADS, s, 64), BF16)
            qg = jnp.concatenate([sv["qh"], ones], axis=2)
            kg = jnp.concatenate([sv["kh"], ones * (1.0 / FOX_SCALE)], axis=2)
            dqe, dke, dvh = _flash_bwd("fox_bwd", sv["qh"], sv["kh"], sv["vh"], do_f, sv["lse_fox"], dl_f,
                                       sv["fq"], sv["fk"], qg, kg, None, FOX_SCALE, None)
            dqh, dkh = dqe[:, :, :64], dke[:, :, :64]
            dfl, dfb = _fgate_bwd(dqe, dke, 64, sv["fl"], fgate_b[i][:, None])
            dfl = dfl.T
            G["fgate_b"][i] = dfb[:, 0]
            do_m = _heads(dmix[:, FOX_WIDTH:], MLA_HEADS)
            dl_m = _rowdot(do_m.reshape(-1, 64), sv["o_mla"].reshape(-1, 64)).reshape(MLA_HEADS, s, 1)
            dqm, dkm, dvm = _flash_bwd("mla_bwd", sv["qm"], sv["km"], sv["vm"], do_m, sv["lse_mla"], dl_m,
                                       None, None, sv["qm"], sv["km"], MLA_SCALE, MLA_SCALE, MLA_SCALE)
            n0, n1 = MLA_NOPE, MLA_NOPE + half
            dq1, dq2 = _rope("rope_q_bwd", _unheads(dqm[:, :, n0:n1]), _unheads(dqm[:, :, n1:n1 + half]),
                             cos128, -sin128)
            dqp = jnp.concatenate([_unheads(dqm[:, :, :n0]), dq1, dq2], axis=1).astype(BF16)
            G["w_uq"][i] = _unperm_uq(_mm("mm_dw_uq", sv["qn"], dqp, "tn", F32))
            dqn = _mm("mm_dx_uq", dqp, w_uq_p[i], "nt", F32)
            dcq, dqw = _rms_bwd(dqn, sv["cq"], row(q_norm_full[i]))
            G["q_norm_w"][i] = dqw[0]
            dk1, dk2 = _headsum_rope_bwd(_unheads(dkm[:, :, n0:n1]), _unheads(dkm[:, :, n1:n1 + half]), cos16, sin16)
            dkvp = jnp.concatenate([_unheads(dkm[:, :, :n0]), _unheads(dvm)], axis=1).astype(BF16)
            G["w_ukv"][i] = _unperm_ukv(_mm("mm_dw_ukv", sv["kvn"], dkvp, "tn", F32))
            dkvn = _mm("mm_dx_ukv", dkvp, w_ukv_p[i], "nt", F32)
            dckv, dkvw = _rms_bwd(dkvn, sv["ckv"], row(kv_norm_full[i]))
            G["kv_norm_w"][i] = dkvw[0]
            dproj = jnp.concatenate([_unheads(dqh), _unheads(dkh), _unheads(dvh), dcq, dckv, dk1, dk2, dfl,
                                     jnp.zeros((s, ODD_IN_PAD - ODD_IN), F32)], axis=1).astype(BF16)
            G["odd_w_in"][i] = _unperm_odd_in(_mm("mm_dw_in_odd", sv["x_in_b"], dproj, "tn", F32))
            dxb = _mm("mm_dx_in_odd", dproj, w_in_o[i], "nt", F32)
        acur, dcur = dr1, dxb
    grad_x = _axpy(acur, dcur)[None]

    gfull = {n: jnp.stack(G[n]) for n in names}
    send = jnp.concatenate([
        jnp.pad(_reshard(gfull[n], ax, _PACK_COLS).astype(BF16), ((0, 0), (0, pr - nr), (0, 0)))
        for (n, ax), nr, pr in zip(_BIG, big_rows, pad_rows)], axis=1)
    recv = _all_to_all("a2a_grads", send)
    gsum = _sum8("sum_grads", recv)
    grads = {}
    off = 0
    for (n, ax), nr, pr in zip(_BIG, big_rows, pad_rows):
        grads[n] = gsum[off:off + nr].reshape(P[n].shape)
        off += pr
    small = [n for n in names if n not in dict(_BIG)]
    sflat = jnp.concatenate([gfull[n].reshape(-1) for n in small])
    n_small = sflat.shape[0]
    rows_small = -(-n_small // (128 * 8)) * 8
    sflat = jnp.concatenate([sflat, jnp.zeros((rows_small * 128 - n_small,), F32)])
    sg = _sum8("sum_small_grads", _all_gather("ag_small_grads", sflat.reshape(rows_small, 128))).reshape(-1)
    off = 0
    for n in small:
        cnt = gfull[n].size
        gf = sg[off:off + cnt].reshape(gfull[n].shape)
        off += cnt
        if gf.shape != P[n].shape:
            width = P[n].shape[-1]
            gf = lax.dynamic_slice_in_dim(gf, me * width, width, axis=gf.ndim - 1)
        grads[n] = gf

    delta, new_m, new_v = {}, {}, {}
    for n, _ in _BIG:
        shp = P[n].shape
        two = lambda t: t.reshape(-1, shp[-1])
        d_, m_, v_ = _adamw("adamw_" + n, two(P[n]), two(grads[n]), two(M[n]), two(V[n]))
        delta[n], new_m[n], new_v[n] = d_.reshape(shp), m_.reshape(shp), v_.reshape(shp)

    def packs(d):
        f = jnp.concatenate([d[n].reshape(-1) for n in small])
        pad = -(-f.shape[0] // (128 * 8)) * 8 * 128 - f.shape[0]
        return jnp.concatenate([f, jnp.zeros((pad,), F32)]).reshape(-1, 128)

    d_, m_, v_ = _adamw("adamw_small", packs(P), packs(grads), packs(M), packs(V))
    off = 0
    for n in small:
        cnt = P[n].size
        for dst, src in ((delta, d_), (new_m, m_), (new_v, v_)):
            dst[n] = src.reshape(-1)[off:off + cnt].reshape(P[n].shape)
        off += cnt

    return (loss, grad_x, *[grads[n] for n in names], *[delta[n] for n in names],
            *[new_m[n] for n in names], *[new_v[n] for n in names])
```

```python
import math

import jax
import jax.numpy as jnp
import numpy as np
from jax import lax
from jax.experimental import pallas as pl
from jax.experimental.pallas import tpu as pltpu

F32 = jnp.float32
BF16 = jnp.bfloat16
HI = lax.Precision.HIGHEST

N_DEV = 8
D_MODEL = 1024
DEPTH = 4
POOL_WINDOWS = (2, 4, 8, 16)
POOL_GROUP = 128
POOL_WIDTH = 512
SSM_D_INNER = 1024
SSM_HEAD_DIM = 64
SSM_HEADS = 16
SSM_GROUPS = 2
SSM_STATE = 128
SSM_CONV = 4
SSM_CHUNK = 128
SSM_CONV_DIM = 1536
EVEN_IN = 3088
EVEN_IN_PAD = 3200
FOX_HEADS = 8
FOX_WIDTH = 512
MLA_HEADS = 8
MLA_NOPE = 64
MLA_ROPE = 32
MLA_V = 64
MLA_Q_RANK = 512
MLA_KV_RANK = 256
MLA_DK_PAD = 128
ROPE_THETA = 10000.0
FOX_SCALE = 64 ** -0.5
MLA_SCALE = (MLA_NOPE + MLA_ROPE) ** -0.5
ODD_IN = 2344
ODD_IN_PAD = 2560
D_FF = 2816
ALPHA = (2 * DEPTH) ** 0.25
LN_EPS = 1e-5
RMS_EPS = 1e-6
ADAM_LR = 0.001
ADAM_B1 = 0.9
ADAM_B2 = 0.999
ADAM_EPS = 1e-08
ADAM_WD = 0.01
ADAM_STEP = 10
NEG = -1e30
VMEM_LIMIT = 48 * 1024 * 1024


def _cp(sem):
    return pltpu.CompilerParams(dimension_semantics=sem, vmem_limit_bytes=VMEM_LIMIT)


def _dot(a, b, ca, cb, prec=None):
    return lax.dot_general(a, b, (((ca,), (cb,)), ((), ())), preferred_element_type=F32, precision=prec)


def _sigmoid(x):
    return 1.0 / (1.0 + jnp.exp(-x))


def _softplus(x):
    return jnp.maximum(x, 0.0) + jnp.log(1.0 + jnp.exp(-jnp.abs(x)))


MESH = pl.DeviceIdType.MESH
HBM_SPEC = pl.BlockSpec(memory_space=pltpu.HBM)


def _all_gather(name, xs):
    r, c_ = xs.shape

    def body(x_ref, out_ref, send_sems, recv_sems, local_sem):
        x, y, c = lax.axis_index("x"), lax.axis_index("y"), lax.axis_index("c")
        me, sibling = (x, y, c), (x, y, 1 - c)
        chips = [(1 - x, y), (x, 1 - y), (1 - x, 1 - y)]

        def rows(px, py, pc):
            return out_ref.at[4 * px + 2 * py + pc]

        def copy(k, block, to, src=None):
            return pltpu.make_async_remote_copy(
                src_ref=rows(*block) if src is None else src,
                dst_ref=rows(*block),
                send_sem=send_sems.at[k],
                recv_sem=recv_sems.at[k],
                device_id=to,
                device_id_type=MESH,
            )

        mine = pltpu.make_async_copy(x_ref, rows(*me), local_sem)
        mine.start()
        first = [copy(0, me, sibling, src=x_ref)]
        first += [copy(1 + j, me, (*chip, c), src=x_ref) for j, chip in enumerate(chips)]
        for cp in first:
            cp.start()
        passed = [copy(4 + j, (*chip, c), sibling) for j, chip in enumerate(chips)]
        for j, chip in enumerate(chips):
            copy(1 + j, (*chip, c), me).wait_recv()
            passed[j].start()
        copy(0, sibling, me).wait_recv()
        for j, chip in enumerate(chips):
            copy(4 + j, (*chip, 1 - c), me).wait_recv()
        for cp in first + passed:
            cp.wait_send()
        mine.wait()

    return pl.pallas_call(
        body,
        name=name,
        out_shape=jax.ShapeDtypeStruct((N_DEV, r, c_), xs.dtype),
        in_specs=[HBM_SPEC],
        out_specs=HBM_SPEC,
        scratch_shapes=[pltpu.SemaphoreType.DMA((7,)), pltpu.SemaphoreType.DMA((7,)), pltpu.SemaphoreType.DMA(())],
    )(xs)


def _all_to_all(name, send):
    _, r, c_ = send.shape

    def body(s_ref, r_ref, send_sems, recv_sems, local_sem):
        x, y, c = lax.axis_index("x"), lax.axis_index("y"), lax.axis_index("c")
        me = 4 * x + 2 * y + c
        mine = pltpu.make_async_copy(s_ref.at[me], r_ref.at[me], local_sem)
        mine.start()
        copies = []
        for k in range(1, N_DEV):
            tx = 1 - x if k & 4 else x
            ty = 1 - y if k & 2 else y
            tc = 1 - c if k & 1 else c
            peer = 4 * tx + 2 * ty + tc
            cp = pltpu.make_async_remote_copy(
                src_ref=s_ref.at[peer],
                dst_ref=r_ref.at[me],
                send_sem=send_sems.at[k - 1],
                recv_sem=recv_sems.at[k - 1],
                device_id=(tx, ty, tc),
                device_id_type=MESH,
            )
            cp.start()
            landing = pltpu.make_async_remote_copy(
                src_ref=s_ref.at[me],
                dst_ref=r_ref.at[peer],
                send_sem=send_sems.at[k - 1],
                recv_sem=recv_sems.at[k - 1],
                device_id=(tx, ty, tc),
                device_id_type=MESH,
            )
            copies.append((cp, landing))
        for cp, landing in copies:
            landing.wait_recv()
        for cp, landing in copies:
            cp.wait_send()
        mine.wait()

    return pl.pallas_call(
        body,
        name=name,
        out_shape=jax.ShapeDtypeStruct(send.shape, send.dtype),
        in_specs=[HBM_SPEC],
        out_specs=HBM_SPEC,
        scratch_shapes=[pltpu.SemaphoreType.DMA((7,)), pltpu.SemaphoreType.DMA((7,)), pltpu.SemaphoreType.DMA(())],
    )(send)


def _pick(n, cands):
    for t in cands:
        if n % t == 0:
            return t
    return n


def _mm(name, a, b, mode, out_dtype):
    if mode == "nn":
        (m, k), n = a.shape, b.shape[1]
    elif mode == "nt":
        (m, k), n = a.shape, b.shape[0]
    else:
        (k, m), n = a.shape, b.shape[1]
    tm = _pick(m, (512, 256, 128))
    tn = _pick(n, (1408, 1280, 1024, 768, 640, 512, 384, 256, 128))
    tk = _pick(k, (1024, 1408, 768, 640, 512, 256, 128))
    nk = k // tk
    swap = nk == 1 and a.size * a.dtype.itemsize * (n // tn) + b.size * b.dtype.itemsize < (
        a.size * a.dtype.itemsize + b.size * b.dtype.itemsize * (m // tm))
    ij = (lambda g0, g1: (g1, g0)) if swap else (lambda g0, g1: (g0, g1))
    if mode == "nn":
        a_spec = pl.BlockSpec((tm, tk), lambda g0, g1, kk: (ij(g0, g1)[0], kk))
        b_spec = pl.BlockSpec((tk, tn), lambda g0, g1, kk: (kk, ij(g0, g1)[1]))
        ca, cb = 1, 0
    elif mode == "nt":
        a_spec = pl.BlockSpec((tm, tk), lambda g0, g1, kk: (ij(g0, g1)[0], kk))
        b_spec = pl.BlockSpec((tn, tk), lambda g0, g1, kk: (ij(g0, g1)[1], kk))
        ca, cb = 1, 1
    else:
        a_spec = pl.BlockSpec((tk, tm), lambda g0, g1, kk: (kk, ij(g0, g1)[0]))
        b_spec = pl.BlockSpec((tk, tn), lambda g0, g1, kk: (kk, ij(g0, g1)[1]))
        ca, cb = 0, 0

    def kern(a_ref, b_ref, o_ref, acc):
        kk = pl.program_id(2)

        @pl.when(kk == 0)
        def _():
            acc[...] = jnp.zeros_like(acc)

        acc[...] += _dot(a_ref[...].astype(BF16), b_ref[...].astype(BF16), ca, cb)

        @pl.when(kk == nk - 1)
        def _():
            o_ref[...] = acc[...].astype(out_dtype)

    return pl.pallas_call(
        kern,
        name=name,
        grid=(n // tn, m // tm, nk) if swap else (m // tm, n // tn, nk),
        in_specs=[a_spec, b_spec],
        out_specs=pl.BlockSpec((tm, tn), lambda g0, g1, kk: ij(g0, g1)),
        out_shape=jax.ShapeDtypeStruct((m, n), out_dtype),
        scratch_shapes=[pltpu.VMEM((tm, tn), F32)],
        compiler_params=_cp(("parallel", "parallel", "arbitrary")),
    )(a, b)


def _rowwise(name, body, ins, outs, tm):
    n_rows = next(a.shape[0] for a, kind in ins if kind == "row")
    n = n_rows // tm
    in_specs = []
    for a, kind in ins:
        if kind == "full":
            in_specs.append(pl.BlockSpec(a.shape, lambda i, nd=a.ndim: (0,) * nd))
        elif kind == "row":
            in_specs.append(pl.BlockSpec((tm, a.shape[1]), lambda i: (i, 0)))
        elif kind == "prev":
            in_specs.append(pl.BlockSpec((tm, a.shape[1]), lambda i: (jnp.maximum(i - 1, 0), 0)))
        else:
            in_specs.append(pl.BlockSpec((tm, a.shape[1]), lambda i: (jnp.minimum(i + 1, n - 1), 0)))
    out_specs, out_shape = [], []
    for shp, dt, kind in outs:
        out_shape.append(jax.ShapeDtypeStruct(shp, dt))
        if kind == "row":
            out_specs.append(pl.BlockSpec((tm, shp[1]), lambda i: (i, 0)))
        else:
            out_specs.append(pl.BlockSpec(shp, lambda i, nd=len(shp): (0,) * nd))
    n_in = len(ins)

    def kern(*refs):
        i = pl.program_id(0)
        res = body(i, n, *[r if kind == "full" else r[...] for r, (_, kind) in zip(refs[:n_in], ins)])
        for (shp, dt, kind), val, o in zip(outs, res, refs[n_in:]):
            if kind == "row":
                o[...] = val.astype(dt)
            else:

                @pl.when(i == 0)
                def _(o=o):
                    o[...] = jnp.zeros_like(o)

                o[...] += val.astype(dt)

    return pl.pallas_call(
        kern,
        name=name,
        grid=(n,),
        in_specs=in_specs,
        out_specs=out_specs,
        out_shape=out_shape,
        compiler_params=_cp(("arbitrary",)),
    )(*[a for a, _ in ins])


def _tm(s, t):
    return min(s, t)


def _ln_fwd(x, h, g, b):
    s = x.shape[0]

    def body(i, n, xv, hv, gv, bv):
        r = ALPHA * xv + hv.astype(F32)
        mu = jnp.mean(r, axis=-1, keepdims=True)
        d = r - mu
        var = jnp.mean(d * d, axis=-1, keepdims=True)
        y = d * lax.rsqrt(var + LN_EPS) * gv[...] + bv[...]
        return y, y, r

    shp = (s, D_MODEL)
    return _rowwise("ln_fwd", body, [(x, "row"), (h, "row"), (g, "full"), (b, "full")],
                    [(shp, F32, "row"), (shp, BF16, "row"), (shp, F32, "row")], _tm(s, 256))


def _ln_bwd(a, bterm, r, g):
    s = r.shape[0]

    def body(i, n, *vals):
        if a is None:
            dyv, rv, gv = vals
        else:
            av, dyv, rv, gv = vals
            dyv = ALPHA * av + dyv
        mu = jnp.mean(rv, axis=-1, keepdims=True)
        d = rv - mu
        var = jnp.mean(d * d, axis=-1, keepdims=True)
        rstd = lax.rsqrt(var + LN_EPS)
        xhat = d * rstd
        dxh = dyv * gv[...]
        dr = rstd * (dxh - jnp.mean(dxh, axis=-1, keepdims=True) - xhat * jnp.mean(dxh * xhat, axis=-1, keepdims=True))
        return dr, dr, jnp.sum(dyv * xhat, axis=0, keepdims=True), jnp.sum(dyv, axis=0, keepdims=True)

    ins = ([] if a is None else [(a, "row")]) + [(bterm, "row"), (r, "row"), (g, "full")]
    shp = (s, D_MODEL)
    return _rowwise("ln_bwd" if a is None else "ln_bwd_res", body, ins,
                    [(shp, F32, "row"), (shp, BF16, "row"), ((1, D_MODEL), F32, "acc"), ((1, D_MODEL), F32, "acc")],
                    _tm(s, 256))


def _axpy(a, b):
    def body(i, n, av, bv):
        return (ALPHA * av + bv,)

    return _rowwise("axpy", body, [(a, "row"), (b, "row")], [(a.shape, F32, "row")], _tm(a.shape[0], 256))[0]


def _loss_head(y, target):
    s = y.shape[0]

    def body(i, n, yv, tv):
        err = yv - tv
        part = 0.5 * jnp.sum(jnp.mean(err * err, axis=-1, keepdims=True), axis=0, keepdims=True)
        return err * (1.0 / D_MODEL), part

    return _rowwise("loss_head", body, [(y, "row"), (target, "row")],
                    [((s, D_MODEL), F32, "row"), ((1, 1), F32, "acc")], _tm(s, 256))


def _swiglu_fwd(gu):
    s = gu.shape[0]

    def body(i, n, v):
        g, u = v[:, :D_FF].astype(F32), v[:, D_FF:].astype(F32)
        return (g * _sigmoid(g) * u,)

    return _rowwise("swiglu_fwd", body, [(gu, "row")], [((s, D_FF), BF16, "row")], _tm(s, 256))[0]


def _swiglu_bwd(da, gu):
    s = gu.shape[0]

    def body(i, n, dav, v):
        dav = dav.astype(F32)
        g, u = v[:, :D_FF].astype(F32), v[:, D_FF:].astype(F32)
        sg = _sigmoid(g)
        dg = dav * u * (sg * (1.0 + g * (1.0 - sg)))
        du = dav * (g * sg)
        return (jnp.concatenate([dg, du], axis=1),)

    return _rowwise("swiglu_bwd", body, [(da, "row"), (gu, "row")], [((s, 2 * D_FF), BF16, "row")], _tm(s, 256))[0]


def _rms_fwd(x, w):
    s, c = x.shape

    def body(i, n, xv, wv):
        rs = lax.rsqrt(jnp.mean(xv * xv, axis=-1, keepdims=True) + RMS_EPS)
        return (xv * rs * wv[...],)

    return _rowwise("rms_fwd", body, [(x, "row"), (w, "full")], [((s, c), BF16, "row")], _tm(s, 512))[0]


def _rms_bwd(dy, x, w):
    s, c = x.shape

    def body(i, n, dyv, xv, wv):
        rs = lax.rsqrt(jnp.mean(xv * xv, axis=-1, keepdims=True) + RMS_EPS)
        nv = xv * rs
        dn = dyv * wv[...]
        dx = rs * (dn - nv * jnp.mean(dn * nv, axis=-1, keepdims=True))
        return dx, jnp.sum(dyv * nv, axis=0, keepdims=True)

    return _rowwise("rms_bwd", body, [(dy, "row"), (x, "row"), (w, "full")],
                    [((s, c), F32, "row"), ((1, c), F32, "acc")], _tm(s, 512))


def _gated_rms_fwd(y, z, w):
    s, c = y.shape

    def body(i, n, yv, zv, wv):
        y2 = yv * (zv * _sigmoid(zv))
        rs = lax.rsqrt(jnp.mean(y2 * y2, axis=-1, keepdims=True) + RMS_EPS)
        return (y2 * rs * wv[...],)

    return _rowwise("gated_rms_fwd", body, [(y, "row"), (z, "row"), (w, "full")], [((s, c), BF16, "row")], _tm(s, 256))[0]


def _gated_rms_bwd(do, y, z, w):
    s, c = y.shape

    def body(i, n, dov, yv, zv, wv):
        sz = _sigmoid(zv)
        silu = zv * sz
        y2 = yv * silu
        rs = lax.rsqrt(jnp.mean(y2 * y2, axis=-1, keepdims=True) + RMS_EPS)
        nv = y2 * rs
        dn = dov * wv[...]
        dy2 = rs * (dn - nv * jnp.mean(dn * nv, axis=-1, keepdims=True))
        return dy2 * silu, dy2 * yv * (sz * (1.0 + zv * (1.0 - sz))), jnp.sum(dov * nv, axis=0, keepdims=True)

    return _rowwise("gated_rms_bwd", body, [(do, "row"), (y, "row"), (z, "row"), (w, "full")],
                    [((s, c), F32, "row"), ((s, c), F32, "row"), ((1, c), F32, "acc")], _tm(s, 256))


def _rope(name, r1, r2, cos, sin):
    def body(i, n, a, b, cv, sv):
        return a * cv - b * sv, b * cv + a * sv

    return _rowwise(name, body, [(r1, "row"), (r2, "row"), (cos, "row"), (sin, "row")],
                    [(r1.shape, F32, "row"), (r1.shape, F32, "row")], _tm(r1.shape[0], 512))


def _headsum_rope_bwd(d1_all, d2_all, cos, sin):
    s = d1_all.shape[0]
    half = MLA_ROPE // 2

    def body(i, n, a_all, b_all, cv, sv):
        rr = lax.broadcasted_iota(jnp.int32, (MLA_HEADS * half, half), 0)
        cc = lax.broadcasted_iota(jnp.int32, (MLA_HEADS * half, half), 1)
        sel = jnp.where(rr % half == cc, 1.0, 0.0).astype(F32)
        a = _dot(a_all, sel, 1, 0, HI)
        b = _dot(b_all, sel, 1, 0, HI)
        return a * cv + b * sv, b * cv - a * sv

    return _rowwise("headsum_rope_bwd", body, [(d1_all, "row"), (d2_all, "row"), (cos, "row"), (sin, "row")],
                    [((s, half), F32, "row"), ((s, half), F32, "row")], _tm(s, 512))


def _softplus_fwd(dtr, bias):
    def body(i, n, v, bv):
        return (_softplus(v + bv[...]),)

    return _rowwise("softplus_fwd", body, [(dtr, "row"), (bias, "full")], [(dtr.shape, F32, "row")], _tm(dtr.shape[0], 1024))[0]


def _softplus_bwd(ddt, dtr, bias):
    def body(i, n, dv, v, bv):
        d = dv * _sigmoid(v + bv[...])
        return d, jnp.sum(d, axis=0, keepdims=True)

    return _rowwise("softplus_bwd", body, [(ddt, "row"), (dtr, "row"), (bias, "full")],
                    [(dtr.shape, F32, "row"), ((1, dtr.shape[1]), F32, "acc")], _tm(dtr.shape[0], 1024))


def _rowdot(a, b):
    def body(i, n, av, bv):
        return (jnp.sum(av.astype(F32) * bv.astype(F32), axis=-1, keepdims=True),)

    return _rowwise("rowdot", body, [(a, "row"), (b, "row")], [((a.shape[0], 1), F32, "row")], _tm(a.shape[0], 2048))[0]


def _sum8(name, blocks):
    _, r, c = blocks.shape
    tr = _pick(r, (512, 256, 128, 64, 32, 16, 8))

    def kern(b_ref, o_ref):
        acc = b_ref[0].astype(F32)
        for d in range(1, N_DEV):
            acc = acc + b_ref[d].astype(F32)
        o_ref[...] = acc

    return pl.pallas_call(
        kern,
        name=name,
        grid=(r // tr,),
        in_specs=[pl.BlockSpec((N_DEV, tr, c), lambda i: (0, i, 0))],
        out_specs=pl.BlockSpec((tr, c), lambda i: (i, 0)),
        out_shape=jax.ShapeDtypeStruct((r, c), F32),
        compiler_params=_cp(("parallel",)),
    )(blocks)


def _adamw(name, w, g, m, v):
    r = w.shape[0]
    tm = _pick(r, (512, 256, 128, 64, 32, 16, 8))

    def body(i, n, wv, gv, mv, vv):
        m2 = ADAM_B1 * mv + (1.0 - ADAM_B1) * gv
        v2 = ADAM_B2 * vv + (1.0 - ADAM_B2) * (gv * gv)
        m_hat = m2 / (1.0 - ADAM_B1 ** ADAM_STEP)
        v_hat = v2 / (1.0 - ADAM_B2 ** ADAM_STEP)
        delta = -ADAM_LR * (m_hat / (jnp.sqrt(v_hat) + ADAM_EPS) + ADAM_WD * wv)
        return delta, m2, v2

    return _rowwise(name, body, [(w, "row"), (g, "row"), (m, "row"), (v, "row")],
                    [(w.shape, F32, "row")] * 3, tm)


def _pool_diff(i, t_rows, u, up):
    ucat = jnp.concatenate([jnp.where(i > 0, up, 0.0), u], axis=0)
    r = lax.broadcasted_iota(jnp.int32, (t_rows, 2 * t_rows), 0)
    cc = lax.broadcasted_iota(jnp.int32, (t_rows, 2 * t_rows), 1)
    lag = r + t_rows - cc
    t_col = i * t_rows + lax.broadcasted_iota(jnp.int32, (t_rows, 1), 0)
    diffs = []
    for gi, wdw in enumerate(POOL_WINDOWS):
        win = jnp.where((lag >= 0) & (lag < wdw), 1.0, 0.0).astype(F32)
        cnt = jnp.minimum(t_col + 1, wdw).astype(F32)
        ws = _dot(win, ucat[:, gi * POOL_GROUP:(gi + 1) * POOL_GROUP], 1, 0, HI)
        diffs.append(ws / cnt - u[:, gi * POOL_GROUP:(gi + 1) * POOL_GROUP])
    return diffs


def _pool_fwd(u, pw, sc):
    s = u.shape[0]
    tm = _tm(s, 256)

    def body(i, n, uv, upv, wv, scv):
        diffs = _pool_diff(i, tm, uv, upv)
        ys = [_dot(d.astype(BF16), wv[gi].astype(BF16), 1, 0) for gi, d in enumerate(diffs)]
        return (jnp.concatenate(ys, axis=1) * scv[...],)

    return _rowwise("pool_fwd", body, [(u, "row"), (u, "prev"), (pw, "full"), (sc, "full")],
                    [((s, POOL_WIDTH), BF16, "row")], tm)[0]


def _pool_bwd1(u, dy, pw, sc):
    s = u.shape[0]
    tm = _tm(s, 256)

    def body(i, n, uv, upv, dyv, wv, scv):
        diffs = _pool_diff(i, tm, uv, upv)
        scv = scv[...]
        dsc, dws, dds = [], [], []
        for gi, d in enumerate(diffs):
            sl = slice(gi * POOL_GROUP, (gi + 1) * POOL_GROUP)
            db, wb = d.astype(BF16), wv[gi].astype(BF16)
            yg = _dot(db, wb, 1, 0)
            dsc.append(jnp.sum(yg * dyv[:, sl], axis=0, keepdims=True))
            eb = (dyv[:, sl] * scv[:, sl]).astype(BF16)
            dws.append(_dot(db, eb, 0, 0)[None])
            dds.append(_dot(eb, wb, 1, 1))
        return jnp.concatenate(dds, axis=1), jnp.concatenate(dsc, axis=1), jnp.concatenate(dws, axis=0)

    return _rowwise("pool_bwd1", body, [(u, "row"), (u, "prev"), (dy, "row"), (pw, "full"), (sc, "full")],
                    [((s, POOL_WIDTH), F32, "row"), ((1, POOL_WIDTH), F32, "acc"), (pw.shape, F32, "acc")], tm)


def _pool_bwd2(dd):
    s = dd.shape[0]
    tm = _tm(s, 256)

    def body(i, n, dv, dnv):
        dcat = jnp.concatenate([dv, jnp.where(i < n - 1, dnv, 0.0)], axis=0)
        r = lax.broadcasted_iota(jnp.int32, (tm, 2 * tm), 0)
        cc = lax.broadcasted_iota(jnp.int32, (tm, 2 * tm), 1)
        lag = cc - r
        t_col = i * tm + lax.broadcasted_iota(jnp.int32, (2 * tm, 1), 0)
        outs = []
        for gi, wdw in enumerate(POOL_WINDOWS):
            sl = slice(gi * POOL_GROUP, (gi + 1) * POOL_GROUP)
            win = jnp.where((lag >= 0) & (lag < wdw), 1.0, 0.0).astype(F32)
            cnt = jnp.minimum(t_col + 1, wdw).astype(F32)
            outs.append(_dot(win, dcat[:, sl] / cnt, 1, 0, HI) - dv[:, sl])
        return (jnp.concatenate(outs, axis=1),)

    return _rowwise("pool_bwd2", body, [(dd, "row"), (dd, "next")], [((s, POOL_WIDTH), BF16, "row")], tm)[0]


def _shift_down(cur, other, j, tm):
    if j == 0:
        return cur
    rows = lax.broadcasted_iota(jnp.int32, cur.shape, 0)
    return jnp.where(rows < j, pltpu.roll(other, j, 0), pltpu.roll(cur, j, 0))


def _shift_up(cur, other, j, tm):
    if j == 0:
        return cur
    rows = lax.broadcasted_iota(jnp.int32, cur.shape, 0)
    return jnp.where(rows >= tm - j, pltpu.roll(other, tm - j, 0), pltpu.roll(cur, tm - j, 0))


def _conv_pre(i, tm, xv, xpv, wv, bv):
    xpv = jnp.where(i > 0, xpv, 0.0)
    taps = [_shift_down(xv, xpv, SSM_CONV - 1 - k, tm) for k in range(SSM_CONV)]
    pre = bv[...]
    for k in range(SSM_CONV):
        pre = pre + wv[k:k + 1, :] * taps[k]
    return pre, taps


def _conv_fwd(xbc, w, b):
    s = xbc.shape[0]
    tm = _tm(s, 256)

    def body(i, n, xv, xpv, wv, bv):
        pre, _ = _conv_pre(i, tm, xv, xpv, wv, bv)
        return (pre * _sigmoid(pre),)

    return _rowwise("conv_fwd", body, [(xbc, "row"), (xbc, "prev"), (w, "full"), (b, "full")],
                    [(xbc.shape, F32, "row")], tm)[0]


def _conv_bwd1(dxc, xbc, w, b):
    s, c = xbc.shape
    tm = _tm(s, 256)

    def body(i, n, dv, xv, xpv, wv, bv):
        pre, taps = _conv_pre(i, tm, xv, xpv, wv, bv)
        sg = _sigmoid(pre)
        dpre = dv * (sg * (1.0 + pre * (1.0 - sg)))
        tap_row = lax.broadcasted_iota(jnp.int32, (SSM_CONV, c), 0)
        dw = jnp.zeros((SSM_CONV, c), F32)
        for k in range(SSM_CONV):
            dw = dw + jnp.where(tap_row == k, jnp.sum(dpre * taps[k], axis=0, keepdims=True), 0.0)
        return dpre, dw, jnp.sum(dpre, axis=0, keepdims=True)

    return _rowwise("conv_bwd1", body, [(dxc, "row"), (xbc, "row"), (xbc, "prev"), (w, "full"), (b, "full")],
                    [((s, c), F32, "row"), ((SSM_CONV, c), F32, "acc"), ((1, c), F32, "acc")], tm)


def _conv_bwd2(dpre, w):
    s, c = dpre.shape
    tm = _tm(s, 256)

    def body(i, n, dv, dnv, wv):
        dnv = jnp.where(i < n - 1, dnv, 0.0)
        out = jnp.zeros_like(dv)
        for k in range(SSM_CONV):
            out = out + wv[k:k + 1, :] * _shift_up(dv, dnv, SSM_CONV - 1 - k, tm)
        return (out,)

    return _rowwise("conv_bwd2", body, [(dpre, "row"), (dpre, "next"), (w, "full")], [((s, c), BF16, "row")], tm)[0]


def _ssd_common(alog, dt_c, dt_r, tril, tri):
    a = -jnp.exp(alog)
    acs_c = _dot(tril, dt_c * a, 1, 0, HI)
    acs_r = _dot(dt_r * a, tril, 1, 1, HI)
    a_last = jnp.sum(dt_c * a, axis=0, keepdims=True)
    return a, acs_c, acs_r, a_last


def _ssd_fwd(xh, xht, dtc, dtr, bm, cm, alog, dsk):
    nh, s, p = xh.shape
    L = SSM_CHUNK
    nc = s // L
    E = nh // SSM_GROUPS

    def kern(xh_ref, xht_ref, dtc_ref, dtr_ref, b_ref, c_ref, alog_ref, dsk_ref, y_ref, hp_ref, h_s):
        c = pl.program_id(1)

        @pl.when(c == 0)
        def _():
            h_s[...] = jnp.zeros_like(h_s)

        bb = b_ref[...].astype(BF16)
        cb_ = c_ref[...].astype(BF16)
        cbm = _dot(cb_, bb, 1, 1)
        ri = lax.broadcasted_iota(jnp.int32, (L, L), 0)
        ci = lax.broadcasted_iota(jnp.int32, (L, L), 1)
        tri = ri >= ci
        tril = tri.astype(F32)
        hs = range(E)
        com = [_ssd_common(alog_ref[e], dtc_ref[e], dtr_ref[e], tril, tri) for e in hs]
        lam = [jnp.exp(jnp.where(tri, com[e][1] - com[e][2], NEG)) for e in hs]
        xs = [xh_ref[e] for e in hs]
        xdt = [(xs[e] * dtc_ref[e]).astype(BF16) for e in hs]
        hh = [h_s[e] for e in hs]
        y_diag = [_dot((cbm * lam[e]).astype(BF16), xdt[e], 1, 0) for e in hs]
        y_off = [_dot(cb_, hh[e].astype(BF16), 1, 1) for e in hs]
        xw = [(xht_ref[e] * (dtr_ref[e] * jnp.exp(com[e][3] - com[e][2]))).astype(BF16) for e in hs]
        st = [_dot(xw[e], bb, 1, 0) for e in hs]
        for e in hs:
            y_ref[e] = y_diag[e] + y_off[e] * jnp.exp(com[e][1]) + xs[e] * dsk_ref[e]
            hp_ref[e] = hh[e]
            h_s[e] = hh[e] * jnp.exp(com[e][3]) + st[e]

    return pl.pallas_call(
        kern,
        name="ssd_fwd",
        grid=(SSM_GROUPS, nc),
        in_specs=[
            pl.BlockSpec((E, L, p), lambda g, c: (g, c, 0)),
            pl.BlockSpec((E, p, L), lambda g, c: (g, 0, c)),
            pl.BlockSpec((E, L, 1), lambda g, c: (g, c, 0)),
            pl.BlockSpec((E, 1, L), lambda g, c: (g, 0, c)),
            pl.BlockSpec((L, SSM_STATE), lambda g, c: (c, g)),
            pl.BlockSpec((L, SSM_STATE), lambda g, c: (c, g)),
            pl.BlockSpec((E, 1, 1), lambda g, c: (g, 0, 0)),
            pl.BlockSpec((E, 1, 1), lambda g, c: (g, 0, 0)),
        ],
        out_specs=[
            pl.BlockSpec((E, L, p), lambda g, c: (g, c, 0)),
            pl.BlockSpec((E, None, p, SSM_STATE), lambda g, c: (g, c, 0, 0)),
        ],
        out_shape=[jax.ShapeDtypeStruct((nh, s, p), F32), jax.ShapeDtypeStruct((nh, nc, p, SSM_STATE), F32)],
        scratch_shapes=[pltpu.VMEM((E, p, SSM_STATE), F32)],
        compiler_params=_cp(("arbitrary", "arbitrary")),
    )(xh, xht, dtc, dtr, bm, cm, alog, dsk)


def _ssd_bwd(dy, dyt, xh, dtc, dtr, bm, cm, hprev, alog, dsk):
    nh, s, p = xh.shape
    L = SSM_CHUNK
    nc = s // L
    E = nh // SSM_GROUPS

    def kern(dy_ref, dyt_ref, xh_ref, dtc_ref, dtr_ref, b_ref, c_ref, hp_ref, alog_ref, dsk_ref,
             dx_ref, ddt_ref, db_ref, dc_ref, dalog_ref, dd_ref, dh_s):
        c = pl.program_id(1)

        @pl.when(c == 0)
        def _():
            dh_s[...] = jnp.zeros_like(dh_s)
            dalog_ref[...] = jnp.zeros_like(dalog_ref)
            dd_ref[...] = jnp.zeros_like(dd_ref)

        bb = b_ref[...].astype(BF16)
        cb_ = c_ref[...].astype(BF16)
        cbm = _dot(cb_, bb, 1, 1)
        cbt = _dot(bb, cb_, 1, 1)
        ri = lax.broadcasted_iota(jnp.int32, (L, L), 0)
        ci = lax.broadcasted_iota(jnp.int32, (L, L), 1)
        tri = ri >= ci
        trit = ci >= ri
        tril = tri.astype(F32)
        triu = trit.astype(F32)
        hs = range(E)
        com = [_ssd_common(alog_ref[e], dtc_ref[e], dtr_ref[e], tril, tri) for e in hs]
        lam = [jnp.exp(jnp.where(tri, com[e][1] - com[e][2], NEG)) for e in hs]
        lamt = [jnp.exp(jnp.where(trit, com[e][2] - com[e][1], NEG)) for e in hs]
        xs = [xh_ref[e] for e in hs]
        xdt = [xs[e] * dtc_ref[e] for e in hs]
        xdtb = [t.astype(BF16) for t in xdt]
        dyv = [dy_ref[e] for e in hs]
        dyb = [t.astype(BF16) for t in dyv]
        hh = [hp_ref[e] for e in hs]
        hb = [t.astype(BF16) for t in hh]
        dhn = [dh_s[e] for e in hs]
        dhnb = [t.astype(BF16) for t in dhn]
        ea_c = [jnp.exp(com[e][1]) for e in hs]
        decay_c = [jnp.exp(com[e][3] - com[e][1]) for e in hs]
        e_last = [jnp.exp(com[e][3]) for e in hs]
        gm = [_dot(dyb[e], xdtb[e], 1, 1) for e in hs]
        gt = [_dot(xdtb[e], dyb[e], 1, 1) for e in hs]
        bdh = [_dot(bb, dhnb[e], 1, 1) for e in hs]
        dxdt = [_dot((cbt * lamt[e]).astype(BF16), dyb[e], 1, 0) + bdh[e] * decay_c[e] for e in hs]
        dcb = [gm[e] * lam[e] for e in hs]
        dcbt = [gt[e] * lamt[e] for e in hs]
        yoff = [_dot(cb_, hb[e], 1, 1) * ea_c[e] for e in hs]
        dcs = [_dot(dcb[e].astype(BF16), bb, 1, 0) + _dot(dyb[e], hb[e], 1, 0) * ea_c[e] for e in hs]
        dbs = [_dot(dcbt[e].astype(BF16), cb_, 1, 0) + _dot(xdtb[e], dhnb[e], 1, 0) * decay_c[e] for e in hs]
        dyw = [(dyt_ref[e] * jnp.exp(com[e][2])).astype(BF16) for e in hs]
        dh_new = [dhn[e] * e_last[e] + _dot(dyw[e], cb_, 1, 0) for e in hs]
        db_acc, dc_acc = dbs[0], dcs[0]
        for e in range(1, E):
            db_acc = db_acc + dbs[e]
            dc_acc = dc_acc + dcs[e]
        for e in hs:
            a = com[e][0]
            tt = decay_c[e] * jnp.sum(xdt[e] * bdh[e], axis=1, keepdims=True)
            dacs = (jnp.sum(dcb[e] * cbm, axis=1, keepdims=True) - jnp.sum(dcbt[e] * cbt, axis=1, keepdims=True)
                    + jnp.sum(dyv[e] * yoff[e], axis=1, keepdims=True) - tt)
            tail = jnp.sum(tt, axis=0, keepdims=True) + e_last[e] * jnp.sum(
                jnp.sum(dhn[e] * hh[e], axis=1, keepdims=True), axis=0, keepdims=True)
            dda = _dot(triu, dacs, 1, 0, HI) + tail
            ddt_ref[e] = dda * a + jnp.sum(dxdt[e] * xs[e], axis=1, keepdims=True)
            dalog_ref[e] += jnp.sum(dda * dtc_ref[e], axis=0, keepdims=True) * a
            dd_ref[e] += jnp.sum(jnp.sum(dyv[e] * xs[e], axis=1, keepdims=True), axis=0, keepdims=True)
            dx_ref[e] = dxdt[e] * dtc_ref[e] + dyv[e] * dsk_ref[e]
            dh_s[e] = dh_new[e]
        db_ref[...] = db_acc
        dc_ref[...] = dc_acc

    rc = lambda c: nc - 1 - c
    return pl.pallas_call(
        kern,
        name="ssd_bwd",
        grid=(SSM_GROUPS, nc),
        in_specs=[
            pl.BlockSpec((E, L, p), lambda g, c: (g, rc(c), 0)),
            pl.BlockSpec((E, p, L), lambda g, c: (g, 0, rc(c))),
            pl.BlockSpec((E, L, p), lambda g, c: (g, rc(c), 0)),
            pl.BlockSpec((E, L, 1), lambda g, c: (g, rc(c), 0)),
            pl.BlockSpec((E, 1, L), lambda g, c: (g, 0, rc(c))),
            pl.BlockSpec((L, SSM_STATE), lambda g, c: (rc(c), g)),
            pl.BlockSpec((L, SSM_STATE), lambda g, c: (rc(c), g)),
            pl.BlockSpec((E, None, p, SSM_STATE), lambda g, c: (g, rc(c), 0, 0)),
            pl.BlockSpec((E, 1, 1), lambda g, c: (g, 0, 0)),
            pl.BlockSpec((E, 1, 1), lambda g, c: (g, 0, 0)),
        ],
        out_specs=[
            pl.BlockSpec((E, L, p), lambda g, c: (g, rc(c), 0)),
            pl.BlockSpec((E, L, 1), lambda g, c: (g, rc(c), 0)),
            pl.BlockSpec((L, SSM_STATE), lambda g, c: (rc(c), g)),
            pl.BlockSpec((L, SSM_STATE), lambda g, c: (rc(c), g)),
            pl.BlockSpec((E, 1, 1), lambda g, c: (g, 0, 0)),
            pl.BlockSpec((E, 1, 1), lambda g, c: (g, 0, 0)),
        ],
        out_shape=[
            jax.ShapeDtypeStruct((nh, s, p), F32),
            jax.ShapeDtypeStruct((nh, s, 1), F32),
            jax.ShapeDtypeStruct((s, SSM_GROUPS * SSM_STATE), F32),
            jax.ShapeDtypeStruct((s, SSM_GROUPS * SSM_STATE), F32),
            jax.ShapeDtypeStruct((nh, 1, 1), F32),
            jax.ShapeDtypeStruct((nh, 1, 1), F32),
        ],
        scratch_shapes=[pltpu.VMEM((E, p, SSM_STATE), F32)],
        compiler_params=_cp(("arbitrary", "arbitrary")),
    )(dy, dyt, xh, dtc, dtr, bm, cm, hprev, alog, dsk)


def _fgate_fwd(fl, bias):
    h, s = fl.shape
    t = _tm(s, 512)

    def kern(fl_ref, b_ref, o_ref, carry):
        i = pl.program_id(0)

        @pl.when(i == 0)
        def _():
            carry[...] = jnp.zeros_like(carry)

        lf = -_softplus(-(fl_ref[...] + b_ref[...]))
        ri = lax.broadcasted_iota(jnp.int32, (t, t), 0)
        ci = lax.broadcasted_iota(jnp.int32, (t, t), 1)
        o_ref[...] = _dot(lf, (ri <= ci).astype(F32), 1, 0, HI) + carry[...]
        carry[...] += jnp.sum(lf, axis=1, keepdims=True)

    return pl.pallas_call(
        kern,
        name="fgate_fwd",
        grid=(s // t,),
        in_specs=[pl.BlockSpec((h, t), lambda i: (0, i)), pl.BlockSpec((h, 1), lambda i: (0, 0))],
        out_specs=pl.BlockSpec((h, t), lambda i: (0, i)),
        out_shape=jax.ShapeDtypeStruct((h, s), F32),
        scratch_shapes=[pltpu.VMEM((h, 1), F32)],
        compiler_params=_cp(("arbitrary",)),
    )(fl, bias)


def _fgate_bwd(dqe, dke, col, fl, bias):
    h, s = fl.shape
    w = dqe.shape[-1]
    t = _tm(s, 512)
    n = s // t

    def kern(dq_ref, dk_ref, fl_ref, b_ref, o_ref, db_ref, carry):
        i = pl.program_id(0)

        @pl.when(i == 0)
        def _():
            carry[...] = jnp.zeros_like(carry)
            db_ref[...] = jnp.zeros_like(db_ref)

        sel = (lax.broadcasted_iota(jnp.int32, (1, w), 1) == col).astype(F32)
        hid = lax.broadcasted_iota(jnp.int32, (h, t), 0)
        d = jnp.zeros((h, t), F32)
        for hh in range(h):
            r = _dot(sel, dq_ref[hh], 1, 1, HI) - _dot(sel, dk_ref[hh], 1, 1, HI)
            d = d + jnp.where(hid == hh, r, 0.0)
        ri = lax.broadcasted_iota(jnp.int32, (t, t), 0)
        ci = lax.broadcasted_iota(jnp.int32, (t, t), 1)
        rev = _dot(d, (ri >= ci).astype(F32), 1, 0, HI) + carry[...]
        carry[...] += jnp.sum(d, axis=1, keepdims=True)
        dz = rev * _sigmoid(-(fl_ref[...] + b_ref[...]))
        o_ref[...] = dz
        db_ref[...] += jnp.sum(dz, axis=1, keepdims=True)

    rev_blk = lambda i: (0, n - 1 - i)
    return pl.pallas_call(
        kern,
        name="fgate_bwd",
        grid=(n,),
        in_specs=[pl.BlockSpec((h, t, w), lambda i: (0, n - 1 - i, 0)), pl.BlockSpec((h, t, w), lambda i: (0, n - 1 - i, 0)),
                  pl.BlockSpec((h, t), rev_blk), pl.BlockSpec((h, 1), lambda i: (0, 0))],
        out_specs=[pl.BlockSpec((h, t), rev_blk), pl.BlockSpec((h, 1), lambda i: (0, 0))],
        out_shape=[jax.ShapeDtypeStruct((h, s), F32), jax.ShapeDtypeStruct((h, 1), F32)],
        scratch_shapes=[pltpu.VMEM((h, 1), F32)],
        compiler_params=_cp(("arbitrary",)),
    )(dqe, dke, fl, bias)


HPS_FWD = 8
HPS_BWD = 4


def _attn_scores(q, k, fq, fk, scale, masked, t):
    sc = _dot(q, k, 1, 1)
    if scale is not None:
        sc = sc * scale
    if fq is not None:
        sc = sc + fq - fk
    if masked:
        rows = lax.broadcasted_iota(jnp.int32, (t, t), 0)
        cols = lax.broadcasted_iota(jnp.int32, (t, t), 1)
        sc = jnp.where(cols <= rows, sc, NEG)
    return sc


def _flash_fwd(name, q, k, v, fq, fk, scale):
    nh, s, dk = q.shape
    dv = v.shape[-1]
    t = _tm(s, 512)
    nq = s // t
    bias = fq is not None
    HPS = HPS_FWD

    pairs = [(i, j) for i in range(nq) for j in range(i + 1)]
    qi_of = jnp.asarray(np.array([p[0] for p in pairs], np.int32))
    kj_of = jnp.asarray(np.array([p[1] for p in pairs], np.int32))

    def kern(qi_ref, kj_ref, *refs):
        if bias:
            q_ref, k_ref, v_ref, fq_ref, fk_ref, o_ref, lse_ref, m_s, l_s, acc_s = refs
        else:
            q_ref, k_ref, v_ref, o_ref, lse_ref, m_s, l_s, acc_s = refs
        qi, kj = qi_ref[pl.program_id(1)], kj_ref[pl.program_id(1)]

        @pl.when(kj == 0)
        def _():
            m_s[...] = jnp.full_like(m_s, NEG)
            l_s[...] = jnp.zeros_like(l_s)
            acc_s[...] = jnp.zeros_like(acc_s)

        def step(masked):
            scs = [_attn_scores(q_ref[hh], k_ref[hh], fq_ref[hh] if bias else None, fk_ref[hh] if bias else None,
                                scale, masked, t) for hh in range(HPS)]
            pbs, corrs = [], []
            for hh in range(HPS):
                m_old = m_s[hh]
                m_new = jnp.maximum(m_old, jnp.max(scs[hh], axis=1, keepdims=True))
                corr = jnp.exp(m_old - m_new)
                p = jnp.exp(scs[hh] - m_new)
                l_s[hh] = corr * l_s[hh] + jnp.sum(p, axis=1, keepdims=True)
                m_s[hh] = m_new
                pbs.append(p.astype(BF16))
                corrs.append(corr)
            for hh in range(HPS):
                acc_s[hh] = acc_s[hh] * corrs[hh] + _dot(pbs[hh], v_ref[hh], 1, 0)

        @pl.when(kj < qi)
        def _():
            step(False)

        @pl.when(kj == qi)
        def _():
            step(True)
            o_ref[...] = acc_s[...] / l_s[...]
            lse_ref[...] = m_s[...] + jnp.log(l_s[...])

    qspec = lambda d: pl.BlockSpec((HPS, t, d), lambda h, p, qi_r, kj_r: (h, qi_r[p], 0))
    kspec = lambda d: pl.BlockSpec((HPS, t, d), lambda h, p, qi_r, kj_r: (h, kj_r[p], 0))
    in_specs = [qspec(dk), kspec(dk), kspec(dv)]
    args = [q, k, v]
    if bias:
        in_specs += [qspec(1), pl.BlockSpec((HPS, 1, t), lambda h, p, qi_r, kj_r: (h, 0, kj_r[p]))]
        args += [fq, fk]
    return pl.pallas_call(
        kern,
        name=name,
        grid_spec=pltpu.PrefetchScalarGridSpec(
            num_scalar_prefetch=2,
            grid=(nh // HPS, len(pairs)),
            in_specs=in_specs,
            out_specs=[qspec(dv), qspec(1)],
            scratch_shapes=[pltpu.VMEM((HPS, t, 1), F32), pltpu.VMEM((HPS, t, 1), F32),
                            pltpu.VMEM((HPS, t, dv), F32)],
        ),
        out_shape=[jax.ShapeDtypeStruct((nh, s, dv), F32), jax.ShapeDtypeStruct((nh, s, 1), F32)],
        compiler_params=_cp(("parallel", "arbitrary")),
    )(qi_of, kj_of, *args)


def _flash_bwd(name, q, k, v, do, lse, delta, fq, fk, qg, kg, scale, dq_scale, dk_scale):
    nh, s, dk = q.shape
    dv = v.shape[-1]
    dg = qg.shape[-1]
    t = _tm(s, 512)
    nq = s // t
    bias = fq is not None
    ext = qg is not q
    HPS = HPS_BWD

    pairs = [(j, i) for j in range(nq) for i in range(j, nq)]
    kb_of = jnp.asarray(np.array([p[0] for p in pairs], np.int32))
    qi_of = jnp.asarray(np.array([p[1] for p in pairs], np.int32))

    def kern(kb_ref, qi_ref, *refs):
        refs = list(refs)
        q_ref, k_ref, v_ref, do_ref, lse_ref, dl_ref = refs[:6]
        del refs[:6]
        fq_ref, fk_ref = (refs.pop(0), refs.pop(0)) if bias else (None, None)
        qg_ref, kg_ref = (refs.pop(0), refs.pop(0)) if ext else (q_ref, k_ref)
        dq_ref, dk_ref, dv_ref, dk_s, dv_s = refs
        kb, qi = kb_ref[pl.program_id(1)], qi_ref[pl.program_id(1)]

        @pl.when(pl.program_id(1) == 0)
        def _():
            dq_ref[...] = jnp.zeros_like(dq_ref)

        @pl.when(qi == kb)
        def _():
            dk_s[...] = jnp.zeros_like(dk_s)
            dv_s[...] = jnp.zeros_like(dv_s)

        def step(masked):
            rows = pl.ds(pl.multiple_of(qi * t, t), t)
            heads = range(HPS)
            scs = [_attn_scores(q_ref[hh], k_ref[hh], fq_ref[hh] if bias else None, fk_ref[hh] if bias else None,
                                scale, masked, t) for hh in heads]
            dps = [_dot(do_ref[hh], v_ref[hh], 1, 1) for hh in heads]
            ps = [jnp.exp(scs[hh] - lse_ref[hh]) for hh in heads]
            dsbs = [(ps[hh] * (dps[hh] - dl_ref[hh])).astype(BF16) for hh in heads]
            pbs = [p.astype(BF16) for p in ps]
            for hh in heads:
                dv_s[hh] += _dot(pbs[hh], do_ref[hh], 0, 0)
                dk_s[hh] += _dot(dsbs[hh], qg_ref[hh], 0, 0)
                dq_ref[hh, rows, :] += _dot(dsbs[hh], kg_ref[hh], 1, 0) * dq_scale

        @pl.when(qi > kb)
        def _():
            step(False)

        @pl.when(qi == kb)
        def _():
            step(True)

        @pl.when(qi == nq - 1)
        def _():
            dk_ref[...] = dk_s[...] if dk_scale is None else dk_s[...] * dk_scale
            dv_ref[...] = dv_s[...]

    qspec = lambda d: pl.BlockSpec((HPS, t, d), lambda h, p, kb_r, qi_r: (h, qi_r[p], 0))
    kspec = lambda d: pl.BlockSpec((HPS, t, d), lambda h, p, kb_r, qi_r: (h, kb_r[p], 0))
    in_specs = [qspec(dk), kspec(dk), kspec(dv), qspec(dv), qspec(1), qspec(1)]
    args = [q, k, v, do, lse, delta]
    if bias:
        in_specs += [qspec(1), pl.BlockSpec((HPS, 1, t), lambda h, p, kb_r, qi_r: (h, 0, kb_r[p]))]
        args += [fq, fk]
    if ext:
        in_specs += [qspec(dg), kspec(dg)]
        args += [qg, kg]
    return pl.pallas_call(
        kern,
        name=name,
        grid_spec=pltpu.PrefetchScalarGridSpec(
            num_scalar_prefetch=2,
            grid=(nh // HPS, len(pairs)),
            in_specs=in_specs,
            out_specs=[pl.BlockSpec((HPS, s, dg), lambda h, p, kb_r, qi_r: (h, 0, 0), pipeline_mode=pl.Buffered(1)),
                       kspec(dg), kspec(dv)],
            scratch_shapes=[pltpu.VMEM((HPS, t, dg), F32), pltpu.VMEM((HPS, t, dv), F32)],
        ),
        out_shape=[jax.ShapeDtypeStruct((nh, s, dg), F32), jax.ShapeDtypeStruct((nh, s, dg), F32),
                   jax.ShapeDtypeStruct((nh, s, dv), F32)],
        compiler_params=_cp(("arbitrary", "arbitrary")),
    )(kb_of, qi_of, *args)


def _heads(t, nh):
    s = t.shape[0]
    return t.reshape(s, nh, -1).transpose(1, 0, 2)


def _unheads(t):
    nh, s, d = t.shape
    return t.transpose(1, 0, 2).reshape(s, nh * d)


def _perm_uq(w):
    r = w.shape[0]
    w3 = w.reshape(r, MLA_HEADS, MLA_NOPE + MLA_ROPE)
    half = MLA_ROPE // 2
    return jnp.concatenate([w3[:, :, :MLA_NOPE].reshape(r, -1), w3[:, :, MLA_NOPE:MLA_NOPE + half].reshape(r, -1),
                            w3[:, :, MLA_NOPE + half:].reshape(r, -1)], axis=1)


def _unperm_uq(w):
    r = w.shape[0]
    half = MLA_ROPE // 2
    n0 = MLA_HEADS * MLA_NOPE
    n1 = n0 + MLA_HEADS * half
    return jnp.concatenate([w[:, :n0].reshape(r, MLA_HEADS, MLA_NOPE), w[:, n0:n1].reshape(r, MLA_HEADS, half),
                            w[:, n1:].reshape(r, MLA_HEADS, half)], axis=2).reshape(r, -1)


def _perm_ukv(w):
    r = w.shape[0]
    w3 = w.reshape(r, MLA_HEADS, MLA_NOPE + MLA_V)
    return jnp.concatenate([w3[:, :, :MLA_NOPE].reshape(r, -1), w3[:, :, MLA_NOPE:].reshape(r, -1)], axis=1)


def _unperm_ukv(w):
    r = w.shape[0]
    n0 = MLA_HEADS * MLA_NOPE
    return jnp.concatenate([w[:, :n0].reshape(r, MLA_HEADS, MLA_NOPE), w[:, n0:].reshape(r, MLA_HEADS, MLA_V)],
                           axis=2).reshape(r, -1)


_ODD_CUTS = np.cumsum([0, FOX_WIDTH, FOX_WIDTH, FOX_WIDTH, FOX_HEADS, MLA_Q_RANK, MLA_KV_RANK, MLA_ROPE]).tolist()
_ODD_ORDER = (0, 1, 2, 4, 5, 6, 3)


def _perm_odd_in(w):
    parts = [w[:, _ODD_CUTS[j]:_ODD_CUTS[j + 1]] for j in _ODD_ORDER]
    parts.append(jnp.zeros((w.shape[0], ODD_IN_PAD - ODD_IN), w.dtype))
    return jnp.concatenate(parts, axis=1)


def _unperm_odd_in(w):
    widths = [_ODD_CUTS[j + 1] - _ODD_CUTS[j] for j in _ODD_ORDER]
    offs = np.cumsum([0] + widths).tolist()
    pieces = {j: w[:, offs[n]:offs[n + 1]] for n, j in enumerate(_ODD_ORDER)}
    return jnp.concatenate([pieces[j] for j in range(7)], axis=1)


def _pad_cols(w, n):
    return jnp.concatenate([w, jnp.zeros((w.shape[0], n - w.shape[1]), w.dtype)], axis=1)


_BIG = (("even_w_in", 2), ("even_w_out", 1), ("odd_w_in", 2), ("w_uq", 2), ("w_ukv", 2), ("odd_w_out", 1),
        ("ffn_w_gate", 2), ("ffn_w_up", 2), ("ffn_w_down", 1))
_PACK_COLS = 1024
SMALL_LANES = 128


def _unshard(blocks, shp, ax):
    t = jnp.moveaxis(blocks.reshape((N_DEV,) + tuple(shp)), 0, ax)
    full = list(shp)
    full[ax] = shp[ax] * N_DEV
    return t.reshape(full)


def _reshard(full, ax, cols):
    shp = list(full.shape)
    t = full.reshape(shp[:ax] + [N_DEV, shp[ax] // N_DEV] + shp[ax + 1:])
    return jnp.moveaxis(t, ax, 0).reshape(N_DEV, -1, cols)


def kernel(x, even_w_in, pool_w, pool_scale, conv_w, conv_b, dt_bias, a_log, d_skip, ssm_norm_w, even_w_out, odd_w_in, fgate_b, q_norm_w, w_uq, kv_norm_w, w_ukv, odd_w_out, ffn_w_gate, ffn_w_up, ffn_w_down, ln_mix_g, ln_mix_b, ln_ffn_g, ln_ffn_b, loss_target, m_even_w_in, m_pool_w, m_pool_scale, m_conv_w, m_conv_b, m_dt_bias, m_a_log, m_d_skip, m_ssm_norm_w, m_even_w_out, m_odd_w_in, m_fgate_b, m_q_norm_w, m_w_uq, m_kv_norm_w, m_w_ukv, m_odd_w_out, m_ffn_w_gate, m_ffn_w_up, m_ffn_w_down, m_ln_mix_g, m_ln_mix_b, m_ln_ffn_g, m_ln_ffn_b, v_even_w_in, v_pool_w, v_pool_scale, v_conv_w, v_conv_b, v_dt_bias, v_a_log, v_d_skip, v_ssm_norm_w, v_even_w_out, v_odd_w_in, v_fgate_b, v_q_norm_w, v_w_uq, v_kv_norm_w, v_w_ukv, v_odd_w_out, v_ffn_w_gate, v_ffn_w_up, v_ffn_w_down, v_ln_mix_g, v_ln_mix_b, v_ln_ffn_g, v_ln_ffn_b):
    P = dict(even_w_in=even_w_in, pool_w=pool_w, pool_scale=pool_scale, conv_w=conv_w, conv_b=conv_b, dt_bias=dt_bias,
             a_log=a_log, d_skip=d_skip, ssm_norm_w=ssm_norm_w, even_w_out=even_w_out, odd_w_in=odd_w_in,
             fgate_b=fgate_b, q_norm_w=q_norm_w, w_uq=w_uq, kv_norm_w=kv_norm_w, w_ukv=w_ukv, odd_w_out=odd_w_out,
             ffn_w_gate=ffn_w_gate, ffn_w_up=ffn_w_up, ffn_w_down=ffn_w_down, ln_mix_g=ln_mix_g, ln_mix_b=ln_mix_b,
             ln_ffn_g=ln_ffn_g, ln_ffn_b=ln_ffn_b)
    M = dict(even_w_in=m_even_w_in, pool_w=m_pool_w, pool_scale=m_pool_scale, conv_w=m_conv_w, conv_b=m_conv_b,
             dt_bias=m_dt_bias, a_log=m_a_log, d_skip=m_d_skip, ssm_norm_w=m_ssm_norm_w, even_w_out=m_even_w_out,
             odd_w_in=m_odd_w_in, fgate_b=m_fgate_b, q_norm_w=m_q_norm_w, w_uq=m_w_uq, kv_norm_w=m_kv_norm_w,
             w_ukv=m_w_ukv, odd_w_out=m_odd_w_out, ffn_w_gate=m_ffn_w_gate, ffn_w_up=m_ffn_w_up,
             ffn_w_down=m_ffn_w_down, ln_mix_g=m_ln_mix_g, ln_mix_b=m_ln_mix_b, ln_ffn_g=m_ln_ffn_g,
             ln_ffn_b=m_ln_ffn_b)
    V = dict(even_w_in=v_even_w_in, pool_w=v_pool_w, pool_scale=v_pool_scale, conv_w=v_conv_w, conv_b=v_conv_b,
             dt_bias=v_dt_bias, a_log=v_a_log, d_skip=v_d_skip, ssm_norm_w=v_ssm_norm_w, even_w_out=v_even_w_out,
             odd_w_in=v_odd_w_in, fgate_b=v_fgate_b, q_norm_w=v_q_norm_w, w_uq=v_w_uq, kv_norm_w=v_kv_norm_w,
             w_ukv=v_w_ukv, odd_w_out=v_odd_w_out, ffn_w_gate=v_ffn_w_gate, ffn_w_up=v_ffn_w_up,
             ffn_w_down=v_ffn_w_down, ln_mix_g=v_ln_mix_g, ln_mix_b=v_ln_mix_b, ln_ffn_g=v_ln_ffn_g,
             ln_ffn_b=v_ln_ffn_b)
    names = list(P)
    s = x.shape[1]
    me = 4 * lax.axis_index("x") + 2 * lax.axis_index("y") + lax.axis_index("c")

    big_rows = [math.prod(P[n].shape) // _PACK_COLS for n, _ in _BIG]
    pad_rows = [-(-nr // 16) * 16 for nr in big_rows]
    rows_big = sum(pad_rows)
    packed = jnp.concatenate([
        jnp.pad(P[n].astype(BF16).reshape(-1), (0, (pr - nr) * _PACK_COLS))
        for (n, _), nr, pr in zip(_BIG, big_rows, pad_rows)]).reshape(rows_big, _PACK_COLS)
    gathered = _all_gather("ag_weights", packed)
    W = {}
    off = 0
    for (n, ax), nr, pr in zip(_BIG, big_rows, pad_rows):
        W[n] = _unshard(gathered[:, off:off + nr, :], P[n].shape, ax)
        off += pr
    cw_rows, qn_rows, kv_n = conv_w.size // SMALL_LANES, q_norm_w.size // SMALL_LANES, kv_norm_w.size
    small_rows = -(-(cw_rows + qn_rows + 1) // 8) * 8
    small_sh = jnp.concatenate([conv_w.reshape(-1), q_norm_w.reshape(-1), kv_norm_w.reshape(-1),
                                jnp.zeros((SMALL_LANES - kv_n,), F32),
                                jnp.zeros(((small_rows - cw_rows - qn_rows - 1) * SMALL_LANES,), F32)])
    g_small = _all_gather("ag_small_weights", small_sh.reshape(small_rows, SMALL_LANES))
    conv_w_full = _unshard(g_small[:, :cw_rows, :], conv_w.shape, 2)
    q_norm_full = _unshard(g_small[:, cw_rows:cw_rows + qn_rows, :], q_norm_w.shape, 1)
    kv_norm_full = _unshard(g_small[:, cw_rows + qn_rows:cw_rows + qn_rows + 1, :kv_n], kv_norm_w.shape, 1)

    w_in_e = [_pad_cols(W["even_w_in"][i], EVEN_IN_PAD) for i in range(2)]
    w_in_o = [_perm_odd_in(W["odd_w_in"][i]) for i in range(2)]
    w_uq_p = [_perm_uq(W["w_uq"][i]) for i in range(2)]
    w_ukv_p = [_perm_ukv(W["w_ukv"][i]) for i in range(2)]
    w_gu = [jnp.concatenate([W["ffn_w_gate"][l], W["ffn_w_up"][l]], axis=1) for l in range(DEPTH)]

    pos = jnp.arange(s, dtype=F32)
    half = MLA_ROPE // 2
    freqs = jnp.power(ROPE_THETA, -jnp.arange(half, dtype=F32) / half)
    ang = pos[:, None] * freqs[None, :]
    cos16, sin16 = jnp.cos(ang), jnp.sin(ang)
    cos128, sin128 = jnp.tile(cos16, (1, MLA_HEADS)), jnp.tile(sin16, (1, MLA_HEADS))
    row = lambda t: t.reshape(1, -1)

    xcur = x[0]
    xb = xcur.astype(BF16)
    saved = []
    for l in range(DEPTH):
        i = l // 2
        sv = dict(x_in_b=xb)
        if l % 2 == 0:
            proj = _mm("mm_in_even", xb, w_in_e[i], "nn", F32)
            u, z = proj[:, :512], proj[:, 512:1536]
            xbc, dtraw = proj[:, 1536:3072], proj[:, 3072:3088]
            ypool = _pool_fwd(u, pool_w[i], row(pool_scale[i]))
            xc = _conv_fwd(xbc, conv_w_full[i], row(conv_b[i]))
            dt = _softplus_fwd(dtraw, row(dt_bias[i]))
            xh = _heads(xc[:, :SSM_D_INNER], SSM_HEADS)
            dtc = dt.T[:, :, None]
            dtr = dt.T[:, None, :]
            bm, cm = xc[:, SSM_D_INNER:SSM_D_INNER + 256], xc[:, SSM_D_INNER + 256:]
            alog3, dsk3 = a_log[i].reshape(-1, 1, 1), d_skip[i].reshape(-1, 1, 1)
            yh, hprev = _ssd_fwd(xh, xh.transpose(0, 2, 1), dtc, dtr, bm, cm, alog3, dsk3)
            y_ssm = _unheads(yh)
            yn = _gated_rms_fwd(y_ssm, z, row(ssm_norm_w[i]))
            mix = jnp.concatenate([ypool, yn], axis=1)
            h = _mm("mm_out_even", mix, W["even_w_out"][i], "nn", F32)
            sv.update(u=u, z=z, xbc=xbc, dtraw=dtraw, xh=xh, dtc=dtc, dtr=dtr, bm=bm, cm=cm, hprev=hprev,
                      y_ssm=y_ssm, mix=mix)
        else:
            proj = _mm("mm_in_odd", xb, w_in_o[i], "nn", F32)
            qf, kf, vf = proj[:, :512], proj[:, 512:1024], proj[:, 1024:1536]
            cq, ckv = proj[:, 1536:2048], proj[:, 2048:2304]
            kr, fl = proj[:, 2304:2336], proj[:, 2336:2344]
            fl = fl.T
            fcum = _fgate_fwd(fl, fgate_b[i][:, None])
            fq_ = fcum[:, :, None]
            fk_ = fcum[:, None, :]
            qh, kh, vh = (_heads(t.astype(BF16), FOX_HEADS) for t in (qf * FOX_SCALE, kf, vf))
            o_fox, lse_fox = _flash_fwd("fox_fwd", qh, kh, vh, fq_, fk_, None)
            qn = _rms_fwd(cq, row(q_norm_full[i]))
            qp = _mm("mm_uq", qn, w_uq_p[i], "nn", F32)
            q1, q2 = _rope("rope_q", qp[:, 512:640], qp[:, 640:768], cos128, sin128)
            kvn = _rms_fwd(ckv, row(kv_norm_full[i]))
            kvp = _mm("mm_ukv", kvn, w_ukv_p[i], "nn", F32)
            k1, k2 = _rope("rope_k", kr[:, :half], kr[:, half:], cos16, sin16)
            zpad = jnp.zeros((MLA_HEADS, s, MLA_DK_PAD - MLA_NOPE - MLA_ROPE), BF16)
            qm = jnp.concatenate([_heads(qp[:, :512], MLA_HEADS), _heads(q1, MLA_HEADS), _heads(q2, MLA_HEADS)],
                                 axis=2).astype(BF16)
            qm = jnp.concatenate([qm, zpad], axis=2)
            krope = jnp.broadcast_to(jnp.concatenate([k1, k2], axis=1)[None], (MLA_HEADS, s, MLA_ROPE))
            km = jnp.concatenate([_heads(kvp[:, :512], MLA_HEADS), krope], axis=2).astype(BF16)
            km = jnp.concatenate([km, zpad], axis=2)
            vm = _heads(kvp[:, 512:], MLA_HEADS).astype(BF16)
            zero_q, zero_k = jnp.zeros((MLA_HEADS, s, 1), F32), jnp.zeros((MLA_HEADS, 1, s), F32)
            o_mla, lse_mla = _flash_fwd("mla_fwd", qm, km, vm, zero_q, zero_k, MLA_SCALE)
            mix = jnp.concatenate([_unheads(o_fox), _unheads(o_mla)], axis=1).astype(BF16)
            h = _mm("mm_out_odd", mix, W["odd_w_out"][i], "nn", F32)
            sv.update(fl=fl, fq=fq_, fk=fk_, qh=qh, kh=kh, vh=vh, o_fox=o_fox, lse_fox=lse_fox, cq=cq, ckv=ckv,
                      qn=qn, kvn=kvn, qm=qm, km=km, vm=vm, o_mla=o_mla, lse_mla=lse_mla, mix=mix)
        y1, y1b, r1 = _ln_fwd(xcur, h, row(ln_mix_g[l]), row(ln_mix_b[l]))
        gu = _mm("mm_ffn_in", y1b, w_gu[l], "nn", BF16)
        act = _swiglu_fwd(gu)
        h2 = _mm("mm_ffn_out", act, W["ffn_w_down"][l], "nn", F32)
        y2, y2b, r2 = _ln_fwd(y1, h2, row(ln_ffn_g[l]), row(ln_ffn_b[l]))
        sv.update(r1=r1, y1b=y1b, gu=gu, act=act, r2=r2)
        saved.append(sv)
        xcur, xb = y2, y2b

    dy, loss_part = _loss_head(xcur, loss_target[0])
    loss = lax.psum(loss_part[0, 0], ("x", "y", "c"))

    G = {n: [None] * P[n].shape[0] for n in names}
    acur, dcur = None, dy
    for l in reversed(range(DEPTH)):
        i = l // 2
        sv = saved[l]
        dr2, dr2b, dg, db = _ln_bwd(acur, dcur, sv["r2"], row(ln_ffn_g[l]))
        G["ln_ffn_g"][l], G["ln_ffn_b"][l] = dg[0], db[0]
        G["ffn_w_down"][l] = _mm("mm_dw_ffn_out", sv["act"], dr2b, "tn", F32)
        dact = _mm("mm_dx_ffn_out", dr2b, W["ffn_w_down"][l], "nt", BF16)
        dgu = _swiglu_bwd(dact, sv["gu"])
        dwgu = _mm("mm_dw_ffn_in", sv["y1b"], dgu, "tn", F32)
        G["ffn_w_gate"][l], G["ffn_w_up"][l] = dwgu[:, :D_FF], dwgu[:, D_FF:]
        dy1 = _mm("mm_dx_ffn_in", dgu, w_gu[l], "nt", F32)
        dr1, dr1b, dg, db = _ln_bwd(dr2, dy1, sv["r1"], row(ln_mix_g[l]))
        G["ln_mix_g"][l], G["ln_mix_b"][l] = dg[0], db[0]
        if l % 2 == 0:
            G["even_w_out"][i] = _mm("mm_dw_out_even", sv["mix"], dr1b, "tn", F32)
            dmix = _mm("mm_dx_out_even", dr1b, W["even_w_out"][i], "nt", F32)
            dd, dsc, dpw = _pool_bwd1(sv["u"], dmix[:, :POOL_WIDTH], pool_w[i], row(pool_scale[i]))
            G["pool_scale"][i], G["pool_w"][i] = dsc[0], dpw
            du = _pool_bwd2(dd)
            dys, dz, dnw = _gated_rms_bwd(dmix[:, POOL_WIDTH:], sv["y_ssm"], sv["z"], row(ssm_norm_w[i]))
            G["ssm_norm_w"][i] = dnw[0]
            dyh = _heads(dys, SSM_HEADS)
            alog3, dsk3 = a_log[i].reshape(-1, 1, 1), d_skip[i].reshape(-1, 1, 1)
            dxh, ddt3, dbm, dcm, dalog, ddsk = _ssd_bwd(dyh, dyh.transpose(0, 2, 1), sv["xh"], sv["dtc"], sv["dtr"],
                                                        sv["bm"], sv["cm"], sv["hprev"], alog3, dsk3)
            G["a_log"][i], G["d_skip"][i] = dalog.reshape(-1), ddsk.reshape(-1)
            ddtraw, ddtb = _softplus_bwd(ddt3[:, :, 0].T, sv["dtraw"], row(dt_bias[i]))
            G["dt_bias"][i] = ddtb[0]
            dxc = jnp.concatenate([_unheads(dxh), dbm, dcm], axis=1)
            dpre, dcw, dcb = _conv_bwd1(dxc, sv["xbc"], conv_w_full[i], row(conv_b[i]))
            G["conv_w"][i], G["conv_b"][i] = dcw, dcb[0]
            dxbc = _conv_bwd2(dpre, conv_w_full[i])
            dproj = jnp.concatenate([du, dz.astype(BF16), dxbc, ddtraw.astype(BF16),
                                     jnp.zeros((s, EVEN_IN_PAD - EVEN_IN), BF16)], axis=1)
            G["even_w_in"][i] = _mm("mm_dw_in_even", sv["x_in_b"], dproj, "tn", F32)[:, :EVEN_IN]
            dxb = _mm("mm_dx_in_even", dproj, w_in_e[i], "nt", F32)
        else:
            G["odd_w_out"][i] = _mm("mm_dw_out_odd", sv["mix"], dr1b, "tn", F32)
            dmix = _mm("mm_dx_out_odd", dr1b, W["odd_w_out"][i], "nt", BF16)
            do_f = _heads(dmix[:, :FOX_WIDTH], FOX_HEADS)
            dl_f = _rowdot(do_f.reshape(-1, 64), sv["o_fox"].reshape(-1, 64)).reshape(FOX_HEADS, s, 1)
            ones = jnp.ones((FOX_HEADS, s, 64), BF16)
            qg = jnp.concatenate([sv["qh"], ones], axis=2)
            kg = jnp.concatenate([sv["kh"], ones * (1.0 / FOX_SCALE)], axis=2)
            dqe, dke, dvh = _flash_bwd("fox_bwd", sv["qh"], sv["kh"], sv["vh"], do_f, sv["lse_fox"], dl_f,
                                       sv["fq"], sv["fk"], qg, kg, None, FOX_SCALE, None)
            dqh, dkh = dqe[:, :, :64], dke[:, :, :64]
            dfl, dfb = _fgate_bwd(dqe, dke, 64, sv["fl"], fgate_b[i][:, None])
            dfl = dfl.T
            G["fgate_b"][i] = dfb[:, 0]
            do_m = _heads(dmix[:, FOX_WIDTH:], MLA_HEADS)
            dl_m = _rowdot(do_m.reshape(-1, 64), sv["o_mla"].reshape(-1, 64)).reshape(MLA_HEADS, s, 1)
            dqm, dkm, dvm = _flash_bwd("mla_bwd", sv["qm"], sv["km"], sv["vm"], do_m, sv["lse_mla"], dl_m,
                                       None, None, sv["qm"], sv["km"], MLA_SCALE, MLA_SCALE, MLA_SCALE)
            n0, n1 = MLA_NOPE, MLA_NOPE + half
            dq1, dq2 = _rope("rope_q_bwd", _unheads(dqm[:, :, n0:n1]), _unheads(dqm[:, :, n1:n1 + half]),
                             cos128, -sin128)
            dqp = jnp.concatenate([_unheads(dqm[:, :, :n0]), dq1, dq2], axis=1).astype(BF16)
            G["w_uq"][i] = _unperm_uq(_mm("mm_dw_uq", sv["qn"], dqp, "tn", F32))
            dqn = _mm("mm_dx_uq", dqp, w_uq_p[i], "nt", F32)
            dcq, dqw = _rms_bwd(dqn, sv["cq"], row(q_norm_full[i]))
            G["q_norm_w"][i] = dqw[0]
            dk1, dk2 = _headsum_rope_bwd(_unheads(dkm[:, :, n0:n1]), _unheads(dkm[:, :, n1:n1 + half]), cos16, sin16)
            dkvp = jnp.concatenate([_unheads(dkm[:, :, :n0]), _unheads(dvm)], axis=1).astype(BF16)
            G["w_ukv"][i] = _unperm_ukv(_mm("mm_dw_ukv", sv["kvn"], dkvp, "tn", F32))
            dkvn = _mm("mm_dx_ukv", dkvp, w_ukv_p[i], "nt", F32)
            dckv, dkvw = _rms_bwd(dkvn, sv["ckv"], row(kv_norm_full[i]))
            G["kv_norm_w"][i] = dkvw[0]
            dproj = jnp.concatenate([_unheads(dqh), _unheads(dkh), _unheads(dvh), dcq, dckv, dk1, dk2, dfl,
                                     jnp.zeros((s, ODD_IN_PAD - ODD_IN), F32)], axis=1).astype(BF16)
            G["odd_w_in"][i] = _unperm_odd_in(_mm("mm_dw_in_odd", sv["x_in_b"], dproj, "tn", F32))
            dxb = _mm("mm_dx_in_odd", dproj, w_in_o[i], "nt", F32)
        acur, dcur = dr1, dxb
    grad_x = _axpy(acur, dcur)[None]

    gfull = {n: jnp.stack(G[n]) for n in names}
    send = jnp.concatenate([
        jnp.pad(_reshard(gfull[n], ax, _PACK_COLS).astype(BF16), ((0, 0), (0, pr - nr), (0, 0)))
        for (n, ax), nr, pr in zip(_BIG, big_rows, pad_rows)], axis=1)
    recv = _all_to_all("a2a_grads", send)
    gsum = _sum8("sum_grads", recv)
    grads = {}
    off = 0
    for (n, ax), nr, pr in zip(_BIG, big_rows, pad_rows):
        grads[n] = gsum[off:off + nr].reshape(P[n].shape)
        off += pr
    small = [n for n in names if n not in dict(_BIG)]
    sflat = jnp.concatenate([gfull[n].reshape(-1) for n in small])
    n_small = sflat.shape[0]
    rows_small = -(-n_small // (128 * 8)) * 8
    sflat = jnp.concatenate([sflat, jnp.zeros((rows_small * 128 - n_small,), F32)])
    sg = _sum8("sum_small_grads", _all_gather("ag_small_grads", sflat.reshape(rows_small, 128))).reshape(-1)
    off = 0
    for n in small:
        cnt = gfull[n].size
        gf = sg[off:off + cnt].reshape(gfull[n].shape)
        off += cnt
        if gf.shape != P[n].shape:
            width = P[n].shape[-1]
            gf = lax.dynamic_slice_in_dim(gf, me * width, width, axis=gf.ndim - 1)
        grads[n] = gf

    delta, new_m, new_v = {}, {}, {}
    for n, _ in _BIG:
        shp = P[n].shape
        two = lambda t: t.reshape(-1, shp[-1])
        d_, m_, v_ = _adamw("adamw_" + n, two(P[n]), two(grads[n]), two(M[n]), two(V[n]))
        delta[n], new_m[n], new_v[n] = d_.reshape(shp), m_.reshape(shp), v_.reshape(shp)

    def packs(d):
        f = jnp.concatenate([d[n].reshape(-1) for n in small])
        pad = -(-f.shape[0] // (128 * 8)) * 8 * 128 - f.shape[0]
        return jnp.concatenate([f, jnp.zeros((pad,), F32)]).reshape(-1, 128)

    d_, m_, v_ = _adamw("adamw_small", packs(P), packs(grads), packs(M), packs(V))
    off = 0
    for n in small:
        cnt = P[n].size
        for dst, src in ((delta, d_), (new_m, m_), (new_v, v_)):
            dst[n] = src.reshape(-1)[off:off + cnt].reshape(P[n].shape)
        off += cnt

    return (loss, grad_x, *[grads[n] for n in names], *[delta[n] for n in names],
            *[new_m[n] for n in names], *[new_v[n] for n in names])
```
